```python
import math
import jax, jax.numpy as jnp
from jax import lax
import numpy as np

D_MODEL = 4096
BATCH = 1
SEQ = 16384
DEPTH = 1
DEC_BATCH = 32
DEC_SEQ = 32
PAST_LEN = 1024

CHUNK = 64
N_META = 16
Q_BLOCK = 128
EPS = 1e-6

W_A = D_MODEL // 2
HEAD_DIM_A = 128
N_HEADS_A = W_A // (2 * HEAD_DIM_A)
N_BUCKETS = 32
MAX_DISTANCE = 1024

W_B = D_MODEL - W_A
HEAD_DIM_B = 128
N_HEADS_B = W_B // HEAD_DIM_B
CONV_W = 4

IN_COLS = 4 * W_A + 4 * W_B + 2 * N_HEADS_B
SPLIT_POINTS = (W_A, 2 * W_A, 3 * W_A, 4 * W_A, 4 * W_A + 3 * W_B, 4 * W_A + 4 * W_B,
                4 * W_A + 4 * W_B + N_HEADS_B)

kernel_name = 'hymba_diffattn_gdn_streaming_step'


def rms_norm(x, w):
    xf = x.astype(jnp.float32)
    y = xf * lax.rsqrt(jnp.mean(xf * xf, axis=-1, keepdims=True) + EPS)
    return (y * w.astype(jnp.float32)).astype(x.dtype)


def l2_norm(x):
    return x * lax.rsqrt(jnp.sum(x * x, axis=-1, keepdims=True) + EPS)


def lambda_init(layer):
    return 0.8 - 0.6 * math.exp(-0.3 * layer)


def diff_lambda(lq1, lk1, lq2, lk2, lam_init):
    f = jnp.float32
    return (jnp.exp(jnp.sum(lq1.astype(f) * lk1.astype(f)))
            - jnp.exp(jnp.sum(lq2.astype(f) * lk2.astype(f))) + lam_init)


def rel_bucket(rel):
    nb = N_BUCKETS // 2
    max_exact = nb // 2
    n = jnp.abs(rel)
    nf = jnp.maximum(n, 1).astype(jnp.float32)
    large = max_exact + (jnp.log(nf / max_exact) / math.log(MAX_DISTANCE / max_exact)
                         * (nb - max_exact)).astype(jnp.int32)
    large = jnp.minimum(large, nb - 1)
    return jnp.where(rel > 0, nb, 0) + jnp.where(n < max_exact, n, large)


def attend(q, k, v, q_pos, q_chunk, k_pos, k_chunk, rel_bias, lam):
    bias = jnp.transpose(rel_bias[rel_bucket(k_pos[None, :] - q_pos[:, None])], (2, 0, 1))
    visible = k_chunk[None, :] <= q_chunk[:, None]
    s = jnp.einsum('bqhcd,bkhcd->bhcqk', q, k).astype(jnp.float32) * (HEAD_DIM_A ** -0.5)
    s = jnp.where(visible, s + bias.astype(jnp.float32)[None, :, None], -jnp.inf)
    p = jax.nn.softmax(s, axis=-1)
    a = p[:, :, 0] - lam * p[:, :, 1]
    return jnp.einsum('bhqk,bkhe->bqhe', a.astype(v.dtype), v)


def in_proj(xn, w_in):
    B, L, _ = xn.shape
    q_a, k_a, v_a, z_a, qkv_b, z_b, b_raw, a_raw = jnp.split(xn @ w_in, SPLIT_POINTS, axis=-1)
    q_a = q_a.reshape(B, L, N_HEADS_A, 2, HEAD_DIM_A)
    k_a = k_a.reshape(B, L, N_HEADS_A, 2, HEAD_DIM_A)
    v_a = v_a.reshape(B, L, N_HEADS_A, 2 * HEAD_DIM_A)
    return q_a, k_a, v_a, z_a, qkv_b, z_b, b_raw, a_raw


def causal_conv_silu(x, buf, w):
    L = x.shape[1]
    xp = jnp.concatenate([buf.astype(x.dtype), x], axis=1)
    y = xp[:, 0:L] * w[0]
    for i in range(1, CONV_W):
        y = y + xp[:, i:i + L] * w[i]
    return jax.nn.silu(y), xp[:, -(CONV_W - 1):]


def gdn_inputs(qkv, b_raw, a_raw, a_log, dt_bias):
    B, L, _ = qkv.shape
    f = jnp.float32
    q, k, v = jnp.split(qkv.astype(f), 3, axis=-1)
    q = l2_norm(q.reshape(B, L, N_HEADS_B, HEAD_DIM_B)) * (HEAD_DIM_B ** -0.5)
    k = l2_norm(k.reshape(B, L, N_HEADS_B, HEAD_DIM_B))
    v = v.reshape(B, L, N_HEADS_B, HEAD_DIM_B)
    beta = jax.nn.sigmoid(b_raw.astype(f))
    g = -jnp.exp(a_log.astype(f)) * jax.nn.softplus(a_raw.astype(f) + dt_bias.astype(f))
    return q, k, v, g, beta


def gdn_chunk(S, q, k, v, g, beta):
    L = q.shape[1]
    dv = v.shape[-1]
    q, k, v = (jnp.swapaxes(t, 1, 2) for t in (q, k, v))
    g, beta = jnp.swapaxes(g, 1, 2), jnp.swapaxes(beta, 1, 2)
    G = jnp.cumsum(g, axis=-1)
    incl = jnp.tril(jnp.ones((L, L), dtype=bool))
    strict = jnp.tril(jnp.ones((L, L), dtype=bool), -1)
    gamma = jnp.exp(jnp.where(incl, G[..., :, None] - G[..., None, :], -jnp.inf))
    a_mat = jnp.where(strict, beta[..., :, None] * jnp.einsum('bhid,bhjd->bhij', k, k) * gamma, 0.0)
    a_mat = a_mat + jnp.eye(L, dtype=q.dtype)
    rhs = jnp.concatenate([v * beta[..., None], k * (beta * jnp.exp(G))[..., None]], axis=-1)
    sol = jax.lax.linalg.triangular_solve(a_mat, rhs, left_side=True, lower=True, unit_diagonal=True)
    u, w = sol[..., :dv], sol[..., dv:]
    v_new = u - jnp.einsum('bhlk,bhkv->bhlv', w, S)
    o = (jnp.einsum('bhlk,bhkv->bhlv', q * jnp.exp(G)[..., None], S)
         + jnp.einsum('bhij,bhjv->bhiv', jnp.einsum('bhid,bhjd->bhij', q, k) * gamma, v_new))
    g_last = G[..., -1:]
    S_new = (S * jnp.exp(g_last)[..., None]
             + jnp.einsum('bhlk,bhlv->bhkv', k * jnp.exp(g_last - G)[..., None], v_new))
    return S_new, jnp.swapaxes(o, 1, 2)


def attn_gate_out(o, z, subln, lam_init):
    B, L = o.shape[:2]
    o = (rms_norm(o, subln) * (1.0 - lam_init)).reshape(B, L, W_A)
    return o * jax.nn.silu(z.astype(o.dtype))


def gdn_gate_out(o, z, norm_w):
    B, L = o.shape[:2]
    o = rms_norm(o, norm_w).reshape(B, L, W_B)
    return o * jax.nn.silu(z.astype(o.dtype))


def merge(h, y_a, y_b, w_out, post_norm):
    y = jnp.concatenate([y_a.astype(h.dtype), y_b.astype(h.dtype)], axis=-1) @ w_out
    return h + rms_norm(y, post_norm)


def prompt_layer(h, layer, rel_bias, pre_norm, w_in, lq1, lk1, lq2, lk2, subln_a, conv_b,
                 a_log, dt_bias, norm_b, w_out, post_norm):
    B = h.shape[0]
    q_a, k_a, v_a, z_a, qkv_b, z_b, b_raw, a_raw = in_proj(rms_norm(h, pre_norm), w_in)
    lam_init = lambda_init(layer)
    lam = diff_lambda(lq1, lk1, lq2, lk2, lam_init)
    pos = jnp.concatenate([jnp.arange(-N_META, 0, dtype=jnp.int32), jnp.arange(SEQ, dtype=jnp.int32)])
    chunk = jnp.concatenate([jnp.full((N_META,), -1, jnp.int32),
                             jnp.arange(SEQ, dtype=jnp.int32) // CHUNK])
    o_meta = attend(q_a[:, :N_META], k_a, v_a, pos[:N_META], chunk[:N_META], pos, chunk, rel_bias, lam)
    n_blk = SEQ // Q_BLOCK
    q_blocks = jnp.moveaxis(q_a[:, N_META:].reshape(B, n_blk, Q_BLOCK, N_HEADS_A, 2, HEAD_DIM_A), 1, 0)
    o_frames = lax.map(
        lambda a: attend(a[0], k_a, v_a, a[1], a[2], pos, chunk, rel_bias, lam),
        (q_blocks, pos[N_META:].reshape(n_blk, Q_BLOCK), chunk[N_META:].reshape(n_blk, Q_BLOCK)))
    o_frames = jnp.moveaxis(o_frames, 0, 1).reshape(B, SEQ, N_HEADS_A, 2 * HEAD_DIM_A)
    y_a = attn_gate_out(jnp.concatenate([o_meta, o_frames], axis=1), z_a, subln_a, lam_init)
    qkv_c, conv_state = causal_conv_silu(qkv_b, jnp.zeros((B, CONV_W - 1, 3 * W_B), qkv_b.dtype), conv_b)
    q, k, v, g, beta = gdn_inputs(qkv_c, b_raw, a_raw, a_log, dt_bias)
    s0 = jnp.zeros((B, N_HEADS_B, HEAD_DIM_B, HEAD_DIM_B), jnp.float32)
    s1, o_meta_b = gdn_chunk(s0, q[:, :N_META], k[:, :N_META], v[:, :N_META], g[:, :N_META], beta[:, :N_META])
    n_ch = SEQ // CHUNK
    to_chunks = lambda t: jnp.moveaxis(t[:, N_META:].reshape((B, n_ch, CHUNK) + t.shape[2:]), 1, 0)
    s_fin, o_b = lax.scan(lambda s, xs: gdn_chunk(s, *xs), s1,
                          tuple(to_chunks(t) for t in (q, k, v, g, beta)))
    o_b = jnp.moveaxis(o_b, 0, 1).reshape(B, SEQ, N_HEADS_B, HEAD_DIM_B)
    y_b = gdn_gate_out(jnp.concatenate([o_meta_b, o_b], axis=1), z_b, norm_b)
    return merge(h, y_a, y_b, w_out, post_norm), (k_a, v_a, s_fin, conv_state)


def sample_layer(h, k_cache, v_cache, ssm, conv_buf, layer, rel_bias, pre_norm, w_in, lq1, lk1, lq2, lk2,
                 subln_a, conv_b, a_log, dt_bias, norm_b, w_out, post_norm):
    L = h.shape[1]
    q_a, k_a, v_a, z_a, qkv_b, z_b, b_raw, a_raw = in_proj(rms_norm(h, pre_norm), w_in)
    lam_init = lambda_init(layer)
    lam = diff_lambda(lq1, lk1, lq2, lk2, lam_init)
    k_all = jnp.concatenate([k_cache.astype(k_a.dtype), k_a], axis=1)
    v_all = jnp.concatenate([v_cache.astype(v_a.dtype), v_a], axis=1)
    k_pos = jnp.concatenate([jnp.arange(-N_META, 0, dtype=jnp.int32), jnp.arange(PAST_LEN + L, dtype=jnp.int32)])
    k_chunk = jnp.concatenate([jnp.full((N_META,), -1, jnp.int32),
                               jnp.arange(PAST_LEN + L, dtype=jnp.int32) // CHUNK])
    q_pos = PAST_LEN + jnp.arange(L, dtype=jnp.int32)
    o_a = attend(q_a, k_all, v_all, q_pos, q_pos // CHUNK, k_pos, k_chunk, rel_bias, lam)
    y_a = attn_gate_out(o_a, z_a, subln_a, lam_init)
    qkv_c, conv_new = causal_conv_silu(qkv_b, conv_buf, conv_b)
    q, k, v, g, beta = gdn_inputs(qkv_c, b_raw, a_raw, a_log, dt_bias)
    s_new, o_b = gdn_chunk(ssm.astype(jnp.float32), q, k, v, g, beta)
    y_b = gdn_gate_out(o_b, z_b, norm_b)
    return merge(h, y_a, y_b, w_out, post_norm), (k_a, v_a, s_new, conv_new)


def setup_inputs(seed: int = 0) -> dict:
    key = jax.random.key(seed)
    ks = jax.random.split(key, 24)
    f = jnp.float32
    nrm = lambda k, shape, s: jax.random.normal(k, shape, f) * s
    dt = jnp.exp(jax.random.uniform(ks[17], (DEPTH, N_HEADS_B), f, math.log(1e-3), math.log(1e-1)))
    return {
        'x_prompt': nrm(ks[0], (BATCH, SEQ, D_MODEL), 1.0),
        'x_sample': nrm(ks[1], (DEC_BATCH, DEC_SEQ, D_MODEL), 1.0),
        'cache_k_a': nrm(ks[2], (DEPTH, DEC_BATCH, N_META + PAST_LEN, N_HEADS_A, 2, HEAD_DIM_A), 1.0),
        'cache_v_a': nrm(ks[3], (DEPTH, DEC_BATCH, N_META + PAST_LEN, N_HEADS_A, 2 * HEAD_DIM_A), 1.0),
        'state_ssm_b': nrm(ks[4], (DEPTH, DEC_BATCH, N_HEADS_B, HEAD_DIM_B, HEAD_DIM_B), 0.1),
        'state_conv_b': nrm(ks[5], (DEPTH, DEC_BATCH, CONV_W - 1, 3 * W_B), 1.0),
        'meta_tokens': nrm(ks[6], (N_META, D_MODEL), 1.0),
        'rel_bias': nrm(ks[7], (N_BUCKETS, N_HEADS_A), 0.5),
        'pre_norm': 1.0 + nrm(ks[8], (DEPTH, D_MODEL), 0.05),
        'w_in': nrm(ks[9], (DEPTH, D_MODEL, IN_COLS), D_MODEL ** -0.5),
        'lambda_q1': nrm(ks[10], (DEPTH, HEAD_DIM_A), 0.1),
        'lambda_k1': nrm(ks[11], (DEPTH, HEAD_DIM_A), 0.1),
        'lambda_q2': nrm(ks[12], (DEPTH, HEAD_DIM_A), 0.1),
        'lambda_k2': nrm(ks[13], (DEPTH, HEAD_DIM_A), 0.1),
        'subln_a': 1.0 + nrm(ks[14], (DEPTH, 2 * HEAD_DIM_A), 0.05),
        'conv_b': nrm(ks[15], (DEPTH, CONV_W, 3 * W_B), CONV_W ** -0.5),
        'a_log_b': jnp.log(jax.random.uniform(ks[16], (DEPTH, N_HEADS_B), f, 1.0, 16.0)),
        'dt_bias_b': dt + jnp.log(-jnp.expm1(-dt)),
        'norm_b': 1.0 + nrm(ks[18], (DEPTH, HEAD_DIM_B), 0.05),
        'w_out': nrm(ks[19], (DEPTH, D_MODEL, D_MODEL), D_MODEL ** -0.5),
        'post_norm': 1.0 + nrm(ks[20], (DEPTH, D_MODEL), 0.05),
    }


def reference(x_prompt, x_sample, cache_k_a, cache_v_a, state_ssm_b, state_conv_b, meta_tokens, rel_bias,
              pre_norm, w_in, lambda_q1, lambda_k1, lambda_q2, lambda_k2, subln_a, conv_b, a_log_b, dt_bias_b,
              norm_b, w_out, post_norm):
    B = x_prompt.shape[0]
    hp = jnp.concatenate([jnp.broadcast_to(meta_tokens[None].astype(x_prompt.dtype), (B, N_META, D_MODEL)),
                          x_prompt], axis=1)
    hs = x_sample
    kp, vp, sp, cp, ksm, vsm, ssm_s, csm = [], [], [], [], [], [], [], []
    for l in range(DEPTH):
        lp = (pre_norm[l], w_in[l], lambda_q1[l], lambda_k1[l], lambda_q2[l], lambda_k2[l], subln_a[l],
              conv_b[l], a_log_b[l], dt_bias_b[l], norm_b[l], w_out[l], post_norm[l])
        hp, (k_p, v_p, s_p, c_p) = prompt_layer(hp, l, rel_bias, *lp)
        hs, (k_s, v_s, s_s, c_s) = sample_layer(hs, cache_k_a[l], cache_v_a[l], state_ssm_b[l],
                                                state_conv_b[l], l, rel_bias, *lp)
        kp.append(k_p); vp.append(v_p); sp.append(s_p); cp.append(c_p)
        ksm.append(k_s); vsm.append(v_s); ssm_s.append(s_s); csm.append(c_s)
    return (hp[:, N_META:], hs, jnp.stack(kp), jnp.stack(vp), jnp.stack(sp), jnp.stack(cp),
            jnp.stack(ksm), jnp.stack(vsm), jnp.stack(ssm_s), jnp.stack(csm))
```

```python
import functools
import math

import jax
import jax.numpy as jnp
from jax import lax
from jax.experimental import pallas as pl
from jax.experimental.pallas import tpu as pltpu

F32 = jnp.float32
BF16 = jnp.bfloat16
HIGHEST = lax.Precision.HIGHEST

EPS = 1e-6
CHUNK = 64
N_META = 16
HEAD_DIM_A = 128
HEAD_DIM_B = 128
CONV_W = 4
N_BUCKETS = 32
MAX_DISTANCE = 1024
NEG = -1e30

LANES = 128
ATT_BLOCK = 256
GDN_CHUNK = 128
GDN_HEADS_PER_STEP = 4
PROJ_TM = 512
PROJ_TN = 512
OUT_TM = 256
OUT_TN = 512
VMEM_LIMIT = 56 * 1024 * 1024


def _lambda_init(layer):
    return 0.8 - 0.6 * math.exp(-0.3 * layer)


def _bias_saturation_distance():
    nb = N_BUCKETS // 2
    max_exact = nb // 2
    return int(math.ceil(max_exact * (MAX_DISTANCE / max_exact) ** ((nb - max_exact - 1) / (nb - max_exact)))) + 1


N_NEAR = -(-(_bias_saturation_distance() + ATT_BLOCK) // ATT_BLOCK) + 1
FAR_BUCKET = N_BUCKETS // 2 - 1


def _silu(x):
    return x * jax.nn.sigmoid(x)


def _in_proj_kernel(x_ref, pn_ref, w_ref, wt_ref, o32_ref, o16_ref, ba_ref, xn_ref, *, q_tiles, bf_tiles,
                    q_scale):
    j = pl.program_id(1)
    tm = x_ref.shape[0]
    rows = 64

    @pl.when(j == 0)
    def _():
        def body(r, carry):
            sl = pl.ds(pl.multiple_of(r * rows, rows), rows)
            x = x_ref[sl, :]
            ms = jnp.mean(x * x, axis=-1, keepdims=True)
            xn_ref[sl, :] = (x * lax.rsqrt(ms + EPS) * pn_ref[...]).astype(BF16)
            return carry
        lax.fori_loop(0, tm // rows, body, 0)
        ba_ref[...] = jnp.dot(xn_ref[...], wt_ref[...], preferred_element_type=F32)

    r = jnp.dot(xn_ref[...], w_ref[...], preferred_element_type=F32)

    @pl.when(j >= q_tiles)
    def _():
        o32_ref[...] = r

    @pl.when(j < q_tiles)
    def _():
        o16_ref[...] = (r * q_scale).astype(BF16)

    @pl.when(jnp.logical_and(j >= q_tiles, j < bf_tiles))
    def _():
        o16_ref[...] = r.astype(BF16)


def _in_proj(x, pre_norm, w_main, w_tail, w_a):
    m, d = x.shape
    n = w_main.shape[1]
    tm, tn = PROJ_TM, PROJ_TN
    q_tiles = w_a // tn
    bf_tiles = 3 * w_a // tn
    nt = w_tail.shape[1]
    kern = functools.partial(_in_proj_kernel, q_tiles=q_tiles, bf_tiles=bf_tiles, q_scale=HEAD_DIM_A ** -0.5)
    return pl.pallas_call(
        kern,
        grid=(m // tm, n // tn),
        in_specs=[
            pl.BlockSpec((tm, d), lambda i, j: (i, 0)),
            pl.BlockSpec((1, d), lambda i, j: (0, 0)),
            pl.BlockSpec((d, tn), lambda i, j: (0, j)),
            pl.BlockSpec((d, nt), lambda i, j: (0, 0)),
        ],
        out_specs=[
            pl.BlockSpec((tm, tn), lambda i, j: (i, jnp.maximum(j - q_tiles, 0))),
            pl.BlockSpec((tm, tn), lambda i, j: (i, jnp.minimum(j, bf_tiles - 1))),
            pl.BlockSpec((tm, nt), lambda i, j: (i, 0)),
        ],
        out_shape=[
            jax.ShapeDtypeStruct((m, n - w_a), F32),
            jax.ShapeDtypeStruct((m, 3 * w_a), BF16),
            jax.ShapeDtypeStruct((m, nt), F32),
        ],
        scratch_shapes=[pltpu.VMEM((tm, d), BF16)],
        compiler_params=pltpu.CompilerParams(
            dimension_semantics=("arbitrary", "arbitrary"), vmem_limit_bytes=VMEM_LIMIT),
        name="in_proj",
    )(x, pre_norm, w_main, w_tail)


def _bias_values(rb_ref, h, rel):
    nb = N_BUCKETS // 2
    max_exact = nb // 2
    n = jnp.abs(rel)
    nf = jnp.maximum(n, 1).astype(F32)
    large = max_exact + (jnp.log(nf / max_exact) / math.log(MAX_DISTANCE / max_exact)
                         * (nb - max_exact)).astype(jnp.int32)
    large = jnp.minimum(large, nb - 1)
    bucket = jnp.where(rel > 0, nb, 0) + jnp.where(n < max_exact, n, large)
    val = jnp.zeros(rel.shape, F32)
    for b in range(N_BUCKETS):
        val = jnp.where(bucket == b, rb_ref[b, h], val)
    return val - rb_ref[FAR_BUCKET, h]


def _bias_kernel(rb_ref, lamp_ref, fb_ref, mb_ref, sa_ref, se_ref, lam_ref, *, past, dec_seq, n_ca, lam_init):
    h = pl.program_id(0)
    t = ATT_BLOCK
    shift = CHUNK.bit_length() - 1
    i = lax.broadcasted_iota(jnp.int32, (t, t), 0)
    j = lax.broadcasted_iota(jnp.int32, (t, t), 1)
    for dd in range(N_NEAR):
        val = _bias_values(rb_ref, h, (j - i) - t * dd)
        if dd == 0:
            val = jnp.where((j >> shift) <= (i >> shift), val, NEG)
        fb_ref[0, dd] = val
    i = lax.broadcasted_iota(jnp.int32, (t, LANES), 0)
    j = lax.broadcasted_iota(jnp.int32, (t, LANES), 1)
    for qb in range(N_NEAR):
        val = _bias_values(rb_ref, h, (j - N_META) - (t * qb + i))
        mb_ref[0, qb] = jnp.where(j < N_META, val, NEG)
    i = lax.broadcasted_iota(jnp.int32, (dec_seq, n_ca), 0)
    j = lax.broadcasted_iota(jnp.int32, (dec_seq, n_ca), 1)
    sa_ref[0] = _bias_values(rb_ref, h, (j - N_META) - (past + i))
    i = lax.broadcasted_iota(jnp.int32, (dec_seq, LANES), 0)
    j = lax.broadcasted_iota(jnp.int32, (dec_seq, LANES), 1)
    n_extra = N_META + past - n_ca + dec_seq
    val = _bias_values(rb_ref, h, (n_ca - N_META + j) - (past + i))
    se_ref[0] = jnp.where(j < n_extra, val, NEG)
    lp = lamp_ref[...]
    s1 = jnp.sum(lp[0:1] * lp[1:2], axis=-1, keepdims=True)
    s2 = jnp.sum(lp[2:3] * lp[3:4], axis=-1, keepdims=True)
    lam_ref[0] = jnp.broadcast_to(jnp.exp(s1) - jnp.exp(s2) + lam_init, (8, LANES))


def _bias_tables(rel_bias, lam_params, past, dec_seq, n_ca, lam_init):
    nh = rel_bias.shape[1]
    t = ATT_BLOCK
    kern = functools.partial(_bias_kernel, past=past, dec_seq=dec_seq, n_ca=n_ca, lam_init=lam_init)
    return pl.pallas_call(
        kern,
        grid=(nh,),
        in_specs=[
            pl.BlockSpec(memory_space=pltpu.SMEM),
            pl.BlockSpec((4, HEAD_DIM_A), lambda h: (0, 0)),
        ],
        out_specs=[
            pl.BlockSpec((1, N_NEAR, t, t), lambda h: (h, 0, 0, 0)),
            pl.BlockSpec((1, N_NEAR, t, LANES), lambda h: (h, 0, 0, 0)),
            pl.BlockSpec((1, dec_seq, n_ca), lambda h: (h, 0, 0)),
            pl.BlockSpec((1, dec_seq, LANES), lambda h: (h, 0, 0)),
            pl.BlockSpec((1, 8, LANES), lambda h: (h, 0, 0)),
        ],
        out_shape=[
            jax.ShapeDtypeStruct((nh, N_NEAR, t, t), F32),
            jax.ShapeDtypeStruct((nh, N_NEAR, t, LANES), F32),
            jax.ShapeDtypeStruct((nh, dec_seq, n_ca), F32),
            jax.ShapeDtypeStruct((nh, dec_seq, LANES), F32),
            jax.ShapeDtypeStruct((nh, 8, LANES), F32),
        ],
        compiler_params=pltpu.CompilerParams(dimension_semantics=("arbitrary",)),
        name="bias_tables",
    )(rel_bias, lam_params)


def _nt_dot(a, b):
    return lax.dot_general(a, b, (((1,), (1,)), ((), ())), preferred_element_type=F32)


def _attn_finish(o, z, subln, lam_init):
    ms = jnp.mean(o * o, axis=-1, keepdims=True)
    on = o * lax.rsqrt(ms + EPS) * subln * (1.0 - lam_init)
    return on * _silu(z)


def _attn_prompt_kernel(lam_ref, q_ref, k_ref, v_ref, km_ref, vm_ref, fb_ref, mb_ref, z_ref, sub_ref, o_ref,
                        m_ref, l_ref, acc_ref, *, lam_init):
    qi = pl.program_id(1)
    t = ATT_BLOCK
    d = HEAD_DIM_A

    vm = vm_ref[...]
    mb = mb_ref[0, jnp.minimum(qi, N_NEAR - 1)]
    for c in range(2):
        s = _nt_dot(q_ref[:, c * d:(c + 1) * d], km_ref[:, c * d:(c + 1) * d]) + mb
        m = jnp.max(s, axis=-1, keepdims=True)
        p = jnp.exp(s - m)
        m_ref[c] = m
        l_ref[c] = jnp.sum(p, axis=-1, keepdims=True)
        acc_ref[c] = jnp.dot(p.astype(BF16), vm, preferred_element_type=F32)

    def block(jb, bias):
        rows = pl.ds(pl.multiple_of(jb * t, t), t)
        v = v_ref[rows, :]
        for c in range(2):
            s = _nt_dot(q_ref[:, c * d:(c + 1) * d], k_ref[rows, c * d:(c + 1) * d])
            if bias is not None:
                s = s + bias
            m_old = m_ref[c]
            m_new = jnp.maximum(m_old, jnp.max(s, axis=-1, keepdims=True))
            p = jnp.exp(s - m_new)
            alpha = jnp.exp(m_old - m_new)
            m_ref[c] = m_new
            l_ref[c] = alpha * l_ref[c] + jnp.sum(p, axis=-1, keepdims=True)
            acc_ref[c] = alpha * acc_ref[c] + jnp.dot(p.astype(BF16), v, preferred_element_type=F32)

    def far_body(jb, carry):
        block(jb, None)
        return carry
    lax.fori_loop(0, jnp.maximum(qi - (N_NEAR - 1), 0), far_body, 0)

    for dd in range(N_NEAR - 1, -1, -1):
        @pl.when(qi - dd >= 0)
        def _(dd=dd):
            block(qi - dd, fb_ref[0, dd])

    lam = lam_ref[0]
    o = acc_ref[0] / l_ref[0] - lam * (acc_ref[1] / l_ref[1])
    o_ref[...] = _attn_finish(o, z_ref[...], sub_ref[...], lam_init).astype(BF16)


def _attn_prompt(lam, qkv16, o32, km, vm, fb, mb, subln, n_heads, lam_init):
    seq = qkv16.shape[0]
    t = ATT_BLOCK
    hw = 2 * HEAD_DIM_A
    kern = functools.partial(_attn_prompt_kernel, lam_init=lam_init)
    return pl.pallas_call(
        kern,
        grid=(n_heads, seq // t),
        in_specs=[
            pl.BlockSpec(memory_space=pltpu.SMEM),
            pl.BlockSpec((t, hw), lambda h, i: (i, h)),
            pl.BlockSpec((seq, hw), lambda h, i: (0, n_heads + h)),
            pl.BlockSpec((seq, hw), lambda h, i: (0, 2 * n_heads + h)),
            pl.BlockSpec((LANES, hw), lambda h, i: (0, h)),
            pl.BlockSpec((LANES, hw), lambda h, i: (0, h)),
            pl.BlockSpec((1, N_NEAR, t, t), lambda h, i: (h, 0, 0, 0)),
            pl.BlockSpec((1, N_NEAR, t, LANES), lambda h, i: (h, 0, 0, 0)),
            pl.BlockSpec((t, hw), lambda h, i: (i, 2 * n_heads + h)),
            pl.BlockSpec((1, hw), lambda h, i: (0, 0)),
        ],
        out_specs=pl.BlockSpec((t, hw), lambda h, i: (i, h)),
        out_shape=jax.ShapeDtypeStruct((seq, n_heads * hw), BF16),
        scratch_shapes=[
            pltpu.VMEM((2, t, 1), F32),
            pltpu.VMEM((2, t, 1), F32),
            pltpu.VMEM((2, t, hw), F32),
        ],
        compiler_params=pltpu.CompilerParams(
            dimension_semantics=("arbitrary", "arbitrary"), vmem_limit_bytes=VMEM_LIMIT),
        name="attn_prompt",
    )(lam, qkv16, qkv16, qkv16, km, vm, fb, mb, o32, subln)


def _attn_sample_kernel(lam_ref, q_ref, kc_ref, vc_ref, ke_ref, ve_ref, sa_ref, se_ref, z_ref, sub_ref, o_ref,
                        *, lam_init):
    d = HEAD_DIM_A
    vc = vc_ref[0].astype(BF16)
    ve = ve_ref[0]
    outs = []
    for c in range(2):
        q = q_ref[:, c * d:(c + 1) * d]
        s_a = _nt_dot(q, kc_ref[0, :, c * d:(c + 1) * d].astype(BF16)) + sa_ref[0]
        s_e = _nt_dot(q, ke_ref[0, :, c * d:(c + 1) * d]) + se_ref[0]
        m = jnp.maximum(jnp.max(s_a, axis=-1, keepdims=True), jnp.max(s_e, axis=-1, keepdims=True))
        p_a = jnp.exp(s_a - m)
        p_e = jnp.exp(s_e - m)
        l = jnp.sum(p_a, axis=-1, keepdims=True) + jnp.sum(p_e, axis=-1, keepdims=True)
        acc = (jnp.dot(p_a.astype(BF16), vc, preferred_element_type=F32)
               + jnp.dot(p_e.astype(BF16), ve, preferred_element_type=F32))
        outs.append(acc / l)
    o = outs[0] - lam_ref[0] * outs[1]
    o_ref[...] = _attn_finish(o, z_ref[...], sub_ref[...], lam_init).astype(BF16)


def _attn_sample(lam, qkv16, o32, cache_k, cache_v, ke, ve, sa, se, subln, n_heads, dec_b, dec_seq, n_ca,
                 lam_init):
    hw = 2 * HEAD_DIM_A
    kern = functools.partial(_attn_sample_kernel, lam_init=lam_init)
    return pl.pallas_call(
        kern,
        grid=(dec_b, n_heads),
        in_specs=[
            pl.BlockSpec(memory_space=pltpu.SMEM),
            pl.BlockSpec((dec_seq, hw), lambda b, h: (b, h)),
            pl.BlockSpec((1, n_ca, hw), lambda b, h: (b, 0, h)),
            pl.BlockSpec((1, n_ca, hw), lambda b, h: (b, 0, h)),
            pl.BlockSpec((1, LANES, hw), lambda b, h: (b, 0, h)),
            pl.BlockSpec((1, LANES, hw), lambda b, h: (b, 0, h)),
            pl.BlockSpec((1, dec_seq, n_ca), lambda b, h: (h, 0, 0)),
            pl.BlockSpec((1, dec_seq, LANES), lambda b, h: (h, 0, 0)),
            pl.BlockSpec((dec_seq, hw), lambda b, h: (b, 2 * n_heads + h)),
            pl.BlockSpec((1, hw), lambda b, h: (0, 0)),
        ],
        out_specs=pl.BlockSpec((dec_seq, hw), lambda b, h: (b, h)),
        out_shape=jax.ShapeDtypeStruct((qkv16.shape[0], n_heads * hw), BF16),
        compiler_params=pltpu.CompilerParams(
            dimension_semantics=("arbitrary", "arbitrary"), vmem_limit_bytes=VMEM_LIMIT),
        name="attn_sample",
    )(lam, qkv16, cache_k, cache_v, ke, ve, sa, se, o32, subln)


def _hdot(a, b):
    return jnp.dot(a, b, preferred_element_type=F32, precision=HIGHEST)


def _inv_unit_lower(a, row, col):
    sh = 4
    eye = (row == col).astype(F32)
    dblk = jnp.where((row >> sh) == (col >> sh), a, 0.0)
    t = eye - dblk
    pw = dblk
    for _ in range(sh - 1):
        pw = _hdot(pw, pw)
        t = t + _hdot(t, pw)
    while (1 << sh) < a.shape[0]:
        off = jnp.where(jnp.logical_and((row >> (sh + 1)) == (col >> (sh + 1)), (row >> sh) != (col >> sh)),
                        a, 0.0)
        t = t - _hdot(t, _hdot(off, t))
        sh += 1
    return t


def _gdn_kernel(q_ref, k_ref, v_ref, z_ref, ba_ref, cwq_ref, cwk_ref, cwv_ref, gp_ref, nb_ref, s0_ref,
                hq_ref, hk_ref, hv_ref, y_ref, sout_ref, s_ref, xq_ref, xk_ref, xv_ref, *, lr, hg):
    c = pl.program_id(2)
    n = GDN_CHUNK
    dh = HEAD_DIM_B
    halo = 8

    @pl.when(c == 0)
    def _():
        s_ref[...] = s0_ref[0]
        xq_ref[0:halo, :] = hq_ref[0]
        xk_ref[0:halo, :] = hk_ref[0]
        xv_ref[0:halo, :] = hv_ref[0]

    row_w = lax.broadcasted_iota(jnp.int32, (n, hg * dh), 0)

    def conv(x_ref, xbuf, cw_ref):
        xbuf[halo:halo + lr, :] = x_ref[...]
        if lr < n:
            xbuf[halo + lr:halo + n, :] = jnp.zeros((n - lr, hg * dh), F32)
        y = xbuf[halo - 3:halo - 3 + n, :] * cw_ref[0:1, :]
        for i in range(1, CONV_W):
            y = y + xbuf[halo - 3 + i:halo - 3 + i + n, :] * cw_ref[i:i + 1, :]
        y = _silu(y)
        if lr < n:
            y = jnp.where(row_w < lr, y, 0.0)
        xbuf[0:halo, :] = xbuf[lr:lr + halo, :]
        return y

    yq = conv(q_ref, xq_ref, cwq_ref)
    yk = conv(k_ref, xk_ref, cwk_ref)
    yv = conv(v_ref, xv_ref, cwv_ref)

    row = lax.broadcasted_iota(jnp.int32, (n, n), 0)
    col = lax.broadcasted_iota(jnp.int32, (n, n), 1)
    ba = ba_ref[...]
    if lr < n:
        ba = jnp.concatenate([ba, jnp.zeros((n - lr, LANES), F32)], axis=0)
    live = row < lr
    beta = jnp.where(jnp.logical_and(live, col < hg), jax.nn.sigmoid(ba), 0.0)
    gval = -jnp.exp(gp_ref[0, 0:1, :]) * jax.nn.softplus(ba + gp_ref[0, 1:2, :])
    g = jnp.where(jnp.logical_and(live, jnp.logical_and(col >= hg, col < 2 * hg)), gval, 0.0)
    gsum = _hdot((row >= col).astype(F32), g)
    gsum_t = gsum.T
    incl = row >= col
    strict = row > col

    for hh in range(hg):
        cols = slice(hh * dh, (hh + 1) * dh)
        gc = jnp.broadcast_to(gsum[:, hg + hh:hg + hh + 1], (n, n))
        gr = gsum_t[hg + hh:hg + hh + 1, :]
        bc = jnp.broadcast_to(beta[:, hh:hh + 1], (n, n))
        gam = jnp.where(incl, jnp.exp(jnp.minimum(gc - gr, 0.0)), 0.0)
        q = yq[:, cols]
        k = yk[:, cols]
        v = yv[:, cols]
        qh = q * (lax.rsqrt(jnp.sum(q * q, axis=-1, keepdims=True) + EPS) * (dh ** -0.5))
        kh = k * lax.rsqrt(jnp.sum(k * k, axis=-1, keepdims=True) + EPS)
        qb = qh.astype(BF16)
        kb = kh.astype(BF16)
        kk = _nt_dot(kb, kb)
        qk = _nt_dot(qb, kb)
        a = jnp.where(strict, bc * kk * gam, 0.0)
        t = _inv_unit_lower(a, row, col)
        eg = jnp.exp(gc)
        uw = _hdot(t, jnp.concatenate([v * bc, kh * (bc * eg)], axis=1))
        s = s_ref[hh]
        sb = s.astype(BF16)
        v_new = uw[:, :dh] - jnp.dot(uw[:, dh:].astype(BF16), sb, preferred_element_type=F32)
        vnb = v_new.astype(BF16)
        o = (eg * jnp.dot(qb, sb, preferred_element_type=F32)
             + jnp.dot((qk * gam).astype(BF16), vnb, preferred_element_type=F32))
        g_last = gc[n - 1:n, :]
        kd = (kh * jnp.exp(g_last - gc)).astype(BF16)
        s_ref[hh] = s * jnp.exp(g_last) + lax.dot_general(
            kd, vnb, (((0,), (0,)), ((), ())), preferred_element_type=F32)
        on = o * lax.rsqrt(jnp.mean(o * o, axis=-1, keepdims=True) + EPS) * nb_ref[...]
        y = on[0:lr] * _silu(z_ref[:, cols])
        y_ref[:, cols] = y.astype(BF16)

    @pl.when(c == pl.num_programs(2) - 1)
    def _():
        sout_ref[0] = s_ref[...]


def _gdn(o32, ba, conv_w, gate_par, norm_b, s0, halo0, *, n_seq, n_chunks, lr, row_blk_off, w_b, col0):
    hg = GDN_HEADS_PER_STEP
    gw = hg * HEAD_DIM_B
    n_groups = w_b // gw
    n_heads = w_b // HEAD_DIM_B
    cb = col0 // gw
    wb = w_b // gw

    def rmap(off):
        return lambda s, g, c: (row_blk_off + s * n_chunks + c, off + g)

    kern = functools.partial(_gdn_kernel, lr=lr, hg=hg)
    return pl.pallas_call(
        kern,
        grid=(n_seq, n_groups, n_chunks),
        in_specs=[
            pl.BlockSpec((lr, gw), rmap(cb)),
            pl.BlockSpec((lr, gw), rmap(cb + wb)),
            pl.BlockSpec((lr, gw), rmap(cb + 2 * wb)),
            pl.BlockSpec((lr, gw), rmap(cb + 3 * wb)),
            pl.BlockSpec((lr, LANES), rmap(0)),
            pl.BlockSpec((CONV_W, gw), lambda s, g, c: (0, g)),
            pl.BlockSpec((CONV_W, gw), lambda s, g, c: (0, wb + g)),
            pl.BlockSpec((CONV_W, gw), lambda s, g, c: (0, 2 * wb + g)),
            pl.BlockSpec((1, 2, LANES), lambda s, g, c: (g, 0, 0)),
            pl.BlockSpec((1, HEAD_DIM_B), lambda s, g, c: (0, 0)),
            pl.BlockSpec((1, hg, HEAD_DIM_B, HEAD_DIM_B), lambda s, g, c: (s, g, 0, 0)),
            pl.BlockSpec((1, 8, gw), lambda s, g, c: (s, 0, g)),
            pl.BlockSpec((1, 8, gw), lambda s, g, c: (s, 0, wb + g)),
            pl.BlockSpec((1, 8, gw), lambda s, g, c: (s, 0, 2 * wb + g)),
        ],
        out_specs=[
            pl.BlockSpec((lr, gw), rmap(0)),
            pl.BlockSpec((1, hg, HEAD_DIM_B, HEAD_DIM_B), lambda s, g, c: (s, g, 0, 0)),
        ],
        out_shape=[
            jax.ShapeDtypeStruct((o32.shape[0], w_b), BF16),
            jax.ShapeDtypeStruct((n_seq, n_heads, HEAD_DIM_B, HEAD_DIM_B), F32),
        ],
        scratch_shapes=[
            pltpu.VMEM((hg, HEAD_DIM_B, HEAD_DIM_B), F32),
            pltpu.VMEM((8 + GDN_CHUNK, gw), F32),
            pltpu.VMEM((8 + GDN_CHUNK, gw), F32),
            pltpu.VMEM((8 + GDN_CHUNK, gw), F32),
        ],
        compiler_params=pltpu.CompilerParams(
            dimension_semantics=("arbitrary", "arbitrary", "arbitrary"), vmem_limit_bytes=VMEM_LIMIT),
        name=f"gdn_l{lr}",
    )(o32, o32, o32, o32, ba, conv_w, conv_w, conv_w, gate_par, norm_b, s0, halo0, halo0, halo0)


def _out_proj_kernel(ya_ref, yb_ref, w_ref, h_ref, pn_ref, o_ref, *, w_a):
    j = pl.program_id(1)
    tn = w_ref.shape[1]
    r = (jnp.dot(ya_ref[...], w_ref[0:w_a, :], preferred_element_type=F32)
         + jnp.dot(yb_ref[...], w_ref[w_a:, :], preferred_element_type=F32))
    o_ref[:, pl.ds(pl.multiple_of(j * tn, tn), tn)] = r

    @pl.when(j == pl.num_programs(1) - 1)
    def _():
        rows = 64

        def body(i, carry):
            sl = pl.ds(pl.multiple_of(i * rows, rows), rows)
            y = o_ref[sl, :]
            ms = jnp.mean(y * y, axis=-1, keepdims=True)
            o_ref[sl, :] = h_ref[sl, :] + y * lax.rsqrt(ms + EPS) * pn_ref[...]
            return carry
        lax.fori_loop(0, o_ref.shape[0] // rows, body, 0)


def _out_proj(ya, yb, w_out, h, post_norm):
    m, d = h.shape
    w_a = ya.shape[1]
    tm, tn = OUT_TM, OUT_TN
    kern = functools.partial(_out_proj_kernel, w_a=w_a)
    return pl.pallas_call(
        kern,
        grid=(m // tm, d // tn),
        in_specs=[
            pl.BlockSpec((tm, w_a), lambda i, j: (i, 0)),
            pl.BlockSpec((tm, yb.shape[1]), lambda i, j: (i, 0)),
            pl.BlockSpec((d, tn), lambda i, j: (0, j)),
            pl.BlockSpec((tm, d), lambda i, j: (i, 0)),
            pl.BlockSpec((1, d), lambda i, j: (0, 0)),
        ],
        out_specs=pl.BlockSpec((tm, d), lambda i, j: (i, 0)),
        out_shape=jax.ShapeDtypeStruct((m, d), F32),
        compiler_params=pltpu.CompilerParams(
            dimension_semantics=("arbitrary", "arbitrary"), vmem_limit_bytes=VMEM_LIMIT),
        name="out_proj",
    )(ya, yb, w_out, h, post_norm)


def kernel(x_prompt, x_sample, cache_k_a, cache_v_a, state_ssm_b, state_conv_b, meta_tokens, rel_bias, pre_norm,
           w_in, lambda_q1, lambda_k1, lambda_q2, lambda_k2, subln_a, conv_b, a_log_b, dt_bias_b, norm_b, w_out,
           post_norm):
    batch, seq, d_model = x_prompt.shape
    dec_b, dec_seq, _ = x_sample.shape
    depth = w_in.shape[0]
    assert batch == 1 and depth == 1
    n_heads_a = rel_bias.shape[1]
    w_a = n_heads_a * 2 * HEAD_DIM_A
    w_b = d_model - w_a
    n_heads_b = w_b // HEAD_DIM_B
    n_cache = cache_k_a.shape[2]
    past = n_cache - N_META
    n_ca = (n_cache // LANES) * LANES
    n_main = 4 * w_a + 4 * w_b
    lam_init = _lambda_init(0)
    hg = GDN_HEADS_PER_STEP
    n_groups = n_heads_b // hg
    n_dec = dec_b * dec_seq
    assert seq % PROJ_TM == 0 and seq % GDN_CHUNK == 0 and n_cache - n_ca + dec_seq <= LANES

    w_in_bf = w_in[0].astype(BF16)
    w_main = w_in_bf[:, :n_main]
    wb = w_in_bf[:, n_main:n_main + n_heads_b].reshape(d_model, n_groups, hg)
    wa = w_in_bf[:, n_main + n_heads_b:].reshape(d_model, n_groups, hg)
    w_tail = jnp.concatenate([wb, wa, jnp.zeros((d_model, n_groups, LANES - 2 * hg), BF16)], axis=-1)
    w_tail = w_tail.reshape(d_model, n_groups * LANES)
    w_out_bf = w_out[0].astype(BF16)
    gate_par = jnp.zeros((n_groups, 2, LANES), F32)
    gate_par = gate_par.at[:, 0, hg:2 * hg].set(a_log_b[0].reshape(n_groups, hg))
    gate_par = gate_par.at[:, 1, hg:2 * hg].set(dt_bias_b[0].reshape(n_groups, hg))
    lam_params = jnp.stack([lambda_q1[0], lambda_k1[0], lambda_q2[0], lambda_k2[0]])

    xp = x_prompt[0]
    n_small = -(-(n_dec + N_META) // PROJ_TM) * PROJ_TM
    xs = jnp.concatenate([x_sample.reshape(n_dec, d_model), meta_tokens.astype(F32),
                          jnp.zeros((n_small - n_dec - N_META, d_model), F32)], axis=0)
    p32, p16, pba = _in_proj(xp, pre_norm, w_main, w_tail, w_a)
    s32, s16, sba = _in_proj(xs, pre_norm, w_main, w_tail, w_a)

    fb, mb, sa, se, lam_t = _bias_tables(rel_bias, lam_params, past, dec_seq, n_ca, lam_init)
    lam = lam_t[0, 0, 0:1]

    meta16 = s16[n_dec:n_dec + N_META]
    pad_m = jnp.zeros((LANES - N_META, w_a), BF16)
    km = jnp.concatenate([meta16[:, w_a:2 * w_a], pad_m], axis=0)
    vm = jnp.concatenate([meta16[:, 2 * w_a:], pad_m], axis=0)
    ya_p = _attn_prompt(lam, p16, p32, km, vm, fb, mb, subln_a, n_heads_a, lam_init)

    ck = cache_k_a[0].reshape(dec_b, n_cache, w_a)
    cv = cache_v_a[0].reshape(dec_b, n_cache, w_a)
    n_extra = n_cache - n_ca + dec_seq
    pad_e = jnp.zeros((dec_b, LANES - n_extra, w_a), BF16)
    ke = jnp.concatenate([ck[:, n_ca:].astype(BF16), s16[:n_dec, w_a:2 * w_a].reshape(dec_b, dec_seq, w_a),
                          pad_e], axis=1)
    ve = jnp.concatenate([cv[:, n_ca:].astype(BF16), s16[:n_dec, 2 * w_a:].reshape(dec_b, dec_seq, w_a),
                          pad_e], axis=1)
    ya_s = _attn_sample(lam, s16, s32, ck, cv, ke, ve, sa, se, subln_a, n_heads_a, dec_b, dec_seq, n_ca,
                        lam_init)

    col0 = 3 * w_a
    qkv_cols = slice(col0, col0 + 3 * w_b)
    zero_halo = jnp.zeros((1, 8, 3 * w_b), F32)
    gdn = functools.partial(_gdn, conv_w=conv_b[0], gate_par=gate_par, norm_b=norm_b, w_b=w_b, col0=col0)
    _, s_meta = gdn(s32, sba, s0=jnp.zeros((1, n_heads_b, HEAD_DIM_B, HEAD_DIM_B), F32), halo0=zero_halo,
                    n_seq=1, n_chunks=1, lr=N_META, row_blk_off=n_dec // N_META)
    meta_halo = jnp.concatenate([jnp.zeros((8 - (CONV_W - 1), 3 * w_b), F32),
                                 s32[n_dec + N_META - (CONV_W - 1):n_dec + N_META, qkv_cols]], axis=0)[None]
    yb_p, ssm_p = gdn(p32, pba, s0=s_meta, halo0=meta_halo, n_seq=1, n_chunks=seq // GDN_CHUNK, lr=GDN_CHUNK,
                      row_blk_off=0)
    samp_halo = jnp.concatenate([jnp.zeros((dec_b, 8 - (CONV_W - 1), 3 * w_b), F32), state_conv_b[0]], axis=1)
    yb_s, ssm_s = gdn(s32, sba, s0=state_ssm_b[0].astype(F32), halo0=samp_halo, n_seq=dec_b, n_chunks=1,
                      lr=dec_seq, row_blk_off=0)

    y_p = _out_proj(ya_p, yb_p, w_out_bf, xp, post_norm)
    y_s = _out_proj(ya_s[:n_dec], yb_s[:n_dec], w_out_bf, x_sample.reshape(n_dec, d_model), post_norm)

    hd = HEAD_DIM_A
    k_p = jnp.concatenate([s32[n_dec:n_dec + N_META, :w_a], p32[:, :w_a]], axis=0)
    v_p = jnp.concatenate([s32[n_dec:n_dec + N_META, w_a:2 * w_a], p32[:, w_a:2 * w_a]], axis=0)
    conv_p = p32[seq - (CONV_W - 1):, qkv_cols]
    k_s = s32[:n_dec, :w_a]
    v_s = s32[:n_dec, w_a:2 * w_a]
    conv_s = s32[:n_dec, qkv_cols].reshape(dec_b, dec_seq, 3 * w_b)[:, dec_seq - (CONV_W - 1):]
    return (
        y_p[None],
        y_s.reshape(dec_b, dec_seq, d_model),
        k_p.reshape(1, 1, N_META + seq, n_heads_a, 2, hd),
        v_p.reshape(1, 1, N_META + seq, n_heads_a, 2 * hd),
        ssm_p[None],
        conv_p[None, None],
        k_s.reshape(1, dec_b, dec_seq, n_heads_a, 2, hd),
        v_s.reshape(1, dec_b, dec_seq, n_heads_a, 2 * hd),
        ssm_s[None],
        conv_s[None],
    )
```

```python
import functools
import math

import jax
import jax.numpy as jnp
from jax import lax
from jax.experimental import pallas as pl
from jax.experimental.pallas import tpu as pltpu

F32 = jnp.float32
BF16 = jnp.bfloat16
HIGHEST = lax.Precision.HIGHEST

EPS = 1e-6
CHUNK = 64
N_META = 16
HEAD_DIM_A = 128
HEAD_DIM_B = 128
CONV_W = 4
N_BUCKETS = 32
MAX_DISTANCE = 1024
NEG = -1e30

LANES = 128
ATT_BLOCK = 256
GDN_CHUNK = 128
GDN_HEADS_PER_STEP = 4
PROJ_TM = 512
PROJ_TN = 512
OUT_TM = 256
OUT_TN = 512
VMEM_LIMIT = 56 * 1024 * 1024


def _lambda_init(layer):
    return 0.8 - 0.6 * math.exp(-0.3 * layer)


def _bias_saturation_distance():
    nb = N_BUCKETS // 2
    max_exact = nb // 2
    return int(math.ceil(max_exact * (MAX_DISTANCE / max_exact) ** ((nb - max_exact - 1) / (nb - max_exact)))) + 1


N_NEAR = -(-(_bias_saturation_distance() + ATT_BLOCK) // ATT_BLOCK) + 1
FAR_BUCKET = N_BUCKETS // 2 - 1


def _silu(x):
    return x * jax.nn.sigmoid(x)


def _in_proj_kernel(x_ref, pn_ref, w_ref, wt_ref, o32_ref, o16_ref, ba_ref, xn_ref, *, q_tiles, bf_tiles,
                    q_scale):
    j = pl.program_id(1)
    tm = x_ref.shape[0]
    rows = 64

    @pl.when(j == 0)
    def _():
        def body(r, carry):
            sl = pl.ds(pl.multiple_of(r * rows, rows), rows)
            x = x_ref[sl, :]
            ms = jnp.mean(x * x, axis=-1, keepdims=True)
            xn_ref[sl, :] = (x * lax.rsqrt(ms + EPS) * pn_ref[...]).astype(BF16)
            return carry
        lax.fori_loop(0, tm // rows, body, 0)
        ba_ref[...] = jnp.dot(xn_ref[...], wt_ref[...], preferred_element_type=F32)

    r = jnp.dot(xn_ref[...], w_ref[...], preferred_element_type=F32)

    @pl.when(j >= q_tiles)
    def _():
        o32_ref[...] = r

    @pl.when(j < q_tiles)
    def _():
        o16_ref[...] = (r * q_scale).astype(BF16)

    @pl.when(jnp.logical_and(j >= q_tiles, j < bf_tiles))
    def _():
        o16_ref[...] = r.astype(BF16)


def _in_proj(x, pre_norm, w_main, w_tail, w_a):
    m, d = x.shape
    n = w_main.shape[1]
    tm, tn = PROJ_TM, PROJ_TN
    q_tiles = w_a // tn
    bf_tiles = 3 * w_a // tn
    nt = w_tail.shape[1]
    kern = functools.partial(_in_proj_kernel, q_tiles=q_tiles, bf_tiles=bf_tiles, q_scale=HEAD_DIM_A ** -0.5)
    return pl.pallas_call(
        kern,
        grid=(m // tm, n // tn),
        in_specs=[
            pl.BlockSpec((tm, d), lambda i, j: (i, 0)),
            pl.BlockSpec((1, d), lambda i, j: (0, 0)),
            pl.BlockSpec((d, tn), lambda i, j: (0, j)),
            pl.BlockSpec((d, nt), lambda i, j: (0, 0)),
        ],
        out_specs=[
            pl.BlockSpec((tm, tn), lambda i, j: (i, jnp.maximum(j - q_tiles, 0))),
            pl.BlockSpec((tm, tn), lambda i, j: (i, jnp.minimum(j, bf_tiles - 1))),
            pl.BlockSpec((tm, nt), lambda i, j: (i, 0)),
        ],
        out_shape=[
            jax.ShapeDtypeStruct((m, n - w_a), F32),
            jax.ShapeDtypeStruct((m, 3 * w_a), BF16),
            jax.ShapeDtypeStruct((m, nt), F32),
        ],
        scratch_shapes=[pltpu.VMEM((tm, d), BF16)],
        compiler_params=pltpu.CompilerParams(
            dimension_semantics=("arbitrary", "arbitrary"), vmem_limit_bytes=VMEM_LIMIT),
        name="in_proj",
    )(x, pre_norm, w_main, w_tail)


def _bias_values(rb_ref, h, rel):
    nb = N_BUCKETS // 2
    max_exact = nb // 2
    n = jnp.abs(rel)
    nf = jnp.maximum(n, 1).astype(F32)
    large = max_exact + (jnp.log(nf / max_exact) / math.log(MAX_DISTANCE / max_exact)
                         * (nb - max_exact)).astype(jnp.int32)
    large = jnp.minimum(large, nb - 1)
    bucket = jnp.where(rel > 0, nb, 0) + jnp.where(n < max_exact, n, large)
    val = jnp.zeros(rel.shape, F32)
    for b in range(N_BUCKETS):
        val = jnp.where(bucket == b, rb_ref[b, h], val)
    return val - rb_ref[FAR_BUCKET, h]


def _bias_kernel(rb_ref, lamp_ref, fb_ref, mb_ref, sa_ref, se_ref, lam_ref, *, past, dec_seq, n_ca, lam_init):
    h = pl.program_id(0)
    t = ATT_BLOCK
    shift = CHUNK.bit_length() - 1
    i = lax.broadcasted_iota(jnp.int32, (t, t), 0)
    j = lax.broadcasted_iota(jnp.int32, (t, t), 1)
    for dd in range(N_NEAR):
        val = _bias_values(rb_ref, h, (j - i) - t * dd)
        if dd == 0:
            val = jnp.where((j >> shift) <= (i >> shift), val, NEG)
        fb_ref[0, dd] = val
    i = lax.broadcasted_iota(jnp.int32, (t, LANES), 0)
    j = lax.broadcasted_iota(jnp.int32, (t, LANES), 1)
    for qb in range(N_NEAR):
        val = _bias_values(rb_ref, h, (j - N_META) - (t * qb + i))
        mb_ref[0, qb] = jnp.where(j < N_META, val, NEG)
    i = lax.broadcasted_iota(jnp.int32, (dec_seq, n_ca), 0)
    j = lax.broadcasted_iota(jnp.int32, (dec_seq, n_ca), 1)
    sa_ref[0] = _bias_values(rb_ref, h, (j - N_META) - (past + i))
    i = lax.broadcasted_iota(jnp.int32, (dec_seq, LANES), 0)
    j = lax.broadcasted_iota(jnp.int32, (dec_seq, LANES), 1)
    n_extra = N_META + past - n_ca + dec_seq
    val = _bias_values(rb_ref, h, (n_ca - N_META + j) - (past + i))
    se_ref[0] = jnp.where(j < n_extra, val, NEG)
    lp = lamp_ref[...]
    s1 = jnp.sum(lp[0:1] * lp[1:2], axis=-1, keepdims=True)
    s2 = jnp.sum(lp[2:3] * lp[3:4], axis=-1, keepdims=True)
    lam_ref[0] = jnp.broadcast_to(jnp.exp(s1) - jnp.exp(s2) + lam_init, (8, LANES))


def _bias_tables(rel_bias, lam_params, past, dec_seq, n_ca, lam_init):
    nh = rel_bias.shape[1]
    t = ATT_BLOCK
    kern = functools.partial(_bias_kernel, past=past, dec_seq=dec_seq, n_ca=n_ca, lam_init=lam_init)
    return pl.pallas_call(
        kern,
        grid=(nh,),
        in_specs=[
            pl.BlockSpec(memory_space=pltpu.SMEM),
            pl.BlockSpec((4, HEAD_DIM_A), lambda h: (0, 0)),
        ],
        out_specs=[
            pl.BlockSpec((1, N_NEAR, t, t), lambda h: (h, 0, 0, 0)),
            pl.BlockSpec((1, N_NEAR, t, LANES), lambda h: (h, 0, 0, 0)),
            pl.BlockSpec((1, dec_seq, n_ca), lambda h: (h, 0, 0)),
            pl.BlockSpec((1, dec_seq, LANES), lambda h: (h, 0, 0)),
            pl.BlockSpec((1, 8, LANES), lambda h: (h, 0, 0)),
        ],
        out_shape=[
            jax.ShapeDtypeStruct((nh, N_NEAR, t, t), F32),
            jax.ShapeDtypeStruct((nh, N_NEAR, t, LANES), F32),
            jax.ShapeDtypeStruct((nh, dec_seq, n_ca), F32),
            jax.ShapeDtypeStruct((nh, dec_seq, LANES), F32),
            jax.ShapeDtypeStruct((nh, 8, LANES), F32),
        ],
        compiler_params=pltpu.CompilerParams(dimension_semantics=("arbitrary",)),
        name="bias_tables",
    )(rel_bias, lam_params)


def _nt_dot(a, b):
    return lax.dot_general(a, b, (((1,), (1,)), ((), ())), preferred_element_type=F32)


def _attn_finish(o, z, subln, lam_init):
    ms = jnp.mean(o * o, axis=-1, keepdims=True)
    on = o * lax.rsqrt(ms + EPS) * subln * (1.0 - lam_init)
    return on * _silu(z)


def _attn_prompt_kernel(lam_ref, q_ref, k_ref, v_ref, km_ref, vm_ref, fb_ref, mb_ref, z_ref, sub_ref, o_ref,
                        m_ref, l_ref, acc_ref, *, lam_init):
    qi = pl.program_id(1)
    t = ATT_BLOCK
    d = HEAD_DIM_A
    hw = 2 * d

    def rep(x, n):
        return jnp.concatenate([x] * n, axis=1) if n > 1 else x

    def lane_fold(p):
        out = p[:, 0:LANES]
        for g in range(1, p.shape[1] // LANES):
            out = out + p[:, g * LANES:(g + 1) * LANES]
        return out

    vm = vm_ref[...]
    mb = mb_ref[0, jnp.minimum(qi, N_NEAR - 1)]
    for c in range(2):
        s = _nt_dot(q_ref[:, c * d:(c + 1) * d], km_ref[:, c * d:(c + 1) * d]) + mb
        m = jnp.broadcast_to(jnp.max(s, axis=-1, keepdims=True), (t, LANES))
        p = jnp.exp(s - m)
        m_ref[c] = m
        l_ref[c] = p
        acc_ref[c] = jnp.dot(p.astype(BF16), vm, preferred_element_type=F32)

    def block(start, width, bias):
        rows = pl.ds(start, width)
        v = v_ref[rows, :]
        for c in range(2):
            s = _nt_dot(q_ref[:, c * d:(c + 1) * d], k_ref[rows, c * d:(c + 1) * d])
            if bias is not None:
                s = s + bias
            m_old = m_ref[c]
            m_new = jnp.maximum(m_old, jnp.max(s, axis=-1, keepdims=True))
            p = jnp.exp(s - rep(m_new, width // LANES))
            alpha = jnp.exp(m_old - m_new)
            m_ref[c] = m_new
            l_ref[c] = alpha * l_ref[c] + lane_fold(p)
            acc_ref[c] = rep(alpha, hw // LANES) * acc_ref[c] + jnp.dot(
                p.astype(BF16), v, preferred_element_type=F32)

    n_far = jnp.maximum(qi - (N_NEAR - 1), 0)

    def far_body(jp, carry):
        block(pl.multiple_of(jp * (2 * t), 2 * t), 2 * t, None)
        return carry
    lax.fori_loop(0, n_far >> 1, far_body, 0)

    @pl.when((n_far & 1) == 1)
    def _():
        block(pl.multiple_of((n_far - 1) * t, t), t, None)

    for dd in range(N_NEAR - 1, -1, -1):
        @pl.when(qi - dd >= 0)
        def _(dd=dd):
            block(pl.multiple_of((qi - dd) * t, t), t, fb_ref[0, dd])

    inv0 = 1.0 / jnp.sum(l_ref[0], axis=-1, keepdims=True)
    inv1 = lam_ref[0] / jnp.sum(l_ref[1], axis=-1, keepdims=True)
    o = acc_ref[0] * inv0 - acc_ref[1] * inv1
    o_ref[...] = _attn_finish(o, z_ref[...], sub_ref[...], lam_init).astype(BF16)


def _attn_prompt(lam, qkv16, o32, km, vm, fb, mb, subln, n_heads, lam_init):
    seq = qkv16.shape[0]
    t = ATT_BLOCK
    hw = 2 * HEAD_DIM_A
    kern = functools.partial(_attn_prompt_kernel, lam_init=lam_init)
    return pl.pallas_call(
        kern,
        grid=(n_heads, seq // t),
        in_specs=[
            pl.BlockSpec(memory_space=pltpu.SMEM),
            pl.BlockSpec((t, hw), lambda h, i: (i, h)),
            pl.BlockSpec((seq, hw), lambda h, i: (0, n_heads + h)),
            pl.BlockSpec((seq, hw), lambda h, i: (0, 2 * n_heads + h)),
            pl.BlockSpec((LANES, hw), lambda h, i: (0, h)),
            pl.BlockSpec((LANES, hw), lambda h, i: (0, h)),
            pl.BlockSpec((1, N_NEAR, t, t), lambda h, i: (h, 0, 0, 0)),
            pl.BlockSpec((1, N_NEAR, t, LANES), lambda h, i: (h, 0, 0, 0)),
            pl.BlockSpec((t, hw), lambda h, i: (i, 2 * n_heads + h)),
            pl.BlockSpec((1, hw), lambda h, i: (0, 0)),
        ],
        out_specs=pl.BlockSpec((t, hw), lambda h, i: (i, h)),
        out_shape=jax.ShapeDtypeStruct((seq, n_heads * hw), BF16),
        scratch_shapes=[
            pltpu.VMEM((2, t, LANES), F32),
            pltpu.VMEM((2, t, LANES), F32),
            pltpu.VMEM((2, t, hw), F32),
        ],
        compiler_params=pltpu.CompilerParams(
            dimension_semantics=("arbitrary", "arbitrary"), vmem_limit_bytes=VMEM_LIMIT),
        name="attn_prompt",
    )(lam, qkv16, qkv16, qkv16, km, vm, fb, mb, o32, subln)


def _attn_sample_kernel(lam_ref, q_ref, kc_ref, vc_ref, ke_ref, ve_ref, sa_ref, se_ref, z_ref, sub_ref, o_ref,
                        *, lam_init):
    d = HEAD_DIM_A
    vc = vc_ref[0].astype(BF16)
    ve = ve_ref[0]
    outs = []
    for c in range(2):
        q = q_ref[:, c * d:(c + 1) * d]
        s_a = _nt_dot(q, kc_ref[0, :, c * d:(c + 1) * d].astype(BF16)) + sa_ref[0]
        s_e = _nt_dot(q, ke_ref[0, :, c * d:(c + 1) * d]) + se_ref[0]
        m = jnp.maximum(jnp.max(s_a, axis=-1, keepdims=True), jnp.max(s_e, axis=-1, keepdims=True))
        p_a = jnp.exp(s_a - m)
        p_e = jnp.exp(s_e - m)
        l = jnp.sum(p_a, axis=-1, keepdims=True) + jnp.sum(p_e, axis=-1, keepdims=True)
        acc = (jnp.dot(p_a.astype(BF16), vc, preferred_element_type=F32)
               + jnp.dot(p_e.astype(BF16), ve, preferred_element_type=F32))
        outs.append(acc / l)
    o = outs[0] - lam_ref[0] * outs[1]
    o_ref[...] = _attn_finish(o, z_ref[...], sub_ref[...], lam_init).astype(BF16)


def _attn_sample(lam, qkv16, o32, cache_k, cache_v, ke, ve, sa, se, subln, n_heads, dec_b, dec_seq, n_ca,
                 lam_init):
    hw = 2 * HEAD_DIM_A
    kern = functools.partial(_attn_sample_kernel, lam_init=lam_init)
    return pl.pallas_call(
        kern,
        grid=(dec_b, n_heads),
        in_specs=[
            pl.BlockSpec(memory_space=pltpu.SMEM),
            pl.BlockSpec((dec_seq, hw), lambda b, h: (b, h)),
            pl.BlockSpec((1, n_ca, hw), lambda b, h: (b, 0, h)),
            pl.BlockSpec((1, n_ca, hw), lambda b, h: (b, 0, h)),
            pl.BlockSpec((1, LANES, hw), lambda b, h: (b, 0, h)),
            pl.BlockSpec((1, LANES, hw), lambda b, h: (b, 0, h)),
            pl.BlockSpec((1, dec_seq, n_ca), lambda b, h: (h, 0, 0)),
            pl.BlockSpec((1, dec_seq, LANES), lambda b, h: (h, 0, 0)),
            pl.BlockSpec((dec_seq, hw), lambda b, h: (b, 2 * n_heads + h)),
            pl.BlockSpec((1, hw), lambda b, h: (0, 0)),
        ],
        out_specs=pl.BlockSpec((dec_seq, hw), lambda b, h: (b, h)),
        out_shape=jax.ShapeDtypeStruct((qkv16.shape[0], n_heads * hw), BF16),
        compiler_params=pltpu.CompilerParams(
            dimension_semantics=("arbitrary", "arbitrary"), vmem_limit_bytes=VMEM_LIMIT),
        name="attn_sample",
    )(lam, qkv16, cache_k, cache_v, ke, ve, sa, se, o32, subln)


def _hdot(a, b):
    return jnp.dot(a, b, preferred_element_type=F32, precision=HIGHEST)


def _bdot(a, b):
    return jnp.dot(a.astype(BF16), b.astype(BF16), preferred_element_type=F32)


def _inv_unit_lower(a, row, col):
    sh = 4
    eye = (row == col).astype(F32)
    dblk = jnp.where((row >> sh) == (col >> sh), a, 0.0)
    t = eye - dblk
    pw = dblk
    for _ in range(sh - 1):
        pw = _bdot(pw, pw)
        t = t + _bdot(t, pw)
    while (1 << sh) < a.shape[0]:
        off = jnp.where(jnp.logical_and((row >> (sh + 1)) == (col >> (sh + 1)), (row >> sh) != (col >> sh)),
                        a, 0.0)
        t = t - _bdot(t, _bdot(off, t))
        sh += 1
    return t


def _gdn_kernel(q_ref, k_ref, v_ref, z_ref, ba_ref, cwq_ref, cwk_ref, cwv_ref, gp_ref, nb_ref, s0_ref,
                hq_ref, hk_ref, hv_ref, y_ref, sout_ref, s_ref, xq_ref, xk_ref, xv_ref, *, lr, hg):
    c = pl.program_id(2)
    n = GDN_CHUNK
    dh = HEAD_DIM_B
    halo = 8

    @pl.when(c == 0)
    def _():
        s_ref[...] = s0_ref[0]
        xq_ref[0:halo, :] = hq_ref[0]
        xk_ref[0:halo, :] = hk_ref[0]
        xv_ref[0:halo, :] = hv_ref[0]

    row_w = lax.broadcasted_iota(jnp.int32, (n, hg * dh), 0)

    def conv(x_ref, xbuf, cw_ref):
        xbuf[halo:halo + lr, :] = x_ref[...]
        if lr < n:
            xbuf[halo + lr:halo + n, :] = jnp.zeros((n - lr, hg * dh), F32)
        y = xbuf[halo - 3:halo - 3 + n, :] * cw_ref[0:1, :]
        for i in range(1, CONV_W):
            y = y + xbuf[halo - 3 + i:halo - 3 + i + n, :] * cw_ref[i:i + 1, :]
        y = _silu(y)
        if lr < n:
            y = jnp.where(row_w < lr, y, 0.0)
        xbuf[0:halo, :] = xbuf[lr:lr + halo, :]
        return y

    yq = conv(q_ref, xq_ref, cwq_ref)
    yk = conv(k_ref, xk_ref, cwk_ref)
    yv = conv(v_ref, xv_ref, cwv_ref)

    row = lax.broadcasted_iota(jnp.int32, (n, n), 0)
    col = lax.broadcasted_iota(jnp.int32, (n, n), 1)
    ba = ba_ref[...]
    if lr < n:
        ba = jnp.concatenate([ba, jnp.zeros((n - lr, LANES), F32)], axis=0)
    live = row < lr
    beta = jnp.where(jnp.logical_and(live, col < hg), jax.nn.sigmoid(ba), 0.0)
    gval = -jnp.exp(gp_ref[0, 0:1, :]) * jax.nn.softplus(ba + gp_ref[0, 1:2, :])
    g = jnp.where(jnp.logical_and(live, jnp.logical_and(col >= hg, col < 2 * hg)), gval, 0.0)
    gsum = _hdot((row >= col).astype(F32), g)
    gsum_t = gsum.T
    incl = row >= col
    strict = row > col

    for hh in range(hg):
        cols = slice(hh * dh, (hh + 1) * dh)
        gc = jnp.broadcast_to(gsum[:, hg + hh:hg + hh + 1], (n, n))
        gr = gsum_t[hg + hh:hg + hh + 1, :]
        bc = jnp.broadcast_to(beta[:, hh:hh + 1], (n, n))
        gam = jnp.where(incl, jnp.exp(jnp.minimum(gc - gr, 0.0)), 0.0)
        q = yq[:, cols]
        k = yk[:, cols]
        v = yv[:, cols]
        qh = q * (lax.rsqrt(jnp.sum(q * q, axis=-1, keepdims=True) + EPS) * (dh ** -0.5))
        kh = k * lax.rsqrt(jnp.sum(k * k, axis=-1, keepdims=True) + EPS)
        qb = qh.astype(BF16)
        kb = kh.astype(BF16)
        kk = _nt_dot(kb, kb)
        qk = _nt_dot(qb, kb)
        a = jnp.where(strict, bc * kk * gam, 0.0)
        t = _inv_unit_lower(a, row, col)
        eg = jnp.exp(gc)
        uw = _bdot(t, jnp.concatenate([v * bc, kh * (bc * eg)], axis=1))
        s = s_ref[hh]
        sb = s.astype(BF16)
        v_new = uw[:, :dh] - jnp.dot(uw[:, dh:].astype(BF16), sb, preferred_element_type=F32)
        vnb = v_new.astype(BF16)
        o = (eg * jnp.dot(qb, sb, preferred_element_type=F32)
             + jnp.dot((qk * gam).astype(BF16), vnb, preferred_element_type=F32))
        g_last = gc[n - 1:n, :]
        kd = (kh * jnp.exp(g_last - gc)).astype(BF16)
        s_ref[hh] = s * jnp.exp(g_last) + lax.dot_general(
            kd, vnb, (((0,), (0,)), ((), ())), preferred_element_type=F32)
        on = o * lax.rsqrt(jnp.mean(o * o, axis=-1, keepdims=True) + EPS) * nb_ref[...]
        y = on[0:lr] * _silu(z_ref[:, cols])
        y_ref[:, cols] = y.astype(BF16)

    @pl.when(c == pl.num_programs(2) - 1)
    def _():
        sout_ref[0] = s_ref[...]


def _gdn(o32, ba, conv_w, gate_par, norm_b, s0, halo0, *, n_seq, n_chunks, lr, row_blk_off, w_b, col0):
    hg = GDN_HEADS_PER_STEP
    gw = hg * HEAD_DIM_B
    n_groups = w_b // gw
    n_heads = w_b // HEAD_DIM_B
    cb = col0 // gw
    wb = w_b // gw

    def rmap(off):
        return lambda s, g, c: (row_blk_off + s * n_chunks + c, off + g)

    kern = functools.partial(_gdn_kernel, lr=lr, hg=hg)
    return pl.pallas_call(
        kern,
        grid=(n_seq, n_groups, n_chunks),
        in_specs=[
            pl.BlockSpec((lr, gw), rmap(cb)),
            pl.BlockSpec((lr, gw), rmap(cb + wb)),
            pl.BlockSpec((lr, gw), rmap(cb + 2 * wb)),
            pl.BlockSpec((lr, gw), rmap(cb + 3 * wb)),
            pl.BlockSpec((lr, LANES), rmap(0)),
            pl.BlockSpec((CONV_W, gw), lambda s, g, c: (0, g)),
            pl.BlockSpec((CONV_W, gw), lambda s, g, c: (0, wb + g)),
            pl.BlockSpec((CONV_W, gw), lambda s, g, c: (0, 2 * wb + g)),
            pl.BlockSpec((1, 2, LANES), lambda s, g, c: (g, 0, 0)),
            pl.BlockSpec((1, HEAD_DIM_B), lambda s, g, c: (0, 0)),
            pl.BlockSpec((1, hg, HEAD_DIM_B, HEAD_DIM_B), lambda s, g, c: (s, g, 0, 0)),
            pl.BlockSpec((1, 8, gw), lambda s, g, c: (s, 0, g)),
            pl.BlockSpec((1, 8, gw), lambda s, g, c: (s, 0, wb + g)),
            pl.BlockSpec((1, 8, gw), lambda s, g, c: (s, 0, 2 * wb + g)),
        ],
        out_specs=[
            pl.BlockSpec((lr, gw), rmap(0)),
            pl.BlockSpec((1, hg, HEAD_DIM_B, HEAD_DIM_B), lambda s, g, c: (s, g, 0, 0)),
        ],
        out_shape=[
            jax.ShapeDtypeStruct((o32.shape[0], w_b), BF16),
            jax.ShapeDtypeStruct((n_seq, n_heads, HEAD_DIM_B, HEAD_DIM_B), F32),
        ],
        scratch_shapes=[
            pltpu.VMEM((hg, HEAD_DIM_B, HEAD_DIM_B), F32),
            pltpu.VMEM((8 + GDN_CHUNK, gw), F32),
            pltpu.VMEM((8 + GDN_CHUNK, gw), F32),
            pltpu.VMEM((8 + GDN_CHUNK, gw), F32),
        ],
        compiler_params=pltpu.CompilerParams(
            dimension_semantics=("arbitrary", "arbitrary", "arbitrary"), vmem_limit_bytes=VMEM_LIMIT),
        name=f"gdn_l{lr}",
    )(o32, o32, o32, o32, ba, conv_w, conv_w, conv_w, gate_par, norm_b, s0, halo0, halo0, halo0)


def _out_proj_kernel(ya_ref, yb_ref, w_ref, h_ref, pn_ref, o_ref, *, w_a):
    j = pl.program_id(1)
    tn = w_ref.shape[1]
    r = (jnp.dot(ya_ref[...], w_ref[0:w_a, :], preferred_element_type=F32)
         + jnp.dot(yb_ref[...], w_ref[w_a:, :], preferred_element_type=F32))
    o_ref[:, pl.ds(pl.multiple_of(j * tn, tn), tn)] = r

    @pl.when(j == pl.num_programs(1) - 1)
    def _():
        rows = 64

        def body(i, carry):
            sl = pl.ds(pl.multiple_of(i * rows, rows), rows)
            y = o_ref[sl, :]
            ms = jnp.mean(y * y, axis=-1, keepdims=True)
            o_ref[sl, :] = h_ref[sl, :] + y * lax.rsqrt(ms + EPS) * pn_ref[...]
            return carry
        lax.fori_loop(0, o_ref.shape[0] // rows, body, 0)


def _out_proj(ya, yb, w_out, h, post_norm):
    m, d = h.shape
    w_a = ya.shape[1]
    tm, tn = OUT_TM, OUT_TN
    kern = functools.partial(_out_proj_kernel, w_a=w_a)
    return pl.pallas_call(
        kern,
        grid=(m // tm, d // tn),
        in_specs=[
            pl.BlockSpec((tm, w_a), lambda i, j: (i, 0)),
            pl.BlockSpec((tm, yb.shape[1]), lambda i, j: (i, 0)),
            pl.BlockSpec((d, tn), lambda i, j: (0, j)),
            pl.BlockSpec((tm, d), lambda i, j: (i, 0)),
            pl.BlockSpec((1, d), lambda i, j: (0, 0)),
        ],
        out_specs=pl.BlockSpec((tm, d), lambda i, j: (i, 0)),
        out_shape=jax.ShapeDtypeStruct((m, d), F32),
        compiler_params=pltpu.CompilerParams(
            dimension_semantics=("arbitrary", "arbitrary"), vmem_limit_bytes=VMEM_LIMIT),
        name="out_proj",
    )(ya, yb, w_out, h, post_norm)


def kernel(x_prompt, x_sample, cache_k_a, cache_v_a, state_ssm_b, state_conv_b, meta_tokens, rel_bias, pre_norm,
           w_in, lambda_q1, lambda_k1, lambda_q2, lambda_k2, subln_a, conv_b, a_log_b, dt_bias_b, norm_b, w_out,
           post_norm):
    batch, seq, d_model = x_prompt.shape
    dec_b, dec_seq, _ = x_sample.shape
    depth = w_in.shape[0]
    assert batch == 1 and depth == 1
    n_heads_a = rel_bias.shape[1]
    w_a = n_heads_a * 2 * HEAD_DIM_A
    w_b = d_model - w_a
    n_heads_b = w_b // HEAD_DIM_B
    n_cache = cache_k_a.shape[2]
    past = n_cache - N_META
    n_ca = (n_cache // LANES) * LANES
    n_main = 4 * w_a + 4 * w_b
    lam_init = _lambda_init(0)
    hg = GDN_HEADS_PER_STEP
    n_groups = n_heads_b // hg
    n_dec = dec_b * dec_seq
    assert seq % PROJ_TM == 0 and seq % GDN_CHUNK == 0 and n_cache - n_ca + dec_seq <= LANES

    w_in_bf = w_in[0].astype(BF16)
    w_main = w_in_bf[:, :n_main]
    wb = w_in_bf[:, n_main:n_main + n_heads_b].reshape(d_model, n_groups, hg)
    wa = w_in_bf[:, n_main + n_heads_b:].reshape(d_model, n_groups, hg)
    w_tail = jnp.concatenate([wb, wa, jnp.zeros((d_model, n_groups, LANES - 2 * hg), BF16)], axis=-1)
    w_tail = w_tail.reshape(d_model, n_groups * LANES)
    w_out_bf = w_out[0].astype(BF16)
    gate_par = jnp.zeros((n_groups, 2, LANES), F32)
    gate_par = gate_par.at[:, 0, hg:2 * hg].set(a_log_b[0].reshape(n_groups, hg))
    gate_par = gate_par.at[:, 1, hg:2 * hg].set(dt_bias_b[0].reshape(n_groups, hg))
    lam_params = jnp.stack([lambda_q1[0], lambda_k1[0], lambda_q2[0], lambda_k2[0]])

    xp = x_prompt[0]
    n_small = -(-(n_dec + N_META) // PROJ_TM) * PROJ_TM
    xs = jnp.concatenate([x_sample.reshape(n_dec, d_model), meta_tokens.astype(F32),
                          jnp.zeros((n_small - n_dec - N_META, d_model), F32)], axis=0)
    p32, p16, pba = _in_proj(xp, pre_norm, w_main, w_tail, w_a)
    s32, s16, sba = _in_proj(xs, pre_norm, w_main, w_tail, w_a)

    fb, mb, sa, se, lam_t = _bias_tables(rel_bias, lam_params, past, dec_seq, n_ca, lam_init)
    lam = lam_t[0, 0, 0:1]

    meta16 = s16[n_dec:n_dec + N_META]
    pad_m = jnp.zeros((LANES - N_META, w_a), BF16)
    km = jnp.concatenate([meta16[:, w_a:2 * w_a], pad_m], axis=0)
    vm = jnp.concatenate([meta16[:, 2 * w_a:], pad_m], axis=0)
    ya_p = _attn_prompt(lam, p16, p32, km, vm, fb, mb, subln_a, n_heads_a, lam_init)

    ck = cache_k_a[0].reshape(dec_b, n_cache, w_a)
    cv = cache_v_a[0].reshape(dec_b, n_cache, w_a)
    n_extra = n_cache - n_ca + dec_seq
    pad_e = jnp.zeros((dec_b, LANES - n_extra, w_a), BF16)
    ke = jnp.concatenate([ck[:, n_ca:].astype(BF16), s16[:n_dec, w_a:2 * w_a].reshape(dec_b, dec_seq, w_a),
                          pad_e], axis=1)
    ve = jnp.concatenate([cv[:, n_ca:].astype(BF16), s16[:n_dec, 2 * w_a:].reshape(dec_b, dec_seq, w_a),
                          pad_e], axis=1)
    ya_s = _attn_sample(lam, s16, s32, ck, cv, ke, ve, sa, se, subln_a, n_heads_a, dec_b, dec_seq, n_ca,
                        lam_init)

    col0 = 3 * w_a
    qkv_cols = slice(col0, col0 + 3 * w_b)
    zero_halo = jnp.zeros((1, 8, 3 * w_b), F32)
    gdn = functools.partial(_gdn, conv_w=conv_b[0], gate_par=gate_par, norm_b=norm_b, w_b=w_b, col0=col0)
    _, s_meta = gdn(s32, sba, s0=jnp.zeros((1, n_heads_b, HEAD_DIM_B, HEAD_DIM_B), F32), halo0=zero_halo,
                    n_seq=1, n_chunks=1, lr=N_META, row_blk_off=n_dec // N_META)
    meta_halo = jnp.concatenate([jnp.zeros((8 - (CONV_W - 1), 3 * w_b), F32),
                                 s32[n_dec + N_META - (CONV_W - 1):n_dec + N_META, qkv_cols]], axis=0)[None]
    yb_p, ssm_p = gdn(p32, pba, s0=s_meta, halo0=meta_halo, n_seq=1, n_chunks=seq // GDN_CHUNK, lr=GDN_CHUNK,
                      row_blk_off=0)
    samp_halo = jnp.concatenate([jnp.zeros((dec_b, 8 - (CONV_W - 1), 3 * w_b), F32), state_conv_b[0]], axis=1)
    yb_s, ssm_s = gdn(s32, sba, s0=state_ssm_b[0].astype(F32), halo0=samp_halo, n_seq=dec_b, n_chunks=1,
                      lr=dec_seq, row_blk_off=0)

    y_p = _out_proj(ya_p, yb_p, w_out_bf, xp, post_norm)
    y_s = _out_proj(ya_s[:n_dec], yb_s[:n_dec], w_out_bf, x_sample.reshape(n_dec, d_model), post_norm)

    hd = HEAD_DIM_A
    k_p = jnp.concatenate([s32[n_dec:n_dec + N_META, :w_a], p32[:, :w_a]], axis=0)
    v_p = jnp.concatenate([s32[n_dec:n_dec + N_META, w_a:2 * w_a], p32[:, w_a:2 * w_a]], axis=0)
    conv_p = p32[seq - (CONV_W - 1):, qkv_cols]
    k_s = s32[:n_dec, :w_a]
    v_s = s32[:n_dec, w_a:2 * w_a]
    conv_s = s32[:n_dec, qkv_cols].reshape(dec_b, dec_seq, 3 * w_b)[:, dec_seq - (CONV_W - 1):]
    return (
        y_p[None],
        y_s.reshape(dec_b, dec_seq, d_model),
        k_p.reshape(1, 1, N_META + seq, n_heads_a, 2, hd),
        v_p.reshape(1, 1, N_META + seq, n_heads_a, 2 * hd),
        ssm_p[None],
        conv_p[None, None],
        k_s.reshape(1, dec_b, dec_seq, n_heads_a, 2, hd),
        v_s.reshape(1, dec_b, dec_seq, n_heads_a, 2 * hd),
        ssm_s[None],
        conv_s[None],
    )
```

```python
import functools
import math

import jax
import jax.numpy as jnp
from jax import lax
from jax.experimental import pallas as pl
from jax.experimental.pallas import tpu as pltpu

F32 = jnp.float32
BF16 = jnp.bfloat16
HIGHEST = lax.Precision.HIGHEST

EPS = 1e-6
CHUNK = 64
N_META = 16
HEAD_DIM_A = 128
HEAD_DIM_B = 128
CONV_W = 4
N_BUCKETS = 32
MAX_DISTANCE = 1024
NEG = -1e30
LOG2E = math.log2(math.e)

LANES = 128
ATT_BLOCK = 256
GDN_CHUNK = 128
GDN_HEADS_PER_STEP = 8
PROJ_TM = 512
PROJ_TN = 1024
OUT_TM = 512
OUT_TN = 512
VMEM_LIMIT = 56 * 1024 * 1024


def _lambda_init(layer):
    return 0.8 - 0.6 * math.exp(-0.3 * layer)


def _bias_saturation_distance():
    nb = N_BUCKETS // 2
    max_exact = nb // 2
    return int(math.ceil(max_exact * (MAX_DISTANCE / max_exact) ** ((nb - max_exact - 1) / (nb - max_exact)))) + 1


N_NEAR = -(-(_bias_saturation_distance() + ATT_BLOCK) // ATT_BLOCK) + 1
FAR_BUCKET = N_BUCKETS // 2 - 1


def _silu(x):
    return x * jax.nn.sigmoid(x)


def _in_proj_kernel(x_ref, pn_ref, w_ref, wt_ref, o32_ref, o16_ref, ba_ref, xn_ref, *, q_tiles, bf_tiles,
                    q_scale):
    j = pl.program_id(1)
    tm = x_ref.shape[0]
    rows = 64

    @pl.when(j == 0)
    def _():
        def body(r, carry):
            sl = pl.ds(pl.multiple_of(r * rows, rows), rows)
            x = x_ref[sl, :]
            ms = jnp.mean(x * x, axis=-1, keepdims=True)
            xn_ref[sl, :] = (x * lax.rsqrt(ms + EPS) * pn_ref[...]).astype(BF16)
            return carry
        lax.fori_loop(0, tm // rows, body, 0)
        ba_ref[...] = jnp.dot(xn_ref[...], wt_ref[...], preferred_element_type=F32)

    r = jnp.dot(xn_ref[...], w_ref[...], preferred_element_type=F32)

    @pl.when(j >= q_tiles)
    def _():
        o32_ref[...] = r

    @pl.when(j < q_tiles)
    def _():
        o16_ref[...] = (r * q_scale).astype(BF16)

    @pl.when(jnp.logical_and(j >= q_tiles, j < bf_tiles))
    def _():
        o16_ref[...] = r.astype(BF16)


def _in_proj(x, pre_norm, w_all, w_tail, w_a, n):
    m, d = x.shape
    tm, tn = PROJ_TM, min(PROJ_TN, w_a)
    q_tiles = w_a // tn
    bf_tiles = 3 * w_a // tn
    nt = w_tail.shape[1]
    kern = functools.partial(_in_proj_kernel, q_tiles=q_tiles, bf_tiles=bf_tiles, q_scale=HEAD_DIM_A ** -0.5 * LOG2E)
    return pl.pallas_call(
        kern,
        grid=(m // tm, n // tn),
        in_specs=[
            pl.BlockSpec((tm, d), lambda i, j: (i, 0)),
            pl.BlockSpec((1, d), lambda i, j: (0, 0)),
            pl.BlockSpec((d, tn), lambda i, j: (0, j)),
            pl.BlockSpec((d, nt), lambda i, j: (0, 0)),
        ],
        out_specs=[
            pl.BlockSpec((tm, tn), lambda i, j: (i, jnp.maximum(j - q_tiles, 0))),
            pl.BlockSpec((tm, tn), lambda i, j: (i, jnp.minimum(j, bf_tiles - 1))),
            pl.BlockSpec((tm, nt), lambda i, j: (i, 0)),
        ],
        out_shape=[
            jax.ShapeDtypeStruct((m, n - w_a), F32),
            jax.ShapeDtypeStruct((m, 3 * w_a), BF16),
            jax.ShapeDtypeStruct((m, nt), F32),
        ],
        scratch_shapes=[pltpu.VMEM((tm, d), BF16)],
        compiler_params=pltpu.CompilerParams(
            dimension_semantics=("arbitrary", "arbitrary"), vmem_limit_bytes=VMEM_LIMIT),
        name="in_proj",
    )(x, pre_norm, w_all, w_tail)


def _bias_values(rb_ref, h, rel):
    nb = N_BUCKETS // 2
    max_exact = nb // 2
    n = jnp.abs(rel)
    nf = jnp.maximum(n, 1).astype(F32)
    large = max_exact + (jnp.log(nf / max_exact) / math.log(MAX_DISTANCE / max_exact)
                         * (nb - max_exact)).astype(jnp.int32)
    large = jnp.minimum(large, nb - 1)
    bucket = jnp.where(rel > 0, nb, 0) + jnp.where(n < max_exact, n, large)
    val = jnp.zeros(rel.shape, F32)
    for b in range(N_BUCKETS):
        val = jnp.where(bucket == b, rb_ref[b, h], val)
    return (val - rb_ref[FAR_BUCKET, h]) * LOG2E


def _bias_kernel(rb_ref, lamp_ref, fb_ref, mb_ref, sa_ref, se_ref, lam_ref, *, past, dec_seq, n_ca, lam_init):
    h = pl.program_id(0)
    t = ATT_BLOCK
    shift = CHUNK.bit_length() - 1
    i = lax.broadcasted_iota(jnp.int32, (t, t), 0)
    j = lax.broadcasted_iota(jnp.int32, (t, t), 1)
    for dd in range(N_NEAR):
        val = _bias_values(rb_ref, h, (j - i) - t * dd)
        if dd == 0:
            val = jnp.where((j >> shift) <= (i >> shift), val, NEG)
        fb_ref[0, dd] = val
    i = lax.broadcasted_iota(jnp.int32, (t, LANES), 0)
    j = lax.broadcasted_iota(jnp.int32, (t, LANES), 1)
    for qb in range(N_NEAR):
        val = _bias_values(rb_ref, h, (j - N_META) - (t * qb + i))
        mb_ref[0, qb] = jnp.where(j < N_META, val, NEG)
    i = lax.broadcasted_iota(jnp.int32, (dec_seq, n_ca), 0)
    j = lax.broadcasted_iota(jnp.int32, (dec_seq, n_ca), 1)
    sa_ref[0] = _bias_values(rb_ref, h, (j - N_META) - (past + i))
    i = lax.broadcasted_iota(jnp.int32, (dec_seq, LANES), 0)
    j = lax.broadcasted_iota(jnp.int32, (dec_seq, LANES), 1)
    n_extra = N_META + past - n_ca + dec_seq
    val = _bias_values(rb_ref, h, (n_ca - N_META + j) - (past + i))
    se_ref[0] = jnp.where(j < n_extra, val, NEG)
    lp = lamp_ref[...]
    s1 = jnp.sum(lp[0:1] * lp[1:2], axis=-1, keepdims=True)
    s2 = jnp.sum(lp[2:3] * lp[3:4], axis=-1, keepdims=True)
    lam_ref[0] = jnp.broadcast_to(jnp.exp(s1) - jnp.exp(s2) + lam_init, (8, LANES))


def _bias_tables(rel_bias, lam_params, past, dec_seq, n_ca, lam_init):
    nh = rel_bias.shape[1]
    t = ATT_BLOCK
    kern = functools.partial(_bias_kernel, past=past, dec_seq=dec_seq, n_ca=n_ca, lam_init=lam_init)
    return pl.pallas_call(
        kern,
        grid=(nh,),
        in_specs=[
            pl.BlockSpec(memory_space=pltpu.SMEM),
            pl.BlockSpec((4, HEAD_DIM_A), lambda h: (0, 0)),
        ],
        out_specs=[
            pl.BlockSpec((1, N_NEAR, t, t), lambda h: (h, 0, 0, 0)),
            pl.BlockSpec((1, N_NEAR, t, LANES), lambda h: (h, 0, 0, 0)),
            pl.BlockSpec((1, dec_seq, n_ca), lambda h: (h, 0, 0)),
            pl.BlockSpec((1, dec_seq, LANES), lambda h: (h, 0, 0)),
            pl.BlockSpec((1, 8, LANES), lambda h: (h, 0, 0)),
        ],
        out_shape=[
            jax.ShapeDtypeStruct((nh, N_NEAR, t, t), F32),
            jax.ShapeDtypeStruct((nh, N_NEAR, t, LANES), F32),
            jax.ShapeDtypeStruct((nh, dec_seq, n_ca), F32),
            jax.ShapeDtypeStruct((nh, dec_seq, LANES), F32),
            jax.ShapeDtypeStruct((nh, 8, LANES), F32),
        ],
        compiler_params=pltpu.CompilerParams(dimension_semantics=("arbitrary",)),
        name="bias_tables",
    )(rel_bias, lam_params)


def _nt_dot(a, b):
    return lax.dot_general(a, b, (((1,), (1,)), ((), ())), preferred_element_type=F32)


def _attn_finish(o, z, subln, lam_init):
    ms = jnp.mean(o * o, axis=-1, keepdims=True)
    on = o * lax.rsqrt(ms + EPS) * subln * (1.0 - lam_init)
    return on * _silu(z)


def _attn_prompt_kernel(lam_ref, q_ref, k_ref, v_ref, km_ref, vm_ref, fb_ref, mb_ref, z_ref, sub_ref, o_ref,
                        m_ref, l_ref, acc_ref, sa_ref, sb_ref, *, lam_init):
    qi = pl.program_id(1)
    t = ATT_BLOCK
    d = HEAD_DIM_A
    hw = 2 * d

    def rep(x, n):
        return jnp.concatenate([x] * n, axis=1) if n > 1 else x

    def lane_fold(p):
        out = p[:, 0:LANES]
        for g in range(1, p.shape[1] // LANES):
            out = out + p[:, g * LANES:(g + 1) * LANES]
        return out

    vm = vm_ref[...]
    mb = mb_ref[0, jnp.minimum(qi, N_NEAR - 1)]
    for c in range(2):
        s = _nt_dot(q_ref[:, c * d:(c + 1) * d], km_ref[:, c * d:(c + 1) * d]) + mb
        m = jnp.broadcast_to(jnp.max(s, axis=-1, keepdims=True), (t, LANES))
        p = jnp.exp2(s - m)
        m_ref[c] = m
        l_ref[c] = p
        acc_ref[c] = jnp.dot(p.astype(BF16), vm, preferred_element_type=F32)

    def qk(start, width):
        rows = pl.ds(start, width)
        return [_nt_dot(q_ref[:, c * d:(c + 1) * d], k_ref[rows, c * d:(c + 1) * d]) for c in range(2)]

    def softmax_pv(s_both, start, width, bias):
        v = v_ref[pl.ds(start, width), :]
        for c in range(2):
            s = s_both[c]
            if bias is not None:
                s = s + bias
            m_old = m_ref[c]
            m_new = jnp.maximum(m_old, jnp.max(s, axis=-1, keepdims=True))
            p = jnp.exp2(s - rep(m_new, width // LANES))
            alpha = jnp.exp2(m_old - m_new)
            m_ref[c] = m_new
            l_ref[c] = alpha * l_ref[c] + lane_fold(p)
            acc_ref[c] = rep(alpha, hw // LANES) * acc_ref[c] + jnp.dot(
                p.astype(BF16), v, preferred_element_type=F32)

    wide = 2 * t
    n_far = jnp.maximum(qi - (N_NEAR - 1), 0)
    n_wide = n_far >> 1

    def put(ref, s_both):
        ref[0] = s_both[0]
        ref[1] = s_both[1]

    @pl.when(n_wide > 0)
    def _():
        put(sa_ref, qk(0, wide))

    def far_body(i, carry):
        b0 = pl.multiple_of(i * (2 * wide), wide)
        b1 = pl.multiple_of(b0 + wide, wide)
        b2 = pl.multiple_of(jnp.minimum(b0 + 2 * wide, (n_wide - 1) * wide), wide)
        put(sb_ref, qk(b1, wide))
        softmax_pv([sa_ref[0], sa_ref[1]], b0, wide, None)
        put(sa_ref, qk(b2, wide))
        softmax_pv([sb_ref[0], sb_ref[1]], b1, wide, None)
        return carry
    lax.fori_loop(0, n_wide >> 1, far_body, 0)

    @pl.when((n_wide & 1) == 1)
    def _():
        softmax_pv([sa_ref[0], sa_ref[1]], pl.multiple_of((n_wide - 1) * wide, wide), wide, None)

    @pl.when((n_far & 1) == 1)
    def _():
        start = pl.multiple_of((n_far - 1) * t, t)
        softmax_pv(qk(start, t), start, t, None)

    @pl.when(qi >= N_NEAR - 1)
    def _():
        s_next = qk(pl.multiple_of((qi - (N_NEAR - 1)) * t, t), t)
        for dd in range(N_NEAR - 1, -1, -1):
            s_cur = s_next
            if dd > 0:
                s_next = qk(pl.multiple_of((qi - dd + 1) * t, t), t)
            softmax_pv(s_cur, pl.multiple_of((qi - dd) * t, t), t, fb_ref[0, dd])

    @pl.when(qi < N_NEAR - 1)
    def _():
        for dd in range(N_NEAR - 2, -1, -1):
            @pl.when(qi - dd >= 0)
            def _(dd=dd):
                start = pl.multiple_of((qi - dd) * t, t)
                softmax_pv(qk(start, t), start, t, fb_ref[0, dd])

    inv0 = 1.0 / jnp.sum(l_ref[0], axis=-1, keepdims=True)
    inv1 = lam_ref[0] / jnp.sum(l_ref[1], axis=-1, keepdims=True)
    o = acc_ref[0] * inv0 - acc_ref[1] * inv1
    o_ref[...] = _attn_finish(o, z_ref[...], sub_ref[...], lam_init).astype(BF16)


def _attn_prompt(lam, qkv16, o32, km, vm, fb, mb, subln, n_heads, lam_init):
    seq = qkv16.shape[0]
    t = ATT_BLOCK
    hw = 2 * HEAD_DIM_A
    kern = functools.partial(_attn_prompt_kernel, lam_init=lam_init)
    return pl.pallas_call(
        kern,
        grid=(n_heads, seq // t),
        in_specs=[
            pl.BlockSpec(memory_space=pltpu.SMEM),
            pl.BlockSpec((t, hw), lambda h, i: (i, h)),
            pl.BlockSpec((seq, hw), lambda h, i: (0, n_heads + h)),
            pl.BlockSpec((seq, hw), lambda h, i: (0, 2 * n_heads + h)),
            pl.BlockSpec((LANES, hw), lambda h, i: (0, h)),
            pl.BlockSpec((LANES, hw), lambda h, i: (0, h)),
            pl.BlockSpec((1, N_NEAR, t, t), lambda h, i: (h, 0, 0, 0)),
            pl.BlockSpec((1, N_NEAR, t, LANES), lambda h, i: (h, 0, 0, 0)),
            pl.BlockSpec((t, hw), lambda h, i: (i, 2 * n_heads + h)),
            pl.BlockSpec((1, hw), lambda h, i: (0, 0)),
        ],
        out_specs=pl.BlockSpec((t, hw), lambda h, i: (i, h)),
        out_shape=jax.ShapeDtypeStruct((seq, n_heads * hw), BF16),
        scratch_shapes=[
            pltpu.VMEM((2, t, LANES), F32),
            pltpu.VMEM((2, t, LANES), F32),
            pltpu.VMEM((2, t, hw), F32),
            pltpu.VMEM((2, t, 2 * t), F32),
            pltpu.VMEM((2, t, 2 * t), F32),
        ],
        compiler_params=pltpu.CompilerParams(
            dimension_semantics=("arbitrary", "arbitrary"), vmem_limit_bytes=VMEM_LIMIT),
        name="attn_prompt",
    )(lam, qkv16, qkv16, qkv16, km, vm, fb, mb, o32, subln)


def _attn_sample_kernel(lam_ref, q_ref, kc_ref, vc_ref, ke_ref, ve_ref, sa_ref, se_ref, z_ref, sub_ref, o_ref,
                        *, lam_init):
    d = HEAD_DIM_A
    vc = vc_ref[0].astype(BF16)
    ve = ve_ref[0]
    outs = []
    for c in range(2):
        q = q_ref[:, c * d:(c + 1) * d]
        s_a = _nt_dot(q, kc_ref[0, :, c * d:(c + 1) * d].astype(BF16)) + sa_ref[0]
        s_e = _nt_dot(q, ke_ref[0, :, c * d:(c + 1) * d]) + se_ref[0]
        m = jnp.maximum(jnp.max(s_a, axis=-1, keepdims=True), jnp.max(s_e, axis=-1, keepdims=True))
        p_a = jnp.exp2(s_a - m)
        p_e = jnp.exp2(s_e - m)
        l = jnp.sum(p_a, axis=-1, keepdims=True) + jnp.sum(p_e, axis=-1, keepdims=True)
        acc = (jnp.dot(p_a.astype(BF16), vc, preferred_element_type=F32)
               + jnp.dot(p_e.astype(BF16), ve, preferred_element_type=F32))
        outs.append(acc / l)
    o = outs[0] - lam_ref[0] * outs[1]
    o_ref[...] = _attn_finish(o, z_ref[...], sub_ref[...], lam_init).astype(BF16)


def _attn_sample(lam, qkv16, o32, cache_k, cache_v, ke, ve, sa, se, subln, n_heads, dec_b, dec_seq, n_ca,
                 lam_init):
    hw = 2 * HEAD_DIM_A
    kern = functools.partial(_attn_sample_kernel, lam_init=lam_init)
    return pl.pallas_call(
        kern,
        grid=(dec_b, n_heads),
        in_specs=[
            pl.BlockSpec(memory_space=pltpu.SMEM),
            pl.BlockSpec((dec_seq, hw), lambda b, h: (b, h)),
            pl.BlockSpec((1, n_ca, hw), lambda b, h: (b, 0, h)),
            pl.BlockSpec((1, n_ca, hw), lambda b, h: (b, 0, h)),
            pl.BlockSpec((1, LANES, hw), lambda b, h: (b, 0, h)),
            pl.BlockSpec((1, LANES, hw), lambda b, h: (b, 0, h)),
            pl.BlockSpec((1, dec_seq, n_ca), lambda b, h: (h, 0, 0)),
            pl.BlockSpec((1, dec_seq, LANES), lambda b, h: (h, 0, 0)),
            pl.BlockSpec((dec_seq, hw), lambda b, h: (b, 2 * n_heads + h)),
            pl.BlockSpec((1, hw), lambda b, h: (0, 0)),
        ],
        out_specs=pl.BlockSpec((dec_seq, hw), lambda b, h: (b, h)),
        out_shape=jax.ShapeDtypeStruct((dec_b * dec_seq, n_heads * hw), BF16),
        compiler_params=pltpu.CompilerParams(
            dimension_semantics=("arbitrary", "arbitrary"), vmem_limit_bytes=VMEM_LIMIT),
        name="attn_sample",
    )(lam, qkv16, cache_k, cache_v, ke, ve, sa, se, o32, subln)


def _hdot(a, b):
    return jnp.dot(a, b, preferred_element_type=F32, precision=HIGHEST)


def _bdot(a, b):
    return jnp.dot(a.astype(BF16), b.astype(BF16), preferred_element_type=F32)


def _inv_unit_lower(a_list, row, col):
    sh = 4
    n = a_list[0].shape[0]
    eye = (row == col).astype(F32)
    same = (row >> sh) == (col >> sh)
    dblk = [jnp.where(same, a, 0.0) for a in a_list]
    t = [eye - x for x in dblk]
    pw = [x.astype(BF16) for x in dblk]
    for _ in range(sh - 1):
        pw = [jnp.dot(x, x, preferred_element_type=F32).astype(BF16) for x in pw]
        t = [ti + jnp.dot(ti.astype(BF16), x, preferred_element_type=F32) for ti, x in zip(t, pw)]
    while (1 << sh) < n:
        offm = jnp.logical_and((row >> (sh + 1)) == (col >> (sh + 1)), (row >> sh) != (col >> sh))
        off = [jnp.where(offm, a, 0.0).astype(BF16) for a in a_list]
        tb = [ti.astype(BF16) for ti in t]
        mid = [jnp.dot(o, x, preferred_element_type=F32).astype(BF16) for o, x in zip(off, tb)]
        t = [ti - jnp.dot(x, m, preferred_element_type=F32) for ti, x, m in zip(t, tb, mid)]
        sh += 1
    return t


def _gdn_kernel(q_ref, k_ref, v_ref, z_ref, ba_ref, cwq_ref, cwk_ref, cwv_ref, gp_ref, nb_ref, s0_ref,
                hq_ref, hk_ref, hv_ref, y_ref, sout_ref, s_ref, xq_ref, xk_ref, xv_ref, *, lr, hg):
    c = pl.program_id(2)
    n = GDN_CHUNK
    dh = HEAD_DIM_B
    halo = 8

    @pl.when(c == 0)
    def _():
        s_ref[...] = s0_ref[0]
        xq_ref[0:halo, :] = hq_ref[0]
        xk_ref[0:halo, :] = hk_ref[0]
        xv_ref[0:halo, :] = hv_ref[0]

    row_w = lax.broadcasted_iota(jnp.int32, (n, hg * dh), 0)

    def conv(x_ref, xbuf, cw_ref):
        xbuf[halo:halo + lr, :] = x_ref[...]
        if lr < n:
            xbuf[halo + lr:halo + n, :] = jnp.zeros((n - lr, hg * dh), F32)
        y = xbuf[halo - 3:halo - 3 + n, :] * cw_ref[0:1, :]
        for i in range(1, CONV_W):
            y = y + xbuf[halo - 3 + i:halo - 3 + i + n, :] * cw_ref[i:i + 1, :]
        y = _silu(y)
        if lr < n:
            y = jnp.where(row_w < lr, y, 0.0)
        xbuf[0:halo, :] = xbuf[lr:lr + halo, :]
        return y

    yq = conv(q_ref, xq_ref, cwq_ref)
    yk = conv(k_ref, xk_ref, cwk_ref)
    yv = conv(v_ref, xv_ref, cwv_ref)

    row = lax.broadcasted_iota(jnp.int32, (n, n), 0)
    col = lax.broadcasted_iota(jnp.int32, (n, n), 1)
    ba = ba_ref[...]
    if lr < n:
        ba = jnp.concatenate([ba, jnp.zeros((n - lr, LANES), F32)], axis=0)
    live = row < lr
    beta = jnp.where(jnp.logical_and(live, col < hg), jax.nn.sigmoid(ba), 0.0)
    gval = -jnp.exp(gp_ref[0, 0:1, :]) * jax.nn.softplus(ba + gp_ref[0, 1:2, :])
    g = jnp.where(jnp.logical_and(live, jnp.logical_and(col >= hg, col < 2 * hg)), gval, 0.0)
    gsum = _hdot((row >= col).astype(F32), g)
    gsum_t = gsum.T
    incl = row >= col
    strict = row > col

    heads = range(hg)
    cols = [slice(hh * dh, (hh + 1) * dh) for hh in heads]
    gc = [jnp.broadcast_to(gsum[:, hg + hh:hg + hh + 1], (n, n)) for hh in heads]
    bc = [jnp.broadcast_to(beta[:, hh:hh + 1], (n, n)) for hh in heads]
    gam = [jnp.where(incl, jnp.exp(jnp.minimum(gc[hh] - gsum_t[hg + hh:hg + hh + 1, :], 0.0)), 0.0)
           for hh in heads]
    qh = [yq[:, cs] * (lax.rsqrt(jnp.sum(yq[:, cs] * yq[:, cs], axis=-1, keepdims=True) + EPS) * (dh ** -0.5))
          for cs in cols]
    kh = [yk[:, cs] * lax.rsqrt(jnp.sum(yk[:, cs] * yk[:, cs], axis=-1, keepdims=True) + EPS) for cs in cols]
    qb = [x.astype(BF16) for x in qh]
    kb = [x.astype(BF16) for x in kh]
    kk = [_nt_dot(x, x) for x in kb]
    qk = [_nt_dot(x, y) for x, y in zip(qb, kb)]
    a = [jnp.where(strict, bc[hh] * kk[hh] * gam[hh], 0.0) for hh in heads]
    t = _inv_unit_lower(a, row, col)
    eg = [jnp.exp(x) for x in gc]
    rhs = [jnp.concatenate([yv[:, cols[hh]] * bc[hh], kh[hh] * (bc[hh] * eg[hh])], axis=1).astype(BF16)
           for hh in heads]
    uw = [jnp.dot(t[hh].astype(BF16), rhs[hh], preferred_element_type=F32) for hh in heads]
    s = [s_ref[hh] for hh in heads]
    sb = [x.astype(BF16) for x in s]
    v_new = [uw[hh][:, :dh] - jnp.dot(uw[hh][:, dh:].astype(BF16), sb[hh], preferred_element_type=F32)
             for hh in heads]
    vnb = [x.astype(BF16) for x in v_new]
    o = [eg[hh] * jnp.dot(qb[hh], sb[hh], preferred_element_type=F32)
         + jnp.dot((qk[hh] * gam[hh]).astype(BF16), vnb[hh], preferred_element_type=F32) for hh in heads]
    g_last = [x[n - 1:n, :] for x in gc]
    kd = [(kh[hh] * jnp.exp(g_last[hh] - gc[hh])).astype(BF16) for hh in heads]
    for hh in heads:
        s_ref[hh] = s[hh] * jnp.exp(g_last[hh]) + lax.dot_general(
            kd[hh], vnb[hh], (((0,), (0,)), ((), ())), preferred_element_type=F32)
    for hh in heads:
        on = o[hh] * lax.rsqrt(jnp.mean(o[hh] * o[hh], axis=-1, keepdims=True) + EPS) * nb_ref[...]
        y_ref[:, cols[hh]] = (on[0:lr] * _silu(z_ref[:, cols[hh]])).astype(BF16)

    @pl.when(c == pl.num_programs(2) - 1)
    def _():
        sout_ref[0] = s_ref[...]


def _gdn(o32, ba, conv_w, gate_par, norm_b, s0, halo0, *, n_seq, n_chunks, lr, row_blk_off, w_b, col0):
    hg = min(GDN_HEADS_PER_STEP, w_b // HEAD_DIM_B)
    gw = hg * HEAD_DIM_B
    n_groups = w_b // gw
    n_heads = w_b // HEAD_DIM_B
    cb = col0 // gw
    wb = w_b // gw

    def rmap(off):
        return lambda s, g, c: (row_blk_off + s * n_chunks + c, off + g)

    kern = functools.partial(_gdn_kernel, lr=lr, hg=hg)
    return pl.pallas_call(
        kern,
        grid=(n_seq, n_groups, n_chunks),
        in_specs=[
            pl.BlockSpec((lr, gw), rmap(cb)),
            pl.BlockSpec((lr, gw), rmap(cb + wb)),
            pl.BlockSpec((lr, gw), rmap(cb + 2 * wb)),
            pl.BlockSpec((lr, gw), rmap(cb + 3 * wb)),
            pl.BlockSpec((lr, LANES), rmap(0)),
            pl.BlockSpec((CONV_W, gw), lambda s, g, c: (0, g)),
            pl.BlockSpec((CONV_W, gw), lambda s, g, c: (0, wb + g)),
            pl.BlockSpec((CONV_W, gw), lambda s, g, c: (0, 2 * wb + g)),
            pl.BlockSpec((1, 2, LANES), lambda s, g, c: (g, 0, 0)),
            pl.BlockSpec((1, HEAD_DIM_B), lambda s, g, c: (0, 0)),
            pl.BlockSpec((1, hg, HEAD_DIM_B, HEAD_DIM_B), lambda s, g, c: (s, g, 0, 0)),
            pl.BlockSpec((1, 8, gw), lambda s, g, c: (s, 0, g)),
            pl.BlockSpec((1, 8, gw), lambda s, g, c: (s, 0, wb + g)),
            pl.BlockSpec((1, 8, gw), lambda s, g, c: (s, 0, 2 * wb + g)),
        ],
        out_specs=[
            pl.BlockSpec((lr, gw), lambda s, g, c: (s * n_chunks + c, g)),
            pl.BlockSpec((1, hg, HEAD_DIM_B, HEAD_DIM_B), lambda s, g, c: (s, g, 0, 0)),
        ],
        out_shape=[
            jax.ShapeDtypeStruct((n_seq * n_chunks * lr, w_b), BF16),
            jax.ShapeDtypeStruct((n_seq, n_heads, HEAD_DIM_B, HEAD_DIM_B), F32),
        ],
        scratch_shapes=[
            pltpu.VMEM((hg, HEAD_DIM_B, HEAD_DIM_B), F32),
            pltpu.VMEM((8 + GDN_CHUNK, gw), F32),
            pltpu.VMEM((8 + GDN_CHUNK, gw), F32),
            pltpu.VMEM((8 + GDN_CHUNK, gw), F32),
        ],
        compiler_params=pltpu.CompilerParams(
            dimension_semantics=("arbitrary", "arbitrary", "arbitrary"), vmem_limit_bytes=VMEM_LIMIT),
        name=f"gdn_l{lr}",
    )(o32, o32, o32, o32, ba, conv_w, conv_w, conv_w, gate_par, norm_b, s0, halo0, halo0, halo0)


def _out_proj_kernel(ya_ref, yb_ref, w_ref, h_ref, pn_ref, o_ref, *, w_a):
    j = pl.program_id(1)
    tn = w_ref.shape[1]
    r = (jnp.dot(ya_ref[...], w_ref[0:w_a, :], preferred_element_type=F32)
         + jnp.dot(yb_ref[...], w_ref[w_a:, :], preferred_element_type=F32))
    o_ref[:, pl.ds(pl.multiple_of(j * tn, tn), tn)] = r

    @pl.when(j == pl.num_programs(1) - 1)
    def _():
        rows = 64

        def body(i, carry):
            sl = pl.ds(pl.multiple_of(i * rows, rows), rows)
            y = o_ref[sl, :]
            ms = jnp.mean(y * y, axis=-1, keepdims=True)
            o_ref[sl, :] = h_ref[sl, :] + y * lax.rsqrt(ms + EPS) * pn_ref[...]
            return carry
        lax.fori_loop(0, o_ref.shape[0] // rows, body, 0)


def _out_proj(ya, yb, w_out, h, post_norm):
    m, d = h.shape
    w_a = ya.shape[1]
    tm, tn = min(OUT_TM, m), OUT_TN
    kern = functools.partial(_out_proj_kernel, w_a=w_a)
    return pl.pallas_call(
        kern,
        grid=(m // tm, d // tn),
        in_specs=[
            pl.BlockSpec((tm, w_a), lambda i, j: (i, 0)),
            pl.BlockSpec((tm, yb.shape[1]), lambda i, j: (i, 0)),
            pl.BlockSpec((d, tn), lambda i, j: (0, j)),
            pl.BlockSpec((tm, d), lambda i, j: (i, 0)),
            pl.BlockSpec((1, d), lambda i, j: (0, 0)),
        ],
        out_specs=pl.BlockSpec((tm, d), lambda i, j: (i, 0)),
        out_shape=jax.ShapeDtypeStruct((m, d), F32),
        compiler_params=pltpu.CompilerParams(
            dimension_semantics=("arbitrary", "arbitrary"), vmem_limit_bytes=VMEM_LIMIT),
        name="out_proj",
    )(ya, yb, w_out, h, post_norm)


def kernel(x_prompt, x_sample, cache_k_a, cache_v_a, state_ssm_b, state_conv_b, meta_tokens, rel_bias, pre_norm,
           w_in, lambda_q1, lambda_k1, lambda_q2, lambda_k2, subln_a, conv_b, a_log_b, dt_bias_b, norm_b, w_out,
           post_norm):
    batch, seq, d_model = x_prompt.shape
    dec_b, dec_seq, _ = x_sample.shape
    depth = w_in.shape[0]
    assert batch == 1 and depth == 1
    n_heads_a = rel_bias.shape[1]
    w_a = n_heads_a * 2 * HEAD_DIM_A
    w_b = d_model - w_a
    n_heads_b = w_b // HEAD_DIM_B
    n_cache = cache_k_a.shape[2]
    past = n_cache - N_META
    n_ca = (n_cache // LANES) * LANES
    n_main = 4 * w_a + 4 * w_b
    lam_init = _lambda_init(0)
    hg = min(GDN_HEADS_PER_STEP, n_heads_b)
    n_groups = n_heads_b // hg
    n_dec = dec_b * dec_seq
    assert seq % PROJ_TM == 0 and seq % GDN_CHUNK == 0 and n_cache - n_ca + dec_seq <= LANES

    w_in_bf = w_in[0].astype(BF16)
    wb = w_in_bf[:, n_main:n_main + n_heads_b].reshape(d_model, n_groups, hg)
    wa = w_in_bf[:, n_main + n_heads_b:].reshape(d_model, n_groups, hg)
    w_tail = jnp.concatenate([wb, wa, jnp.zeros((d_model, n_groups, LANES - 2 * hg), BF16)], axis=-1)
    w_tail = w_tail.reshape(d_model, n_groups * LANES)
    w_out_bf = w_out[0].astype(BF16)
    gate_par = jnp.zeros((n_groups, 2, LANES), F32)
    gate_par = gate_par.at[:, 0, hg:2 * hg].set(a_log_b[0].reshape(n_groups, hg))
    gate_par = gate_par.at[:, 1, hg:2 * hg].set(dt_bias_b[0].reshape(n_groups, hg))
    lam_params = jnp.stack([lambda_q1[0], lambda_k1[0], lambda_q2[0], lambda_k2[0]])

    xp = x_prompt[0]
    n_small = -(-(n_dec + N_META) // PROJ_TM) * PROJ_TM
    xs = jnp.concatenate([x_sample.reshape(n_dec, d_model), meta_tokens.astype(F32),
                          jnp.zeros((n_small - n_dec - N_META, d_model), F32)], axis=0)
    p32, p16, pba = _in_proj(xp, pre_norm, w_in_bf, w_tail, w_a, n_main)
    s32, s16, sba = _in_proj(xs, pre_norm, w_in_bf, w_tail, w_a, n_main)

    fb, mb, sa, se, lam_t = _bias_tables(rel_bias, lam_params, past, dec_seq, n_ca, lam_init)
    lam = lam_t[0, 0, 0:1]

    meta16 = s16[n_dec:n_dec + N_META]
    pad_m = jnp.zeros((LANES - N_META, w_a), BF16)
    km = jnp.concatenate([meta16[:, w_a:2 * w_a], pad_m], axis=0)
    vm = jnp.concatenate([meta16[:, 2 * w_a:], pad_m], axis=0)
    ya_p = _attn_prompt(lam, p16, p32, km, vm, fb, mb, subln_a, n_heads_a, lam_init)

    ck = cache_k_a[0].reshape(dec_b, n_cache, w_a)
    cv = cache_v_a[0].reshape(dec_b, n_cache, w_a)
    n_extra = n_cache - n_ca + dec_seq
    pad_e = jnp.zeros((dec_b, LANES - n_extra, w_a), BF16)
    ke = jnp.concatenate([ck[:, n_ca:].astype(BF16), s16[:n_dec, w_a:2 * w_a].reshape(dec_b, dec_seq, w_a),
                          pad_e], axis=1)
    ve = jnp.concatenate([cv[:, n_ca:].astype(BF16), s16[:n_dec, 2 * w_a:].reshape(dec_b, dec_seq, w_a),
                          pad_e], axis=1)
    ya_s = _attn_sample(lam, s16, s32, ck, cv, ke, ve, sa, se, subln_a, n_heads_a, dec_b, dec_seq, n_ca,
                        lam_init)

    col0 = 3 * w_a
    qkv_cols = slice(col0, col0 + 3 * w_b)
    zero_halo = jnp.zeros((1, 8, 3 * w_b), F32)
    gdn = functools.partial(_gdn, conv_w=conv_b[0], gate_par=gate_par, norm_b=norm_b, w_b=w_b, col0=col0)
    _, s_meta = gdn(s32, sba, s0=jnp.zeros((1, n_heads_b, HEAD_DIM_B, HEAD_DIM_B), F32), halo0=zero_halo,
                    n_seq=1, n_chunks=1, lr=N_META, row_blk_off=n_dec // N_META)
    meta_halo = jnp.concatenate([jnp.zeros((8 - (CONV_W - 1), 3 * w_b), F32),
                                 s32[n_dec + N_META - (CONV_W - 1):n_dec + N_META, qkv_cols]], axis=0)[None]
    yb_p, ssm_p = gdn(p32, pba, s0=s_meta, halo0=meta_halo, n_seq=1, n_chunks=seq // GDN_CHUNK, lr=GDN_CHUNK,
                      row_blk_off=0)
    samp_halo = jnp.concatenate([jnp.zeros((dec_b, 8 - (CONV_W - 1), 3 * w_b), F32), state_conv_b[0]], axis=1)
    yb_s, ssm_s = gdn(s32, sba, s0=state_ssm_b[0].astype(F32), halo0=samp_halo, n_seq=dec_b, n_chunks=1,
                      lr=dec_seq, row_blk_off=0)

    y_p = _out_proj(ya_p, yb_p, w_out_bf, xp, post_norm)
    y_s = _out_proj(ya_s, yb_s, w_out_bf, x_sample.reshape(n_dec, d_model), post_norm)

    hd = HEAD_DIM_A
    k_p = jnp.concatenate([s32[n_dec:n_dec + N_META, :w_a], p32[:, :w_a]], axis=0)
    v_p = jnp.concatenate([s32[n_dec:n_dec + N_META, w_a:2 * w_a], p32[:, w_a:2 * w_a]], axis=0)
    conv_p = p32[seq - (CONV_W - 1):, qkv_cols]
    k_s = s32[:n_dec, :w_a]
    v_s = s32[:n_dec, w_a:2 * w_a]
    conv_s = s32[:n_dec, qkv_cols].reshape(dec_b, dec_seq, 3 * w_b)[:, dec_seq - (CONV_W - 1):]
    return (
        y_p[None],
        y_s.reshape(dec_b, dec_seq, d_model),
        k_p.reshape(1, 1, N_META + seq, n_heads_a, 2, hd),
        v_p.reshape(1, 1, N_META + seq, n_heads_a, 2 * hd),
        ssm_p[None],
        conv_p[None, None],
        k_s.reshape(1, dec_b, dec_seq, n_heads_a, 2, hd),
        v_s.reshape(1, dec_b, dec_seq, n_heads_a, 2 * hd),
        ssm_s[None],
        conv_s[None],
    )
```

```python
import functools
import math

import jax
import jax.numpy as jnp
from jax import lax
from jax.experimental import pallas as pl
from jax.experimental.pallas import tpu as pltpu

F32 = jnp.float32
BF16 = jnp.bfloat16
HIGHEST = lax.Precision.HIGHEST

EPS = 1e-6
CHUNK = 64
N_META = 16
HEAD_DIM_A = 128
HEAD_DIM_B = 128
CONV_W = 4
N_BUCKETS = 32
MAX_DISTANCE = 1024
NEG = -1e30
LOG2E = math.log2(math.e)

LANES = 128
ATT_BLOCK = 256
GDN_CHUNK = 128
GDN_HEADS_PER_STEP = 8
PROJ_TM = 512
PROJ_TN = 1024
OUT_TM = 512
OUT_TN = 512
VMEM_LIMIT = 56 * 1024 * 1024


def _lambda_init(layer):
    return 0.8 - 0.6 * math.exp(-0.3 * layer)


def _bias_saturation_distance():
    nb = N_BUCKETS // 2
    max_exact = nb // 2
    return int(math.ceil(max_exact * (MAX_DISTANCE / max_exact) ** ((nb - max_exact - 1) / (nb - max_exact)))) + 1


N_NEAR = -(-(_bias_saturation_distance() + ATT_BLOCK) // ATT_BLOCK) + 1
FAR_BUCKET = N_BUCKETS // 2 - 1


def _silu(x):
    return x * jax.nn.sigmoid(x)


def _in_proj_kernel(x_ref, pn_ref, w_ref, wt_ref, o32_ref, o16_ref, ba_ref, xn_ref, *, q_tiles, bf_tiles,
                    q_scale):
    j = pl.program_id(1)
    tm = x_ref.shape[0]
    rows = 64

    @pl.when(j == 0)
    def _():
        def body(r, carry):
            sl = pl.ds(pl.multiple_of(r * rows, rows), rows)
            x = x_ref[sl, :]
            ms = jnp.mean(x * x, axis=-1, keepdims=True)
            xn_ref[sl, :] = (x * lax.rsqrt(ms + EPS) * pn_ref[...]).astype(BF16)
            return carry
        lax.fori_loop(0, tm // rows, body, 0)
        ba_ref[...] = jnp.dot(xn_ref[...], wt_ref[...], preferred_element_type=F32)

    r = jnp.dot(xn_ref[...], w_ref[...], preferred_element_type=F32)

    @pl.when(j >= q_tiles)
    def _():
        o32_ref[...] = r

    @pl.when(j < q_tiles)
    def _():
        o16_ref[...] = (r * q_scale).astype(BF16)

    @pl.when(jnp.logical_and(j >= q_tiles, j < bf_tiles))
    def _():
        o16_ref[...] = r.astype(BF16)


def _in_proj(x, pre_norm, w_all, w_tail, w_a, n):
    m, d = x.shape
    tm, tn = PROJ_TM, min(PROJ_TN, w_a)
    q_tiles = w_a // tn
    bf_tiles = 3 * w_a // tn
    nt = w_tail.shape[1]
    kern = functools.partial(_in_proj_kernel, q_tiles=q_tiles, bf_tiles=bf_tiles, q_scale=HEAD_DIM_A ** -0.5 * LOG2E)
    return pl.pallas_call(
        kern,
        grid=(m // tm, n // tn),
        in_specs=[
            pl.BlockSpec((tm, d), lambda i, j: (i, 0)),
            pl.BlockSpec((1, d), lambda i, j: (0, 0)),
            pl.BlockSpec((d, tn), lambda i, j: (0, j)),
            pl.BlockSpec((d, nt), lambda i, j: (0, 0)),
        ],
        out_specs=[
            pl.BlockSpec((tm, tn), lambda i, j: (i, jnp.maximum(j - q_tiles, 0))),
            pl.BlockSpec((tm, tn), lambda i, j: (i, jnp.minimum(j, bf_tiles - 1))),
            pl.BlockSpec((tm, nt), lambda i, j: (i, 0)),
        ],
        out_shape=[
            jax.ShapeDtypeStruct((m, n - w_a), F32),
            jax.ShapeDtypeStruct((m, 3 * w_a), BF16),
            jax.ShapeDtypeStruct((m, nt), F32),
        ],
        scratch_shapes=[pltpu.VMEM((tm, d), BF16)],
        compiler_params=pltpu.CompilerParams(
            dimension_semantics=("arbitrary", "arbitrary"), vmem_limit_bytes=VMEM_LIMIT),
        name="in_proj",
    )(x, pre_norm, w_all, w_tail)


def _bias_values(rb_ref, h, rel):
    nb = N_BUCKETS // 2
    max_exact = nb // 2
    n = jnp.abs(rel)
    nf = jnp.maximum(n, 1).astype(F32)
    large = max_exact + (jnp.log(nf / max_exact) / math.log(MAX_DISTANCE / max_exact)
                         * (nb - max_exact)).astype(jnp.int32)
    large = jnp.minimum(large, nb - 1)
    bucket = jnp.where(rel > 0, nb, 0) + jnp.where(n < max_exact, n, large)
    val = jnp.zeros(rel.shape, F32)
    for b in range(N_BUCKETS):
        val = jnp.where(bucket == b, rb_ref[b, h], val)
    return (val - rb_ref[FAR_BUCKET, h]) * LOG2E


def _bias_kernel(rb_ref, lamp_ref, fb_ref, mb_ref, sa_ref, se_ref, lam_ref, *, past, dec_seq, n_ca, lam_init):
    h = pl.program_id(0)
    t = ATT_BLOCK
    shift = CHUNK.bit_length() - 1
    i = lax.broadcasted_iota(jnp.int32, (t, t), 0)
    j = lax.broadcasted_iota(jnp.int32, (t, t), 1)
    for dd in range(N_NEAR):
        val = _bias_values(rb_ref, h, (j - i) - t * dd)
        if dd == 0:
            val = jnp.where((j >> shift) <= (i >> shift), val, NEG)
        fb_ref[0, dd] = val
    i = lax.broadcasted_iota(jnp.int32, (t, LANES), 0)
    j = lax.broadcasted_iota(jnp.int32, (t, LANES), 1)
    for qb in range(N_NEAR):
        val = _bias_values(rb_ref, h, (j - N_META) - (t * qb + i))
        mb_ref[0, qb] = jnp.where(j < N_META, val, NEG)
    i = lax.broadcasted_iota(jnp.int32, (dec_seq, n_ca), 0)
    j = lax.broadcasted_iota(jnp.int32, (dec_seq, n_ca), 1)
    sa_ref[0] = _bias_values(rb_ref, h, (j - N_META) - (past + i))
    i = lax.broadcasted_iota(jnp.int32, (dec_seq, LANES), 0)
    j = lax.broadcasted_iota(jnp.int32, (dec_seq, LANES), 1)
    n_extra = N_META + past - n_ca + dec_seq
    val = _bias_values(rb_ref, h, (n_ca - N_META + j) - (past + i))
    se_ref[0] = jnp.where(j < n_extra, val, NEG)
    lp = lamp_ref[...]
    s1 = jnp.sum(lp[0:1] * lp[1:2], axis=-1, keepdims=True)
    s2 = jnp.sum(lp[2:3] * lp[3:4], axis=-1, keepdims=True)
    lam_ref[0] = jnp.broadcast_to(jnp.exp(s1) - jnp.exp(s2) + lam_init, (8, LANES))


def _bias_tables(rel_bias, lam_params, past, dec_seq, n_ca, lam_init):
    nh = rel_bias.shape[1]
    t = ATT_BLOCK
    kern = functools.partial(_bias_kernel, past=past, dec_seq=dec_seq, n_ca=n_ca, lam_init=lam_init)
    return pl.pallas_call(
        kern,
        grid=(nh,),
        in_specs=[
            pl.BlockSpec(memory_space=pltpu.SMEM),
            pl.BlockSpec((4, HEAD_DIM_A), lambda h: (0, 0)),
        ],
        out_specs=[
            pl.BlockSpec((1, N_NEAR, t, t), lambda h: (h, 0, 0, 0)),
            pl.BlockSpec((1, N_NEAR, t, LANES), lambda h: (h, 0, 0, 0)),
            pl.BlockSpec((1, dec_seq, n_ca), lambda h: (h, 0, 0)),
            pl.BlockSpec((1, dec_seq, LANES), lambda h: (h, 0, 0)),
            pl.BlockSpec((1, 8, LANES), lambda h: (h, 0, 0)),
        ],
        out_shape=[
            jax.ShapeDtypeStruct((nh, N_NEAR, t, t), F32),
            jax.ShapeDtypeStruct((nh, N_NEAR, t, LANES), F32),
            jax.ShapeDtypeStruct((nh, dec_seq, n_ca), F32),
            jax.ShapeDtypeStruct((nh, dec_seq, LANES), F32),
            jax.ShapeDtypeStruct((nh, 8, LANES), F32),
        ],
        compiler_params=pltpu.CompilerParams(dimension_semantics=("arbitrary",)),
        name="bias_tables",
    )(rel_bias, lam_params)


def _nt_dot(a, b):
    return lax.dot_general(a, b, (((1,), (1,)), ((), ())), preferred_element_type=F32)


def _attn_finish(o, z, subln, lam_init):
    ms = jnp.mean(o * o, axis=-1, keepdims=True)
    on = o * lax.rsqrt(ms + EPS) * subln * (1.0 - lam_init)
    return on * _silu(z)


def _attn_prompt_kernel(lam_ref, q_ref, k_ref, v_ref, km_ref, vm_ref, fb_ref, mb_ref, z_ref, sub_ref, o_ref,
                        m_ref, l_ref, acc_ref, sa_ref, sb_ref, *, lam_init):
    qi = pl.program_id(1)
    t = ATT_BLOCK
    d = HEAD_DIM_A
    hw = 2 * d

    def rep(x, n):
        return jnp.concatenate([x] * n, axis=1) if n > 1 else x

    def lane_fold(p):
        out = p[:, 0:LANES]
        for g in range(1, p.shape[1] // LANES):
            out = out + p[:, g * LANES:(g + 1) * LANES]
        return out

    vm = vm_ref[...]
    mb = mb_ref[0, jnp.minimum(qi, N_NEAR - 1)]
    for c in range(2):
        s = _nt_dot(q_ref[:, c * d:(c + 1) * d], km_ref[:, c * d:(c + 1) * d]) + mb
        m = jnp.broadcast_to(jnp.max(s, axis=-1, keepdims=True), (t, LANES))
        p = jnp.exp2(s - m)
        m_ref[c] = m
        l_ref[c] = p
        acc_ref[c] = jnp.dot(p.astype(BF16), vm, preferred_element_type=F32)

    def qk(start, width):
        rows = pl.ds(start, width)
        return [_nt_dot(q_ref[:, c * d:(c + 1) * d], k_ref[rows, c * d:(c + 1) * d]) for c in range(2)]

    def softmax_pv(s_both, start, width, bias):
        v = v_ref[pl.ds(start, width), :]
        for c in range(2):
            s = s_both[c]
            if bias is not None:
                s = s + bias
            m_old = m_ref[c]
            m_new = jnp.maximum(m_old, jnp.max(s, axis=-1, keepdims=True))
            p = jnp.exp2(s - rep(m_new, width // LANES))
            alpha = jnp.exp2(m_old - m_new)
            m_ref[c] = m_new
            l_ref[c] = alpha * l_ref[c] + lane_fold(p)
            acc_ref[c] = rep(alpha, hw // LANES) * acc_ref[c] + jnp.dot(
                p.astype(BF16), v, preferred_element_type=F32)

    wide = 2 * t
    n_far = jnp.maximum(qi - (N_NEAR - 1), 0)
    n_wide = n_far >> 1

    def put(ref, s_both):
        ref[0] = s_both[0]
        ref[1] = s_both[1]

    @pl.when(n_wide > 0)
    def _():
        put(sa_ref, qk(0, wide))

    def far_body(i, carry):
        b0 = pl.multiple_of(i * (2 * wide), wide)
        b1 = pl.multiple_of(b0 + wide, wide)
        b2 = pl.multiple_of(jnp.minimum(b0 + 2 * wide, (n_wide - 1) * wide), wide)
        put(sb_ref, qk(b1, wide))
        softmax_pv([sa_ref[0], sa_ref[1]], b0, wide, None)
        put(sa_ref, qk(b2, wide))
        softmax_pv([sb_ref[0], sb_ref[1]], b1, wide, None)
        return carry

    def far_body2(i, carry):
        far_body(2 * i, carry)
        return far_body(2 * i + 1, carry)
    lax.fori_loop(0, n_wide >> 2, far_body2, 0)
    lax.fori_loop((n_wide >> 2) * 2, n_wide >> 1, far_body, 0)

    @pl.when((n_wide & 1) == 1)
    def _():
        softmax_pv([sa_ref[0], sa_ref[1]], pl.multiple_of((n_wide - 1) * wide, wide), wide, None)

    @pl.when((n_far & 1) == 1)
    def _():
        start = pl.multiple_of((n_far - 1) * t, t)
        softmax_pv(qk(start, t), start, t, None)

    @pl.when(qi >= N_NEAR - 1)
    def _():
        s_next = qk(pl.multiple_of((qi - (N_NEAR - 1)) * t, t), t)
        for dd in range(N_NEAR - 1, -1, -1):
            s_cur = s_next
            if dd > 0:
                s_next = qk(pl.multiple_of((qi - dd + 1) * t, t), t)
            softmax_pv(s_cur, pl.multiple_of((qi - dd) * t, t), t, fb_ref[0, dd])

    @pl.when(qi < N_NEAR - 1)
    def _():
        for dd in range(N_NEAR - 2, -1, -1):
            @pl.when(qi - dd >= 0)
            def _(dd=dd):
                start = pl.multiple_of((qi - dd) * t, t)
                softmax_pv(qk(start, t), start, t, fb_ref[0, dd])

    inv0 = 1.0 / jnp.sum(l_ref[0], axis=-1, keepdims=True)
    inv1 = lam_ref[0] / jnp.sum(l_ref[1], axis=-1, keepdims=True)
    o = acc_ref[0] * inv0 - acc_ref[1] * inv1
    o_ref[...] = _attn_finish(o, z_ref[...], sub_ref[...], lam_init).astype(BF16)


def _attn_prompt(lam, qkv16, o32, km, vm, fb, mb, subln, n_heads, lam_init):
    seq = qkv16.shape[0]
    t = ATT_BLOCK
    hw = 2 * HEAD_DIM_A
    kern = functools.partial(_attn_prompt_kernel, lam_init=lam_init)
    return pl.pallas_call(
        kern,
        grid=(n_heads, seq // t),
        in_specs=[
            pl.BlockSpec(memory_space=pltpu.SMEM),
            pl.BlockSpec((t, hw), lambda h, i: (i, h)),
            pl.BlockSpec((seq, hw), lambda h, i: (0, n_heads + h)),
            pl.BlockSpec((seq, hw), lambda h, i: (0, 2 * n_heads + h)),
            pl.BlockSpec((LANES, hw), lambda h, i: (0, h)),
            pl.BlockSpec((LANES, hw), lambda h, i: (0, h)),
            pl.BlockSpec((1, N_NEAR, t, t), lambda h, i: (h, 0, 0, 0)),
            pl.BlockSpec((1, N_NEAR, t, LANES), lambda h, i: (h, 0, 0, 0)),
            pl.BlockSpec((t, hw), lambda h, i: (i, 2 * n_heads + h)),
            pl.BlockSpec((1, hw), lambda h, i: (0, 0)),
        ],
        out_specs=pl.BlockSpec((t, hw), lambda h, i: (i, h)),
        out_shape=jax.ShapeDtypeStruct((seq, n_heads * hw), BF16),
        scratch_shapes=[
            pltpu.VMEM((2, t, LANES), F32),
            pltpu.VMEM((2, t, LANES), F32),
            pltpu.VMEM((2, t, hw), F32),
            pltpu.VMEM((2, t, 2 * t), F32),
            pltpu.VMEM((2, t, 2 * t), F32),
        ],
        compiler_params=pltpu.CompilerParams(
            dimension_semantics=("arbitrary", "arbitrary"), vmem_limit_bytes=VMEM_LIMIT),
        name="attn_prompt",
    )(lam, qkv16, qkv16, qkv16, km, vm, fb, mb, o32, subln)


def _attn_sample_kernel(lam_ref, q_ref, kc_ref, vc_ref, ke_ref, ve_ref, sa_ref, se_ref, z_ref, sub_ref, o_ref,
                        *, lam_init, n_heads, n_ca):
    h = pl.program_id(1)
    d = HEAD_DIM_A
    vc = jnp.concatenate([vc_ref[0, pl.ds(c * n_heads + h, n_ca, stride=2 * n_heads), :] for c in range(2)],
                         axis=1).astype(BF16)
    ve = ve_ref[0]
    outs = []
    for c in range(2):
        q = q_ref[:, c * d:(c + 1) * d]
        kc = kc_ref[0, pl.ds(2 * h + c, n_ca, stride=2 * n_heads), :].astype(BF16)
        s_a = _nt_dot(q, kc) + sa_ref[0]
        s_e = _nt_dot(q, ke_ref[0, :, c * d:(c + 1) * d]) + se_ref[0]
        m = jnp.maximum(jnp.max(s_a, axis=-1, keepdims=True), jnp.max(s_e, axis=-1, keepdims=True))
        p_a = jnp.exp2(s_a - m)
        p_e = jnp.exp2(s_e - m)
        l = jnp.sum(p_a, axis=-1, keepdims=True) + jnp.sum(p_e, axis=-1, keepdims=True)
        acc = (jnp.dot(p_a.astype(BF16), vc, preferred_element_type=F32)
               + jnp.dot(p_e.astype(BF16), ve, preferred_element_type=F32))
        outs.append(acc / l)
    o = outs[0] - lam_ref[0] * outs[1]
    o_ref[...] = _attn_finish(o, z_ref[...], sub_ref[...], lam_init).astype(BF16)


def _attn_sample(lam, qkv16, o32, cache_k, cache_v, ke, ve, sa, se, subln, n_heads, dec_b, dec_seq, n_ca,
                 lam_init):
    hw = 2 * HEAD_DIM_A
    kern = functools.partial(_attn_sample_kernel, lam_init=lam_init, n_heads=n_heads, n_ca=n_ca)
    return pl.pallas_call(
        kern,
        grid=(dec_b, n_heads),
        in_specs=[
            pl.BlockSpec(memory_space=pltpu.SMEM),
            pl.BlockSpec((dec_seq, hw), lambda b, h: (b, h)),
            pl.BlockSpec((1, n_ca * n_heads * 2, HEAD_DIM_A), lambda b, h: (b, 0, 0)),
            pl.BlockSpec((1, n_ca * n_heads * 2, HEAD_DIM_A), lambda b, h: (b, 0, 0)),
            pl.BlockSpec((1, LANES, hw), lambda b, h: (b, 0, h)),
            pl.BlockSpec((1, LANES, hw), lambda b, h: (b, 0, h)),
            pl.BlockSpec((1, dec_seq, n_ca), lambda b, h: (h, 0, 0)),
            pl.BlockSpec((1, dec_seq, LANES), lambda b, h: (h, 0, 0)),
            pl.BlockSpec((dec_seq, hw), lambda b, h: (b, 2 * n_heads + h)),
            pl.BlockSpec((1, hw), lambda b, h: (0, 0)),
        ],
        out_specs=pl.BlockSpec((dec_seq, hw), lambda b, h: (b, h)),
        out_shape=jax.ShapeDtypeStruct((dec_b * dec_seq, n_heads * hw), BF16),
        compiler_params=pltpu.CompilerParams(
            dimension_semantics=("arbitrary", "arbitrary"), vmem_limit_bytes=VMEM_LIMIT),
        name="attn_sample",
    )(lam, qkv16, cache_k, cache_v, ke, ve, sa, se, o32, subln)


def _hdot(a, b):
    return jnp.dot(a, b, preferred_element_type=F32, precision=HIGHEST)


def _bdot(a, b):
    return jnp.dot(a.astype(BF16), b.astype(BF16), preferred_element_type=F32)


def _inv_unit_lower(a_list, row, col):
    sh = 4
    n = a_list[0].shape[0]
    eye = (row == col).astype(F32)
    same = (row >> sh) == (col >> sh)
    dblk = [jnp.where(same, a, 0.0) for a in a_list]
    t = [eye - x for x in dblk]
    pw = [x.astype(BF16) for x in dblk]
    for _ in range(sh - 1):
        pw = [jnp.dot(x, x, preferred_element_type=F32).astype(BF16) for x in pw]
        t = [ti + jnp.dot(ti.astype(BF16), x, preferred_element_type=F32) for ti, x in zip(t, pw)]
    while (1 << sh) < n:
        offm = jnp.logical_and((row >> (sh + 1)) == (col >> (sh + 1)), (row >> sh) != (col >> sh))
        off = [jnp.where(offm, a, 0.0).astype(BF16) for a in a_list]
        tb = [ti.astype(BF16) for ti in t]
        mid = [jnp.dot(o, x, preferred_element_type=F32).astype(BF16) for o, x in zip(off, tb)]
        t = [ti - jnp.dot(x, m, preferred_element_type=F32) for ti, x, m in zip(t, tb, mid)]
        sh += 1
    return t


def _gdn_kernel(q_ref, k_ref, v_ref, z_ref, ba_ref, cwq_ref, cwk_ref, cwv_ref, gp_ref, nb_ref, s0_ref,
                hq_ref, hk_ref, hv_ref, y_ref, sout_ref, s_ref, xq_ref, xk_ref, xv_ref, *, lr, hg):
    c = pl.program_id(2)
    n = GDN_CHUNK
    dh = HEAD_DIM_B
    halo = 8

    @pl.when(c == 0)
    def _():
        s_ref[...] = s0_ref[0]
        xq_ref[0:halo, :] = hq_ref[0]
        xk_ref[0:halo, :] = hk_ref[0]
        xv_ref[0:halo, :] = hv_ref[0]

    row_w = lax.broadcasted_iota(jnp.int32, (n, hg * dh), 0)

    def conv(x_ref, xbuf, cw_ref):
        xbuf[halo:halo + lr, :] = x_ref[...]
        if lr < n:
            xbuf[halo + lr:halo + n, :] = jnp.zeros((n - lr, hg * dh), F32)
        y = xbuf[halo - 3:halo - 3 + n, :] * cw_ref[0:1, :]
        for i in range(1, CONV_W):
            y = y + xbuf[halo - 3 + i:halo - 3 + i + n, :] * cw_ref[i:i + 1, :]
        y = _silu(y)
        if lr < n:
            y = jnp.where(row_w < lr, y, 0.0)
        xbuf[0:halo, :] = xbuf[lr:lr + halo, :]
        return y

    yq = conv(q_ref, xq_ref, cwq_ref)
    yk = conv(k_ref, xk_ref, cwk_ref)
    yv = conv(v_ref, xv_ref, cwv_ref)

    row = lax.broadcasted_iota(jnp.int32, (n, n), 0)
    col = lax.broadcasted_iota(jnp.int32, (n, n), 1)
    ba = ba_ref[...]
    if lr < n:
        ba = jnp.concatenate([ba, jnp.zeros((n - lr, LANES), F32)], axis=0)
    live = row < lr
    beta = jnp.where(jnp.logical_and(live, col < hg), jax.nn.sigmoid(ba), 0.0)
    gval = -jnp.exp(gp_ref[0, 0:1, :]) * jax.nn.softplus(ba + gp_ref[0, 1:2, :])
    g = jnp.where(jnp.logical_and(live, jnp.logical_and(col >= hg, col < 2 * hg)), gval, 0.0)
    gsum = _hdot((row >= col).astype(F32), g)
    gsum_t = gsum.T
    incl = row >= col
    strict = row > col

    heads = range(hg)
    cols = [slice(hh * dh, (hh + 1) * dh) for hh in heads]
    gc = [jnp.broadcast_to(gsum[:, hg + hh:hg + hh + 1], (n, n)) for hh in heads]
    bc = [jnp.broadcast_to(beta[:, hh:hh + 1], (n, n)) for hh in heads]
    gam = [jnp.where(incl, jnp.exp(jnp.minimum(gc[hh] - gsum_t[hg + hh:hg + hh + 1, :], 0.0)), 0.0)
           for hh in heads]
    qh = [yq[:, cs] * (lax.rsqrt(jnp.sum(yq[:, cs] * yq[:, cs], axis=-1, keepdims=True) + EPS) * (dh ** -0.5))
          for cs in cols]
    kh = [yk[:, cs] * lax.rsqrt(jnp.sum(yk[:, cs] * yk[:, cs], axis=-1, keepdims=True) + EPS) for cs in cols]
    qb = [x.astype(BF16) for x in qh]
    kb = [x.astype(BF16) for x in kh]
    kk = [_nt_dot(x, x) for x in kb]
    qk = [_nt_dot(x, y) for x, y in zip(qb, kb)]
    a = [jnp.where(strict, bc[hh] * kk[hh] * gam[hh], 0.0) for hh in heads]
    t = _inv_unit_lower(a, row, col)
    eg = [jnp.exp(x) for x in gc]
    rhs = [jnp.concatenate([yv[:, cols[hh]] * bc[hh], kh[hh] * (bc[hh] * eg[hh])], axis=1).astype(BF16)
           for hh in heads]
    uw = [jnp.dot(t[hh].astype(BF16), rhs[hh], preferred_element_type=F32) for hh in heads]
    s = [s_ref[hh] for hh in heads]
    sb = [x.astype(BF16) for x in s]
    v_new = [uw[hh][:, :dh] - jnp.dot(uw[hh][:, dh:].astype(BF16), sb[hh], preferred_element_type=F32)
             for hh in heads]
    vnb = [x.astype(BF16) for x in v_new]
    o = [eg[hh] * jnp.dot(qb[hh], sb[hh], preferred_element_type=F32)
         + jnp.dot((qk[hh] * gam[hh]).astype(BF16), vnb[hh], preferred_element_type=F32) for hh in heads]
    g_last = [x[n - 1:n, :] for x in gc]
    kd = [(kh[hh] * jnp.exp(g_last[hh] - gc[hh])).astype(BF16) for hh in heads]
    for hh in heads:
        s_ref[hh] = s[hh] * jnp.exp(g_last[hh]) + lax.dot_general(
            kd[hh], vnb[hh], (((0,), (0,)), ((), ())), preferred_element_type=F32)
    for hh in heads:
        on = o[hh] * lax.rsqrt(jnp.mean(o[hh] * o[hh], axis=-1, keepdims=True) + EPS) * nb_ref[...]
        y_ref[:, cols[hh]] = (on[0:lr] * _silu(z_ref[:, cols[hh]])).astype(BF16)

    @pl.when(c == pl.num_programs(2) - 1)
    def _():
        sout_ref[0] = s_ref[...]


def _gdn(o32, ba, conv_w, gate_par, norm_b, s0, halo0, *, n_seq, n_chunks, lr, row_blk_off, w_b, col0):
    hg = min(GDN_HEADS_PER_STEP, w_b // HEAD_DIM_B)
    gw = hg * HEAD_DIM_B
    n_groups = w_b // gw
    n_heads = w_b // HEAD_DIM_B
    cb = col0 // gw
    wb = w_b // gw

    def rmap(off):
        return lambda s, g, c: (row_blk_off + s * n_chunks + c, off + g)

    kern = functools.partial(_gdn_kernel, lr=lr, hg=hg)
    return pl.pallas_call(
        kern,
        grid=(n_seq, n_groups, n_chunks),
        in_specs=[
            pl.BlockSpec((lr, gw), rmap(cb)),
            pl.BlockSpec((lr, gw), rmap(cb + wb)),
            pl.BlockSpec((lr, gw), rmap(cb + 2 * wb)),
            pl.BlockSpec((lr, gw), rmap(cb + 3 * wb)),
            pl.BlockSpec((lr, LANES), rmap(0)),
            pl.BlockSpec((CONV_W, gw), lambda s, g, c: (0, g)),
            pl.BlockSpec((CONV_W, gw), lambda s, g, c: (0, wb + g)),
            pl.BlockSpec((CONV_W, gw), lambda s, g, c: (0, 2 * wb + g)),
            pl.BlockSpec((1, 2, LANES), lambda s, g, c: (g, 0, 0)),
            pl.BlockSpec((1, HEAD_DIM_B), lambda s, g, c: (0, 0)),
            pl.BlockSpec((1, hg, HEAD_DIM_B, HEAD_DIM_B), lambda s, g, c: (s, g, 0, 0)),
            pl.BlockSpec((1, 8, gw), lambda s, g, c: (s, 0, g)),
            pl.BlockSpec((1, 8, gw), lambda s, g, c: (s, 0, wb + g)),
            pl.BlockSpec((1, 8, gw), lambda s, g, c: (s, 0, 2 * wb + g)),
        ],
        out_specs=[
            pl.BlockSpec((lr, gw), lambda s, g, c: (s * n_chunks + c, g)),
            pl.BlockSpec((1, hg, HEAD_DIM_B, HEAD_DIM_B), lambda s, g, c: (s, g, 0, 0)),
        ],
        out_shape=[
            jax.ShapeDtypeStruct((n_seq * n_chunks * lr, w_b), BF16),
            jax.ShapeDtypeStruct((n_seq, n_heads, HEAD_DIM_B, HEAD_DIM_B), F32),
        ],
        scratch_shapes=[
            pltpu.VMEM((hg, HEAD_DIM_B, HEAD_DIM_B), F32),
            pltpu.VMEM((8 + GDN_CHUNK, gw), F32),
            pltpu.VMEM((8 + GDN_CHUNK, gw), F32),
            pltpu.VMEM((8 + GDN_CHUNK, gw), F32),
        ],
        compiler_params=pltpu.CompilerParams(
            dimension_semantics=("arbitrary", "arbitrary", "arbitrary"), vmem_limit_bytes=VMEM_LIMIT),
        name=f"gdn_l{lr}",
    )(o32, o32, o32, o32, ba, conv_w, conv_w, conv_w, gate_par, norm_b, s0, halo0, halo0, halo0)


def _out_proj_kernel(ya_ref, yb_ref, w_ref, h_ref, pn_ref, o_ref, *, w_a):
    j = pl.program_id(1)
    tn = w_ref.shape[1]
    r = (jnp.dot(ya_ref[...], w_ref[0:w_a, :], preferred_element_type=F32)
         + jnp.dot(yb_ref[...], w_ref[w_a:, :], preferred_element_type=F32))
    o_ref[:, pl.ds(pl.multiple_of(j * tn, tn), tn)] = r

    @pl.when(j == pl.num_programs(1) - 1)
    def _():
        rows = 64

        def body(i, carry):
            sl = pl.ds(pl.multiple_of(i * rows, rows), rows)
            y = o_ref[sl, :]
            ms = jnp.mean(y * y, axis=-1, keepdims=True)
            o_ref[sl, :] = h_ref[sl, :] + y * lax.rsqrt(ms + EPS) * pn_ref[...]
            return carry
        lax.fori_loop(0, o_ref.shape[0] // rows, body, 0)


def _out_proj(ya, yb, w_out, h, post_norm):
    m, d = h.shape
    w_a = ya.shape[1]
    tm, tn = min(OUT_TM, m), OUT_TN
    kern = functools.partial(_out_proj_kernel, w_a=w_a)
    return pl.pallas_call(
        kern,
        grid=(m // tm, d // tn),
        in_specs=[
            pl.BlockSpec((tm, w_a), lambda i, j: (i, 0)),
            pl.BlockSpec((tm, yb.shape[1]), lambda i, j: (i, 0)),
            pl.BlockSpec((d, tn), lambda i, j: (0, j)),
            pl.BlockSpec((tm, d), lambda i, j: (i, 0)),
            pl.BlockSpec((1, d), lambda i, j: (0, 0)),
        ],
        out_specs=pl.BlockSpec((tm, d), lambda i, j: (i, 0)),
        out_shape=jax.ShapeDtypeStruct((m, d), F32),
        compiler_params=pltpu.CompilerParams(
            dimension_semantics=("arbitrary", "arbitrary"), vmem_limit_bytes=VMEM_LIMIT),
        name="out_proj",
    )(ya, yb, w_out, h, post_norm)


def _kv_layout_kernel(k_ref, v_ref, ko_ref, vo_ref, *, n_heads):
    tm = k_ref.shape[0]
    for p in range(2 * n_heads):
        piece = slice(p * LANES, (p + 1) * LANES)
        ko_ref[pl.ds(p, tm, stride=2 * n_heads), :] = k_ref[:, piece]
        vo_ref[pl.ds((p % 2) * n_heads + p // 2, tm, stride=2 * n_heads), :] = v_ref[:, piece]


def _kv_layout(p32, n_heads, w_a, tok_off):
    seq = p32.shape[0]
    tm = PROJ_TM
    rpt = 2 * n_heads
    spec_out = pl.BlockSpec((pl.Element(tm * rpt), pl.Element(LANES)),
                            lambda i: ((tok_off + i * tm) * rpt, 0))
    shape_out = jax.ShapeDtypeStruct(((tok_off + seq) * rpt, LANES), F32)
    return pl.pallas_call(
        functools.partial(_kv_layout_kernel, n_heads=n_heads),
        grid=(seq // tm,),
        in_specs=[
            pl.BlockSpec((tm, w_a), lambda i: (i, 0)),
            pl.BlockSpec((tm, w_a), lambda i: (i, 1)),
        ],
        out_specs=[spec_out, spec_out],
        out_shape=[shape_out, shape_out],
        compiler_params=pltpu.CompilerParams(dimension_semantics=("arbitrary",), vmem_limit_bytes=VMEM_LIMIT),
        name="kv_layout",
    )(p32, p32)


def kernel(x_prompt, x_sample, cache_k_a, cache_v_a, state_ssm_b, state_conv_b, meta_tokens, rel_bias, pre_norm,
           w_in, lambda_q1, lambda_k1, lambda_q2, lambda_k2, subln_a, conv_b, a_log_b, dt_bias_b, norm_b, w_out,
           post_norm):
    batch, seq, d_model = x_prompt.shape
    dec_b, dec_seq, _ = x_sample.shape
    depth = w_in.shape[0]
    assert batch == 1 and depth == 1
    n_heads_a = rel_bias.shape[1]
    w_a = n_heads_a * 2 * HEAD_DIM_A
    w_b = d_model - w_a
    n_heads_b = w_b // HEAD_DIM_B
    n_cache = cache_k_a.shape[2]
    past = n_cache - N_META
    n_ca = (n_cache // LANES) * LANES
    n_main = 4 * w_a + 4 * w_b
    lam_init = _lambda_init(0)
    hg = min(GDN_HEADS_PER_STEP, n_heads_b)
    n_groups = n_heads_b // hg
    n_dec = dec_b * dec_seq
    assert seq % PROJ_TM == 0 and seq % GDN_CHUNK == 0 and n_cache - n_ca + dec_seq <= LANES

    w_in_bf = w_in[0].astype(BF16)
    wb = w_in_bf[:, n_main:n_main + n_heads_b].reshape(d_model, n_groups, hg)
    wa = w_in_bf[:, n_main + n_heads_b:].reshape(d_model, n_groups, hg)
    w_tail = jnp.concatenate([wb, wa, jnp.zeros((d_model, n_groups, LANES - 2 * hg), BF16)], axis=-1)
    w_tail = w_tail.reshape(d_model, n_groups * LANES)
    w_out_bf = w_out[0].astype(BF16)
    gate_par = jnp.zeros((n_groups, 2, LANES), F32)
    gate_par = gate_par.at[:, 0, hg:2 * hg].set(a_log_b[0].reshape(n_groups, hg))
    gate_par = gate_par.at[:, 1, hg:2 * hg].set(dt_bias_b[0].reshape(n_groups, hg))
    lam_params = jnp.stack([lambda_q1[0], lambda_k1[0], lambda_q2[0], lambda_k2[0]])

    xp = x_prompt[0]
    n_small = -(-(n_dec + N_META) // PROJ_TM) * PROJ_TM
    xs = jnp.concatenate([x_sample.reshape(n_dec, d_model), meta_tokens.astype(F32),
                          jnp.zeros((n_small - n_dec - N_META, d_model), F32)], axis=0)
    p32, p16, pba = _in_proj(xp, pre_norm, w_in_bf, w_tail, w_a, n_main)
    s32, s16, sba = _in_proj(xs, pre_norm, w_in_bf, w_tail, w_a, n_main)

    fb, mb, sa, se, lam_t = _bias_tables(rel_bias, lam_params, past, dec_seq, n_ca, lam_init)
    lam = lam_t[0, 0, 0:1]

    meta16 = s16[n_dec:n_dec + N_META]
    pad_m = jnp.zeros((LANES - N_META, w_a), BF16)
    km = jnp.concatenate([meta16[:, w_a:2 * w_a], pad_m], axis=0)
    vm = jnp.concatenate([meta16[:, 2 * w_a:], pad_m], axis=0)
    ya_p = _attn_prompt(lam, p16, p32, km, vm, fb, mb, subln_a, n_heads_a, lam_init)

    ck = cache_k_a[0].reshape(dec_b, n_cache * n_heads_a * 2, HEAD_DIM_A)
    cv = cache_v_a[0].reshape(dec_b, n_cache, n_heads_a, 2, HEAD_DIM_A).transpose(0, 1, 3, 2, 4)
    cv = cv.reshape(dec_b, n_cache * 2 * n_heads_a, HEAD_DIM_A)
    n_extra = n_cache - n_ca + dec_seq
    pad_e = jnp.zeros((dec_b, LANES - n_extra, w_a), BF16)
    ck_tail = cache_k_a[0, :, n_ca:].reshape(dec_b, n_cache - n_ca, w_a).astype(BF16)
    cv_tail = cache_v_a[0, :, n_ca:].reshape(dec_b, n_cache - n_ca, w_a).astype(BF16)
    ke = jnp.concatenate([ck_tail, s16[:n_dec, w_a:2 * w_a].reshape(dec_b, dec_seq, w_a), pad_e], axis=1)
    ve = jnp.concatenate([cv_tail, s16[:n_dec, 2 * w_a:].reshape(dec_b, dec_seq, w_a), pad_e], axis=1)
    ya_s = _attn_sample(lam, s16, s32, ck, cv, ke, ve, sa, se, subln_a, n_heads_a, dec_b, dec_seq, n_ca,
                        lam_init)

    col0 = 3 * w_a
    qkv_cols = slice(col0, col0 + 3 * w_b)
    zero_halo = jnp.zeros((1, 8, 3 * w_b), F32)
    gdn = functools.partial(_gdn, conv_w=conv_b[0], gate_par=gate_par, norm_b=norm_b, w_b=w_b, col0=col0)
    _, s_meta = gdn(s32, sba, s0=jnp.zeros((1, n_heads_b, HEAD_DIM_B, HEAD_DIM_B), F32), halo0=zero_halo,
                    n_seq=1, n_chunks=1, lr=N_META, row_blk_off=n_dec // N_META)
    meta_halo = jnp.concatenate([jnp.zeros((8 - (CONV_W - 1), 3 * w_b), F32),
                                 s32[n_dec + N_META - (CONV_W - 1):n_dec + N_META, qkv_cols]], axis=0)[None]
    yb_p, ssm_p = gdn(p32, pba, s0=s_meta, halo0=meta_halo, n_seq=1, n_chunks=seq // GDN_CHUNK, lr=GDN_CHUNK,
                      row_blk_off=0)
    samp_halo = jnp.concatenate([jnp.zeros((dec_b, 8 - (CONV_W - 1), 3 * w_b), F32), state_conv_b[0]], axis=1)
    yb_s, ssm_s = gdn(s32, sba, s0=state_ssm_b[0].astype(F32), halo0=samp_halo, n_seq=dec_b, n_chunks=1,
                      lr=dec_seq, row_blk_off=0)

    y_p = _out_proj(ya_p, yb_p, w_out_bf, xp, post_norm)
    y_s = _out_proj(ya_s, yb_s, w_out_bf, x_sample.reshape(n_dec, d_model), post_norm)

    hd = HEAD_DIM_A
    k_p, v_p = _kv_layout(p32, n_heads_a, w_a, N_META)
    meta32 = s32[n_dec:n_dec + N_META]
    k_p = lax.dynamic_update_slice(k_p, meta32[:, :w_a].reshape(N_META * 2 * n_heads_a, hd), (0, 0))
    v_meta = meta32[:, w_a:2 * w_a].reshape(N_META, n_heads_a, 2, hd).transpose(0, 2, 1, 3)
    v_p = lax.dynamic_update_slice(v_p, v_meta.reshape(N_META * 2 * n_heads_a, hd), (0, 0))
    v_p = v_p.reshape(N_META + seq, 2, n_heads_a, hd).transpose(0, 2, 1, 3)
    conv_p = p32[seq - (CONV_W - 1):, qkv_cols]
    k_s = s32[:n_dec, :w_a]
    v_s = s32[:n_dec, w_a:2 * w_a]
    conv_s = s32[:n_dec, qkv_cols].reshape(dec_b, dec_seq, 3 * w_b)[:, dec_seq - (CONV_W - 1):]
    return (
        y_p[None],
        y_s.reshape(dec_b, dec_seq, d_model),
        k_p.reshape(1, 1, N_META + seq, n_heads_a, 2, hd),
        v_p.reshape(1, 1, N_META + seq, n_heads_a, 2 * hd),
        ssm_p[None],
        conv_p[None, None],
        k_s.reshape(1, dec_b, dec_seq, n_heads_a, 2, hd),
        v_s.reshape(1, dec_b, dec_seq, n_heads_a, 2 * hd),
        ssm_s[None],
        conv_s[None],
    )
```

```python
import functools
import math

import jax
import jax.numpy as jnp
from jax import lax
from jax.experimental import pallas as pl
from jax.experimental.pallas import tpu as pltpu

F32 = jnp.float32
BF16 = jnp.bfloat16
HIGHEST = lax.Precision.HIGHEST

EPS = 1e-6
CHUNK = 64
N_META = 16
HEAD_DIM_A = 128
HEAD_DIM_B = 128
CONV_W = 4
N_BUCKETS = 32
MAX_DISTANCE = 1024
NEG = -1e30
LOG2E = math.log2(math.e)

LANES = 128
ATT_BLOCK = 256
FAR_BLOCKS = 4
GDN_CHUNK = 128
GDN_HEADS_PER_STEP = 16
GDN_INTERLEAVE = 16
PROJ_TM = 512
PROJ_TN = 1024
PROJ_TN_SINGLE = 512
OUT_TM = 512
OUT_TN = 512
VMEM_LIMIT = 56 * 1024 * 1024


def _lambda_init(layer):
    return 0.8 - 0.6 * math.exp(-0.3 * layer)


def _bias_saturation_distance():
    nb = N_BUCKETS // 2
    max_exact = nb // 2
    return int(math.ceil(max_exact * (MAX_DISTANCE / max_exact) ** ((nb - max_exact - 1) / (nb - max_exact)))) + 1


N_NEAR = -(-(_bias_saturation_distance() + ATT_BLOCK) // ATT_BLOCK) + 1
FAR_BUCKET = N_BUCKETS // 2 - 1


def _silu(x):
    return x * jax.nn.sigmoid(x)


def _in_proj_kernel(x_ref, pn_ref, w_ref, wt_ref, o32_ref, o16_ref, ba_ref, xn_ref, *, q_tiles, bf_tiles,
                    q_scale):
    j = pl.program_id(1)
    tm = x_ref.shape[0]
    rows = math.gcd(tm, 64) if tm % 64 == 0 else math.gcd(tm, 80)

    @pl.when(j == 0)
    def _():
        def body(r, carry):
            sl = pl.ds(pl.multiple_of(r * rows, 8), rows)
            x = x_ref[sl, :]
            ms = jnp.mean(x * x, axis=-1, keepdims=True)
            xn_ref[sl, :] = (x * lax.rsqrt(ms + EPS) * pn_ref[...]).astype(BF16)
            return carry
        lax.fori_loop(0, tm // rows, body, 0)
        ba_ref[...] = jnp.dot(xn_ref[...], wt_ref[...], preferred_element_type=F32)

    r = jnp.dot(xn_ref[...], w_ref[...], preferred_element_type=F32)

    @pl.when(j >= q_tiles)
    def _():
        o32_ref[...] = r

    @pl.when(j < q_tiles)
    def _():
        o16_ref[...] = (r * q_scale).astype(BF16)

    @pl.when(jnp.logical_and(j >= q_tiles, j < bf_tiles))
    def _():
        o16_ref[...] = r.astype(BF16)


def _in_proj(x, pre_norm, w_all, w_tail, w_a, n, single_tile=False):
    m, d = x.shape
    tm, tn = PROJ_TM, min(PROJ_TN, w_a)
    x_mode = {}
    if single_tile:
        tm, tn = m, min(PROJ_TN_SINGLE, w_a)
        x_mode = dict(pipeline_mode=pl.Buffered(1))
    q_tiles = w_a // tn
    bf_tiles = 3 * w_a // tn
    nt = w_tail.shape[1]
    kern = functools.partial(_in_proj_kernel, q_tiles=q_tiles, bf_tiles=bf_tiles, q_scale=HEAD_DIM_A ** -0.5 * LOG2E)
    return pl.pallas_call(
        kern,
        grid=(m // tm, n // tn),
        in_specs=[
            pl.BlockSpec((tm, d), lambda i, j: (i, 0), **x_mode),
            pl.BlockSpec((1, d), lambda i, j: (0, 0)),
            pl.BlockSpec((d, tn), lambda i, j: (0, j)),
            pl.BlockSpec((d, nt), lambda i, j: (0, 0)),
        ],
        out_specs=[
            pl.BlockSpec((tm, tn), lambda i, j: (i, jnp.maximum(j - q_tiles, 0))),
            pl.BlockSpec((tm, tn), lambda i, j: (i, jnp.minimum(j, bf_tiles - 1))),
            pl.BlockSpec((tm, nt), lambda i, j: (i, 0)),
        ],
        out_shape=[
            jax.ShapeDtypeStruct((m, n - w_a), F32),
            jax.ShapeDtypeStruct((m, 3 * w_a), BF16),
            jax.ShapeDtypeStruct((m, nt), F32),
        ],
        scratch_shapes=[pltpu.VMEM((tm, d), BF16)],
        compiler_params=pltpu.CompilerParams(
            dimension_semantics=("arbitrary", "arbitrary"), vmem_limit_bytes=VMEM_LIMIT),
        name="in_proj",
    )(x, pre_norm, w_all, w_tail)


def _bias_values(rb_ref, h, rel):
    nb = N_BUCKETS // 2
    max_exact = nb // 2
    n = jnp.abs(rel)
    nf = jnp.maximum(n, 1).astype(F32)
    large = max_exact + (jnp.log(nf / max_exact) / math.log(MAX_DISTANCE / max_exact)
                         * (nb - max_exact)).astype(jnp.int32)
    large = jnp.minimum(large, nb - 1)
    bucket = jnp.where(rel > 0, nb, 0) + jnp.where(n < max_exact, n, large)
    val = jnp.zeros(rel.shape, F32)
    for b in range(N_BUCKETS):
        val = jnp.where(bucket == b, rb_ref[b, h], val)
    return (val - rb_ref[FAR_BUCKET, h]) * LOG2E


def _bias_kernel(rb_ref, lamp_ref, fb_ref, mb_ref, sa_ref, se_ref, lam_ref, *, past, dec_seq, n_ca, lam_init):
    h = pl.program_id(0)
    t = ATT_BLOCK
    shift = CHUNK.bit_length() - 1
    i = lax.broadcasted_iota(jnp.int32, (t, t), 0)
    j = lax.broadcasted_iota(jnp.int32, (t, t), 1)
    for dd in range(N_NEAR):
        val = _bias_values(rb_ref, h, (j - i) - t * dd)
        if dd == 0:
            val = jnp.where((j >> shift) <= (i >> shift), val, NEG)
        fb_ref[0, dd] = val
    i = lax.broadcasted_iota(jnp.int32, (t, LANES), 0)
    j = lax.broadcasted_iota(jnp.int32, (t, LANES), 1)
    for qb in range(N_NEAR):
        val = _bias_values(rb_ref, h, (j - N_META) - (t * qb + i))
        mb_ref[0, qb] = jnp.where(j < N_META, val, NEG)
    i = lax.broadcasted_iota(jnp.int32, (dec_seq, n_ca), 0)
    j = lax.broadcasted_iota(jnp.int32, (dec_seq, n_ca), 1)
    sa_ref[0] = _bias_values(rb_ref, h, (j - N_META) - (past + i))
    i = lax.broadcasted_iota(jnp.int32, (dec_seq, LANES), 0)
    j = lax.broadcasted_iota(jnp.int32, (dec_seq, LANES), 1)
    n_extra = N_META + past - n_ca + dec_seq
    val = _bias_values(rb_ref, h, (n_ca - N_META + j) - (past + i))
    se_ref[0] = jnp.where(j < n_extra, val, NEG)
    lp = lamp_ref[...]
    s1 = jnp.sum(lp[0:1] * lp[1:2], axis=-1, keepdims=True)
    s2 = jnp.sum(lp[2:3] * lp[3:4], axis=-1, keepdims=True)
    lam_ref[0] = jnp.broadcast_to(jnp.exp(s1) - jnp.exp(s2) + lam_init, (8, LANES))


def _bias_tables(rel_bias, lam_params, past, dec_seq, n_ca, lam_init):
    nh = rel_bias.shape[1]
    t = ATT_BLOCK
    kern = functools.partial(_bias_kernel, past=past, dec_seq=dec_seq, n_ca=n_ca, lam_init=lam_init)
    return pl.pallas_call(
        kern,
        grid=(nh,),
        in_specs=[
            pl.BlockSpec(memory_space=pltpu.SMEM),
            pl.BlockSpec((4, HEAD_DIM_A), lambda h: (0, 0)),
        ],
        out_specs=[
            pl.BlockSpec((1, N_NEAR, t, t), lambda h: (h, 0, 0, 0)),
            pl.BlockSpec((1, N_NEAR, t, LANES), lambda h: (h, 0, 0, 0)),
            pl.BlockSpec((1, dec_seq, n_ca), lambda h: (h, 0, 0)),
            pl.BlockSpec((1, dec_seq, LANES), lambda h: (h, 0, 0)),
            pl.BlockSpec((1, 8, LANES), lambda h: (h, 0, 0)),
        ],
        out_shape=[
            jax.ShapeDtypeStruct((nh, N_NEAR, t, t), F32),
            jax.ShapeDtypeStruct((nh, N_NEAR, t, LANES), F32),
            jax.ShapeDtypeStruct((nh, dec_seq, n_ca), F32),
            jax.ShapeDtypeStruct((nh, dec_seq, LANES), F32),
            jax.ShapeDtypeStruct((nh, 8, LANES), F32),
        ],
        compiler_params=pltpu.CompilerParams(dimension_semantics=("arbitrary",)),
        name="bias_tables",
    )(rel_bias, lam_params)


def _nt_dot(a, b):
    return lax.dot_general(a, b, (((1,), (1,)), ((), ())), preferred_element_type=F32)


def _attn_finish(o, z, subln, lam_init):
    ms = jnp.mean(o * o, axis=-1, keepdims=True)
    on = o * lax.rsqrt(ms + EPS) * subln * (1.0 - lam_init)
    return on * _silu(z)


def _attn_prompt_kernel(lam_ref, q_ref, k_ref, v_ref, km_ref, vm_ref, fb_ref, mb_ref, z_ref, sub_ref, o_ref,
                        m_ref, l_ref, acc_ref, sa_ref, sb_ref, *, lam_init):
    qi = pl.program_id(1)
    t = ATT_BLOCK
    d = HEAD_DIM_A
    hw = 2 * d

    def rep(x, n):
        return jnp.concatenate([x] * n, axis=1) if n > 1 else x

    def lane_fold(p):
        out = p[:, 0:LANES]
        for g in range(1, p.shape[1] // LANES):
            out = out + p[:, g * LANES:(g + 1) * LANES]
        return out

    vm = vm_ref[...]
    mb = mb_ref[0, jnp.minimum(qi, N_NEAR - 1)]
    for c in range(2):
        s = _nt_dot(q_ref[:, c * d:(c + 1) * d], km_ref[:, c * d:(c + 1) * d]) + mb
        m = jnp.broadcast_to(jnp.max(s, axis=-1, keepdims=True), (t, LANES))
        p = jnp.exp2(s - m)
        m_ref[c] = m
        l_ref[c] = p
        acc_ref[c] = jnp.dot(p.astype(BF16), vm, preferred_element_type=F32)

    def qk(start, width):
        rows = pl.ds(start, width)
        return [_nt_dot(q_ref[:, c * d:(c + 1) * d], k_ref[rows, c * d:(c + 1) * d]) for c in range(2)]

    def softmax_pv(s_both, start, width, bias):
        v = v_ref[pl.ds(start, width), :]
        for c in range(2):
            s = s_both[c]
            if bias is not None:
                s = s + bias
            m_old = m_ref[c]
            m_new = jnp.maximum(m_old, jnp.max(s, axis=-1, keepdims=True))
            p = jnp.exp2(s - rep(m_new, width // LANES))
            alpha = jnp.exp2(m_old - m_new)
            m_ref[c] = m_new
            l_ref[c] = alpha * l_ref[c] + lane_fold(p)
            acc_ref[c] = rep(alpha, hw // LANES) * acc_ref[c] + jnp.dot(
                p.astype(BF16), v, preferred_element_type=F32)

    wide = FAR_BLOCKS * t
    far_shift = FAR_BLOCKS.bit_length() - 1
    n_far = jnp.maximum(qi - (N_NEAR - 1), 0)
    n_wide = n_far >> far_shift

    def put(ref, s_both):
        ref[0] = s_both[0]
        ref[1] = s_both[1]

    @pl.when(n_wide > 0)
    def _():
        put(sa_ref, qk(0, wide))

    def far_body(i, carry):
        b0 = pl.multiple_of(i * (2 * wide), wide)
        b1 = pl.multiple_of(b0 + wide, wide)
        b2 = pl.multiple_of(jnp.minimum(b0 + 2 * wide, (n_wide - 1) * wide), wide)
        put(sb_ref, qk(b1, wide))
        softmax_pv([sa_ref[0], sa_ref[1]], b0, wide, None)
        put(sa_ref, qk(b2, wide))
        softmax_pv([sb_ref[0], sb_ref[1]], b1, wide, None)
        return carry

    def far_body2(i, carry):
        far_body(2 * i, carry)
        return far_body(2 * i + 1, carry)
    lax.fori_loop(0, n_wide >> 2, far_body2, 0)
    lax.fori_loop((n_wide >> 2) * 2, n_wide >> 1, far_body, 0)

    @pl.when((n_wide & 1) == 1)
    def _():
        softmax_pv([sa_ref[0], sa_ref[1]], pl.multiple_of((n_wide - 1) * wide, wide), wide, None)

    n_rest = n_far - (n_wide << far_shift)
    for cnt in range(1, FAR_BLOCKS):
        @pl.when(n_rest == cnt)
        def _(cnt=cnt):
            first = n_wide << far_shift
            s_next = qk(pl.multiple_of(first * t, t), t)
            for b in range(cnt):
                s_cur = s_next
                if b + 1 < cnt:
                    s_next = qk(pl.multiple_of((first + b + 1) * t, t), t)
                softmax_pv(s_cur, pl.multiple_of((first + b) * t, t), t, None)

    @pl.when(qi >= N_NEAR - 1)
    def _():
        s_next = qk(pl.multiple_of((qi - (N_NEAR - 1)) * t, t), t)
        for dd in range(N_NEAR - 1, -1, -1):
            s_cur = s_next
            if dd > 0:
                s_next = qk(pl.multiple_of((qi - dd + 1) * t, t), t)
            softmax_pv(s_cur, pl.multiple_of((qi - dd) * t, t), t, fb_ref[0, dd])

    @pl.when(qi < N_NEAR - 1)
    def _():
        for dd in range(N_NEAR - 2, -1, -1):
            @pl.when(qi - dd >= 0)
            def _(dd=dd):
                start = pl.multiple_of((qi - dd) * t, t)
                softmax_pv(qk(start, t), start, t, fb_ref[0, dd])

    inv0 = 1.0 / jnp.sum(l_ref[0], axis=-1, keepdims=True)
    inv1 = lam_ref[0] / jnp.sum(l_ref[1], axis=-1, keepdims=True)
    o = acc_ref[0] * inv0 - acc_ref[1] * inv1
    o_ref[...] = _attn_finish(o, z_ref[...], sub_ref[...], lam_init).astype(BF16)


def _attn_prompt(lam, qkv16, o32, km, vm, fb, mb, subln, n_heads, lam_init):
    seq = qkv16.shape[0]
    t = ATT_BLOCK
    hw = 2 * HEAD_DIM_A
    kern = functools.partial(_attn_prompt_kernel, lam_init=lam_init)
    return pl.pallas_call(
        kern,
        grid=(n_heads, seq // t),
        in_specs=[
            pl.BlockSpec(memory_space=pltpu.SMEM),
            pl.BlockSpec((t, hw), lambda h, i: (i, h)),
            pl.BlockSpec((seq, hw), lambda h, i: (0, n_heads + h)),
            pl.BlockSpec((seq, hw), lambda h, i: (0, 2 * n_heads + h)),
            pl.BlockSpec((LANES, hw), lambda h, i: (0, h)),
            pl.BlockSpec((LANES, hw), lambda h, i: (0, h)),
            pl.BlockSpec((1, N_NEAR, t, t), lambda h, i: (h, 0, 0, 0)),
            pl.BlockSpec((1, N_NEAR, t, LANES), lambda h, i: (h, 0, 0, 0)),
            pl.BlockSpec((t, hw), lambda h, i: (i, 2 * n_heads + h)),
            pl.BlockSpec((1, hw), lambda h, i: (0, 0)),
        ],
        out_specs=pl.BlockSpec((t, hw), lambda h, i: (i, h)),
        out_shape=jax.ShapeDtypeStruct((seq, n_heads * hw), BF16),
        scratch_shapes=[
            pltpu.VMEM((2, t, LANES), F32),
            pltpu.VMEM((2, t, LANES), F32),
            pltpu.VMEM((2, t, hw), F32),
            pltpu.VMEM((2, t, FAR_BLOCKS * t), F32),
            pltpu.VMEM((2, t, FAR_BLOCKS * t), F32),
        ],
        compiler_params=pltpu.CompilerParams(
            dimension_semantics=("arbitrary", "arbitrary"), vmem_limit_bytes=VMEM_LIMIT),
        name="attn_prompt",
    )(lam, qkv16, qkv16, qkv16, km, vm, fb, mb, o32, subln)


def _attn_sample_kernel(lam_ref, q_ref, kc_ref, vc_ref, ke_ref, ve_ref, sa_ref, se_ref, z_ref, sub_ref, o_ref,
                        *, lam_init, n_heads, n_ca):
    h = pl.program_id(1)
    d = HEAD_DIM_A
    vc = jnp.concatenate([vc_ref[0, pl.ds(c * n_heads + h, n_ca, stride=2 * n_heads), :] for c in range(2)],
                         axis=1).astype(BF16)
    ve = ve_ref[0]
    outs = []
    for c in range(2):
        q = q_ref[:, c * d:(c + 1) * d]
        kc = kc_ref[0, pl.ds(2 * h + c, n_ca, stride=2 * n_heads), :].astype(BF16)
        s_a = _nt_dot(q, kc) + sa_ref[0]
        s_e = _nt_dot(q, ke_ref[0, :, c * d:(c + 1) * d]) + se_ref[0]
        m = jnp.maximum(jnp.max(s_a, axis=-1, keepdims=True), jnp.max(s_e, axis=-1, keepdims=True))
        p_a = jnp.exp2(s_a - m)
        p_e = jnp.exp2(s_e - m)
        l = jnp.sum(p_a, axis=-1, keepdims=True) + jnp.sum(p_e, axis=-1, keepdims=True)
        acc = (jnp.dot(p_a.astype(BF16), vc, preferred_element_type=F32)
               + jnp.dot(p_e.astype(BF16), ve, preferred_element_type=F32))
        outs.append(acc / l)
    o = outs[0] - lam_ref[0] * outs[1]
    o_ref[...] = _attn_finish(o, z_ref[...], sub_ref[...], lam_init).astype(BF16)


def _attn_sample(lam, qkv16, o32, cache_k, cache_v, ke, ve, sa, se, subln, n_heads, dec_b, dec_seq, n_ca,
                 lam_init):
    hw = 2 * HEAD_DIM_A
    kern = functools.partial(_attn_sample_kernel, lam_init=lam_init, n_heads=n_heads, n_ca=n_ca)
    return pl.pallas_call(
        kern,
        grid=(dec_b, n_heads),
        in_specs=[
            pl.BlockSpec(memory_space=pltpu.SMEM),
            pl.BlockSpec((dec_seq, hw), lambda b, h: (b, h)),
            pl.BlockSpec((1, n_ca * n_heads * 2, HEAD_DIM_A), lambda b, h: (b, 0, 0)),
            pl.BlockSpec((1, n_ca * n_heads * 2, HEAD_DIM_A), lambda b, h: (b, 0, 0)),
            pl.BlockSpec((1, LANES, hw), lambda b, h: (b, 0, h)),
            pl.BlockSpec((1, LANES, hw), lambda b, h: (b, 0, h)),
            pl.BlockSpec((1, dec_seq, n_ca), lambda b, h: (h, 0, 0)),
            pl.BlockSpec((1, dec_seq, LANES), lambda b, h: (h, 0, 0)),
            pl.BlockSpec((dec_seq, hw), lambda b, h: (b, 2 * n_heads + h)),
            pl.BlockSpec((1, hw), lambda b, h: (0, 0)),
        ],
        out_specs=pl.BlockSpec((dec_seq, hw), lambda b, h: (b, h)),
        out_shape=jax.ShapeDtypeStruct((dec_b * dec_seq, n_heads * hw), BF16),
        compiler_params=pltpu.CompilerParams(
            dimension_semantics=("arbitrary", "arbitrary"), vmem_limit_bytes=VMEM_LIMIT),
        name="attn_sample",
    )(lam, qkv16, cache_k, cache_v, ke, ve, sa, se, o32, subln)


def _hdot(a, b):
    return jnp.dot(a, b, preferred_element_type=F32, precision=HIGHEST)


def _bdot(a, b):
    return jnp.dot(a.astype(BF16), b.astype(BF16), preferred_element_type=F32)


def _inv_unit_lower(a_list, row, col):
    sh = 4
    n = a_list[0].shape[0]
    eye = (row == col).astype(F32)
    same = (row >> sh) == (col >> sh)
    dblk = [jnp.where(same, a, 0.0) for a in a_list]
    t = [eye - x for x in dblk]
    pw = [x.astype(BF16) for x in dblk]
    for _ in range(sh - 1):
        pw = [jnp.dot(x, x, preferred_element_type=F32).astype(BF16) for x in pw]
        t = [ti + jnp.dot(ti.astype(BF16), x, preferred_element_type=F32) for ti, x in zip(t, pw)]
    while (1 << sh) < n:
        offm = jnp.logical_and((row >> (sh + 1)) == (col >> (sh + 1)), (row >> sh) != (col >> sh))
        off = [jnp.where(offm, a, 0.0).astype(BF16) for a in a_list]
        tb = [ti.astype(BF16) for ti in t]
        mid = [jnp.dot(o, x, preferred_element_type=F32).astype(BF16) for o, x in zip(off, tb)]
        t = [ti - jnp.dot(x, m, preferred_element_type=F32) for ti, x, m in zip(t, tb, mid)]
        sh += 1
    return t


def _gdn_kernel(q_ref, k_ref, v_ref, z_ref, ba_ref, cwq_ref, cwk_ref, cwv_ref, gp_ref, nb_ref, s0_ref,
                hq_ref, hk_ref, hv_ref, y_ref, sout_ref, s_ref, xq_ref, xk_ref, xv_ref, *, lr, hg):
    c = pl.program_id(2)
    n = GDN_CHUNK
    dh = HEAD_DIM_B
    halo = 8

    @pl.when(c == 0)
    def _():
        s_ref[...] = s0_ref[0]
        xq_ref[0:halo, :] = hq_ref[0]
        xk_ref[0:halo, :] = hk_ref[0]
        xv_ref[0:halo, :] = hv_ref[0]

    row_w = lax.broadcasted_iota(jnp.int32, (n, hg * dh), 0)

    def conv(x_ref, xbuf, cw_ref):
        xbuf[halo:halo + lr, :] = x_ref[...]
        if lr < n:
            xbuf[halo + lr:halo + n, :] = jnp.zeros((n - lr, hg * dh), F32)
        y = xbuf[halo - 3:halo - 3 + n, :] * cw_ref[0:1, :]
        for i in range(1, CONV_W):
            y = y + xbuf[halo - 3 + i:halo - 3 + i + n, :] * cw_ref[i:i + 1, :]
        y = _silu(y)
        if lr < n:
            y = jnp.where(row_w < lr, y, 0.0)
        xbuf[0:halo, :] = xbuf[lr:lr + halo, :]
        return y

    yq = conv(q_ref, xq_ref, cwq_ref)
    yk = conv(k_ref, xk_ref, cwk_ref)
    yv = conv(v_ref, xv_ref, cwv_ref)

    row = lax.broadcasted_iota(jnp.int32, (n, n), 0)
    col = lax.broadcasted_iota(jnp.int32, (n, n), 1)
    ba = ba_ref[...]
    if lr < n:
        ba = jnp.concatenate([ba, jnp.zeros((n - lr, LANES), F32)], axis=0)
    live = row < lr
    beta = jnp.where(jnp.logical_and(live, col < hg), jax.nn.sigmoid(ba), 0.0)
    gval = -jnp.exp(gp_ref[0, 0:1, :]) * jax.nn.softplus(ba + gp_ref[0, 1:2, :])
    g = jnp.where(jnp.logical_and(live, jnp.logical_and(col >= hg, col < 2 * hg)), gval, 0.0)
    gsum = _hdot((row >= col).astype(F32), g)
    gsum_t = gsum.T
    incl = row >= col
    strict = row > col

    def run(heads):
        cols = {hh: slice(hh * dh, (hh + 1) * dh) for hh in heads}
        gc = {hh: jnp.broadcast_to(gsum[:, hg + hh:hg + hh + 1], (n, n)) for hh in heads}
        bc = {hh: jnp.broadcast_to(beta[:, hh:hh + 1], (n, n)) for hh in heads}
        gam = {hh: jnp.where(incl, jnp.exp(jnp.minimum(gc[hh] - gsum_t[hg + hh:hg + hh + 1, :], 0.0)), 0.0)
               for hh in heads}
        qh = {hh: yq[:, cols[hh]] * (lax.rsqrt(jnp.sum(yq[:, cols[hh]] * yq[:, cols[hh]], axis=-1, keepdims=True)
                                               + EPS) * (dh ** -0.5)) for hh in heads}
        kh = {hh: yk[:, cols[hh]] * lax.rsqrt(jnp.sum(yk[:, cols[hh]] * yk[:, cols[hh]], axis=-1, keepdims=True)
                                              + EPS) for hh in heads}
        qb = {hh: qh[hh].astype(BF16) for hh in heads}
        kb = {hh: kh[hh].astype(BF16) for hh in heads}
        kk = {hh: _nt_dot(kb[hh], kb[hh]) for hh in heads}
        qk = {hh: _nt_dot(qb[hh], kb[hh]) for hh in heads}
        a = [jnp.where(strict, bc[hh] * kk[hh] * gam[hh], 0.0) for hh in heads]
        t = dict(zip(heads, _inv_unit_lower(a, row, col)))
        eg = {hh: jnp.exp(gc[hh]) for hh in heads}
        rhs = {hh: jnp.concatenate([yv[:, cols[hh]] * bc[hh], kh[hh] * (bc[hh] * eg[hh])], axis=1).astype(BF16)
               for hh in heads}
        uw = {hh: jnp.dot(t[hh].astype(BF16), rhs[hh], preferred_element_type=F32) for hh in heads}
        s = {hh: s_ref[hh] for hh in heads}
        sb = {hh: s[hh].astype(BF16) for hh in heads}
        v_new = {hh: uw[hh][:, :dh] - jnp.dot(uw[hh][:, dh:].astype(BF16), sb[hh], preferred_element_type=F32)
                 for hh in heads}
        vnb = {hh: v_new[hh].astype(BF16) for hh in heads}
        o = {hh: eg[hh] * jnp.dot(qb[hh], sb[hh], preferred_element_type=F32)
             + jnp.dot((qk[hh] * gam[hh]).astype(BF16), vnb[hh], preferred_element_type=F32) for hh in heads}
        g_last = {hh: gc[hh][n - 1:n, :] for hh in heads}
        kd = {hh: (kh[hh] * jnp.exp(g_last[hh] - gc[hh])).astype(BF16) for hh in heads}
        for hh in heads:
            s_ref[hh] = s[hh] * jnp.exp(g_last[hh]) + lax.dot_general(
                kd[hh], vnb[hh], (((0,), (0,)), ((), ())), preferred_element_type=F32)
        for hh in heads:
            on = o[hh] * lax.rsqrt(jnp.mean(o[hh] * o[hh], axis=-1, keepdims=True) + EPS) * nb_ref[...]
            y_ref[:, cols[hh]] = (on[0:lr] * _silu(z_ref[:, cols[hh]])).astype(BF16)

    for first in range(0, hg, GDN_INTERLEAVE):
        run(list(range(first, min(first + GDN_INTERLEAVE, hg))))

    @pl.when(c == pl.num_programs(2) - 1)
    def _():
        sout_ref[0] = s_ref[...]


def _gdn(o32, ba, conv_w, gate_par, norm_b, s0, halo0, *, n_seq, n_chunks, lr, row_blk_off, w_b, col0):
    hg = min(GDN_HEADS_PER_STEP, w_b // HEAD_DIM_B)
    gw = hg * HEAD_DIM_B
    n_groups = w_b // gw
    n_heads = w_b // HEAD_DIM_B
    cb = col0 // gw
    wb = w_b // gw

    def rmap(off):
        return lambda s, g, c: (row_blk_off + s * n_chunks + c, off + g)

    kern = functools.partial(_gdn_kernel, lr=lr, hg=hg)
    return pl.pallas_call(
        kern,
        grid=(n_seq, n_groups, n_chunks),
        in_specs=[
            pl.BlockSpec((lr, gw), rmap(cb)),
            pl.BlockSpec((lr, gw), rmap(cb + wb)),
            pl.BlockSpec((lr, gw), rmap(cb + 2 * wb)),
            pl.BlockSpec((lr, gw), rmap(cb + 3 * wb)),
            pl.BlockSpec((lr, LANES), rmap(0)),
            pl.BlockSpec((CONV_W, gw), lambda s, g, c: (0, g)),
            pl.BlockSpec((CONV_W, gw), lambda s, g, c: (0, wb + g)),
            pl.BlockSpec((CONV_W, gw), lambda s, g, c: (0, 2 * wb + g)),
            pl.BlockSpec((1, 2, LANES), lambda s, g, c: (g, 0, 0)),
            pl.BlockSpec((1, HEAD_DIM_B), lambda s, g, c: (0, 0)),
            pl.BlockSpec((1, hg, HEAD_DIM_B, HEAD_DIM_B), lambda s, g, c: (s, g, 0, 0)),
            pl.BlockSpec((1, 8, gw), lambda s, g, c: (s, 0, g)),
            pl.BlockSpec((1, 8, gw), lambda s, g, c: (s, 0, wb + g)),
            pl.BlockSpec((1, 8, gw), lambda s, g, c: (s, 0, 2 * wb + g)),
        ],
        out_specs=[
            pl.BlockSpec((lr, gw), lambda s, g, c: (s * n_chunks + c, g)),
            pl.BlockSpec((1, hg, HEAD_DIM_B, HEAD_DIM_B), lambda s, g, c: (s, g, 0, 0)),
        ],
        out_shape=[
            jax.ShapeDtypeStruct((n_seq * n_chunks * lr, w_b), BF16),
            jax.ShapeDtypeStruct((n_seq, n_heads, HEAD_DIM_B, HEAD_DIM_B), F32),
        ],
        scratch_shapes=[
            pltpu.VMEM((hg, HEAD_DIM_B, HEAD_DIM_B), F32),
            pltpu.VMEM((8 + GDN_CHUNK, gw), F32),
            pltpu.VMEM((8 + GDN_CHUNK, gw), F32),
            pltpu.VMEM((8 + GDN_CHUNK, gw), F32),
        ],
        compiler_params=pltpu.CompilerParams(
            dimension_semantics=("arbitrary", "arbitrary", "arbitrary"), vmem_limit_bytes=VMEM_LIMIT),
        name=f"gdn_l{lr}",
    )(o32, o32, o32, o32, ba, conv_w, conv_w, conv_w, gate_par, norm_b, s0, halo0, halo0, halo0)


def _out_proj_kernel(ya_ref, yb_ref, w_ref, h_ref, pn_ref, o_ref, *, w_a):
    j = pl.program_id(1)
    tn = w_ref.shape[1]
    r = (jnp.dot(ya_ref[...], w_ref[0:w_a, :], preferred_element_type=F32)
         + jnp.dot(yb_ref[...], w_ref[w_a:, :], preferred_element_type=F32))
    o_ref[:, pl.ds(pl.multiple_of(j * tn, tn), tn)] = r

    @pl.when(j == pl.num_programs(1) - 1)
    def _():
        rows = 64

        def body(i, carry):
            sl = pl.ds(pl.multiple_of(i * rows, rows), rows)
            y = o_ref[sl, :]
            ms = jnp.mean(y * y, axis=-1, keepdims=True)
            o_ref[sl, :] = h_ref[sl, :] + y * lax.rsqrt(ms + EPS) * pn_ref[...]
            return carry
        lax.fori_loop(0, o_ref.shape[0] // rows, body, 0)


def _out_proj(ya, yb, w_out, h, post_norm):
    m, d = h.shape
    w_a = ya.shape[1]
    tm, tn = min(OUT_TM, m), OUT_TN
    kern = functools.partial(_out_proj_kernel, w_a=w_a)
    return pl.pallas_call(
        kern,
        grid=(m // tm, d // tn),
        in_specs=[
            pl.BlockSpec((tm, w_a), lambda i, j: (i, 0)),
            pl.BlockSpec((tm, yb.shape[1]), lambda i, j: (i, 0)),
            pl.BlockSpec((d, tn), lambda i, j: (0, j)),
            pl.BlockSpec((tm, d), lambda i, j: (i, 0)),
            pl.BlockSpec((1, d), lambda i, j: (0, 0)),
        ],
        out_specs=pl.BlockSpec((tm, d), lambda i, j: (i, 0)),
        out_shape=jax.ShapeDtypeStruct((m, d), F32),
        compiler_params=pltpu.CompilerParams(
            dimension_semantics=("arbitrary", "arbitrary"), vmem_limit_bytes=VMEM_LIMIT),
        name="out_proj",
    )(ya, yb, w_out, h, post_norm)


def _kv_layout_kernel(k_ref, v_ref, ko_ref, vo_ref, *, n_heads):
    tm = k_ref.shape[0]
    for p in range(2 * n_heads):
        piece = slice(p * LANES, (p + 1) * LANES)
        ko_ref[pl.ds(p, tm, stride=2 * n_heads), :] = k_ref[:, piece]
        vo_ref[pl.ds((p % 2) * n_heads + p // 2, tm, stride=2 * n_heads), :] = v_ref[:, piece]


def _kv_layout(p32, n_heads, w_a, tok_off):
    seq = p32.shape[0]
    tm = PROJ_TM
    rpt = 2 * n_heads
    spec_out = pl.BlockSpec((pl.Element(tm * rpt), pl.Element(LANES)),
                            lambda i: ((tok_off + i * tm) * rpt, 0))
    shape_out = jax.ShapeDtypeStruct(((tok_off + seq) * rpt, LANES), F32)
    return pl.pallas_call(
        functools.partial(_kv_layout_kernel, n_heads=n_heads),
        grid=(seq // tm,),
        in_specs=[
            pl.BlockSpec((tm, w_a), lambda i: (i, 0)),
            pl.BlockSpec((tm, w_a), lambda i: (i, 1)),
        ],
        out_specs=[spec_out, spec_out],
        out_shape=[shape_out, shape_out],
        compiler_params=pltpu.CompilerParams(dimension_semantics=("arbitrary",), vmem_limit_bytes=VMEM_LIMIT),
        name="kv_layout",
    )(p32, p32)


def kernel(x_prompt, x_sample, cache_k_a, cache_v_a, state_ssm_b, state_conv_b, meta_tokens, rel_bias, pre_norm,
           w_in, lambda_q1, lambda_k1, lambda_q2, lambda_k2, subln_a, conv_b, a_log_b, dt_bias_b, norm_b, w_out,
           post_norm):
    batch, seq, d_model = x_prompt.shape
    dec_b, dec_seq, _ = x_sample.shape
    depth = w_in.shape[0]
    assert batch == 1 and depth == 1
    n_heads_a = rel_bias.shape[1]
    w_a = n_heads_a * 2 * HEAD_DIM_A
    w_b = d_model - w_a
    n_heads_b = w_b // HEAD_DIM_B
    n_cache = cache_k_a.shape[2]
    past = n_cache - N_META
    n_ca = (n_cache // LANES) * LANES
    n_main = 4 * w_a + 4 * w_b
    lam_init = _lambda_init(0)
    hg = min(GDN_HEADS_PER_STEP, n_heads_b)
    n_groups = n_heads_b // hg
    n_dec = dec_b * dec_seq
    assert seq % PROJ_TM == 0 and seq % GDN_CHUNK == 0 and n_cache - n_ca + dec_seq <= LANES

    w_in_bf = w_in[0].astype(BF16)
    wb = w_in_bf[:, n_main:n_main + n_heads_b].reshape(d_model, n_groups, hg)
    wa = w_in_bf[:, n_main + n_heads_b:].reshape(d_model, n_groups, hg)
    w_tail = jnp.concatenate([wb, wa, jnp.zeros((d_model, n_groups, LANES - 2 * hg), BF16)], axis=-1)
    w_tail = w_tail.reshape(d_model, n_groups * LANES)
    w_out_bf = w_out[0].astype(BF16)
    gate_par = jnp.zeros((n_groups, 2, LANES), F32)
    gate_par = gate_par.at[:, 0, hg:2 * hg].set(a_log_b[0].reshape(n_groups, hg))
    gate_par = gate_par.at[:, 1, hg:2 * hg].set(dt_bias_b[0].reshape(n_groups, hg))
    lam_params = jnp.stack([lambda_q1[0], lambda_k1[0], lambda_q2[0], lambda_k2[0]])

    xp = x_prompt[0]
    n_small = -(-(n_dec + N_META) // 16) * 16
    xs = jnp.concatenate([x_sample.reshape(n_dec, d_model), meta_tokens.astype(F32),
                          jnp.zeros((n_small - n_dec - N_META, d_model), F32)], axis=0)
    p32, p16, pba = _in_proj(xp, pre_norm, w_in_bf, w_tail, w_a, n_main)
    s32, s16, sba = _in_proj(xs, pre_norm, w_in_bf, w_tail, w_a, n_main, single_tile=True)

    fb, mb, sa, se, lam_t = _bias_tables(rel_bias, lam_params, past, dec_seq, n_ca, lam_init)
    lam = lam_t[0, 0, 0:1]

    meta16 = s16[n_dec:n_dec + N_META]
    pad_m = jnp.zeros((LANES - N_META, w_a), BF16)
    km = jnp.concatenate([meta16[:, w_a:2 * w_a], pad_m], axis=0)
    vm = jnp.concatenate([meta16[:, 2 * w_a:], pad_m], axis=0)
    ya_p = _attn_prompt(lam, p16, p32, km, vm, fb, mb, subln_a, n_heads_a, lam_init)

    ck = cache_k_a[0].reshape(dec_b, n_cache * n_heads_a * 2, HEAD_DIM_A)
    cv = cache_v_a[0].reshape(dec_b, n_cache, n_heads_a, 2, HEAD_DIM_A).transpose(0, 1, 3, 2, 4)
    cv = cv.reshape(dec_b, n_cache * 2 * n_heads_a, HEAD_DIM_A)
    n_extra = n_cache - n_ca + dec_seq
    pad_e = jnp.zeros((dec_b, LANES - n_extra, w_a), BF16)
    ck_tail = cache_k_a[0, :, n_ca:].reshape(dec_b, n_cache - n_ca, w_a).astype(BF16)
    cv_tail = cache_v_a[0, :, n_ca:].reshape(dec_b, n_cache - n_ca, w_a).astype(BF16)
    ke = jnp.concatenate([ck_tail, s16[:n_dec, w_a:2 * w_a].reshape(dec_b, dec_seq, w_a), pad_e], axis=1)
    ve = jnp.concatenate([cv_tail, s16[:n_dec, 2 * w_a:].reshape(dec_b, dec_seq, w_a), pad_e], axis=1)
    ya_s = _attn_sample(lam, s16, s32, ck, cv, ke, ve, sa, se, subln_a, n_heads_a, dec_b, dec_seq, n_ca,
                        lam_init)

    col0 = 3 * w_a
    qkv_cols = slice(col0, col0 + 3 * w_b)
    zero_halo = jnp.zeros((1, 8, 3 * w_b), F32)
    gdn = functools.partial(_gdn, conv_w=conv_b[0], gate_par=gate_par, norm_b=norm_b, w_b=w_b, col0=col0)
    _, s_meta = gdn(s32, sba, s0=jnp.zeros((1, n_heads_b, HEAD_DIM_B, HEAD_DIM_B), F32), halo0=zero_halo,
                    n_seq=1, n_chunks=1, lr=N_META, row_blk_off=n_dec // N_META)
    meta_halo = jnp.concatenate([jnp.zeros((8 - (CONV_W - 1), 3 * w_b), F32),
                                 s32[n_dec + N_META - (CONV_W - 1):n_dec + N_META, qkv_cols]], axis=0)[None]
    yb_p, ssm_p = gdn(p32, pba, s0=s_meta, halo0=meta_halo, n_seq=1, n_chunks=seq // GDN_CHUNK, lr=GDN_CHUNK,
                      row_blk_off=0)
    samp_halo = jnp.concatenate([jnp.zeros((dec_b, 8 - (CONV_W - 1), 3 * w_b), F32), state_conv_b[0]], axis=1)
    yb_s, ssm_s = gdn(s32, sba, s0=state_ssm_b[0].astype(F32), halo0=samp_halo, n_seq=dec_b, n_chunks=1,
                      lr=dec_seq, row_blk_off=0)

    y_p = _out_proj(ya_p, yb_p, w_out_bf, xp, post_norm)
    y_s = _out_proj(ya_s, yb_s, w_out_bf, x_sample.reshape(n_dec, d_model), post_norm)

    hd = HEAD_DIM_A
    k_p, v_p = _kv_layout(p32, n_heads_a, w_a, N_META)
    meta32 = s32[n_dec:n_dec + N_META]
    k_p = lax.dynamic_update_slice(k_p, meta32[:, :w_a].reshape(N_META * 2 * n_heads_a, hd), (0, 0))
    v_meta = meta32[:, w_a:2 * w_a].reshape(N_META, n_heads_a, 2, hd).transpose(0, 2, 1, 3)
    v_p = lax.dynamic_update_slice(v_p, v_meta.reshape(N_META * 2 * n_heads_a, hd), (0, 0))
    v_p = v_p.reshape(N_META + seq, 2, n_heads_a, hd).transpose(0, 2, 1, 3)
    conv_p = p32[seq - (CONV_W - 1):, qkv_cols]
    k_s = s32[:n_dec, :w_a]
    v_s = s32[:n_dec, w_a:2 * w_a]
    conv_s = s32[:n_dec, qkv_cols].reshape(dec_b, dec_seq, 3 * w_b)[:, dec_seq - (CONV_W - 1):]
    return (
        y_p[None],
        y_s.reshape(dec_b, dec_seq, d_model),
        k_p.reshape(1, 1, N_META + seq, n_heads_a, 2, hd),
        v_p.reshape(1, 1, N_META + seq, n_heads_a, 2 * hd),
        ssm_p[None],
        conv_p[None, None],
        k_s.reshape(1, dec_b, dec_seq, n_heads_a, 2, hd),
        v_s.reshape(1, dec_b, dec_seq, n_heads_a, 2 * hd),
        ssm_s[None],
        conv_s[None],
    )
```

```python
import functools
import math

import jax
import jax.numpy as jnp
from jax import lax
from jax.experimental import pallas as pl
from jax.experimental.pallas import tpu as pltpu

F32 = jnp.float32
BF16 = jnp.bfloat16
HIGHEST = lax.Precision.HIGHEST

EPS = 1e-6
CHUNK = 64
N_META = 16
HEAD_DIM_A = 128
HEAD_DIM_B = 128
CONV_W = 4
N_BUCKETS = 32
MAX_DISTANCE = 1024
NEG = -1e30
LOG2E = math.log2(math.e)

LANES = 128
ATT_BLOCK = 256
FAR_BLOCKS = 8
GDN_CHUNK = 128
GDN_HEADS_PER_STEP = 16
GDN_INTERLEAVE = 16
PROJ_TM = 1024
PROJ_TN = 512
KV_TM = 512
PROJ_TN_SINGLE = 512
OUT_TM = 512
OUT_TN = 1024
VMEM_LIMIT = 56 * 1024 * 1024


def _lambda_init(layer):
    return 0.8 - 0.6 * math.exp(-0.3 * layer)


def _bias_saturation_distance():
    nb = N_BUCKETS // 2
    max_exact = nb // 2
    return int(math.ceil(max_exact * (MAX_DISTANCE / max_exact) ** ((nb - max_exact - 1) / (nb - max_exact)))) + 1


N_NEAR = -(-(_bias_saturation_distance() + ATT_BLOCK) // ATT_BLOCK) + 1
FAR_BUCKET = N_BUCKETS // 2 - 1


def _silu(x):
    return x * jax.nn.sigmoid(x)


def _in_proj_kernel(x_ref, pn_ref, w_ref, wt_ref, o32_ref, o16_ref, ba_ref, xn_ref, *, q_tiles, bf_tiles,
                    q_scale):
    j = pl.program_id(1)
    tm = x_ref.shape[0]
    rows = math.gcd(tm, 64) if tm % 64 == 0 else math.gcd(tm, 80)

    @pl.when(j == 0)
    def _():
        def body(r, carry):
            sl = pl.ds(pl.multiple_of(r * rows, 8), rows)
            x = x_ref[sl, :]
            ms = jnp.mean(x * x, axis=-1, keepdims=True)
            xn_ref[sl, :] = (x * lax.rsqrt(ms + EPS) * pn_ref[...]).astype(BF16)
            return carry
        lax.fori_loop(0, tm // rows, body, 0)
        ba_ref[...] = jnp.dot(xn_ref[...], wt_ref[...], preferred_element_type=F32)

    r = jnp.dot(xn_ref[...], w_ref[...], preferred_element_type=F32)

    @pl.when(j >= q_tiles)
    def _():
        o32_ref[...] = r

    @pl.when(j < q_tiles)
    def _():
        o16_ref[...] = (r * q_scale).astype(BF16)

    @pl.when(jnp.logical_and(j >= q_tiles, j < bf_tiles))
    def _():
        o16_ref[...] = r.astype(BF16)


def _in_proj(x, pre_norm, w_all, w_tail, w_a, n, single_tile=False):
    m, d = x.shape
    tm, tn = PROJ_TM, min(PROJ_TN, w_a)
    x_mode = dict(pipeline_mode=pl.Buffered(1))
    if single_tile:
        tm, tn = m, min(PROJ_TN_SINGLE, w_a)
    q_tiles = w_a // tn
    bf_tiles = 3 * w_a // tn
    nt = w_tail.shape[1]
    kern = functools.partial(_in_proj_kernel, q_tiles=q_tiles, bf_tiles=bf_tiles, q_scale=HEAD_DIM_A ** -0.5 * LOG2E)
    return pl.pallas_call(
        kern,
        grid=(m // tm, n // tn),
        in_specs=[
            pl.BlockSpec((tm, d), lambda i, j: (i, 0), **x_mode),
            pl.BlockSpec((1, d), lambda i, j: (0, 0)),
            pl.BlockSpec((d, tn), lambda i, j: (0, j)),
            pl.BlockSpec((d, nt), lambda i, j: (0, 0)),
        ],
        out_specs=[
            pl.BlockSpec((tm, tn), lambda i, j: (i, jnp.maximum(j - q_tiles, 0))),
            pl.BlockSpec((tm, tn), lambda i, j: (i, jnp.minimum(j, bf_tiles - 1))),
            pl.BlockSpec((tm, nt), lambda i, j: (i, 0)),
        ],
        out_shape=[
            jax.ShapeDtypeStruct((m, n - w_a), F32),
            jax.ShapeDtypeStruct((m, 3 * w_a), BF16),
            jax.ShapeDtypeStruct((m, nt), F32),
        ],
        scratch_shapes=[pltpu.VMEM((tm, d), BF16)],
        compiler_params=pltpu.CompilerParams(
            dimension_semantics=("arbitrary", "arbitrary"), vmem_limit_bytes=VMEM_LIMIT),
        name="in_proj",
    )(x, pre_norm, w_all, w_tail)


def _bias_values(rb_ref, h, rel):
    nb = N_BUCKETS // 2
    max_exact = nb // 2
    n = jnp.abs(rel)
    nf = jnp.maximum(n, 1).astype(F32)
    large = max_exact + (jnp.log(nf / max_exact) / math.log(MAX_DISTANCE / max_exact)
                         * (nb - max_exact)).astype(jnp.int32)
    large = jnp.minimum(large, nb - 1)
    bucket = jnp.where(rel > 0, nb, 0) + jnp.where(n < max_exact, n, large)
    val = jnp.zeros(rel.shape, F32)
    for b in range(N_BUCKETS):
        val = jnp.where(bucket == b, rb_ref[b, h], val)
    return (val - rb_ref[FAR_BUCKET, h]) * LOG2E


def _bias_kernel(rb_ref, lamp_ref, fb_ref, mb_ref, sa_ref, se_ref, lam_ref, *, past, dec_seq, n_ca, lam_init):
    h = pl.program_id(0)
    t = ATT_BLOCK
    shift = CHUNK.bit_length() - 1
    i = lax.broadcasted_iota(jnp.int32, (t, t), 0)
    j = lax.broadcasted_iota(jnp.int32, (t, t), 1)
    for dd in range(N_NEAR):
        val = _bias_values(rb_ref, h, (j - i) - t * dd)
        if dd == 0:
            val = jnp.where((j >> shift) <= (i >> shift), val, NEG)
        fb_ref[0, dd] = val
    i = lax.broadcasted_iota(jnp.int32, (t, LANES), 0)
    j = lax.broadcasted_iota(jnp.int32, (t, LANES), 1)
    for qb in range(N_NEAR):
        val = _bias_values(rb_ref, h, (j - N_META) - (t * qb + i))
        mb_ref[0, qb] = jnp.where(j < N_META, val, NEG)
    i = lax.broadcasted_iota(jnp.int32, (dec_seq, n_ca), 0)
    j = lax.broadcasted_iota(jnp.int32, (dec_seq, n_ca), 1)
    sa_ref[0] = _bias_values(rb_ref, h, (j - N_META) - (past + i))
    i = lax.broadcasted_iota(jnp.int32, (dec_seq, LANES), 0)
    j = lax.broadcasted_iota(jnp.int32, (dec_seq, LANES), 1)
    n_extra = N_META + past - n_ca + dec_seq
    val = _bias_values(rb_ref, h, (n_ca - N_META + j) - (past + i))
    se_ref[0] = jnp.where(j < n_extra, val, NEG)
    lp = lamp_ref[...]
    s1 = jnp.sum(lp[0:1] * lp[1:2], axis=-1, keepdims=True)
    s2 = jnp.sum(lp[2:3] * lp[3:4], axis=-1, keepdims=True)
    lam_ref[0] = jnp.broadcast_to(jnp.exp(s1) - jnp.exp(s2) + lam_init, (8, LANES))


def _bias_tables(rel_bias, lam_params, past, dec_seq, n_ca, lam_init):
    nh = rel_bias.shape[1]
    t = ATT_BLOCK
    kern = functools.partial(_bias_kernel, past=past, dec_seq=dec_seq, n_ca=n_ca, lam_init=lam_init)
    return pl.pallas_call(
        kern,
        grid=(nh,),
        in_specs=[
            pl.BlockSpec(memory_space=pltpu.SMEM),
            pl.BlockSpec((4, HEAD_DIM_A), lambda h: (0, 0)),
        ],
        out_specs=[
            pl.BlockSpec((1, N_NEAR, t, t), lambda h: (h, 0, 0, 0)),
            pl.BlockSpec((1, N_NEAR, t, LANES), lambda h: (h, 0, 0, 0)),
            pl.BlockSpec((1, dec_seq, n_ca), lambda h: (h, 0, 0)),
            pl.BlockSpec((1, dec_seq, LANES), lambda h: (h, 0, 0)),
            pl.BlockSpec((1, 8, LANES), lambda h: (h, 0, 0)),
        ],
        out_shape=[
            jax.ShapeDtypeStruct((nh, N_NEAR, t, t), F32),
            jax.ShapeDtypeStruct((nh, N_NEAR, t, LANES), F32),
            jax.ShapeDtypeStruct((nh, dec_seq, n_ca), F32),
            jax.ShapeDtypeStruct((nh, dec_seq, LANES), F32),
            jax.ShapeDtypeStruct((nh, 8, LANES), F32),
        ],
        compiler_params=pltpu.CompilerParams(dimension_semantics=("arbitrary",)),
        name="bias_tables",
    )(rel_bias, lam_params)


def _nt_dot(a, b):
    return lax.dot_general(a, b, (((1,), (1,)), ((), ())), preferred_element_type=F32)


def _attn_finish(o, z, subln, lam_init):
    ms = jnp.mean(o * o, axis=-1, keepdims=True)
    on = o * lax.rsqrt(ms + EPS) * subln * (1.0 - lam_init)
    return on * _silu(z)


def _attn_prompt_kernel(lam_ref, q_ref, k_ref, v_ref, km_ref, vm_ref, fb_ref, mb_ref, z_ref, sub_ref, o_ref,
                        m_ref, l_ref, acc_ref, sa_ref, sb_ref, *, lam_init):
    qi = pl.program_id(1)
    t = ATT_BLOCK
    d = HEAD_DIM_A
    hw = 2 * d

    def rep(x, n):
        return jnp.concatenate([x] * n, axis=1) if n > 1 else x

    def lane_fold(p):
        out = p[:, 0:LANES]
        for g in range(1, p.shape[1] // LANES):
            out = out + p[:, g * LANES:(g + 1) * LANES]
        return out

    vm = vm_ref[...]
    mb = mb_ref[0, jnp.minimum(qi, N_NEAR - 1)]
    for c in range(2):
        s = _nt_dot(q_ref[:, c * d:(c + 1) * d], km_ref[:, c * d:(c + 1) * d]) + mb
        m = jnp.broadcast_to(jnp.max(s, axis=-1, keepdims=True), (t, LANES))
        p = jnp.exp2(s - m)
        m_ref[c] = m
        l_ref[c] = p
        acc_ref[c] = jnp.dot(p.astype(BF16), vm, preferred_element_type=F32)

    def qk(start, width):
        rows = pl.ds(start, width)
        return [_nt_dot(q_ref[:, c * d:(c + 1) * d], k_ref[rows, c * d:(c + 1) * d]) for c in range(2)]

    def softmax_pv(s_both, start, width, bias):
        v = v_ref[pl.ds(start, width), :]
        for c in range(2):
            s = s_both[c]
            if bias is not None:
                s = s + bias
            m_old = m_ref[c]
            m_new = jnp.maximum(m_old, jnp.max(s, axis=-1, keepdims=True))
            p = jnp.exp2(s - rep(m_new, width // LANES))
            alpha = jnp.exp2(m_old - m_new)
            m_ref[c] = m_new
            l_ref[c] = alpha * l_ref[c] + lane_fold(p)
            acc_ref[c] = rep(alpha, hw // LANES) * acc_ref[c] + jnp.dot(
                p.astype(BF16), v, preferred_element_type=F32)

    wide = FAR_BLOCKS * t
    far_shift = FAR_BLOCKS.bit_length() - 1
    n_far = jnp.maximum(qi - (N_NEAR - 1), 0)
    n_wide = n_far >> far_shift

    def put(ref, s_both):
        ref[0] = s_both[0]
        ref[1] = s_both[1]

    @pl.when(n_wide > 0)
    def _():
        put(sa_ref, qk(0, wide))

    def far_body(i, carry):
        b0 = pl.multiple_of(i * (2 * wide), wide)
        b1 = pl.multiple_of(b0 + wide, wide)
        b2 = pl.multiple_of(jnp.minimum(b0 + 2 * wide, (n_wide - 1) * wide), wide)
        put(sb_ref, qk(b1, wide))
        softmax_pv([sa_ref[0], sa_ref[1]], b0, wide, None)
        put(sa_ref, qk(b2, wide))
        softmax_pv([sb_ref[0], sb_ref[1]], b1, wide, None)
        return carry

    def far_body2(i, carry):
        far_body(2 * i, carry)
        return far_body(2 * i + 1, carry)
    lax.fori_loop(0, n_wide >> 2, far_body2, 0)
    lax.fori_loop((n_wide >> 2) * 2, n_wide >> 1, far_body, 0)

    @pl.when((n_wide & 1) == 1)
    def _():
        softmax_pv([sa_ref[0], sa_ref[1]], pl.multiple_of((n_wide - 1) * wide, wide), wide, None)

    n_rest = n_far - (n_wide << far_shift)
    for cnt in range(1, FAR_BLOCKS):
        @pl.when(n_rest == cnt)
        def _(cnt=cnt):
            first = n_wide << far_shift
            s_next = qk(pl.multiple_of(first * t, t), t)
            for b in range(cnt):
                s_cur = s_next
                if b + 1 < cnt:
                    s_next = qk(pl.multiple_of((first + b + 1) * t, t), t)
                softmax_pv(s_cur, pl.multiple_of((first + b) * t, t), t, None)

    @pl.when(qi >= N_NEAR - 1)
    def _():
        s_next = qk(pl.multiple_of((qi - (N_NEAR - 1)) * t, t), t)
        for dd in range(N_NEAR - 1, -1, -1):
            s_cur = s_next
            if dd > 0:
                s_next = qk(pl.multiple_of((qi - dd + 1) * t, t), t)
            softmax_pv(s_cur, pl.multiple_of((qi - dd) * t, t), t, fb_ref[0, dd])

    @pl.when(qi < N_NEAR - 1)
    def _():
        for dd in range(N_NEAR - 2, -1, -1):
            @pl.when(qi - dd >= 0)
            def _(dd=dd):
                start = pl.multiple_of((qi - dd) * t, t)
                softmax_pv(qk(start, t), start, t, fb_ref[0, dd])

    inv0 = 1.0 / jnp.sum(l_ref[0], axis=-1, keepdims=True)
    inv1 = lam_ref[0] / jnp.sum(l_ref[1], axis=-1, keepdims=True)
    o = acc_ref[0] * inv0 - acc_ref[1] * inv1
    o_ref[...] = _attn_finish(o, z_ref[...], sub_ref[...], lam_init).astype(BF16)


def _attn_prompt(lam, qkv16, o32, km, vm, fb, mb, subln, n_heads, lam_init):
    seq = qkv16.shape[0]
    t = ATT_BLOCK
    hw = 2 * HEAD_DIM_A
    kern = functools.partial(_attn_prompt_kernel, lam_init=lam_init)
    return pl.pallas_call(
        kern,
        grid=(n_heads, seq // t),
        in_specs=[
            pl.BlockSpec(memory_space=pltpu.SMEM),
            pl.BlockSpec((t, hw), lambda h, i: (i, h)),
            pl.BlockSpec((seq, hw), lambda h, i: (0, n_heads + h)),
            pl.BlockSpec((seq, hw), lambda h, i: (0, 2 * n_heads + h)),
            pl.BlockSpec((LANES, hw), lambda h, i: (0, h)),
            pl.BlockSpec((LANES, hw), lambda h, i: (0, h)),
            pl.BlockSpec((1, N_NEAR, t, t), lambda h, i: (h, 0, 0, 0)),
            pl.BlockSpec((1, N_NEAR, t, LANES), lambda h, i: (h, 0, 0, 0)),
            pl.BlockSpec((t, hw), lambda h, i: (i, 2 * n_heads + h)),
            pl.BlockSpec((1, hw), lambda h, i: (0, 0)),
        ],
        out_specs=pl.BlockSpec((t, hw), lambda h, i: (i, h)),
        out_shape=jax.ShapeDtypeStruct((seq, n_heads * hw), BF16),
        scratch_shapes=[
            pltpu.VMEM((2, t, LANES), F32),
            pltpu.VMEM((2, t, LANES), F32),
            pltpu.VMEM((2, t, hw), F32),
            pltpu.VMEM((2, t, FAR_BLOCKS * t), F32),
            pltpu.VMEM((2, t, FAR_BLOCKS * t), F32),
        ],
        compiler_params=pltpu.CompilerParams(
            dimension_semantics=("arbitrary", "arbitrary"), vmem_limit_bytes=VMEM_LIMIT),
        name="attn_prompt",
    )(lam, qkv16, qkv16, qkv16, km, vm, fb, mb, o32, subln)


def _attn_sample_kernel(lam_ref, q_ref, kc_ref, vc_ref, ke_ref, ve_ref, sa_ref, se_ref, z_ref, sub_ref, o_ref,
                        *, lam_init, n_heads, n_ca):
    h = pl.program_id(1)
    d = HEAD_DIM_A
    vc = jnp.concatenate([vc_ref[0, pl.ds(c * n_heads + h, n_ca, stride=2 * n_heads), :] for c in range(2)],
                         axis=1).astype(BF16)
    ve = ve_ref[0]
    outs = []
    for c in range(2):
        q = q_ref[:, c * d:(c + 1) * d]
        kc = kc_ref[0, pl.ds(2 * h + c, n_ca, stride=2 * n_heads), :].astype(BF16)
        s_a = _nt_dot(q, kc) + sa_ref[0]
        s_e = _nt_dot(q, ke_ref[0, :, c * d:(c + 1) * d]) + se_ref[0]
        m = jnp.maximum(jnp.max(s_a, axis=-1, keepdims=True), jnp.max(s_e, axis=-1, keepdims=True))
        p_a = jnp.exp2(s_a - m)
        p_e = jnp.exp2(s_e - m)
        l = jnp.sum(p_a, axis=-1, keepdims=True) + jnp.sum(p_e, axis=-1, keepdims=True)
        acc = (jnp.dot(p_a.astype(BF16), vc, preferred_element_type=F32)
               + jnp.dot(p_e.astype(BF16), ve, preferred_element_type=F32))
        outs.append(acc / l)
    o = outs[0] - lam_ref[0] * outs[1]
    o_ref[...] = _attn_finish(o, z_ref[...], sub_ref[...], lam_init).astype(BF16)


def _attn_sample(lam, qkv16, o32, cache_k, cache_v, ke, ve, sa, se, subln, n_heads, dec_b, dec_seq, n_ca,
                 lam_init):
    hw = 2 * HEAD_DIM_A
    kern = functools.partial(_attn_sample_kernel, lam_init=lam_init, n_heads=n_heads, n_ca=n_ca)
    return pl.pallas_call(
        kern,
        grid=(dec_b, n_heads),
        in_specs=[
            pl.BlockSpec(memory_space=pltpu.SMEM),
            pl.BlockSpec((dec_seq, hw), lambda b, h: (b, h)),
            pl.BlockSpec((1, n_ca * n_heads * 2, HEAD_DIM_A), lambda b, h: (b, 0, 0)),
            pl.BlockSpec((1, n_ca * n_heads * 2, HEAD_DIM_A), lambda b, h: (b, 0, 0)),
            pl.BlockSpec((1, LANES, hw), lambda b, h: (b, 0, h)),
            pl.BlockSpec((1, LANES, hw), lambda b, h: (b, 0, h)),
            pl.BlockSpec((1, dec_seq, n_ca), lambda b, h: (h, 0, 0)),
            pl.BlockSpec((1, dec_seq, LANES), lambda b, h: (h, 0, 0)),
            pl.BlockSpec((dec_seq, hw), lambda b, h: (b, 2 * n_heads + h)),
            pl.BlockSpec((1, hw), lambda b, h: (0, 0)),
        ],
        out_specs=pl.BlockSpec((dec_seq, hw), lambda b, h: (b, h)),
        out_shape=jax.ShapeDtypeStruct((dec_b * dec_seq, n_heads * hw), BF16),
        compiler_params=pltpu.CompilerParams(
            dimension_semantics=("arbitrary", "arbitrary"), vmem_limit_bytes=VMEM_LIMIT),
        name="attn_sample",
    )(lam, qkv16, cache_k, cache_v, ke, ve, sa, se, o32, subln)


def _hdot(a, b):
    return jnp.dot(a, b, preferred_element_type=F32, precision=HIGHEST)


def _bdot(a, b):
    return jnp.dot(a.astype(BF16), b.astype(BF16), preferred_element_type=F32)


def _inv_unit_lower(a_list, row, col):
    sh = 4
    n = a_list[0].shape[0]
    eye = (row == col).astype(F32)
    same = (row >> sh) == (col >> sh)
    dblk = [jnp.where(same, a, 0.0) for a in a_list]
    t = [eye - x for x in dblk]
    pw = [x.astype(BF16) for x in dblk]
    for _ in range(sh - 1):
        pw = [jnp.dot(x, x, preferred_element_type=F32).astype(BF16) for x in pw]
        t = [ti + jnp.dot(ti.astype(BF16), x, preferred_element_type=F32) for ti, x in zip(t, pw)]
    while (1 << sh) < n:
        offm = jnp.logical_and((row >> (sh + 1)) == (col >> (sh + 1)), (row >> sh) != (col >> sh))
        off = [jnp.where(offm, a, 0.0).astype(BF16) for a in a_list]
        tb = [ti.astype(BF16) for ti in t]
        mid = [jnp.dot(o, x, preferred_element_type=F32).astype(BF16) for o, x in zip(off, tb)]
        t = [ti - jnp.dot(x, m, preferred_element_type=F32) for ti, x, m in zip(t, tb, mid)]
        sh += 1
    return t


def _gdn_kernel(q_ref, k_ref, v_ref, z_ref, ba_ref, cwq_ref, cwk_ref, cwv_ref, gp_ref, nb_ref, s0_ref,
                hq_ref, hk_ref, hv_ref, y_ref, sout_ref, s_ref, xq_ref, xk_ref, xv_ref, *, lr, hg):
    c = pl.program_id(2)
    n = GDN_CHUNK
    dh = HEAD_DIM_B
    halo = 8

    @pl.when(c == 0)
    def _():
        s_ref[...] = s0_ref[0]
        xq_ref[0:halo, :] = hq_ref[0]
        xk_ref[0:halo, :] = hk_ref[0]
        xv_ref[0:halo, :] = hv_ref[0]

    row_w = lax.broadcasted_iota(jnp.int32, (n, hg * dh), 0)

    def conv(x_ref, xbuf, cw_ref):
        xbuf[halo:halo + lr, :] = x_ref[...]
        if lr < n:
            xbuf[halo + lr:halo + n, :] = jnp.zeros((n - lr, hg * dh), F32)
        y = xbuf[halo - 3:halo - 3 + n, :] * cw_ref[0:1, :]
        for i in range(1, CONV_W):
            y = y + xbuf[halo - 3 + i:halo - 3 + i + n, :] * cw_ref[i:i + 1, :]
        y = _silu(y)
        if lr < n:
            y = jnp.where(row_w < lr, y, 0.0)
        xbuf[0:halo, :] = xbuf[lr:lr + halo, :]
        return y

    yq = conv(q_ref, xq_ref, cwq_ref)
    yk = conv(k_ref, xk_ref, cwk_ref)
    yv = conv(v_ref, xv_ref, cwv_ref)

    row = lax.broadcasted_iota(jnp.int32, (n, n), 0)
    col = lax.broadcasted_iota(jnp.int32, (n, n), 1)
    ba = ba_ref[...]
    if lr < n:
        ba = jnp.concatenate([ba, jnp.zeros((n - lr, LANES), F32)], axis=0)
    live = row < lr
    beta = jnp.where(jnp.logical_and(live, col < hg), jax.nn.sigmoid(ba), 0.0)
    gval = -jnp.exp(gp_ref[0, 0:1, :]) * jax.nn.softplus(ba + gp_ref[0, 1:2, :])
    g = jnp.where(jnp.logical_and(live, jnp.logical_and(col >= hg, col < 2 * hg)), gval, 0.0)
    gsum = _hdot((row >= col).astype(F32), g)
    gsum_t = gsum.T
    incl = row >= col
    strict = row > col

    def run(heads):
        cols = {hh: slice(hh * dh, (hh + 1) * dh) for hh in heads}
        gc = {hh: jnp.broadcast_to(gsum[:, hg + hh:hg + hh + 1], (n, n)) for hh in heads}
        bc = {hh: jnp.broadcast_to(beta[:, hh:hh + 1], (n, n)) for hh in heads}
        gam = {hh: jnp.where(incl, jnp.exp(jnp.minimum(gc[hh] - gsum_t[hg + hh:hg + hh + 1, :], 0.0)), 0.0)
               for hh in heads}
        qh = {hh: yq[:, cols[hh]] * (lax.rsqrt(jnp.sum(yq[:, cols[hh]] * yq[:, cols[hh]], axis=-1, keepdims=True)
                                               + EPS) * (dh ** -0.5)) for hh in heads}
        kh = {hh: yk[:, cols[hh]] * lax.rsqrt(jnp.sum(yk[:, cols[hh]] * yk[:, cols[hh]], axis=-1, keepdims=True)
                                              + EPS) for hh in heads}
        qb = {hh: qh[hh].astype(BF16) for hh in heads}
        kb = {hh: kh[hh].astype(BF16) for hh in heads}
        kk = {hh: _nt_dot(kb[hh], kb[hh]) for hh in heads}
        qk = {hh: _nt_dot(qb[hh], kb[hh]) for hh in heads}
        a = [jnp.where(strict, bc[hh] * kk[hh] * gam[hh], 0.0) for hh in heads]
        t = dict(zip(heads, _inv_unit_lower(a, row, col)))
        eg = {hh: jnp.exp(gc[hh]) for hh in heads}
        rhs = {hh: jnp.concatenate([yv[:, cols[hh]] * bc[hh], kh[hh] * (bc[hh] * eg[hh])], axis=1).astype(BF16)
               for hh in heads}
        uw = {hh: jnp.dot(t[hh].astype(BF16), rhs[hh], preferred_element_type=F32) for hh in heads}
        s = {hh: s_ref[hh] for hh in heads}
        sb = {hh: s[hh].astype(BF16) for hh in heads}
        v_new = {hh: uw[hh][:, :dh] - jnp.dot(uw[hh][:, dh:].astype(BF16), sb[hh], preferred_element_type=F32)
                 for hh in heads}
        vnb = {hh: v_new[hh].astype(BF16) for hh in heads}
        o = {hh: eg[hh] * jnp.dot(qb[hh], sb[hh], preferred_element_type=F32)
             + jnp.dot((qk[hh] * gam[hh]).astype(BF16), vnb[hh], preferred_element_type=F32) for hh in heads}
        g_last = {hh: gc[hh][n - 1:n, :] for hh in heads}
        kd = {hh: (kh[hh] * jnp.exp(g_last[hh] - gc[hh])).astype(BF16) for hh in heads}
        for hh in heads:
            s_ref[hh] = s[hh] * jnp.exp(g_last[hh]) + lax.dot_general(
                kd[hh], vnb[hh], (((0,), (0,)), ((), ())), preferred_element_type=F32)
        for hh in heads:
            on = o[hh] * lax.rsqrt(jnp.mean(o[hh] * o[hh], axis=-1, keepdims=True) + EPS) * nb_ref[...]
            y_ref[:, cols[hh]] = (on[0:lr] * _silu(z_ref[:, cols[hh]])).astype(BF16)

    for first in range(0, hg, GDN_INTERLEAVE):
        run(list(range(first, min(first + GDN_INTERLEAVE, hg))))

    @pl.when(c == pl.num_programs(2) - 1)
    def _():
        sout_ref[0] = s_ref[...]


def _gdn(o32, ba, conv_w, gate_par, norm_b, s0, halo0, *, n_seq, n_chunks, lr, row_blk_off, w_b, col0):
    hg = min(GDN_HEADS_PER_STEP, w_b // HEAD_DIM_B)
    gw = hg * HEAD_DIM_B
    n_groups = w_b // gw
    n_heads = w_b // HEAD_DIM_B
    cb = col0 // gw
    wb = w_b // gw

    def rmap(off):
        return lambda s, g, c: (row_blk_off + s * n_chunks + c, off + g)

    kern = functools.partial(_gdn_kernel, lr=lr, hg=hg)
    return pl.pallas_call(
        kern,
        grid=(n_seq, n_groups, n_chunks),
        in_specs=[
            pl.BlockSpec((lr, gw), rmap(cb)),
            pl.BlockSpec((lr, gw), rmap(cb + wb)),
            pl.BlockSpec((lr, gw), rmap(cb + 2 * wb)),
            pl.BlockSpec((lr, gw), rmap(cb + 3 * wb)),
            pl.BlockSpec((lr, LANES), rmap(0)),
            pl.BlockSpec((CONV_W, gw), lambda s, g, c: (0, g)),
            pl.BlockSpec((CONV_W, gw), lambda s, g, c: (0, wb + g)),
            pl.BlockSpec((CONV_W, gw), lambda s, g, c: (0, 2 * wb + g)),
            pl.BlockSpec((1, 2, LANES), lambda s, g, c: (g, 0, 0)),
            pl.BlockSpec((1, HEAD_DIM_B), lambda s, g, c: (0, 0)),
            pl.BlockSpec((1, hg, HEAD_DIM_B, HEAD_DIM_B), lambda s, g, c: (s, g, 0, 0)),
            pl.BlockSpec((1, 8, gw), lambda s, g, c: (s, 0, g)),
            pl.BlockSpec((1, 8, gw), lambda s, g, c: (s, 0, wb + g)),
            pl.BlockSpec((1, 8, gw), lambda s, g, c: (s, 0, 2 * wb + g)),
        ],
        out_specs=[
            pl.BlockSpec((lr, gw), lambda s, g, c: (s * n_chunks + c, g)),
            pl.BlockSpec((1, hg, HEAD_DIM_B, HEAD_DIM_B), lambda s, g, c: (s, g, 0, 0)),
        ],
        out_shape=[
            jax.ShapeDtypeStruct((n_seq * n_chunks * lr, w_b), BF16),
            jax.ShapeDtypeStruct((n_seq, n_heads, HEAD_DIM_B, HEAD_DIM_B), F32),
        ],
        scratch_shapes=[
            pltpu.VMEM((hg, HEAD_DIM_B, HEAD_DIM_B), F32),
            pltpu.VMEM((8 + GDN_CHUNK, gw), F32),
            pltpu.VMEM((8 + GDN_CHUNK, gw), F32),
            pltpu.VMEM((8 + GDN_CHUNK, gw), F32),
        ],
        compiler_params=pltpu.CompilerParams(
            dimension_semantics=("arbitrary", "arbitrary", "arbitrary"), vmem_limit_bytes=VMEM_LIMIT),
        name=f"gdn_l{lr}",
    )(o32, o32, o32, o32, ba, conv_w, conv_w, conv_w, gate_par, norm_b, s0, halo0, halo0, halo0)


def _out_proj_kernel(ya_ref, yb_ref, w_ref, h_ref, pn_ref, o_ref, *, w_a):
    j = pl.program_id(1)
    tn = w_ref.shape[1]
    r = (jnp.dot(ya_ref[...], w_ref[0:w_a, :], preferred_element_type=F32)
         + jnp.dot(yb_ref[...], w_ref[w_a:, :], preferred_element_type=F32))
    o_ref[:, pl.ds(pl.multiple_of(j * tn, tn), tn)] = r

    @pl.when(j == pl.num_programs(1) - 1)
    def _():
        rows = 64

        def body(i, carry):
            sl = pl.ds(pl.multiple_of(i * rows, rows), rows)
            y = o_ref[sl, :]
            ms = jnp.mean(y * y, axis=-1, keepdims=True)
            o_ref[sl, :] = h_ref[sl, :] + y * lax.rsqrt(ms + EPS) * pn_ref[...]
            return carry
        lax.fori_loop(0, o_ref.shape[0] // rows, body, 0)


def _out_proj(ya, yb, w_out, h, post_norm):
    m, d = h.shape
    w_a = ya.shape[1]
    tm, tn = min(OUT_TM, m), OUT_TN
    kern = functools.partial(_out_proj_kernel, w_a=w_a)
    return pl.pallas_call(
        kern,
        grid=(m // tm, d // tn),
        in_specs=[
            pl.BlockSpec((tm, w_a), lambda i, j: (i, 0)),
            pl.BlockSpec((tm, yb.shape[1]), lambda i, j: (i, 0)),
            pl.BlockSpec((d, tn), lambda i, j: (0, j)),
            pl.BlockSpec((tm, d), lambda i, j: (i, 0), pipeline_mode=pl.Buffered(1)),
            pl.BlockSpec((1, d), lambda i, j: (0, 0)),
        ],
        out_specs=pl.BlockSpec((tm, d), lambda i, j: (i, 0)),
        out_shape=jax.ShapeDtypeStruct((m, d), F32),
        compiler_params=pltpu.CompilerParams(
            dimension_semantics=("arbitrary", "arbitrary"), vmem_limit_bytes=VMEM_LIMIT),
        name="out_proj",
    )(ya, yb, w_out, h, post_norm)


def _kv_layout_kernel(k_ref, v_ref, ko_ref, vo_ref, *, n_heads):
    tm = k_ref.shape[0]
    for p in range(2 * n_heads):
        piece = slice(p * LANES, (p + 1) * LANES)
        ko_ref[pl.ds(p, tm, stride=2 * n_heads), :] = k_ref[:, piece]
        vo_ref[pl.ds((p % 2) * n_heads + p // 2, tm, stride=2 * n_heads), :] = v_ref[:, piece]


def _kv_layout(p32, n_heads, w_a, tok_off):
    seq = p32.shape[0]
    tm = KV_TM
    rpt = 2 * n_heads
    spec_out = pl.BlockSpec((pl.Element(tm * rpt), pl.Element(LANES)),
                            lambda i: ((tok_off + i * tm) * rpt, 0))
    shape_out = jax.ShapeDtypeStruct(((tok_off + seq) * rpt, LANES), F32)
    return pl.pallas_call(
        functools.partial(_kv_layout_kernel, n_heads=n_heads),
        grid=(seq // tm,),
        in_specs=[
            pl.BlockSpec((tm, w_a), lambda i: (i, 0)),
            pl.BlockSpec((tm, w_a), lambda i: (i, 1)),
        ],
        out_specs=[spec_out, spec_out],
        out_shape=[shape_out, shape_out],
        compiler_params=pltpu.CompilerParams(dimension_semantics=("arbitrary",), vmem_limit_bytes=VMEM_LIMIT),
        name="kv_layout",
    )(p32, p32)


def kernel(x_prompt, x_sample, cache_k_a, cache_v_a, state_ssm_b, state_conv_b, meta_tokens, rel_bias, pre_norm,
           w_in, lambda_q1, lambda_k1, lambda_q2, lambda_k2, subln_a, conv_b, a_log_b, dt_bias_b, norm_b, w_out,
           post_norm):
    batch, seq, d_model = x_prompt.shape
    dec_b, dec_seq, _ = x_sample.shape
    depth = w_in.shape[0]
    assert batch == 1 and depth == 1
    n_heads_a = rel_bias.shape[1]
    w_a = n_heads_a * 2 * HEAD_DIM_A
    w_b = d_model - w_a
    n_heads_b = w_b // HEAD_DIM_B
    n_cache = cache_k_a.shape[2]
    past = n_cache - N_META
    n_ca = (n_cache // LANES) * LANES
    n_main = 4 * w_a + 4 * w_b
    lam_init = _lambda_init(0)
    hg = min(GDN_HEADS_PER_STEP, n_heads_b)
    n_groups = n_heads_b // hg
    n_dec = dec_b * dec_seq
    assert seq % PROJ_TM == 0 and seq % GDN_CHUNK == 0 and n_cache - n_ca + dec_seq <= LANES

    w_in_bf = w_in[0].astype(BF16)
    wb = w_in_bf[:, n_main:n_main + n_heads_b].reshape(d_model, n_groups, hg)
    wa = w_in_bf[:, n_main + n_heads_b:].reshape(d_model, n_groups, hg)
    w_tail = jnp.concatenate([wb, wa, jnp.zeros((d_model, n_groups, LANES - 2 * hg), BF16)], axis=-1)
    w_tail = w_tail.reshape(d_model, n_groups * LANES)
    w_out_bf = w_out[0].astype(BF16)
    gate_par = jnp.zeros((n_groups, 2, LANES), F32)
    gate_par = gate_par.at[:, 0, hg:2 * hg].set(a_log_b[0].reshape(n_groups, hg))
    gate_par = gate_par.at[:, 1, hg:2 * hg].set(dt_bias_b[0].reshape(n_groups, hg))
    lam_params = jnp.stack([lambda_q1[0], lambda_k1[0], lambda_q2[0], lambda_k2[0]])

    xp = x_prompt[0]
    n_small = -(-(n_dec + N_META) // 16) * 16
    xs = jnp.concatenate([x_sample.reshape(n_dec, d_model), meta_tokens.astype(F32),
                          jnp.zeros((n_small - n_dec - N_META, d_model), F32)], axis=0)
    p32, p16, pba = _in_proj(xp, pre_norm, w_in_bf, w_tail, w_a, n_main)
    s32, s16, sba = _in_proj(xs, pre_norm, w_in_bf, w_tail, w_a, n_main, single_tile=True)

    fb, mb, sa, se, lam_t = _bias_tables(rel_bias, lam_params, past, dec_seq, n_ca, lam_init)
    lam = lam_t[0, 0, 0:1]

    meta16 = s16[n_dec:n_dec + N_META]
    pad_m = jnp.zeros((LANES - N_META, w_a), BF16)
    km = jnp.concatenate([meta16[:, w_a:2 * w_a], pad_m], axis=0)
    vm = jnp.concatenate([meta16[:, 2 * w_a:], pad_m], axis=0)
    ya_p = _attn_prompt(lam, p16, p32, km, vm, fb, mb, subln_a, n_heads_a, lam_init)

    ck = cache_k_a[0].reshape(dec_b, n_cache * n_heads_a * 2, HEAD_DIM_A)
    cv = cache_v_a[0].reshape(dec_b, n_cache, n_heads_a, 2, HEAD_DIM_A).transpose(0, 1, 3, 2, 4)
    cv = cv.reshape(dec_b, n_cache * 2 * n_heads_a, HEAD_DIM_A)
    n_extra = n_cache - n_ca + dec_seq
    pad_e = jnp.zeros((dec_b, LANES - n_extra, w_a), BF16)
    ck_tail = cache_k_a[0, :, n_ca:].reshape(dec_b, n_cache - n_ca, w_a).astype(BF16)
    cv_tail = cache_v_a[0, :, n_ca:].reshape(dec_b, n_cache - n_ca, w_a).astype(BF16)
    ke = jnp.concatenate([ck_tail, s16[:n_dec, w_a:2 * w_a].reshape(dec_b, dec_seq, w_a), pad_e], axis=1)
    ve = jnp.concatenate([cv_tail, s16[:n_dec, 2 * w_a:].reshape(dec_b, dec_seq, w_a), pad_e], axis=1)
    ya_s = _attn_sample(lam, s16, s32, ck, cv, ke, ve, sa, se, subln_a, n_heads_a, dec_b, dec_seq, n_ca,
                        lam_init)

    col0 = 3 * w_a
    qkv_cols = slice(col0, col0 + 3 * w_b)
    zero_halo = jnp.zeros((1, 8, 3 * w_b), F32)
    gdn = functools.partial(_gdn, conv_w=conv_b[0], gate_par=gate_par, norm_b=norm_b, w_b=w_b, col0=col0)
    _, s_meta = gdn(s32, sba, s0=jnp.zeros((1, n_heads_b, HEAD_DIM_B, HEAD_DIM_B), F32), halo0=zero_halo,
                    n_seq=1, n_chunks=1, lr=N_META, row_blk_off=n_dec // N_META)
    meta_halo = jnp.concatenate([jnp.zeros((8 - (CONV_W - 1), 3 * w_b), F32),
                                 s32[n_dec + N_META - (CONV_W - 1):n_dec + N_META, qkv_cols]], axis=0)[None]
    yb_p, ssm_p = gdn(p32, pba, s0=s_meta, halo0=meta_halo, n_seq=1, n_chunks=seq // GDN_CHUNK, lr=GDN_CHUNK,
                      row_blk_off=0)
    samp_halo = jnp.concatenate([jnp.zeros((dec_b, 8 - (CONV_W - 1), 3 * w_b), F32), state_conv_b[0]], axis=1)
    yb_s, ssm_s = gdn(s32, sba, s0=state_ssm_b[0].astype(F32), halo0=samp_halo, n_seq=dec_b, n_chunks=1,
                      lr=dec_seq, row_blk_off=0)

    y_p = _out_proj(ya_p, yb_p, w_out_bf, xp, post_norm)
    y_s = _out_proj(ya_s, yb_s, w_out_bf, x_sample.reshape(n_dec, d_model), post_norm)

    hd = HEAD_DIM_A
    k_p, v_p = _kv_layout(p32, n_heads_a, w_a, N_META)
    meta32 = s32[n_dec:n_dec + N_META]
    k_p = lax.dynamic_update_slice(k_p, meta32[:, :w_a].reshape(N_META * 2 * n_heads_a, hd), (0, 0))
    v_meta = meta32[:, w_a:2 * w_a].reshape(N_META, n_heads_a, 2, hd).transpose(0, 2, 1, 3)
    v_p = lax.dynamic_update_slice(v_p, v_meta.reshape(N_META * 2 * n_heads_a, hd), (0, 0))
    v_p = v_p.reshape(N_META + seq, 2, n_heads_a, hd).transpose(0, 2, 1, 3)
    conv_p = p32[seq - (CONV_W - 1):, qkv_cols]
    k_s = s32[:n_dec, :w_a]
    v_s = s32[:n_dec, w_a:2 * w_a]
    conv_s = s32[:n_dec, qkv_cols].reshape(dec_b, dec_seq, 3 * w_b)[:, dec_seq - (CONV_W - 1):]
    return (
        y_p[None],
        y_s.reshape(dec_b, dec_seq, d_model),
        k_p.reshape(1, 1, N_META + seq, n_heads_a, 2, hd),
        v_p.reshape(1, 1, N_META + seq, n_heads_a, 2 * hd),
        ssm_p[None],
        conv_p[None, None],
        k_s.reshape(1, dec_b, dec_seq, n_heads_a, 2, hd),
        v_s.reshape(1, dec_b, dec_seq, n_heads_a, 2 * hd),
        ssm_s[None],
        conv_s[None],
    )
```

```python
import functools
import math

import jax
import jax.numpy as jnp
from jax import lax
from jax.experimental import pallas as pl
from jax.experimental.pallas import tpu as pltpu

F32 = jnp.float32
BF16 = jnp.bfloat16
HIGHEST = lax.Precision.HIGHEST

EPS = 1e-6
CHUNK = 64
N_META = 16
HEAD_DIM_A = 128
HEAD_DIM_B = 128
CONV_W = 4
N_BUCKETS = 32
MAX_DISTANCE = 1024
NEG = -1e30
LOG2E = math.log2(math.e)

LANES = 128
ATT_BLOCK = 256
FAR_BLOCKS = 8
GDN_CHUNK = 128
GDN_HEADS_PER_STEP = 16
GDN_INTERLEAVE = 16
PROJ_TM = 512
PROJ_TN = 1024
KV_TM = 512
PROJ_TN_SINGLE = 512
OUT_TM = 512
OUT_TN = 512
VMEM_LIMIT = 56 * 1024 * 1024


def _lambda_init(layer):
    return 0.8 - 0.6 * math.exp(-0.3 * layer)


def _bias_saturation_distance():
    nb = N_BUCKETS // 2
    max_exact = nb // 2
    return int(math.ceil(max_exact * (MAX_DISTANCE / max_exact) ** ((nb - max_exact - 1) / (nb - max_exact)))) + 1


N_NEAR = -(-(_bias_saturation_distance() + ATT_BLOCK - 1) // ATT_BLOCK)
FAR_BUCKET = N_BUCKETS // 2 - 1


def _silu(x):
    return x * jax.nn.sigmoid(x)


def _in_proj_kernel(x_ref, pn_ref, w_ref, wt_ref, o32_ref, o16_ref, ba_ref, xn_ref, *, q_tiles, bf_tiles,
                    q_scale):
    j = pl.program_id(1)
    tm = x_ref.shape[0]
    rows = math.gcd(tm, 64) if tm % 64 == 0 else math.gcd(tm, 80)

    @pl.when(j == 0)
    def _():
        def body(r, carry):
            sl = pl.ds(pl.multiple_of(r * rows, 8), rows)
            x = x_ref[sl, :]
            ms = jnp.mean(x * x, axis=-1, keepdims=True)
            xn_ref[sl, :] = (x * lax.rsqrt(ms + EPS) * pn_ref[...]).astype(BF16)
            return carry
        lax.fori_loop(0, tm // rows, body, 0)
        ba_ref[...] = jnp.dot(xn_ref[...], wt_ref[...], preferred_element_type=F32)

    r = jnp.dot(xn_ref[...], w_ref[...], preferred_element_type=F32)

    @pl.when(j >= q_tiles)
    def _():
        o32_ref[...] = r

    @pl.when(j < q_tiles)
    def _():
        o16_ref[...] = (r * q_scale).astype(BF16)

    @pl.when(jnp.logical_and(j >= q_tiles, j < bf_tiles))
    def _():
        o16_ref[...] = r.astype(BF16)


def _in_proj(x, pre_norm, w_all, w_tail, w_a, n, single_tile=False):
    m, d = x.shape
    tm, tn = PROJ_TM, min(PROJ_TN, w_a)
    x_mode = {}
    if single_tile:
        tm, tn = m, min(PROJ_TN_SINGLE, w_a)
        x_mode = dict(pipeline_mode=pl.Buffered(1))
    q_tiles = w_a // tn
    bf_tiles = 3 * w_a // tn
    nt = w_tail.shape[1]
    kern = functools.partial(_in_proj_kernel, q_tiles=q_tiles, bf_tiles=bf_tiles, q_scale=HEAD_DIM_A ** -0.5 * LOG2E)
    return pl.pallas_call(
        kern,
        grid=(m // tm, n // tn),
        in_specs=[
            pl.BlockSpec((tm, d), lambda i, j: (i, 0), **x_mode),
            pl.BlockSpec((1, d), lambda i, j: (0, 0)),
            pl.BlockSpec((d, tn), lambda i, j: (0, j)),
            pl.BlockSpec((d, nt), lambda i, j: (0, 0)),
        ],
        out_specs=[
            pl.BlockSpec((tm, tn), lambda i, j: (i, jnp.maximum(j - q_tiles, 0))),
            pl.BlockSpec((tm, tn), lambda i, j: (i, jnp.minimum(j, bf_tiles - 1))),
            pl.BlockSpec((tm, nt), lambda i, j: (i, 0)),
        ],
        out_shape=[
            jax.ShapeDtypeStruct((m, n - w_a), F32),
            jax.ShapeDtypeStruct((m, 3 * w_a), BF16),
            jax.ShapeDtypeStruct((m, nt), F32),
        ],
        scratch_shapes=[pltpu.VMEM((tm, d), BF16)],
        compiler_params=pltpu.CompilerParams(
            dimension_semantics=("arbitrary", "arbitrary"), vmem_limit_bytes=VMEM_LIMIT),
        name="in_proj",
    )(x, pre_norm, w_all, w_tail)


def _bias_values(rb_ref, h, rel):
    nb = N_BUCKETS // 2
    max_exact = nb // 2
    n = jnp.abs(rel)
    nf = jnp.maximum(n, 1).astype(F32)
    large = max_exact + (jnp.log(nf / max_exact) / math.log(MAX_DISTANCE / max_exact)
                         * (nb - max_exact)).astype(jnp.int32)
    large = jnp.minimum(large, nb - 1)
    bucket = jnp.where(rel > 0, nb, 0) + jnp.where(n < max_exact, n, large)
    val = jnp.zeros(rel.shape, F32)
    for b in range(N_BUCKETS):
        val = jnp.where(bucket == b, rb_ref[b, h], val)
    return (val - rb_ref[FAR_BUCKET, h]) * LOG2E


def _bias_kernel(rb_ref, lamp_ref, fb_ref, mb_ref, sa_ref, se_ref, lam_ref, *, past, dec_seq, n_ca, lam_init):
    h = pl.program_id(0)
    t = ATT_BLOCK
    shift = CHUNK.bit_length() - 1
    i = lax.broadcasted_iota(jnp.int32, (t, t), 0)
    j = lax.broadcasted_iota(jnp.int32, (t, t), 1)
    for dd in range(N_NEAR):
        val = _bias_values(rb_ref, h, (j - i) - t * dd)
        if dd == 0:
            val = jnp.where((j >> shift) <= (i >> shift), val, NEG)
        fb_ref[0, dd] = val
    i = lax.broadcasted_iota(jnp.int32, (t, LANES), 0)
    j = lax.broadcasted_iota(jnp.int32, (t, LANES), 1)
    for qb in range(N_NEAR):
        val = _bias_values(rb_ref, h, (j - N_META) - (t * qb + i))
        mb_ref[0, qb] = jnp.where(j < N_META, val, NEG)
    i = lax.broadcasted_iota(jnp.int32, (dec_seq, n_ca), 0)
    j = lax.broadcasted_iota(jnp.int32, (dec_seq, n_ca), 1)
    sa_ref[0] = _bias_values(rb_ref, h, (j - N_META) - (past + i))
    i = lax.broadcasted_iota(jnp.int32, (dec_seq, LANES), 0)
    j = lax.broadcasted_iota(jnp.int32, (dec_seq, LANES), 1)
    n_extra = N_META + past - n_ca + dec_seq
    val = _bias_values(rb_ref, h, (n_ca - N_META + j) - (past + i))
    se_ref[0] = jnp.where(j < n_extra, val, NEG)
    lp = lamp_ref[...]
    s1 = jnp.sum(lp[0:1] * lp[1:2], axis=-1, keepdims=True)
    s2 = jnp.sum(lp[2:3] * lp[3:4], axis=-1, keepdims=True)
    lam_ref[0] = jnp.broadcast_to(jnp.exp(s1) - jnp.exp(s2) + lam_init, (8, LANES))


def _bias_tables(rel_bias, lam_params, past, dec_seq, n_ca, lam_init):
    nh = rel_bias.shape[1]
    t = ATT_BLOCK
    kern = functools.partial(_bias_kernel, past=past, dec_seq=dec_seq, n_ca=n_ca, lam_init=lam_init)
    return pl.pallas_call(
        kern,
        grid=(nh,),
        in_specs=[
            pl.BlockSpec(memory_space=pltpu.SMEM),
            pl.BlockSpec((4, HEAD_DIM_A), lambda h: (0, 0)),
        ],
        out_specs=[
            pl.BlockSpec((1, N_NEAR, t, t), lambda h: (h, 0, 0, 0)),
            pl.BlockSpec((1, N_NEAR, t, LANES), lambda h: (h, 0, 0, 0)),
            pl.BlockSpec((1, dec_seq, n_ca), lambda h: (h, 0, 0)),
            pl.BlockSpec((1, dec_seq, LANES), lambda h: (h, 0, 0)),
            pl.BlockSpec((1, 8, LANES), lambda h: (h, 0, 0)),
        ],
        out_shape=[
            jax.ShapeDtypeStruct((nh, N_NEAR, t, t), F32),
            jax.ShapeDtypeStruct((nh, N_NEAR, t, LANES), F32),
            jax.ShapeDtypeStruct((nh, dec_seq, n_ca), F32),
            jax.ShapeDtypeStruct((nh, dec_seq, LANES), F32),
            jax.ShapeDtypeStruct((nh, 8, LANES), F32),
        ],
        compiler_params=pltpu.CompilerParams(dimension_semantics=("arbitrary",)),
        name="bias_tables",
    )(rel_bias, lam_params)


def _nt_dot(a, b):
    return lax.dot_general(a, b, (((1,), (1,)), ((), ())), preferred_element_type=F32)


def _attn_finish(o, z, subln, lam_init):
    ms = jnp.mean(o * o, axis=-1, keepdims=True)
    on = o * lax.rsqrt(ms + EPS) * subln * (1.0 - lam_init)
    return on * _silu(z)


def _attn_prompt_kernel(lam_ref, q_ref, k_ref, v_ref, km_ref, vm_ref, fb_ref, mb_ref, z_ref, sub_ref, o_ref,
                        m_ref, l_ref, acc_ref, sa_ref, sb_ref, *, lam_init):
    qi = pl.program_id(1)
    t = ATT_BLOCK
    d = HEAD_DIM_A
    hw = 2 * d

    def rep(x, n):
        return jnp.concatenate([x] * n, axis=1) if n > 1 else x

    def lane_fold(p):
        out = p[:, 0:LANES]
        for g in range(1, p.shape[1] // LANES):
            out = out + p[:, g * LANES:(g + 1) * LANES]
        return out

    vm = vm_ref[...]
    mb = mb_ref[0, jnp.minimum(qi, N_NEAR - 1)]
    for c in range(2):
        s = _nt_dot(q_ref[:, c * d:(c + 1) * d], km_ref[:, c * d:(c + 1) * d]) + mb
        m = jnp.broadcast_to(jnp.max(s, axis=-1, keepdims=True), (t, LANES))
        p = jnp.exp2(s - m)
        m_ref[c] = m
        l_ref[c] = p
        acc_ref[c] = jnp.dot(p.astype(BF16), vm, preferred_element_type=F32)

    def qk(start, width):
        rows = pl.ds(start, width)
        return [_nt_dot(q_ref[:, c * d:(c + 1) * d], k_ref[rows, c * d:(c + 1) * d]) for c in range(2)]

    def softmax_pv(s_both, start, width, bias):
        v = v_ref[pl.ds(start, width), :]
        for c in range(2):
            s = s_both[c]
            if bias is not None:
                s = s + bias
            m_old = m_ref[c]
            m_new = jnp.maximum(m_old, jnp.max(s, axis=-1, keepdims=True))
            p = jnp.exp2(s - rep(m_new, width // LANES))
            alpha = jnp.exp2(m_old - m_new)
            m_ref[c] = m_new
            l_ref[c] = alpha * l_ref[c] + lane_fold(p)
            acc_ref[c] = rep(alpha, hw // LANES) * acc_ref[c] + jnp.dot(
                p.astype(BF16), v, preferred_element_type=F32)

    wide = FAR_BLOCKS * t
    far_shift = FAR_BLOCKS.bit_length() - 1
    n_far = jnp.maximum(qi - (N_NEAR - 1), 0)
    n_wide = n_far >> far_shift

    def put(ref, s_both):
        ref[0] = s_both[0]
        ref[1] = s_both[1]

    @pl.when(n_wide > 0)
    def _():
        put(sa_ref, qk(0, wide))

    def far_body(i, carry):
        b0 = pl.multiple_of(i * (2 * wide), wide)
        b1 = pl.multiple_of(b0 + wide, wide)
        b2 = pl.multiple_of(jnp.minimum(b0 + 2 * wide, (n_wide - 1) * wide), wide)
        put(sb_ref, qk(b1, wide))
        softmax_pv([sa_ref[0], sa_ref[1]], b0, wide, None)
        put(sa_ref, qk(b2, wide))
        softmax_pv([sb_ref[0], sb_ref[1]], b1, wide, None)
        return carry

    def far_body2(i, carry):
        far_body(2 * i, carry)
        return far_body(2 * i + 1, carry)
    lax.fori_loop(0, n_wide >> 2, far_body2, 0)
    lax.fori_loop((n_wide >> 2) * 2, n_wide >> 1, far_body, 0)

    @pl.when((n_wide & 1) == 1)
    def _():
        softmax_pv([sa_ref[0], sa_ref[1]], pl.multiple_of((n_wide - 1) * wide, wide), wide, None)

    n_rest = n_far - (n_wide << far_shift)
    for cnt in range(1, FAR_BLOCKS):
        @pl.when(n_rest == cnt)
        def _(cnt=cnt):
            first = n_wide << far_shift
            s_next = qk(pl.multiple_of(first * t, t), t)
            for b in range(cnt):
                s_cur = s_next
                if b + 1 < cnt:
                    s_next = qk(pl.multiple_of((first + b + 1) * t, t), t)
                softmax_pv(s_cur, pl.multiple_of((first + b) * t, t), t, None)

    @pl.when(qi >= N_NEAR - 1)
    def _():
        s_next = qk(pl.multiple_of((qi - (N_NEAR - 1)) * t, t), t)
        for dd in range(N_NEAR - 1, -1, -1):
            s_cur = s_next
            if dd > 0:
                s_next = qk(pl.multiple_of((qi - dd + 1) * t, t), t)
            softmax_pv(s_cur, pl.multiple_of((qi - dd) * t, t), t, fb_ref[0, dd])

    @pl.when(qi < N_NEAR - 1)
    def _():
        for dd in range(N_NEAR - 2, -1, -1):
            @pl.when(qi - dd >= 0)
            def _(dd=dd):
                start = pl.multiple_of((qi - dd) * t, t)
                softmax_pv(qk(start, t), start, t, fb_ref[0, dd])

    inv0 = 1.0 / jnp.sum(l_ref[0], axis=-1, keepdims=True)
    inv1 = lam_ref[0] / jnp.sum(l_ref[1], axis=-1, keepdims=True)
    o = acc_ref[0] * inv0 - acc_ref[1] * inv1
    o_ref[...] = _attn_finish(o, z_ref[...], sub_ref[...], lam_init).astype(BF16)


def _attn_prompt(lam, qkv16, o32, km, vm, fb, mb, subln, n_heads, lam_init):
    seq = qkv16.shape[0]
    t = ATT_BLOCK
    hw = 2 * HEAD_DIM_A
    kern = functools.partial(_attn_prompt_kernel, lam_init=lam_init)
    return pl.pallas_call(
        kern,
        grid=(n_heads, seq // t),
        in_specs=[
            pl.BlockSpec(memory_space=pltpu.SMEM),
            pl.BlockSpec((t, hw), lambda h, i: (i, h)),
            pl.BlockSpec((seq, hw), lambda h, i: (0, n_heads + h)),
            pl.BlockSpec((seq, hw), lambda h, i: (0, 2 * n_heads + h)),
            pl.BlockSpec((LANES, hw), lambda h, i: (0, h)),
            pl.BlockSpec((LANES, hw), lambda h, i: (0, h)),
            pl.BlockSpec((1, N_NEAR, t, t), lambda h, i: (h, 0, 0, 0)),
            pl.BlockSpec((1, N_NEAR, t, LANES), lambda h, i: (h, 0, 0, 0)),
            pl.BlockSpec((t, hw), lambda h, i: (i, 2 * n_heads + h)),
            pl.BlockSpec((1, hw), lambda h, i: (0, 0)),
        ],
        out_specs=pl.BlockSpec((t, hw), lambda h, i: (i, h)),
        out_shape=jax.ShapeDtypeStruct((seq, n_heads * hw), BF16),
        scratch_shapes=[
            pltpu.VMEM((2, t, LANES), F32),
            pltpu.VMEM((2, t, LANES), F32),
            pltpu.VMEM((2, t, hw), F32),
            pltpu.VMEM((2, t, FAR_BLOCKS * t), F32),
            pltpu.VMEM((2, t, FAR_BLOCKS * t), F32),
        ],
        compiler_params=pltpu.CompilerParams(
            dimension_semantics=("arbitrary", "arbitrary"), vmem_limit_bytes=VMEM_LIMIT),
        name="attn_prompt",
    )(lam, qkv16, qkv16, qkv16, km, vm, fb, mb, o32, subln)


def _attn_sample_kernel(lam_ref, q_ref, kc_ref, vc_ref, ke_ref, ve_ref, sa_ref, se_ref, z_ref, sub_ref, o_ref,
                        *, lam_init, n_heads, n_ca):
    h = pl.program_id(1)
    d = HEAD_DIM_A
    vc = jnp.concatenate([vc_ref[0, pl.ds(c * n_heads + h, n_ca, stride=2 * n_heads), :] for c in range(2)],
                         axis=1).astype(BF16)
    ve = ve_ref[0]
    outs = []
    for c in range(2):
        q = q_ref[:, c * d:(c + 1) * d]
        kc = kc_ref[0, pl.ds(2 * h + c, n_ca, stride=2 * n_heads), :].astype(BF16)
        s_a = _nt_dot(q, kc) + sa_ref[0]
        s_e = _nt_dot(q, ke_ref[0, :, c * d:(c + 1) * d]) + se_ref[0]
        m = jnp.maximum(jnp.max(s_a, axis=-1, keepdims=True), jnp.max(s_e, axis=-1, keepdims=True))
        p_a = jnp.exp2(s_a - m)
        p_e = jnp.exp2(s_e - m)
        l = jnp.sum(p_a, axis=-1, keepdims=True) + jnp.sum(p_e, axis=-1, keepdims=True)
        acc = (jnp.dot(p_a.astype(BF16), vc, preferred_element_type=F32)
               + jnp.dot(p_e.astype(BF16), ve, preferred_element_type=F32))
        outs.append(acc / l)
    o = outs[0] - lam_ref[0] * outs[1]
    o_ref[...] = _attn_finish(o, z_ref[...], sub_ref[...], lam_init).astype(BF16)


def _attn_sample(lam, qkv16, o32, cache_k, cache_v, ke, ve, sa, se, subln, n_heads, dec_b, dec_seq, n_ca,
                 lam_init):
    hw = 2 * HEAD_DIM_A
    kern = functools.partial(_attn_sample_kernel, lam_init=lam_init, n_heads=n_heads, n_ca=n_ca)
    return pl.pallas_call(
        kern,
        grid=(dec_b, n_heads),
        in_specs=[
            pl.BlockSpec(memory_space=pltpu.SMEM),
            pl.BlockSpec((dec_seq, hw), lambda b, h: (b, h)),
            pl.BlockSpec((1, n_ca * n_heads * 2, HEAD_DIM_A), lambda b, h: (b, 0, 0)),
            pl.BlockSpec((1, n_ca * n_heads * 2, HEAD_DIM_A), lambda b, h: (b, 0, 0)),
            pl.BlockSpec((1, LANES, hw), lambda b, h: (b, 0, h)),
            pl.BlockSpec((1, LANES, hw), lambda b, h: (b, 0, h)),
            pl.BlockSpec((1, dec_seq, n_ca), lambda b, h: (h, 0, 0)),
            pl.BlockSpec((1, dec_seq, LANES), lambda b, h: (h, 0, 0)),
            pl.BlockSpec((dec_seq, hw), lambda b, h: (b, 2 * n_heads + h)),
            pl.BlockSpec((1, hw), lambda b, h: (0, 0)),
        ],
        out_specs=pl.BlockSpec((dec_seq, hw), lambda b, h: (b, h)),
        out_shape=jax.ShapeDtypeStruct((dec_b * dec_seq, n_heads * hw), BF16),
        compiler_params=pltpu.CompilerParams(
            dimension_semantics=("arbitrary", "arbitrary"), vmem_limit_bytes=VMEM_LIMIT),
        name="attn_sample",
    )(lam, qkv16, cache_k, cache_v, ke, ve, sa, se, o32, subln)


def _hdot(a, b):
    return jnp.dot(a, b, preferred_element_type=F32, precision=HIGHEST)


def _bdot(a, b):
    return jnp.dot(a.astype(BF16), b.astype(BF16), preferred_element_type=F32)


def _inv_unit_lower(a_list, row, col):
    sh = 4
    n = a_list[0].shape[0]
    eye = (row == col).astype(F32)
    same = (row >> sh) == (col >> sh)
    dblk = [jnp.where(same, a, 0.0) for a in a_list]
    t = [eye - x for x in dblk]
    pw = [x.astype(BF16) for x in dblk]
    for _ in range(sh - 1):
        pw = [jnp.dot(x, x, preferred_element_type=F32).astype(BF16) for x in pw]
        t = [ti + jnp.dot(ti.astype(BF16), x, preferred_element_type=F32) for ti, x in zip(t, pw)]
    while (1 << sh) < n:
        offm = jnp.logical_and((row >> (sh + 1)) == (col >> (sh + 1)), (row >> sh) != (col >> sh))
        off = [jnp.where(offm, a, 0.0).astype(BF16) for a in a_list]
        tb = [ti.astype(BF16) for ti in t]
        mid = [jnp.dot(o, x, preferred_element_type=F32).astype(BF16) for o, x in zip(off, tb)]
        t = [ti - jnp.dot(x, m, preferred_element_type=F32) for ti, x, m in zip(t, tb, mid)]
        sh += 1
    return t


def _gdn_kernel(q_ref, k_ref, v_ref, z_ref, ba_ref, cwq_ref, cwk_ref, cwv_ref, gp_ref, nb_ref, s0_ref,
                hq_ref, hk_ref, hv_ref, y_ref, sout_ref, s_ref, xq_ref, xk_ref, xv_ref, *, lr, hg):
    c = pl.program_id(2)
    n = GDN_CHUNK
    dh = HEAD_DIM_B
    halo = 8

    @pl.when(c == 0)
    def _():
        s_ref[...] = s0_ref[0]
        xq_ref[0:halo, :] = hq_ref[0]
        xk_ref[0:halo, :] = hk_ref[0]
        xv_ref[0:halo, :] = hv_ref[0]

    row_w = lax.broadcasted_iota(jnp.int32, (n, hg * dh), 0)

    def conv(x_ref, xbuf, cw_ref):
        xbuf[halo:halo + lr, :] = x_ref[...]
        if lr < n:
            xbuf[halo + lr:halo + n, :] = jnp.zeros((n - lr, hg * dh), F32)
        y = xbuf[halo - 3:halo - 3 + n, :] * cw_ref[0:1, :]
        for i in range(1, CONV_W):
            y = y + xbuf[halo - 3 + i:halo - 3 + i + n, :] * cw_ref[i:i + 1, :]
        y = _silu(y)
        if lr < n:
            y = jnp.where(row_w < lr, y, 0.0)
        xbuf[0:halo, :] = xbuf[lr:lr + halo, :]
        return y

    yq = conv(q_ref, xq_ref, cwq_ref)
    yk = conv(k_ref, xk_ref, cwk_ref)
    yv = conv(v_ref, xv_ref, cwv_ref)

    row = lax.broadcasted_iota(jnp.int32, (n, n), 0)
    col = lax.broadcasted_iota(jnp.int32, (n, n), 1)
    ba = ba_ref[...]
    if lr < n:
        ba = jnp.concatenate([ba, jnp.zeros((n - lr, LANES), F32)], axis=0)
    live = row < lr
    beta = jnp.where(jnp.logical_and(live, col < hg), jax.nn.sigmoid(ba), 0.0)
    gval = -jnp.exp(gp_ref[0, 0:1, :]) * jax.nn.softplus(ba + gp_ref[0, 1:2, :])
    g = jnp.where(jnp.logical_and(live, jnp.logical_and(col >= hg, col < 2 * hg)), gval, 0.0)
    gsum = _hdot((row >= col).astype(F32), g)
    gsum_t = gsum.T
    incl = row >= col
    strict = row > col

    def run(heads):
        cols = {hh: slice(hh * dh, (hh + 1) * dh) for hh in heads}
        gc = {hh: jnp.broadcast_to(gsum[:, hg + hh:hg + hh + 1], (n, n)) for hh in heads}
        bc = {hh: jnp.broadcast_to(beta[:, hh:hh + 1], (n, n)) for hh in heads}
        gam = {hh: jnp.where(incl, jnp.exp(jnp.minimum(gc[hh] - gsum_t[hg + hh:hg + hh + 1, :], 0.0)), 0.0)
               for hh in heads}
        qh = {hh: yq[:, cols[hh]] * (lax.rsqrt(jnp.sum(yq[:, cols[hh]] * yq[:, cols[hh]], axis=-1, keepdims=True)
                                               + EPS) * (dh ** -0.5)) for hh in heads}
        kh = {hh: yk[:, cols[hh]] * lax.rsqrt(jnp.sum(yk[:, cols[hh]] * yk[:, cols[hh]], axis=-1, keepdims=True)
                                              + EPS) for hh in heads}
        qb = {hh: qh[hh].astype(BF16) for hh in heads}
        kb = {hh: kh[hh].astype(BF16) for hh in heads}
        kk = {hh: _nt_dot(kb[hh], kb[hh]) for hh in heads}
        qk = {hh: _nt_dot(qb[hh], kb[hh]) for hh in heads}
        a = [jnp.where(strict, bc[hh] * kk[hh] * gam[hh], 0.0) for hh in heads]
        t = dict(zip(heads, _inv_unit_lower(a, row, col)))
        eg = {hh: jnp.exp(gc[hh]) for hh in heads}
        rhs = {hh: jnp.concatenate([yv[:, cols[hh]] * bc[hh], kh[hh] * (bc[hh] * eg[hh])], axis=1).astype(BF16)
               for hh in heads}
        uw = {hh: jnp.dot(t[hh].astype(BF16), rhs[hh], preferred_element_type=F32) for hh in heads}
        s = {hh: s_ref[hh] for hh in heads}
        sb = {hh: s[hh].astype(BF16) for hh in heads}
        v_new = {hh: uw[hh][:, :dh] - jnp.dot(uw[hh][:, dh:].astype(BF16), sb[hh], preferred_element_type=F32)
                 for hh in heads}
        vnb = {hh: v_new[hh].astype(BF16) for hh in heads}
        o = {hh: eg[hh] * jnp.dot(qb[hh], sb[hh], preferred_element_type=F32)
             + jnp.dot((qk[hh] * gam[hh]).astype(BF16), vnb[hh], preferred_element_type=F32) for hh in heads}
        g_last = {hh: gc[hh][n - 1:n, :] for hh in heads}
        kd = {hh: (kh[hh] * jnp.exp(g_last[hh] - gc[hh])).astype(BF16) for hh in heads}
        for hh in heads:
            s_ref[hh] = s[hh] * jnp.exp(g_last[hh]) + lax.dot_general(
                kd[hh], vnb[hh], (((0,), (0,)), ((), ())), preferred_element_type=F32)
        for hh in heads:
            on = o[hh] * lax.rsqrt(jnp.mean(o[hh] * o[hh], axis=-1, keepdims=True) + EPS) * nb_ref[...]
            y_ref[:, cols[hh]] = (on[0:lr] * _silu(z_ref[:, cols[hh]])).astype(BF16)

    for first in range(0, hg, GDN_INTERLEAVE):
        run(list(range(first, min(first + GDN_INTERLEAVE, hg))))

    @pl.when(c == pl.num_programs(2) - 1)
    def _():
        sout_ref[0] = s_ref[...]


def _gdn(o32, ba, conv_w, gate_par, norm_b, s0, halo0, *, n_seq, n_chunks, lr, row_blk_off, w_b, col0):
    hg = min(GDN_HEADS_PER_STEP, w_b // HEAD_DIM_B)
    gw = hg * HEAD_DIM_B
    n_groups = w_b // gw
    n_heads = w_b // HEAD_DIM_B
    cb = col0 // gw
    wb = w_b // gw

    def rmap(off):
        return lambda s, g, c: (row_blk_off + s * n_chunks + c, off + g)

    kern = functools.partial(_gdn_kernel, lr=lr, hg=hg)
    return pl.pallas_call(
        kern,
        grid=(n_seq, n_groups, n_chunks),
        in_specs=[
            pl.BlockSpec((lr, gw), rmap(cb)),
            pl.BlockSpec((lr, gw), rmap(cb + wb)),
            pl.BlockSpec((lr, gw), rmap(cb + 2 * wb)),
            pl.BlockSpec((lr, gw), rmap(cb + 3 * wb)),
            pl.BlockSpec((lr, LANES), rmap(0)),
            pl.BlockSpec((CONV_W, gw), lambda s, g, c: (0, g)),
            pl.BlockSpec((CONV_W, gw), lambda s, g, c: (0, wb + g)),
            pl.BlockSpec((CONV_W, gw), lambda s, g, c: (0, 2 * wb + g)),
            pl.BlockSpec((1, 2, LANES), lambda s, g, c: (g, 0, 0)),
            pl.BlockSpec((1, HEAD_DIM_B), lambda s, g, c: (0, 0)),
            pl.BlockSpec((1, hg, HEAD_DIM_B, HEAD_DIM_B), lambda s, g, c: (s, g, 0, 0)),
            pl.BlockSpec((1, 8, gw), lambda s, g, c: (s, 0, g)),
            pl.BlockSpec((1, 8, gw), lambda s, g, c: (s, 0, wb + g)),
            pl.BlockSpec((1, 8, gw), lambda s, g, c: (s, 0, 2 * wb + g)),
        ],
        out_specs=[
            pl.BlockSpec((lr, gw), lambda s, g, c: (s * n_chunks + c, g)),
            pl.BlockSpec((1, hg, HEAD_DIM_B, HEAD_DIM_B), lambda s, g, c: (s, g, 0, 0)),
        ],
        out_shape=[
            jax.ShapeDtypeStruct((n_seq * n_chunks * lr, w_b), BF16),
            jax.ShapeDtypeStruct((n_seq, n_heads, HEAD_DIM_B, HEAD_DIM_B), F32),
        ],
        scratch_shapes=[
            pltpu.VMEM((hg, HEAD_DIM_B, HEAD_DIM_B), F32),
            pltpu.VMEM((8 + GDN_CHUNK, gw), F32),
            pltpu.VMEM((8 + GDN_CHUNK, gw), F32),
            pltpu.VMEM((8 + GDN_CHUNK, gw), F32),
        ],
        compiler_params=pltpu.CompilerParams(
            dimension_semantics=("arbitrary", "arbitrary", "arbitrary"), vmem_limit_bytes=VMEM_LIMIT),
        name=f"gdn_l{lr}",
    )(o32, o32, o32, o32, ba, conv_w, conv_w, conv_w, gate_par, norm_b, s0, halo0, halo0, halo0)


def _out_proj_kernel(ya_ref, yb_ref, w_ref, h_ref, pn_ref, o_ref, *, w_a):
    j = pl.program_id(1)
    tn = w_ref.shape[1]
    r = (jnp.dot(ya_ref[...], w_ref[0:w_a, :], preferred_element_type=F32)
         + jnp.dot(yb_ref[...], w_ref[w_a:, :], preferred_element_type=F32))
    o_ref[:, pl.ds(pl.multiple_of(j * tn, tn), tn)] = r

    @pl.when(j == pl.num_programs(1) - 1)
    def _():
        rows = 64

        def body(i, carry):
            sl = pl.ds(pl.multiple_of(i * rows, rows), rows)
            y = o_ref[sl, :]
            ms = jnp.mean(y * y, axis=-1, keepdims=True)
            o_ref[sl, :] = h_ref[sl, :] + y * lax.rsqrt(ms + EPS) * pn_ref[...]
            return carry
        lax.fori_loop(0, o_ref.shape[0] // rows, body, 0)


def _out_proj(ya, yb, w_out, h, post_norm):
    m, d = h.shape
    w_a = ya.shape[1]
    tm, tn = min(OUT_TM, m), OUT_TN
    kern = functools.partial(_out_proj_kernel, w_a=w_a)
    return pl.pallas_call(
        kern,
        grid=(m // tm, d // tn),
        in_specs=[
            pl.BlockSpec((tm, w_a), lambda i, j: (i, 0)),
            pl.BlockSpec((tm, yb.shape[1]), lambda i, j: (i, 0)),
            pl.BlockSpec((d, tn), lambda i, j: (0, j)),
            pl.BlockSpec((tm, d), lambda i, j: (i, 0)),
            pl.BlockSpec((1, d), lambda i, j: (0, 0)),
        ],
        out_specs=pl.BlockSpec((tm, d), lambda i, j: (i, 0)),
        out_shape=jax.ShapeDtypeStruct((m, d), F32),
        compiler_params=pltpu.CompilerParams(
            dimension_semantics=("arbitrary", "arbitrary"), vmem_limit_bytes=VMEM_LIMIT),
        name="out_proj",
    )(ya, yb, w_out, h, post_norm)


def _kv_layout_kernel(k_ref, v_ref, ko_ref, vo_ref, *, n_heads):
    tm = k_ref.shape[0]
    for p in range(2 * n_heads):
        piece = slice(p * LANES, (p + 1) * LANES)
        ko_ref[pl.ds(p, tm, stride=2 * n_heads), :] = k_ref[:, piece]
        vo_ref[pl.ds((p % 2) * n_heads + p // 2, tm, stride=2 * n_heads), :] = v_ref[:, piece]


def _kv_layout(p32, n_heads, w_a, tok_off):
    seq = p32.shape[0]
    tm = KV_TM
    rpt = 2 * n_heads
    spec_out = pl.BlockSpec((pl.Element(tm * rpt), pl.Element(LANES)),
                            lambda i: ((tok_off + i * tm) * rpt, 0))
    shape_out = jax.ShapeDtypeStruct(((tok_off + seq) * rpt, LANES), F32)
    return pl.pallas_call(
        functools.partial(_kv_layout_kernel, n_heads=n_heads),
        grid=(seq // tm,),
        in_specs=[
            pl.BlockSpec((tm, w_a), lambda i: (i, 0)),
            pl.BlockSpec((tm, w_a), lambda i: (i, 1)),
        ],
        out_specs=[spec_out, spec_out],
        out_shape=[shape_out, shape_out],
        compiler_params=pltpu.CompilerParams(dimension_semantics=("arbitrary",), vmem_limit_bytes=VMEM_LIMIT),
        name="kv_layout",
    )(p32, p32)


def kernel(x_prompt, x_sample, cache_k_a, cache_v_a, state_ssm_b, state_conv_b, meta_tokens, rel_bias, pre_norm,
           w_in, lambda_q1, lambda_k1, lambda_q2, lambda_k2, subln_a, conv_b, a_log_b, dt_bias_b, norm_b, w_out,
           post_norm):
    batch, seq, d_model = x_prompt.shape
    dec_b, dec_seq, _ = x_sample.shape
    depth = w_in.shape[0]
    assert batch == 1 and depth == 1
    n_heads_a = rel_bias.shape[1]
    w_a = n_heads_a * 2 * HEAD_DIM_A
    w_b = d_model - w_a
    n_heads_b = w_b // HEAD_DIM_B
    n_cache = cache_k_a.shape[2]
    past = n_cache - N_META
    n_ca = (n_cache // LANES) * LANES
    n_main = 4 * w_a + 4 * w_b
    lam_init = _lambda_init(0)
    hg = min(GDN_HEADS_PER_STEP, n_heads_b)
    n_groups = n_heads_b // hg
    n_dec = dec_b * dec_seq
    assert seq % PROJ_TM == 0 and seq % GDN_CHUNK == 0 and n_cache - n_ca + dec_seq <= LANES

    w_in_bf = w_in[0].astype(BF16)
    wb = w_in_bf[:, n_main:n_main + n_heads_b].reshape(d_model, n_groups, hg)
    wa = w_in_bf[:, n_main + n_heads_b:].reshape(d_model, n_groups, hg)
    w_tail = jnp.concatenate([wb, wa, jnp.zeros((d_model, n_groups, LANES - 2 * hg), BF16)], axis=-1)
    w_tail = w_tail.reshape(d_model, n_groups * LANES)
    w_out_bf = w_out[0].astype(BF16)
    gate_par = jnp.zeros((n_groups, 2, LANES), F32)
    gate_par = gate_par.at[:, 0, hg:2 * hg].set(a_log_b[0].reshape(n_groups, hg))
    gate_par = gate_par.at[:, 1, hg:2 * hg].set(dt_bias_b[0].reshape(n_groups, hg))
    lam_params = jnp.stack([lambda_q1[0], lambda_k1[0], lambda_q2[0], lambda_k2[0]])

    xp = x_prompt[0]
    n_small = -(-(n_dec + N_META) // 16) * 16
    xs = jnp.concatenate([x_sample.reshape(n_dec, d_model), meta_tokens.astype(F32),
                          jnp.zeros((n_small - n_dec - N_META, d_model), F32)], axis=0)
    p32, p16, pba = _in_proj(xp, pre_norm, w_in_bf, w_tail, w_a, n_main)
    s32, s16, sba = _in_proj(xs, pre_norm, w_in_bf, w_tail, w_a, n_main, single_tile=True)

    fb, mb, sa, se, lam_t = _bias_tables(rel_bias, lam_params, past, dec_seq, n_ca, lam_init)
    lam = lam_t[0, 0, 0:1]

    meta16 = s16[n_dec:n_dec + N_META]
    pad_m = jnp.zeros((LANES - N_META, w_a), BF16)
    km = jnp.concatenate([meta16[:, w_a:2 * w_a], pad_m], axis=0)
    vm = jnp.concatenate([meta16[:, 2 * w_a:], pad_m], axis=0)
    ya_p = _attn_prompt(lam, p16, p32, km, vm, fb, mb, subln_a, n_heads_a, lam_init)

    ck = cache_k_a[0].reshape(dec_b, n_cache * n_heads_a * 2, HEAD_DIM_A)
    cv = cache_v_a[0].reshape(dec_b, n_cache, n_heads_a, 2, HEAD_DIM_A).transpose(0, 1, 3, 2, 4)
    cv = cv.reshape(dec_b, n_cache * 2 * n_heads_a, HEAD_DIM_A)
    n_extra = n_cache - n_ca + dec_seq
    pad_e = jnp.zeros((dec_b, LANES - n_extra, w_a), BF16)
    ck_tail = cache_k_a[0, :, n_ca:].reshape(dec_b, n_cache - n_ca, w_a).astype(BF16)
    cv_tail = cache_v_a[0, :, n_ca:].reshape(dec_b, n_cache - n_ca, w_a).astype(BF16)
    ke = jnp.concatenate([ck_tail, s16[:n_dec, w_a:2 * w_a].reshape(dec_b, dec_seq, w_a), pad_e], axis=1)
    ve = jnp.concatenate([cv_tail, s16[:n_dec, 2 * w_a:].reshape(dec_b, dec_seq, w_a), pad_e], axis=1)
    ya_s = _attn_sample(lam, s16, s32, ck, cv, ke, ve, sa, se, subln_a, n_heads_a, dec_b, dec_seq, n_ca,
                        lam_init)

    col0 = 3 * w_a
    qkv_cols = slice(col0, col0 + 3 * w_b)
    zero_halo = jnp.zeros((1, 8, 3 * w_b), F32)
    gdn = functools.partial(_gdn, conv_w=conv_b[0], gate_par=gate_par, norm_b=norm_b, w_b=w_b, col0=col0)
    _, s_meta = gdn(s32, sba, s0=jnp.zeros((1, n_heads_b, HEAD_DIM_B, HEAD_DIM_B), F32), halo0=zero_halo,
                    n_seq=1, n_chunks=1, lr=N_META, row_blk_off=n_dec // N_META)
    meta_halo = jnp.concatenate([jnp.zeros((8 - (CONV_W - 1), 3 * w_b), F32),
                                 s32[n_dec + N_META - (CONV_W - 1):n_dec + N_META, qkv_cols]], axis=0)[None]
    yb_p, ssm_p = gdn(p32, pba, s0=s_meta, halo0=meta_halo, n_seq=1, n_chunks=seq // GDN_CHUNK, lr=GDN_CHUNK,
                      row_blk_off=0)
    samp_halo = jnp.concatenate([jnp.zeros((dec_b, 8 - (CONV_W - 1), 3 * w_b), F32), state_conv_b[0]], axis=1)
    yb_s, ssm_s = gdn(s32, sba, s0=state_ssm_b[0].astype(F32), halo0=samp_halo, n_seq=dec_b, n_chunks=1,
                      lr=dec_seq, row_blk_off=0)

    y_p = _out_proj(ya_p, yb_p, w_out_bf, xp, post_norm)
    y_s = _out_proj(ya_s, yb_s, w_out_bf, x_sample.reshape(n_dec, d_model), post_norm)

    hd = HEAD_DIM_A
    k_p, v_p = _kv_layout(p32, n_heads_a, w_a, N_META)
    meta32 = s32[n_dec:n_dec + N_META]
    k_p = lax.dynamic_update_slice(k_p, meta32[:, :w_a].reshape(N_META * 2 * n_heads_a, hd), (0, 0))
    v_meta = meta32[:, w_a:2 * w_a].reshape(N_META, n_heads_a, 2, hd).transpose(0, 2, 1, 3)
    v_p = lax.dynamic_update_slice(v_p, v_meta.reshape(N_META * 2 * n_heads_a, hd), (0, 0))
    v_p = v_p.reshape(N_META + seq, 2, n_heads_a, hd).transpose(0, 2, 1, 3)
    conv_p = p32[seq - (CONV_W - 1):, qkv_cols]
    k_s = s32[:n_dec, :w_a]
    v_s = s32[:n_dec, w_a:2 * w_a]
    conv_s = s32[:n_dec, qkv_cols].reshape(dec_b, dec_seq, 3 * w_b)[:, dec_seq - (CONV_W - 1):]
    return (
        y_p[None],
        y_s.reshape(dec_b, dec_seq, d_model),
        k_p.reshape(1, 1, N_META + seq, n_heads_a, 2, hd),
        v_p.reshape(1, 1, N_META + seq, n_heads_a, 2 * hd),
        ssm_p[None],
        conv_p[None, None],
        k_s.reshape(1, dec_b, dec_seq, n_heads_a, 2, hd),
        v_s.reshape(1, dec_b, dec_seq, n_heads_a, 2 * hd),
        ssm_s[None],
        conv_s[None],
    )
```

```python
import functools
import math

import jax
import jax.numpy as jnp
from jax import lax
from jax.experimental import pallas as pl
from jax.experimental.pallas import tpu as pltpu

F32 = jnp.float32
BF16 = jnp.bfloat16
HIGHEST = lax.Precision.HIGHEST

EPS = 1e-6
CHUNK = 64
N_META = 16
HEAD_DIM_A = 128
HEAD_DIM_B = 128
CONV_W = 4
N_BUCKETS = 32
MAX_DISTANCE = 1024
NEG = -1e30
LOG2E = math.log2(math.e)

LANES = 128
ATT_BLOCK = 256
FAR_BLOCKS = 8
SAMPLE_HEADS_PER_STEP = 4
GDN_CHUNK = 128
GDN_HEADS_PER_STEP = 16
GDN_INTERLEAVE = 16
PROJ_TM = 512
PROJ_TN = 1024
KV_TM = 512
PROJ_TN_SINGLE = 512
OUT_TM = 512
OUT_TN = 512
VMEM_LIMIT = 56 * 1024 * 1024


def _lambda_init(layer):
    return 0.8 - 0.6 * math.exp(-0.3 * layer)


def _bias_saturation_distance():
    nb = N_BUCKETS // 2
    max_exact = nb // 2
    return int(math.ceil(max_exact * (MAX_DISTANCE / max_exact) ** ((nb - max_exact - 1) / (nb - max_exact)))) + 1


N_NEAR = -(-(_bias_saturation_distance() + ATT_BLOCK - 1) // ATT_BLOCK)
FAR_BUCKET = N_BUCKETS // 2 - 1


def _silu(x):
    return x * jax.nn.sigmoid(x)


def _in_proj_kernel(x_ref, pn_ref, w_ref, wt_ref, o32_ref, o16_ref, ba_ref, xn_ref, *, q_tiles, bf_tiles,
                    q_scale):
    j = pl.program_id(1)
    tm = x_ref.shape[0]
    rows = math.gcd(tm, 64) if tm % 64 == 0 else math.gcd(tm, 80)

    @pl.when(j == 0)
    def _():
        def body(r, carry):
            sl = pl.ds(pl.multiple_of(r * rows, 8), rows)
            x = x_ref[sl, :]
            ms = jnp.mean(x * x, axis=-1, keepdims=True)
            xn_ref[sl, :] = (x * lax.rsqrt(ms + EPS) * pn_ref[...]).astype(BF16)
            return carry
        lax.fori_loop(0, tm // rows, body, 0)
        ba_ref[...] = jnp.dot(xn_ref[...], wt_ref[...], preferred_element_type=F32)

    r = jnp.dot(xn_ref[...], w_ref[...], preferred_element_type=F32)

    @pl.when(j >= q_tiles)
    def _():
        o32_ref[...] = r

    @pl.when(j < q_tiles)
    def _():
        o16_ref[...] = (r * q_scale).astype(BF16)

    @pl.when(jnp.logical_and(j >= q_tiles, j < bf_tiles))
    def _():
        o16_ref[...] = r.astype(BF16)


def _in_proj(x, pre_norm, w_all, w_tail, w_a, n, single_tile=False):
    m, d = x.shape
    tm, tn = PROJ_TM, min(PROJ_TN, w_a)
    x_mode = {}
    if single_tile:
        tm, tn = m, min(PROJ_TN_SINGLE, w_a)
        x_mode = dict(pipeline_mode=pl.Buffered(1))
    q_tiles = w_a // tn
    bf_tiles = 3 * w_a // tn
    nt = w_tail.shape[1]
    kern = functools.partial(_in_proj_kernel, q_tiles=q_tiles, bf_tiles=bf_tiles, q_scale=HEAD_DIM_A ** -0.5 * LOG2E)
    return pl.pallas_call(
        kern,
        grid=(m // tm, n // tn),
        in_specs=[
            pl.BlockSpec((tm, d), lambda i, j: (i, 0), **x_mode),
            pl.BlockSpec((1, d), lambda i, j: (0, 0)),
            pl.BlockSpec((d, tn), lambda i, j: (0, j)),
            pl.BlockSpec((d, nt), lambda i, j: (0, 0)),
        ],
        out_specs=[
            pl.BlockSpec((tm, tn), lambda i, j: (i, jnp.maximum(j - q_tiles, 0))),
            pl.BlockSpec((tm, tn), lambda i, j: (i, jnp.minimum(j, bf_tiles - 1))),
            pl.BlockSpec((tm, nt), lambda i, j: (i, 0)),
        ],
        out_shape=[
            jax.ShapeDtypeStruct((m, n - w_a), F32),
            jax.ShapeDtypeStruct((m, 3 * w_a), BF16),
            jax.ShapeDtypeStruct((m, nt), F32),
        ],
        scratch_shapes=[pltpu.VMEM((tm, d), BF16)],
        compiler_params=pltpu.CompilerParams(
            dimension_semantics=("arbitrary", "arbitrary"), vmem_limit_bytes=VMEM_LIMIT),
        name="in_proj",
    )(x, pre_norm, w_all, w_tail)


def _bias_values(rb_ref, h, rel):
    nb = N_BUCKETS // 2
    max_exact = nb // 2
    n = jnp.abs(rel)
    nf = jnp.maximum(n, 1).astype(F32)
    large = max_exact + (jnp.log(nf / max_exact) / math.log(MAX_DISTANCE / max_exact)
                         * (nb - max_exact)).astype(jnp.int32)
    large = jnp.minimum(large, nb - 1)
    bucket = jnp.where(rel > 0, nb, 0) + jnp.where(n < max_exact, n, large)
    val = jnp.zeros(rel.shape, F32)
    for b in range(N_BUCKETS):
        val = jnp.where(bucket == b, rb_ref[b, h], val)
    return (val - rb_ref[FAR_BUCKET, h]) * LOG2E


def _bias_kernel(rb_ref, lamp_ref, fb_ref, mb_ref, sa_ref, se_ref, lam_ref, *, past, dec_seq, n_ca, lam_init):
    h = pl.program_id(0)
    t = ATT_BLOCK
    shift = CHUNK.bit_length() - 1
    i = lax.broadcasted_iota(jnp.int32, (t, t), 0)
    j = lax.broadcasted_iota(jnp.int32, (t, t), 1)
    for dd in range(N_NEAR):
        val = _bias_values(rb_ref, h, (j - i) - t * dd)
        if dd == 0:
            val = jnp.where((j >> shift) <= (i >> shift), val, NEG)
        fb_ref[0, dd] = val
    i = lax.broadcasted_iota(jnp.int32, (t, LANES), 0)
    j = lax.broadcasted_iota(jnp.int32, (t, LANES), 1)
    for qb in range(N_NEAR):
        val = _bias_values(rb_ref, h, (j - N_META) - (t * qb + i))
        mb_ref[0, qb] = jnp.where(j < N_META, val, NEG)
    i = lax.broadcasted_iota(jnp.int32, (dec_seq, n_ca), 0)
    j = lax.broadcasted_iota(jnp.int32, (dec_seq, n_ca), 1)
    sa_ref[0] = _bias_values(rb_ref, h, (j - N_META) - (past + i))
    i = lax.broadcasted_iota(jnp.int32, (dec_seq, LANES), 0)
    j = lax.broadcasted_iota(jnp.int32, (dec_seq, LANES), 1)
    n_extra = N_META + past - n_ca + dec_seq
    val = _bias_values(rb_ref, h, (n_ca - N_META + j) - (past + i))
    se_ref[0] = jnp.where(j < n_extra, val, NEG)
    lp = lamp_ref[...]
    s1 = jnp.sum(lp[0:1] * lp[1:2], axis=-1, keepdims=True)
    s2 = jnp.sum(lp[2:3] * lp[3:4], axis=-1, keepdims=True)
    lam_ref[0] = jnp.broadcast_to(jnp.exp(s1) - jnp.exp(s2) + lam_init, (8, LANES))


def _bias_tables(rel_bias, lam_params, past, dec_seq, n_ca, lam_init):
    nh = rel_bias.shape[1]
    t = ATT_BLOCK
    kern = functools.partial(_bias_kernel, past=past, dec_seq=dec_seq, n_ca=n_ca, lam_init=lam_init)
    return pl.pallas_call(
        kern,
        grid=(nh,),
        in_specs=[
            pl.BlockSpec(memory_space=pltpu.SMEM),
            pl.BlockSpec((4, HEAD_DIM_A), lambda h: (0, 0)),
        ],
        out_specs=[
            pl.BlockSpec((1, N_NEAR, t, t), lambda h: (h, 0, 0, 0)),
            pl.BlockSpec((1, N_NEAR, t, LANES), lambda h: (h, 0, 0, 0)),
            pl.BlockSpec((1, dec_seq, n_ca), lambda h: (h, 0, 0)),
            pl.BlockSpec((1, dec_seq, LANES), lambda h: (h, 0, 0)),
            pl.BlockSpec((1, 8, LANES), lambda h: (h, 0, 0)),
        ],
        out_shape=[
            jax.ShapeDtypeStruct((nh, N_NEAR, t, t), F32),
            jax.ShapeDtypeStruct((nh, N_NEAR, t, LANES), F32),
            jax.ShapeDtypeStruct((nh, dec_seq, n_ca), F32),
            jax.ShapeDtypeStruct((nh, dec_seq, LANES), F32),
            jax.ShapeDtypeStruct((nh, 8, LANES), F32),
        ],
        compiler_params=pltpu.CompilerParams(dimension_semantics=("arbitrary",)),
        name="bias_tables",
    )(rel_bias, lam_params)


def _nt_dot(a, b):
    return lax.dot_general(a, b, (((1,), (1,)), ((), ())), preferred_element_type=F32)


def _attn_finish(o, z, subln, lam_init):
    ms = jnp.mean(o * o, axis=-1, keepdims=True)
    on = o * lax.rsqrt(ms + EPS) * subln * (1.0 - lam_init)
    return on * _silu(z)


def _attn_prompt_kernel(lam_ref, q_ref, k_ref, v_ref, km_ref, vm_ref, fb_ref, mb_ref, z_ref, sub_ref, o_ref,
                        m_ref, l_ref, acc_ref, sa_ref, sb_ref, *, lam_init):
    qi = pl.program_id(1)
    t = ATT_BLOCK
    d = HEAD_DIM_A
    hw = 2 * d

    def rep(x, n):
        return jnp.concatenate([x] * n, axis=1) if n > 1 else x

    def lane_fold(p):
        out = p[:, 0:LANES]
        for g in range(1, p.shape[1] // LANES):
            out = out + p[:, g * LANES:(g + 1) * LANES]
        return out

    vm = vm_ref[...]
    mb = mb_ref[0, jnp.minimum(qi, N_NEAR - 1)]
    for c in range(2):
        s = _nt_dot(q_ref[:, c * d:(c + 1) * d], km_ref[:, c * d:(c + 1) * d]) + mb
        m = jnp.broadcast_to(jnp.max(s, axis=-1, keepdims=True), (t, LANES))
        p = jnp.exp2(s - m)
        m_ref[c] = m
        l_ref[c] = p
        acc_ref[c] = jnp.dot(p.astype(BF16), vm, preferred_element_type=F32)

    def qk(start, width):
        rows = pl.ds(start, width)
        return [_nt_dot(q_ref[:, c * d:(c + 1) * d], k_ref[rows, c * d:(c + 1) * d]) for c in range(2)]

    def softmax_pv(s_both, start, width, bias):
        v = v_ref[pl.ds(start, width), :]
        for c in range(2):
            s = s_both[c]
            if bias is not None:
                s = s + bias
            m_old = m_ref[c]
            m_new = jnp.maximum(m_old, jnp.max(s, axis=-1, keepdims=True))
            p = jnp.exp2(s - rep(m_new, width // LANES))
            alpha = jnp.exp2(m_old - m_new)
            m_ref[c] = m_new
            l_ref[c] = alpha * l_ref[c] + lane_fold(p)
            acc_ref[c] = rep(alpha, hw // LANES) * acc_ref[c] + jnp.dot(
                p.astype(BF16), v, preferred_element_type=F32)

    wide = FAR_BLOCKS * t
    far_shift = FAR_BLOCKS.bit_length() - 1
    n_far = jnp.maximum(qi - (N_NEAR - 1), 0)
    n_wide = n_far >> far_shift

    def put(ref, s_both):
        ref[0] = s_both[0]
        ref[1] = s_both[1]

    @pl.when(n_wide > 0)
    def _():
        put(sa_ref, qk(0, wide))

    def far_body(i, carry):
        b0 = pl.multiple_of(i * (2 * wide), wide)
        b1 = pl.multiple_of(b0 + wide, wide)
        b2 = pl.multiple_of(jnp.minimum(b0 + 2 * wide, (n_wide - 1) * wide), wide)
        put(sb_ref, qk(b1, wide))
        softmax_pv([sa_ref[0], sa_ref[1]], b0, wide, None)
        put(sa_ref, qk(b2, wide))
        softmax_pv([sb_ref[0], sb_ref[1]], b1, wide, None)
        return carry

    def far_body2(i, carry):
        far_body(2 * i, carry)
        return far_body(2 * i + 1, carry)
    lax.fori_loop(0, n_wide >> 2, far_body2, 0)
    lax.fori_loop((n_wide >> 2) * 2, n_wide >> 1, far_body, 0)

    @pl.when((n_wide & 1) == 1)
    def _():
        softmax_pv([sa_ref[0], sa_ref[1]], pl.multiple_of((n_wide - 1) * wide, wide), wide, None)

    n_rest = n_far - (n_wide << far_shift)
    for cnt in range(1, FAR_BLOCKS):
        @pl.when(n_rest == cnt)
        def _(cnt=cnt):
            first = n_wide << far_shift
            s_next = qk(pl.multiple_of(first * t, t), t)
            for b in range(cnt):
                s_cur = s_next
                if b + 1 < cnt:
                    s_next = qk(pl.multiple_of((first + b + 1) * t, t), t)
                softmax_pv(s_cur, pl.multiple_of((first + b) * t, t), t, None)

    @pl.when(qi >= N_NEAR - 1)
    def _():
        s_next = qk(pl.multiple_of((qi - (N_NEAR - 1)) * t, t), t)
        for dd in range(N_NEAR - 1, -1, -1):
            s_cur = s_next
            if dd > 0:
                s_next = qk(pl.multiple_of((qi - dd + 1) * t, t), t)
            softmax_pv(s_cur, pl.multiple_of((qi - dd) * t, t), t, fb_ref[0, dd])

    @pl.when(qi < N_NEAR - 1)
    def _():
        for dd in range(N_NEAR - 2, -1, -1):
            @pl.when(qi - dd >= 0)
            def _(dd=dd):
                start = pl.multiple_of((qi - dd) * t, t)
                softmax_pv(qk(start, t), start, t, fb_ref[0, dd])

    inv0 = 1.0 / jnp.sum(l_ref[0], axis=-1, keepdims=True)
    inv1 = lam_ref[0] / jnp.sum(l_ref[1], axis=-1, keepdims=True)
    o = acc_ref[0] * inv0 - acc_ref[1] * inv1
    o_ref[...] = _attn_finish(o, z_ref[...], sub_ref[...], lam_init).astype(BF16)


def _attn_prompt(lam, qkv16, o32, km, vm, fb, mb, subln, n_heads, lam_init):
    seq = qkv16.shape[0]
    t = ATT_BLOCK
    hw = 2 * HEAD_DIM_A
    kern = functools.partial(_attn_prompt_kernel, lam_init=lam_init)
    return pl.pallas_call(
        kern,
        grid=(n_heads, seq // t),
        in_specs=[
            pl.BlockSpec(memory_space=pltpu.SMEM),
            pl.BlockSpec((t, hw), lambda h, i: (i, h)),
            pl.BlockSpec((seq, hw), lambda h, i: (0, n_heads + h)),
            pl.BlockSpec((seq, hw), lambda h, i: (0, 2 * n_heads + h)),
            pl.BlockSpec((LANES, hw), lambda h, i: (0, h)),
            pl.BlockSpec((LANES, hw), lambda h, i: (0, h)),
            pl.BlockSpec((1, N_NEAR, t, t), lambda h, i: (h, 0, 0, 0)),
            pl.BlockSpec((1, N_NEAR, t, LANES), lambda h, i: (h, 0, 0, 0)),
            pl.BlockSpec((t, hw), lambda h, i: (i, 2 * n_heads + h)),
            pl.BlockSpec((1, hw), lambda h, i: (0, 0)),
        ],
        out_specs=pl.BlockSpec((t, hw), lambda h, i: (i, h)),
        out_shape=jax.ShapeDtypeStruct((seq, n_heads * hw), BF16),
        scratch_shapes=[
            pltpu.VMEM((2, t, LANES), F32),
            pltpu.VMEM((2, t, LANES), F32),
            pltpu.VMEM((2, t, hw), F32),
            pltpu.VMEM((2, t, FAR_BLOCKS * t), F32),
            pltpu.VMEM((2, t, FAR_BLOCKS * t), F32),
        ],
        compiler_params=pltpu.CompilerParams(
            dimension_semantics=("arbitrary", "arbitrary"), vmem_limit_bytes=VMEM_LIMIT),
        name="attn_prompt",
    )(lam, qkv16, qkv16, qkv16, km, vm, fb, mb, o32, subln)


def _attn_sample_kernel(lam_ref, q_ref, kc_ref, vc_ref, ke_ref, ve_ref, sa_ref, se_ref, z_ref, sub_ref, o_ref,
                        *, lam_init, n_heads, n_ca, hs):
    d = HEAD_DIM_A
    hw = 2 * d
    for i in range(hs):
        h = pl.program_id(1) * hs + i
        cols = slice(i * hw, (i + 1) * hw)
        vc = jnp.concatenate([vc_ref[0, pl.ds(c * n_heads + h, n_ca, stride=2 * n_heads), :] for c in range(2)],
                             axis=1).astype(BF16)
        ve = ve_ref[0, :, cols]
        outs = []
        for c in range(2):
            q = q_ref[:, i * hw + c * d:i * hw + (c + 1) * d]
            kc = kc_ref[0, pl.ds(2 * h + c, n_ca, stride=2 * n_heads), :].astype(BF16)
            s_a = _nt_dot(q, kc) + sa_ref[i]
            s_e = _nt_dot(q, ke_ref[0, :, i * hw + c * d:i * hw + (c + 1) * d]) + se_ref[i]
            m = jnp.maximum(jnp.max(s_a, axis=-1, keepdims=True), jnp.max(s_e, axis=-1, keepdims=True))
            p_a = jnp.exp2(s_a - m)
            p_e = jnp.exp2(s_e - m)
            l = jnp.sum(p_a, axis=-1, keepdims=True) + jnp.sum(p_e, axis=-1, keepdims=True)
            acc = (jnp.dot(p_a.astype(BF16), vc, preferred_element_type=F32)
                   + jnp.dot(p_e.astype(BF16), ve, preferred_element_type=F32))
            outs.append(acc / l)
        o = outs[0] - lam_ref[0] * outs[1]
        o_ref[:, cols] = _attn_finish(o, z_ref[:, cols], sub_ref[...], lam_init).astype(BF16)


def _attn_sample(lam, qkv16, o32, cache_k, cache_v, ke, ve, sa, se, subln, n_heads, dec_b, dec_seq, n_ca,
                 lam_init):
    hw = 2 * HEAD_DIM_A
    hs = min(SAMPLE_HEADS_PER_STEP, n_heads)
    n_grp = n_heads // hs
    kern = functools.partial(_attn_sample_kernel, lam_init=lam_init, n_heads=n_heads, n_ca=n_ca, hs=hs)
    return pl.pallas_call(
        kern,
        grid=(dec_b, n_grp),
        in_specs=[
            pl.BlockSpec(memory_space=pltpu.SMEM),
            pl.BlockSpec((dec_seq, hs * hw), lambda b, g: (b, g)),
            pl.BlockSpec((1, n_ca * n_heads * 2, HEAD_DIM_A), lambda b, g: (b, 0, 0)),
            pl.BlockSpec((1, n_ca * n_heads * 2, HEAD_DIM_A), lambda b, g: (b, 0, 0)),
            pl.BlockSpec((1, LANES, hs * hw), lambda b, g: (b, 0, g)),
            pl.BlockSpec((1, LANES, hs * hw), lambda b, g: (b, 0, g)),
            pl.BlockSpec((hs, dec_seq, n_ca), lambda b, g: (g, 0, 0)),
            pl.BlockSpec((hs, dec_seq, LANES), lambda b, g: (g, 0, 0)),
            pl.BlockSpec((dec_seq, hs * hw), lambda b, g: (b, 2 * n_grp + g)),
            pl.BlockSpec((1, hw), lambda b, g: (0, 0)),
        ],
        out_specs=pl.BlockSpec((dec_seq, hs * hw), lambda b, g: (b, g)),
        out_shape=jax.ShapeDtypeStruct((dec_b * dec_seq, n_heads * hw), BF16),
        compiler_params=pltpu.CompilerParams(
            dimension_semantics=("arbitrary", "arbitrary"), vmem_limit_bytes=VMEM_LIMIT),
        name="attn_sample",
    )(lam, qkv16, cache_k, cache_v, ke, ve, sa, se, o32, subln)


def _hdot(a, b):
    return jnp.dot(a, b, preferred_element_type=F32, precision=HIGHEST)


def _bdot(a, b):
    return jnp.dot(a.astype(BF16), b.astype(BF16), preferred_element_type=F32)


def _inv_unit_lower(a_list, row, col):
    sh = 4
    n = a_list[0].shape[0]
    eye = (row == col).astype(F32)
    same = (row >> sh) == (col >> sh)
    dblk = [jnp.where(same, a, 0.0) for a in a_list]
    t = [eye - x for x in dblk]
    pw = [x.astype(BF16) for x in dblk]
    for _ in range(sh - 1):
        pw = [jnp.dot(x, x, preferred_element_type=F32).astype(BF16) for x in pw]
        t = [ti + jnp.dot(ti.astype(BF16), x, preferred_element_type=F32) for ti, x in zip(t, pw)]
    while (1 << sh) < n:
        offm = jnp.logical_and((row >> (sh + 1)) == (col >> (sh + 1)), (row >> sh) != (col >> sh))
        off = [jnp.where(offm, a, 0.0).astype(BF16) for a in a_list]
        tb = [ti.astype(BF16) for ti in t]
        mid = [jnp.dot(o, x, preferred_element_type=F32).astype(BF16) for o, x in zip(off, tb)]
        t = [ti - jnp.dot(x, m, preferred_element_type=F32) for ti, x, m in zip(t, tb, mid)]
        sh += 1
    return t


def _gdn_kernel(q_ref, k_ref, v_ref, z_ref, ba_ref, cwq_ref, cwk_ref, cwv_ref, gp_ref, nb_ref, s0_ref,
                hq_ref, hk_ref, hv_ref, y_ref, sout_ref, s_ref, xq_ref, xk_ref, xv_ref, *, lr, hg):
    c = pl.program_id(2)
    n = GDN_CHUNK
    dh = HEAD_DIM_B
    halo = 8

    @pl.when(c == 0)
    def _():
        s_ref[...] = s0_ref[0]
        xq_ref[0:halo, :] = hq_ref[0]
        xk_ref[0:halo, :] = hk_ref[0]
        xv_ref[0:halo, :] = hv_ref[0]

    row_w = lax.broadcasted_iota(jnp.int32, (n, hg * dh), 0)

    def conv(x_ref, xbuf, cw_ref):
        xbuf[halo:halo + lr, :] = x_ref[...]
        if lr < n:
            xbuf[halo + lr:halo + n, :] = jnp.zeros((n - lr, hg * dh), F32)
        y = xbuf[halo - 3:halo - 3 + n, :] * cw_ref[0:1, :]
        for i in range(1, CONV_W):
            y = y + xbuf[halo - 3 + i:halo - 3 + i + n, :] * cw_ref[i:i + 1, :]
        y = _silu(y)
        if lr < n:
            y = jnp.where(row_w < lr, y, 0.0)
        xbuf[0:halo, :] = xbuf[lr:lr + halo, :]
        return y

    yq = conv(q_ref, xq_ref, cwq_ref)
    yk = conv(k_ref, xk_ref, cwk_ref)
    yv = conv(v_ref, xv_ref, cwv_ref)

    row = lax.broadcasted_iota(jnp.int32, (n, n), 0)
    col = lax.broadcasted_iota(jnp.int32, (n, n), 1)
    ba = ba_ref[...]
    if lr < n:
        ba = jnp.concatenate([ba, jnp.zeros((n - lr, LANES), F32)], axis=0)
    live = row < lr
    beta = jnp.where(jnp.logical_and(live, col < hg), jax.nn.sigmoid(ba), 0.0)
    gval = -jnp.exp(gp_ref[0, 0:1, :]) * jax.nn.softplus(ba + gp_ref[0, 1:2, :])
    g = jnp.where(jnp.logical_and(live, jnp.logical_and(col >= hg, col < 2 * hg)), gval, 0.0)
    gsum = _hdot((row >= col).astype(F32), g)
    gsum_t = gsum.T
    incl = row >= col
    strict = row > col

    def run(heads):
        cols = {hh: slice(hh * dh, (hh + 1) * dh) for hh in heads}
        gc = {hh: jnp.broadcast_to(gsum[:, hg + hh:hg + hh + 1], (n, n)) for hh in heads}
        bc = {hh: jnp.broadcast_to(beta[:, hh:hh + 1], (n, n)) for hh in heads}
        gam = {hh: jnp.where(incl, jnp.exp(jnp.minimum(gc[hh] - gsum_t[hg + hh:hg + hh + 1, :], 0.0)), 0.0)
               for hh in heads}
        qh = {hh: yq[:, cols[hh]] * (lax.rsqrt(jnp.sum(yq[:, cols[hh]] * yq[:, cols[hh]], axis=-1, keepdims=True)
                                               + EPS) * (dh ** -0.5)) for hh in heads}
        kh = {hh: yk[:, cols[hh]] * lax.rsqrt(jnp.sum(yk[:, cols[hh]] * yk[:, cols[hh]], axis=-1, keepdims=True)
                                              + EPS) for hh in heads}
        qb = {hh: qh[hh].astype(BF16) for hh in heads}
        kb = {hh: kh[hh].astype(BF16) for hh in heads}
        kk = {hh: _nt_dot(kb[hh], kb[hh]) for hh in heads}
        qk = {hh: _nt_dot(qb[hh], kb[hh]) for hh in heads}
        a = [jnp.where(strict, bc[hh] * kk[hh] * gam[hh], 0.0) for hh in heads]
        t = dict(zip(heads, _inv_unit_lower(a, row, col)))
        eg = {hh: jnp.exp(gc[hh]) for hh in heads}
        rhs = {hh: jnp.concatenate([yv[:, cols[hh]] * bc[hh], kh[hh] * (bc[hh] * eg[hh])], axis=1).astype(BF16)
               for hh in heads}
        uw = {hh: jnp.dot(t[hh].astype(BF16), rhs[hh], preferred_element_type=F32) for hh in heads}
        s = {hh: s_ref[hh] for hh in heads}
        sb = {hh: s[hh].astype(BF16) for hh in heads}
        v_new = {hh: uw[hh][:, :dh] - jnp.dot(uw[hh][:, dh:].astype(BF16), sb[hh], preferred_element_type=F32)
                 for hh in heads}
        vnb = {hh: v_new[hh].astype(BF16) for hh in heads}
        o = {hh: eg[hh] * jnp.dot(qb[hh], sb[hh], preferred_element_type=F32)
             + jnp.dot((qk[hh] * gam[hh]).astype(BF16), vnb[hh], preferred_element_type=F32) for hh in heads}
        g_last = {hh: gc[hh][n - 1:n, :] for hh in heads}
        kd = {hh: (kh[hh] * jnp.exp(g_last[hh] - gc[hh])).astype(BF16) for hh in heads}
        for hh in heads:
            s_ref[hh] = s[hh] * jnp.exp(g_last[hh]) + lax.dot_general(
                kd[hh], vnb[hh], (((0,), (0,)), ((), ())), preferred_element_type=F32)
        for hh in heads:
            on = o[hh] * lax.rsqrt(jnp.mean(o[hh] * o[hh], axis=-1, keepdims=True) + EPS) * nb_ref[...]
            y_ref[:, cols[hh]] = (on[0:lr] * _silu(z_ref[:, cols[hh]])).astype(BF16)

    for first in range(0, hg, GDN_INTERLEAVE):
        run(list(range(first, min(first + GDN_INTERLEAVE, hg))))

    @pl.when(c == pl.num_programs(2) - 1)
    def _():
        sout_ref[0] = s_ref[...]


def _gdn(o32, ba, conv_w, gate_par, norm_b, s0, halo0, *, n_seq, n_chunks, lr, row_blk_off, w_b, col0):
    hg = min(GDN_HEADS_PER_STEP, w_b // HEAD_DIM_B)
    gw = hg * HEAD_DIM_B
    n_groups = w_b // gw
    n_heads = w_b // HEAD_DIM_B
    cb = col0 // gw
    wb = w_b // gw

    def rmap(off):
        return lambda s, g, c: (row_blk_off + s * n_chunks + c, off + g)

    kern = functools.partial(_gdn_kernel, lr=lr, hg=hg)
    return pl.pallas_call(
        kern,
        grid=(n_seq, n_groups, n_chunks),
        in_specs=[
            pl.BlockSpec((lr, gw), rmap(cb)),
            pl.BlockSpec((lr, gw), rmap(cb + wb)),
            pl.BlockSpec((lr, gw), rmap(cb + 2 * wb)),
            pl.BlockSpec((lr, gw), rmap(cb + 3 * wb)),
            pl.BlockSpec((lr, LANES), rmap(0)),
            pl.BlockSpec((CONV_W, gw), lambda s, g, c: (0, g)),
            pl.BlockSpec((CONV_W, gw), lambda s, g, c: (0, wb + g)),
            pl.BlockSpec((CONV_W, gw), lambda s, g, c: (0, 2 * wb + g)),
            pl.BlockSpec((1, 2, LANES), lambda s, g, c: (g, 0, 0)),
            pl.BlockSpec((1, HEAD_DIM_B), lambda s, g, c: (0, 0)),
            pl.BlockSpec((1, hg, HEAD_DIM_B, HEAD_DIM_B), lambda s, g, c: (s, g, 0, 0)),
            pl.BlockSpec((1, 8, gw), lambda s, g, c: (s, 0, g)),
            pl.BlockSpec((1, 8, gw), lambda s, g, c: (s, 0, wb + g)),
            pl.BlockSpec((1, 8, gw), lambda s, g, c: (s, 0, 2 * wb + g)),
        ],
        out_specs=[
            pl.BlockSpec((lr, gw), lambda s, g, c: (s * n_chunks + c, g)),
            pl.BlockSpec((1, hg, HEAD_DIM_B, HEAD_DIM_B), lambda s, g, c: (s, g, 0, 0)),
        ],
        out_shape=[
            jax.ShapeDtypeStruct((n_seq * n_chunks * lr, w_b), BF16),
            jax.ShapeDtypeStruct((n_seq, n_heads, HEAD_DIM_B, HEAD_DIM_B), F32),
        ],
        scratch_shapes=[
            pltpu.VMEM((hg, HEAD_DIM_B, HEAD_DIM_B), F32),
            pltpu.VMEM((8 + GDN_CHUNK, gw), F32),
            pltpu.VMEM((8 + GDN_CHUNK, gw), F32),
            pltpu.VMEM((8 + GDN_CHUNK, gw), F32),
        ],
        compiler_params=pltpu.CompilerParams(
            dimension_semantics=("arbitrary", "arbitrary", "arbitrary"), vmem_limit_bytes=VMEM_LIMIT),
        name=f"gdn_l{lr}",
    )(o32, o32, o32, o32, ba, conv_w, conv_w, conv_w, gate_par, norm_b, s0, halo0, halo0, halo0)


def _out_proj_kernel(ya_ref, yb_ref, w_ref, h_ref, pn_ref, o_ref, *, w_a):
    j = pl.program_id(1)
    tn = w_ref.shape[1]
    r = (jnp.dot(ya_ref[...], w_ref[0:w_a, :], preferred_element_type=F32)
         + jnp.dot(yb_ref[...], w_ref[w_a:, :], preferred_element_type=F32))
    o_ref[:, pl.ds(pl.multiple_of(j * tn, tn), tn)] = r

    @pl.when(j == pl.num_programs(1) - 1)
    def _():
        rows = 64

        def body(i, carry):
            sl = pl.ds(pl.multiple_of(i * rows, rows), rows)
            y = o_ref[sl, :]
            ms = jnp.mean(y * y, axis=-1, keepdims=True)
            o_ref[sl, :] = h_ref[sl, :] + y * lax.rsqrt(ms + EPS) * pn_ref[...]
            return carry
        lax.fori_loop(0, o_ref.shape[0] // rows, body, 0)


def _out_proj(ya, yb, w_out, h, post_norm):
    m, d = h.shape
    w_a = ya.shape[1]
    tm, tn = min(OUT_TM, m), OUT_TN
    kern = functools.partial(_out_proj_kernel, w_a=w_a)
    return pl.pallas_call(
        kern,
        grid=(m // tm, d // tn),
        in_specs=[
            pl.BlockSpec((tm, w_a), lambda i, j: (i, 0)),
            pl.BlockSpec((tm, yb.shape[1]), lambda i, j: (i, 0)),
            pl.BlockSpec((d, tn), lambda i, j: (0, j)),
            pl.BlockSpec((tm, d), lambda i, j: (i, 0)),
            pl.BlockSpec((1, d), lambda i, j: (0, 0)),
        ],
        out_specs=pl.BlockSpec((tm, d), lambda i, j: (i, 0)),
        out_shape=jax.ShapeDtypeStruct((m, d), F32),
        compiler_params=pltpu.CompilerParams(
            dimension_semantics=("arbitrary", "arbitrary"), vmem_limit_bytes=VMEM_LIMIT),
        name="out_proj",
    )(ya, yb, w_out, h, post_norm)


def _kv_layout_kernel(km_ref, vm_ref, k_ref, v_ref, ko_ref, vo_ref, kt_ref, vt_ref, *, n_heads):
    tm = k_ref.shape[0]
    n_lead = kt_ref.shape[0]
    rpt = 2 * n_heads

    @pl.when(pl.program_id(0) == 0)
    def _():
        kt_ref[...] = km_ref[...]
        vt_ref[...] = vm_ref[...]

    for p in range(rpt):
        piece = slice(p * LANES, (p + 1) * LANES)
        vrow = (p % 2) * n_heads + p // 2
        ko_ref[pl.ds(p, n_lead, stride=rpt), :] = kt_ref[:, piece]
        ko_ref[pl.ds(n_lead * rpt + p, tm - n_lead, stride=rpt), :] = k_ref[0:tm - n_lead, piece]
        vo_ref[pl.ds(vrow, n_lead, stride=rpt), :] = vt_ref[:, piece]
        vo_ref[pl.ds(n_lead * rpt + vrow, tm - n_lead, stride=rpt), :] = v_ref[0:tm - n_lead, piece]
    kt_ref[...] = k_ref[tm - n_lead:tm, :]
    vt_ref[...] = v_ref[tm - n_lead:tm, :]


def _kv_layout(k_lead, v_lead, p32, n_heads, w_a):
    seq = p32.shape[0]
    n_lead = k_lead.shape[0]
    tm = KV_TM
    rpt = 2 * n_heads
    n_blk = seq // tm
    spec_out = pl.BlockSpec((tm * rpt, LANES), lambda i: (i, 0))
    shape_out = jax.ShapeDtypeStruct(((n_lead + seq) * rpt, LANES), F32)
    return pl.pallas_call(
        functools.partial(_kv_layout_kernel, n_heads=n_heads),
        grid=(n_blk + 1,),
        in_specs=[
            pl.BlockSpec((n_lead, w_a), lambda i: (0, 0)),
            pl.BlockSpec((n_lead, w_a), lambda i: (0, 0)),
            pl.BlockSpec((tm, w_a), lambda i: (jnp.minimum(i, n_blk - 1), 0)),
            pl.BlockSpec((tm, w_a), lambda i: (jnp.minimum(i, n_blk - 1), 1)),
        ],
        out_specs=[spec_out, spec_out],
        out_shape=[shape_out, shape_out],
        scratch_shapes=[pltpu.VMEM((n_lead, w_a), F32), pltpu.VMEM((n_lead, w_a), F32)],
        compiler_params=pltpu.CompilerParams(dimension_semantics=("arbitrary",), vmem_limit_bytes=VMEM_LIMIT),
        name="kv_layout",
    )(k_lead, v_lead, p32, p32)


def kernel(x_prompt, x_sample, cache_k_a, cache_v_a, state_ssm_b, state_conv_b, meta_tokens, rel_bias, pre_norm,
           w_in, lambda_q1, lambda_k1, lambda_q2, lambda_k2, subln_a, conv_b, a_log_b, dt_bias_b, norm_b, w_out,
           post_norm):
    batch, seq, d_model = x_prompt.shape
    dec_b, dec_seq, _ = x_sample.shape
    depth = w_in.shape[0]
    assert batch == 1 and depth == 1
    n_heads_a = rel_bias.shape[1]
    w_a = n_heads_a * 2 * HEAD_DIM_A
    w_b = d_model - w_a
    n_heads_b = w_b // HEAD_DIM_B
    n_cache = cache_k_a.shape[2]
    past = n_cache - N_META
    n_ca = (n_cache // LANES) * LANES
    n_main = 4 * w_a + 4 * w_b
    lam_init = _lambda_init(0)
    hg = min(GDN_HEADS_PER_STEP, n_heads_b)
    n_groups = n_heads_b // hg
    n_dec = dec_b * dec_seq
    assert seq % PROJ_TM == 0 and seq % GDN_CHUNK == 0 and n_cache - n_ca + dec_seq <= LANES

    w_in_bf = w_in[0].astype(BF16)
    wb = w_in_bf[:, n_main:n_main + n_heads_b].reshape(d_model, n_groups, hg)
    wa = w_in_bf[:, n_main + n_heads_b:].reshape(d_model, n_groups, hg)
    w_tail = jnp.concatenate([wb, wa, jnp.zeros((d_model, n_groups, LANES - 2 * hg), BF16)], axis=-1)
    w_tail = w_tail.reshape(d_model, n_groups * LANES)
    w_out_bf = w_out[0].astype(BF16)
    gate_par = jnp.zeros((n_groups, 2, LANES), F32)
    gate_par = gate_par.at[:, 0, hg:2 * hg].set(a_log_b[0].reshape(n_groups, hg))
    gate_par = gate_par.at[:, 1, hg:2 * hg].set(dt_bias_b[0].reshape(n_groups, hg))
    lam_params = jnp.stack([lambda_q1[0], lambda_k1[0], lambda_q2[0], lambda_k2[0]])

    xp = x_prompt[0]
    n_small = -(-(n_dec + N_META) // 16) * 16
    xs = jnp.concatenate([x_sample.reshape(n_dec, d_model), meta_tokens.astype(F32),
                          jnp.zeros((n_small - n_dec - N_META, d_model), F32)], axis=0)
    p32, p16, pba = _in_proj(xp, pre_norm, w_in_bf, w_tail, w_a, n_main)
    s32, s16, sba = _in_proj(xs, pre_norm, w_in_bf, w_tail, w_a, n_main, single_tile=True)

    fb, mb, sa, se, lam_t = _bias_tables(rel_bias, lam_params, past, dec_seq, n_ca, lam_init)
    lam = lam_t[0, 0, 0:1]

    meta16 = s16[n_dec:n_dec + N_META]
    pad_m = jnp.zeros((LANES - N_META, w_a), BF16)
    km = jnp.concatenate([meta16[:, w_a:2 * w_a], pad_m], axis=0)
    vm = jnp.concatenate([meta16[:, 2 * w_a:], pad_m], axis=0)
    ya_p = _attn_prompt(lam, p16, p32, km, vm, fb, mb, subln_a, n_heads_a, lam_init)

    ck = cache_k_a[0].reshape(dec_b, n_cache * n_heads_a * 2, HEAD_DIM_A)
    cv = cache_v_a[0].reshape(dec_b, n_cache, n_heads_a, 2, HEAD_DIM_A).transpose(0, 1, 3, 2, 4)
    cv = cv.reshape(dec_b, n_cache * 2 * n_heads_a, HEAD_DIM_A)
    n_extra = n_cache - n_ca + dec_seq
    pad_e = jnp.zeros((dec_b, LANES - n_extra, w_a), BF16)
    ck_tail = cache_k_a[0, :, n_ca:].reshape(dec_b, n_cache - n_ca, w_a).astype(BF16)
    cv_tail = cache_v_a[0, :, n_ca:].reshape(dec_b, n_cache - n_ca, w_a).astype(BF16)
    ke = jnp.concatenate([ck_tail, s16[:n_dec, w_a:2 * w_a].reshape(dec_b, dec_seq, w_a), pad_e], axis=1)
    ve = jnp.concatenate([cv_tail, s16[:n_dec, 2 * w_a:].reshape(dec_b, dec_seq, w_a), pad_e], axis=1)
    ya_s = _attn_sample(lam, s16, s32, ck, cv, ke, ve, sa, se, subln_a, n_heads_a, dec_b, dec_seq, n_ca,
                        lam_init)

    col0 = 3 * w_a
    qkv_cols = slice(col0, col0 + 3 * w_b)
    zero_halo = jnp.zeros((1, 8, 3 * w_b), F32)
    gdn = functools.partial(_gdn, conv_w=conv_b[0], gate_par=gate_par, norm_b=norm_b, w_b=w_b, col0=col0)
    _, s_meta = gdn(s32, sba, s0=jnp.zeros((1, n_heads_b, HEAD_DIM_B, HEAD_DIM_B), F32), halo0=zero_halo,
                    n_seq=1, n_chunks=1, lr=N_META, row_blk_off=n_dec // N_META)
    meta_halo = jnp.concatenate([jnp.zeros((8 - (CONV_W - 1), 3 * w_b), F32),
                                 s32[n_dec + N_META - (CONV_W - 1):n_dec + N_META, qkv_cols]], axis=0)[None]
    yb_p, ssm_p = gdn(p32, pba, s0=s_meta, halo0=meta_halo, n_seq=1, n_chunks=seq // GDN_CHUNK, lr=GDN_CHUNK,
                      row_blk_off=0)
    samp_halo = jnp.concatenate([jnp.zeros((dec_b, 8 - (CONV_W - 1), 3 * w_b), F32), state_conv_b[0]], axis=1)
    yb_s, ssm_s = gdn(s32, sba, s0=state_ssm_b[0].astype(F32), halo0=samp_halo, n_seq=dec_b, n_chunks=1,
                      lr=dec_seq, row_blk_off=0)

    y_p = _out_proj(ya_p, yb_p, w_out_bf, xp, post_norm)
    y_s = _out_proj(ya_s, yb_s, w_out_bf, x_sample.reshape(n_dec, d_model), post_norm)

    hd = HEAD_DIM_A
    meta32 = s32[n_dec:n_dec + N_META]
    k_p, v_p = _kv_layout(meta32[:, :w_a], meta32[:, w_a:2 * w_a], p32, n_heads_a, w_a)
    v_p = v_p.reshape(N_META + seq, 2, n_heads_a, hd).transpose(0, 2, 1, 3)
    conv_p = p32[seq - (CONV_W - 1):, qkv_cols]
    k_s = s32[:n_dec, :w_a]
    v_s = s32[:n_dec, w_a:2 * w_a]
    conv_s = s32[:n_dec, qkv_cols].reshape(dec_b, dec_seq, 3 * w_b)[:, dec_seq - (CONV_W - 1):]
    return (
        y_p[None],
        y_s.reshape(dec_b, dec_seq, d_model),
        k_p.reshape(1, 1, N_META + seq, n_heads_a, 2, hd),
        v_p.reshape(1, 1, N_META + seq, n_heads_a, 2 * hd),
        ssm_p[None],
        conv_p[None, None],
        k_s.reshape(1, dec_b, dec_seq, n_heads_a, 2, hd),
        v_s.reshape(1, dec_b, dec_seq, n_heads_a, 2 * hd),
        ssm_s[None],
        conv_s[None],
    )
```

```python
import functools
import math

import jax
import jax.numpy as jnp
from jax import lax
from jax.experimental import pallas as pl
from jax.experimental.pallas import tpu as pltpu

F32 = jnp.float32
BF16 = jnp.bfloat16
HIGHEST = lax.Precision.HIGHEST

EPS = 1e-6
CHUNK = 64
N_META = 16
HEAD_DIM_A = 128
HEAD_DIM_B = 128
CONV_W = 4
N_BUCKETS = 32
MAX_DISTANCE = 1024
NEG = -1e30
LOG2E = math.log2(math.e)

LANES = 128
ATT_BLOCK = 256
FAR_BLOCKS = 8
SAMPLE_HEADS_PER_STEP = 8
GDN_CHUNK = 128
GDN_HEADS_PER_STEP = 16
GDN_INTERLEAVE = 16
PROJ_TM = 512
PROJ_TN = 1024
KV_TM = 512
PROJ_TN_SINGLE = 512
OUT_TM = 512
OUT_TN = 512
VMEM_LIMIT = 56 * 1024 * 1024


def _lambda_init(layer):
    return 0.8 - 0.6 * math.exp(-0.3 * layer)


def _bias_saturation_distance():
    nb = N_BUCKETS // 2
    max_exact = nb // 2
    return int(math.ceil(max_exact * (MAX_DISTANCE / max_exact) ** ((nb - max_exact - 1) / (nb - max_exact)))) + 1


N_NEAR = -(-(_bias_saturation_distance() + ATT_BLOCK - 1) // ATT_BLOCK)
FAR_BUCKET = N_BUCKETS // 2 - 1


def _silu(x):
    return x * jax.nn.sigmoid(x)


def _in_proj_kernel(x_ref, pn_ref, w_ref, wt_ref, o32_ref, o16_ref, ba_ref, xn_ref, *, q_tiles, bf_tiles,
                    q_scale):
    j = pl.program_id(1)
    tm = x_ref.shape[0]
    rows = math.gcd(tm, 64) if tm % 64 == 0 else math.gcd(tm, 80)

    @pl.when(j == 0)
    def _():
        def body(r, carry):
            sl = pl.ds(pl.multiple_of(r * rows, 8), rows)
            x = x_ref[sl, :]
            ms = jnp.mean(x * x, axis=-1, keepdims=True)
            xn_ref[sl, :] = (x * lax.rsqrt(ms + EPS) * pn_ref[...]).astype(BF16)
            return carry
        lax.fori_loop(0, tm // rows, body, 0)
        ba_ref[...] = jnp.dot(xn_ref[...], wt_ref[...], preferred_element_type=F32)

    r = jnp.dot(xn_ref[...], w_ref[...], preferred_element_type=F32)

    @pl.when(j >= q_tiles)
    def _():
        o32_ref[...] = r

    @pl.when(j < q_tiles)
    def _():
        o16_ref[...] = (r * q_scale).astype(BF16)

    @pl.when(jnp.logical_and(j >= q_tiles, j < bf_tiles))
    def _():
        o16_ref[...] = r.astype(BF16)


def _in_proj(x, pre_norm, w_all, w_tail, w_a, n, single_tile=False):
    m, d = x.shape
    tm, tn = PROJ_TM, min(PROJ_TN, w_a)
    x_mode = {}
    if single_tile:
        tm, tn = m, min(PROJ_TN_SINGLE, w_a)
        x_mode = dict(pipeline_mode=pl.Buffered(1))
    q_tiles = w_a // tn
    bf_tiles = 3 * w_a // tn
    nt = w_tail.shape[1]
    kern = functools.partial(_in_proj_kernel, q_tiles=q_tiles, bf_tiles=bf_tiles, q_scale=HEAD_DIM_A ** -0.5 * LOG2E)
    return pl.pallas_call(
        kern,
        grid=(m // tm, n // tn),
        in_specs=[
            pl.BlockSpec((tm, d), lambda i, j: (i, 0), **x_mode),
            pl.BlockSpec((1, d), lambda i, j: (0, 0)),
            pl.BlockSpec((d, tn), lambda i, j: (0, j)),
            pl.BlockSpec((d, nt), lambda i, j: (0, 0)),
        ],
        out_specs=[
            pl.BlockSpec((tm, tn), lambda i, j: (i, jnp.maximum(j - q_tiles, 0))),
            pl.BlockSpec((tm, tn), lambda i, j: (i, jnp.minimum(j, bf_tiles - 1))),
            pl.BlockSpec((tm, nt), lambda i, j: (i, 0)),
        ],
        out_shape=[
            jax.ShapeDtypeStruct((m, n - w_a), F32),
            jax.ShapeDtypeStruct((m, 3 * w_a), BF16),
            jax.ShapeDtypeStruct((m, nt), F32),
        ],
        scratch_shapes=[pltpu.VMEM((tm, d), BF16)],
        compiler_params=pltpu.CompilerParams(
            dimension_semantics=("arbitrary", "arbitrary"), vmem_limit_bytes=VMEM_LIMIT),
        name="in_proj",
    )(x, pre_norm, w_all, w_tail)


def _bias_values(rb_ref, h, rel):
    nb = N_BUCKETS // 2
    max_exact = nb // 2
    n = jnp.abs(rel)
    nf = jnp.maximum(n, 1).astype(F32)
    large = max_exact + (jnp.log(nf / max_exact) / math.log(MAX_DISTANCE / max_exact)
                         * (nb - max_exact)).astype(jnp.int32)
    large = jnp.minimum(large, nb - 1)
    bucket = jnp.where(rel > 0, nb, 0) + jnp.where(n < max_exact, n, large)
    val = jnp.zeros(rel.shape, F32)
    for b in range(N_BUCKETS):
        val = jnp.where(bucket == b, rb_ref[b, h], val)
    return (val - rb_ref[FAR_BUCKET, h]) * LOG2E


def _bias_kernel(rb_ref, lamp_ref, fb_ref, mb_ref, sa_ref, se_ref, lam_ref, *, past, dec_seq, n_ca, lam_init):
    h = pl.program_id(0)
    t = ATT_BLOCK
    shift = CHUNK.bit_length() - 1
    i = lax.broadcasted_iota(jnp.int32, (t, t), 0)
    j = lax.broadcasted_iota(jnp.int32, (t, t), 1)
    for dd in range(N_NEAR):
        val = _bias_values(rb_ref, h, (j - i) - t * dd)
        if dd == 0:
            val = jnp.where((j >> shift) <= (i >> shift), val, NEG)
        fb_ref[0, dd] = val
    i = lax.broadcasted_iota(jnp.int32, (t, LANES), 0)
    j = lax.broadcasted_iota(jnp.int32, (t, LANES), 1)
    for qb in range(N_NEAR):
        val = _bias_values(rb_ref, h, (j - N_META) - (t * qb + i))
        mb_ref[0, qb] = jnp.where(j < N_META, val, NEG)
    i = lax.broadcasted_iota(jnp.int32, (dec_seq, n_ca), 0)
    j = lax.broadcasted_iota(jnp.int32, (dec_seq, n_ca), 1)
    sa_ref[0] = _bias_values(rb_ref, h, (j - N_META) - (past + i))
    i = lax.broadcasted_iota(jnp.int32, (dec_seq, LANES), 0)
    j = lax.broadcasted_iota(jnp.int32, (dec_seq, LANES), 1)
    n_extra = N_META + past - n_ca + dec_seq
    val = _bias_values(rb_ref, h, (n_ca - N_META + j) - (past + i))
    se_ref[0] = jnp.where(j < n_extra, val, NEG)
    lp = lamp_ref[...]
    s1 = jnp.sum(lp[0:1] * lp[1:2], axis=-1, keepdims=True)
    s2 = jnp.sum(lp[2:3] * lp[3:4], axis=-1, keepdims=True)
    lam_ref[0] = jnp.broadcast_to(jnp.exp(s1) - jnp.exp(s2) + lam_init, (8, LANES))


def _bias_tables(rel_bias, lam_params, past, dec_seq, n_ca, lam_init):
    nh = rel_bias.shape[1]
    t = ATT_BLOCK
    kern = functools.partial(_bias_kernel, past=past, dec_seq=dec_seq, n_ca=n_ca, lam_init=lam_init)
    return pl.pallas_call(
        kern,
        grid=(nh,),
        in_specs=[
            pl.BlockSpec(memory_space=pltpu.SMEM),
            pl.BlockSpec((4, HEAD_DIM_A), lambda h: (0, 0)),
        ],
        out_specs=[
            pl.BlockSpec((1, N_NEAR, t, t), lambda h: (h, 0, 0, 0)),
            pl.BlockSpec((1, N_NEAR, t, LANES), lambda h: (h, 0, 0, 0)),
            pl.BlockSpec((1, dec_seq, n_ca), lambda h: (h, 0, 0)),
            pl.BlockSpec((1, dec_seq, LANES), lambda h: (h, 0, 0)),
            pl.BlockSpec((1, 8, LANES), lambda h: (h, 0, 0)),
        ],
        out_shape=[
            jax.ShapeDtypeStruct((nh, N_NEAR, t, t), F32),
            jax.ShapeDtypeStruct((nh, N_NEAR, t, LANES), F32),
            jax.ShapeDtypeStruct((nh, dec_seq, n_ca), F32),
            jax.ShapeDtypeStruct((nh, dec_seq, LANES), F32),
            jax.ShapeDtypeStruct((nh, 8, LANES), F32),
        ],
        compiler_params=pltpu.CompilerParams(dimension_semantics=("arbitrary",)),
        name="bias_tables",
    )(rel_bias, lam_params)


def _nt_dot(a, b):
    return lax.dot_general(a, b, (((1,), (1,)), ((), ())), preferred_element_type=F32)


def _attn_finish(o, z, subln, lam_init):
    ms = jnp.mean(o * o, axis=-1, keepdims=True)
    on = o * lax.rsqrt(ms + EPS) * subln * (1.0 - lam_init)
    return on * _silu(z)


def _attn_prompt_kernel(lam_ref, q_ref, k_ref, v_ref, km_ref, vm_ref, fb_ref, mb_ref, z_ref, sub_ref, o_ref,
                        m_ref, l_ref, acc_ref, sa_ref, sb_ref, *, lam_init):
    qi = pl.program_id(1)
    t = ATT_BLOCK
    d = HEAD_DIM_A
    hw = 2 * d

    def rep(x, n):
        return jnp.concatenate([x] * n, axis=1) if n > 1 else x

    def lane_fold(p):
        out = p[:, 0:LANES]
        for g in range(1, p.shape[1] // LANES):
            out = out + p[:, g * LANES:(g + 1) * LANES]
        return out

    def meta_init():
        vm = vm_ref[...]
        mb = mb_ref[0, jnp.minimum(qi, N_NEAR - 1)]
        for c in range(2):
            s = _nt_dot(q_ref[:, c * d:(c + 1) * d], km_ref[:, c * d:(c + 1) * d]) + mb
            m = jnp.broadcast_to(jnp.max(s, axis=-1, keepdims=True), (t, LANES))
            p = jnp.exp2(s - m)
            m_ref[c] = m
            l_ref[c] = p
            acc_ref[c] = jnp.dot(p.astype(BF16), vm, preferred_element_type=F32)

    def qk(start, width):
        rows = pl.ds(start, width)
        return [_nt_dot(q_ref[:, c * d:(c + 1) * d], k_ref[rows, c * d:(c + 1) * d]) for c in range(2)]

    def softmax_pv(s_both, start, width, bias):
        v = v_ref[pl.ds(start, width), :]
        for c in range(2):
            s = s_both[c]
            if bias is not None:
                s = s + bias
            m_old = m_ref[c]
            m_new = jnp.maximum(m_old, jnp.max(s, axis=-1, keepdims=True))
            p = jnp.exp2(s - rep(m_new, width // LANES))
            alpha = jnp.exp2(m_old - m_new)
            m_ref[c] = m_new
            l_ref[c] = alpha * l_ref[c] + lane_fold(p)
            acc_ref[c] = rep(alpha, hw // LANES) * acc_ref[c] + jnp.dot(
                p.astype(BF16), v, preferred_element_type=F32)

    wide = FAR_BLOCKS * t
    far_shift = FAR_BLOCKS.bit_length() - 1
    n_far = jnp.maximum(qi - (N_NEAR - 1), 0)
    n_wide = n_far >> far_shift

    def put(ref, s_both):
        ref[0] = s_both[0]
        ref[1] = s_both[1]

    @pl.when(n_wide > 0)
    def _():
        put(sa_ref, qk(0, wide))
        meta_init()

    @pl.when(n_wide == 0)
    def _():
        meta_init()

    def far_body(i, carry):
        b0 = pl.multiple_of(i * (2 * wide), wide)
        b1 = pl.multiple_of(b0 + wide, wide)
        b2 = pl.multiple_of(jnp.minimum(b0 + 2 * wide, (n_wide - 1) * wide), wide)
        put(sb_ref, qk(b1, wide))
        softmax_pv([sa_ref[0], sa_ref[1]], b0, wide, None)
        put(sa_ref, qk(b2, wide))
        softmax_pv([sb_ref[0], sb_ref[1]], b1, wide, None)
        return carry

    def far_body2(i, carry):
        far_body(2 * i, carry)
        return far_body(2 * i + 1, carry)
    lax.fori_loop(0, n_wide >> 2, far_body2, 0)
    lax.fori_loop((n_wide >> 2) * 2, n_wide >> 1, far_body, 0)

    @pl.when((n_wide & 1) == 1)
    def _():
        softmax_pv([sa_ref[0], sa_ref[1]], pl.multiple_of((n_wide - 1) * wide, wide), wide, None)

    n_rest = n_far - (n_wide << far_shift)
    for cnt in range(1, FAR_BLOCKS):
        @pl.when(n_rest == cnt)
        def _(cnt=cnt):
            first = n_wide << far_shift
            s_next = qk(pl.multiple_of(first * t, t), t)
            for b in range(cnt):
                s_cur = s_next
                if b + 1 < cnt:
                    s_next = qk(pl.multiple_of((first + b + 1) * t, t), t)
                softmax_pv(s_cur, pl.multiple_of((first + b) * t, t), t, None)

    @pl.when(qi >= N_NEAR - 1)
    def _():
        s_next = qk(pl.multiple_of((qi - (N_NEAR - 1)) * t, t), t)
        for dd in range(N_NEAR - 1, -1, -1):
            s_cur = s_next
            if dd > 0:
                s_next = qk(pl.multiple_of((qi - dd + 1) * t, t), t)
            softmax_pv(s_cur, pl.multiple_of((qi - dd) * t, t), t, fb_ref[0, dd])

    @pl.when(qi < N_NEAR - 1)
    def _():
        for dd in range(N_NEAR - 2, -1, -1):
            @pl.when(qi - dd >= 0)
            def _(dd=dd):
                start = pl.multiple_of((qi - dd) * t, t)
                softmax_pv(qk(start, t), start, t, fb_ref[0, dd])

    inv0 = 1.0 / jnp.sum(l_ref[0], axis=-1, keepdims=True)
    inv1 = lam_ref[0] / jnp.sum(l_ref[1], axis=-1, keepdims=True)
    o = acc_ref[0] * inv0 - acc_ref[1] * inv1
    o_ref[...] = _attn_finish(o, z_ref[...], sub_ref[...], lam_init).astype(BF16)


def _attn_prompt(lam, qkv16, o32, km, vm, fb, mb, subln, n_heads, lam_init):
    seq = qkv16.shape[0]
    t = ATT_BLOCK
    hw = 2 * HEAD_DIM_A
    kern = functools.partial(_attn_prompt_kernel, lam_init=lam_init)
    return pl.pallas_call(
        kern,
        grid=(n_heads, seq // t),
        in_specs=[
            pl.BlockSpec(memory_space=pltpu.SMEM),
            pl.BlockSpec((t, hw), lambda h, i: (i, h)),
            pl.BlockSpec((seq, hw), lambda h, i: (0, n_heads + h)),
            pl.BlockSpec((seq, hw), lambda h, i: (0, 2 * n_heads + h)),
            pl.BlockSpec((LANES, hw), lambda h, i: (0, h)),
            pl.BlockSpec((LANES, hw), lambda h, i: (0, h)),
            pl.BlockSpec((1, N_NEAR, t, t), lambda h, i: (h, 0, 0, 0)),
            pl.BlockSpec((1, N_NEAR, t, LANES), lambda h, i: (h, 0, 0, 0)),
            pl.BlockSpec((t, hw), lambda h, i: (i, 2 * n_heads + h)),
            pl.BlockSpec((1, hw), lambda h, i: (0, 0)),
        ],
        out_specs=pl.BlockSpec((t, hw), lambda h, i: (i, h)),
        out_shape=jax.ShapeDtypeStruct((seq, n_heads * hw), BF16),
        scratch_shapes=[
            pltpu.VMEM((2, t, LANES), F32),
            pltpu.VMEM((2, t, LANES), F32),
            pltpu.VMEM((2, t, hw), F32),
            pltpu.VMEM((2, t, FAR_BLOCKS * t), F32),
            pltpu.VMEM((2, t, FAR_BLOCKS * t), F32),
        ],
        compiler_params=pltpu.CompilerParams(
            dimension_semantics=("arbitrary", "arbitrary"), vmem_limit_bytes=VMEM_LIMIT),
        name="attn_prompt",
    )(lam, qkv16, qkv16, qkv16, km, vm, fb, mb, o32, subln)


def _attn_sample_kernel(lam_ref, q_ref, kc_ref, vc_ref, ke_ref, ve_ref, sa_ref, se_ref, z_ref, sub_ref, o_ref,
                        *, lam_init, n_heads, n_ca, hs):
    d = HEAD_DIM_A
    hw = 2 * d
    for i in range(hs):
        h = pl.program_id(1) * hs + i
        cols = slice(i * hw, (i + 1) * hw)
        vc = jnp.concatenate([vc_ref[0, pl.ds(c * n_heads + h, n_ca, stride=2 * n_heads), :] for c in range(2)],
                             axis=1).astype(BF16)
        ve = ve_ref[0, :, cols]
        outs = []
        for c in range(2):
            q = q_ref[:, i * hw + c * d:i * hw + (c + 1) * d]
            kc = kc_ref[0, pl.ds(2 * h + c, n_ca, stride=2 * n_heads), :].astype(BF16)
            s_a = _nt_dot(q, kc) + sa_ref[i]
            s_e = _nt_dot(q, ke_ref[0, :, i * hw + c * d:i * hw + (c + 1) * d]) + se_ref[i]
            m = jnp.maximum(jnp.max(s_a, axis=-1, keepdims=True), jnp.max(s_e, axis=-1, keepdims=True))
            p_a = jnp.exp2(s_a - m)
            p_e = jnp.exp2(s_e - m)
            l = jnp.sum(p_a, axis=-1, keepdims=True) + jnp.sum(p_e, axis=-1, keepdims=True)
            acc = (jnp.dot(p_a.astype(BF16), vc, preferred_element_type=F32)
                   + jnp.dot(p_e.astype(BF16), ve, preferred_element_type=F32))
            outs.append(acc / l)
        o = outs[0] - lam_ref[0] * outs[1]
        o_ref[:, cols] = _attn_finish(o, z_ref[:, cols], sub_ref[...], lam_init).astype(BF16)


def _attn_sample(lam, qkv16, o32, cache_k, cache_v, ke, ve, sa, se, subln, n_heads, dec_b, dec_seq, n_ca,
                 lam_init):
    hw = 2 * HEAD_DIM_A
    hs = min(SAMPLE_HEADS_PER_STEP, n_heads)
    n_grp = n_heads // hs
    kern = functools.partial(_attn_sample_kernel, lam_init=lam_init, n_heads=n_heads, n_ca=n_ca, hs=hs)
    return pl.pallas_call(
        kern,
        grid=(dec_b, n_grp),
        in_specs=[
            pl.BlockSpec(memory_space=pltpu.SMEM),
            pl.BlockSpec((dec_seq, hs * hw), lambda b, g: (b, g)),
            pl.BlockSpec((1, n_ca * n_heads * 2, HEAD_DIM_A), lambda b, g: (b, 0, 0)),
            pl.BlockSpec((1, n_ca * n_heads * 2, HEAD_DIM_A), lambda b, g: (b, 0, 0)),
            pl.BlockSpec((1, LANES, hs * hw), lambda b, g: (b, 0, g)),
            pl.BlockSpec((1, LANES, hs * hw), lambda b, g: (b, 0, g)),
            pl.BlockSpec((hs, dec_seq, n_ca), lambda b, g: (g, 0, 0)),
            pl.BlockSpec((hs, dec_seq, LANES), lambda b, g: (g, 0, 0)),
            pl.BlockSpec((dec_seq, hs * hw), lambda b, g: (b, 2 * n_grp + g)),
            pl.BlockSpec((1, hw), lambda b, g: (0, 0)),
        ],
        out_specs=pl.BlockSpec((dec_seq, hs * hw), lambda b, g: (b, g)),
        out_shape=jax.ShapeDtypeStruct((dec_b * dec_seq, n_heads * hw), BF16),
        compiler_params=pltpu.CompilerParams(
            dimension_semantics=("arbitrary", "arbitrary"), vmem_limit_bytes=VMEM_LIMIT),
        name="attn_sample",
    )(lam, qkv16, cache_k, cache_v, ke, ve, sa, se, o32, subln)


def _hdot(a, b):
    return jnp.dot(a, b, preferred_element_type=F32, precision=HIGHEST)


def _bdot(a, b):
    return jnp.dot(a.astype(BF16), b.astype(BF16), preferred_element_type=F32)


def _inv_unit_lower(a_list, row, col):
    sh = 4
    n = a_list[0].shape[0]
    eye = (row == col).astype(F32)
    same = (row >> sh) == (col >> sh)
    dblk = [jnp.where(same, a, 0.0) for a in a_list]
    t = [eye - x for x in dblk]
    pw = [x.astype(BF16) for x in dblk]
    for _ in range(sh - 1):
        pw = [jnp.dot(x, x, preferred_element_type=F32).astype(BF16) for x in pw]
        t = [ti + jnp.dot(ti.astype(BF16), x, preferred_element_type=F32) for ti, x in zip(t, pw)]
    while (1 << sh) < n:
        offm = jnp.logical_and((row >> (sh + 1)) == (col >> (sh + 1)), (row >> sh) != (col >> sh))
        off = [jnp.where(offm, a, 0.0).astype(BF16) for a in a_list]
        tb = [ti.astype(BF16) for ti in t]
        mid = [jnp.dot(o, x, preferred_element_type=F32).astype(BF16) for o, x in zip(off, tb)]
        t = [ti - jnp.dot(x, m, preferred_element_type=F32) for ti, x, m in zip(t, tb, mid)]
        sh += 1
    return t


def _gdn_kernel(q_ref, k_ref, v_ref, z_ref, ba_ref, cwq_ref, cwk_ref, cwv_ref, gp_ref, nb_ref, s0_ref,
                hq_ref, hk_ref, hv_ref, y_ref, sout_ref, s_ref, xq_ref, xk_ref, xv_ref, *, lr, hg):
    c = pl.program_id(2)
    n = GDN_CHUNK
    dh = HEAD_DIM_B
    halo = 8

    @pl.when(c == 0)
    def _():
        s_ref[...] = s0_ref[0]
        xq_ref[0:halo, :] = hq_ref[0]
        xk_ref[0:halo, :] = hk_ref[0]
        xv_ref[0:halo, :] = hv_ref[0]

    row_w = lax.broadcasted_iota(jnp.int32, (n, hg * dh), 0)

    def conv(x_ref, xbuf, cw_ref):
        xbuf[halo:halo + lr, :] = x_ref[...]
        if lr < n:
            xbuf[halo + lr:halo + n, :] = jnp.zeros((n - lr, hg * dh), F32)
        y = xbuf[halo - 3:halo - 3 + n, :] * cw_ref[0:1, :]
        for i in range(1, CONV_W):
            y = y + xbuf[halo - 3 + i:halo - 3 + i + n, :] * cw_ref[i:i + 1, :]
        y = _silu(y)
        if lr < n:
            y = jnp.where(row_w < lr, y, 0.0)
        xbuf[0:halo, :] = xbuf[lr:lr + halo, :]
        return y

    yq = conv(q_ref, xq_ref, cwq_ref)
    yk = conv(k_ref, xk_ref, cwk_ref)
    yv = conv(v_ref, xv_ref, cwv_ref)

    row = lax.broadcasted_iota(jnp.int32, (n, n), 0)
    col = lax.broadcasted_iota(jnp.int32, (n, n), 1)
    ba = ba_ref[...]
    if lr < n:
        ba = jnp.concatenate([ba, jnp.zeros((n - lr, LANES), F32)], axis=0)
    live = row < lr
    beta = jnp.where(jnp.logical_and(live, col < hg), jax.nn.sigmoid(ba), 0.0)
    gval = -jnp.exp(gp_ref[0, 0:1, :]) * jax.nn.softplus(ba + gp_ref[0, 1:2, :])
    g = jnp.where(jnp.logical_and(live, jnp.logical_and(col >= hg, col < 2 * hg)), gval, 0.0)
    gsum = _hdot((row >= col).astype(F32), g)
    gsum_t = gsum.T
    incl = row >= col
    strict = row > col

    def run(heads):
        cols = {hh: slice(hh * dh, (hh + 1) * dh) for hh in heads}
        gc = {hh: jnp.broadcast_to(gsum[:, hg + hh:hg + hh + 1], (n, n)) for hh in heads}
        bc = {hh: jnp.broadcast_to(beta[:, hh:hh + 1], (n, n)) for hh in heads}
        gam = {hh: jnp.where(incl, jnp.exp(jnp.minimum(gc[hh] - gsum_t[hg + hh:hg + hh + 1, :], 0.0)), 0.0)
               for hh in heads}
        qh = {hh: yq[:, cols[hh]] * (lax.rsqrt(jnp.sum(yq[:, cols[hh]] * yq[:, cols[hh]], axis=-1, keepdims=True)
                                               + EPS) * (dh ** -0.5)) for hh in heads}
        kh = {hh: yk[:, cols[hh]] * lax.rsqrt(jnp.sum(yk[:, cols[hh]] * yk[:, cols[hh]], axis=-1, keepdims=True)
                                              + EPS) for hh in heads}
        qb = {hh: qh[hh].astype(BF16) for hh in heads}
        kb = {hh: kh[hh].astype(BF16) for hh in heads}
        kk = {hh: _nt_dot(kb[hh], kb[hh]) for hh in heads}
        qk = {hh: _nt_dot(qb[hh], kb[hh]) for hh in heads}
        a = [jnp.where(strict, bc[hh] * kk[hh] * gam[hh], 0.0) for hh in heads]
        t = dict(zip(heads, _inv_unit_lower(a, row, col)))
        eg = {hh: jnp.exp(gc[hh]) for hh in heads}
        rhs = {hh: jnp.concatenate([yv[:, cols[hh]] * bc[hh], kh[hh] * (bc[hh] * eg[hh])], axis=1).astype(BF16)
               for hh in heads}
        uw = {hh: jnp.dot(t[hh].astype(BF16), rhs[hh], preferred_element_type=F32) for hh in heads}
        s = {hh: s_ref[hh] for hh in heads}
        sb = {hh: s[hh].astype(BF16) for hh in heads}
        v_new = {hh: uw[hh][:, :dh] - jnp.dot(uw[hh][:, dh:].astype(BF16), sb[hh], preferred_element_type=F32)
                 for hh in heads}
        vnb = {hh: v_new[hh].astype(BF16) for hh in heads}
        o = {hh: eg[hh] * jnp.dot(qb[hh], sb[hh], preferred_element_type=F32)
             + jnp.dot((qk[hh] * gam[hh]).astype(BF16), vnb[hh], preferred_element_type=F32) for hh in heads}
        g_last = {hh: gc[hh][n - 1:n, :] for hh in heads}
        kd = {hh: (kh[hh] * jnp.exp(g_last[hh] - gc[hh])).astype(BF16) for hh in heads}
        for hh in heads:
            s_ref[hh] = s[hh] * jnp.exp(g_last[hh]) + lax.dot_general(
                kd[hh], vnb[hh], (((0,), (0,)), ((), ())), preferred_element_type=F32)
        for hh in heads:
            on = o[hh] * lax.rsqrt(jnp.mean(o[hh] * o[hh], axis=-1, keepdims=True) + EPS) * nb_ref[...]
            y_ref[:, cols[hh]] = (on[0:lr] * _silu(z_ref[:, cols[hh]])).astype(BF16)

    for first in range(0, hg, GDN_INTERLEAVE):
        run(list(range(first, min(first + GDN_INTERLEAVE, hg))))

    @pl.when(c == pl.num_programs(2) - 1)
    def _():
        sout_ref[0] = s_ref[...]


def _gdn(o32, ba, conv_w, gate_par, norm_b, s0, halo0, *, n_seq, n_chunks, lr, row_blk_off, w_b, col0):
    hg = min(GDN_HEADS_PER_STEP, w_b // HEAD_DIM_B)
    gw = hg * HEAD_DIM_B
    n_groups = w_b // gw
    n_heads = w_b // HEAD_DIM_B
    cb = col0 // gw
    wb = w_b // gw

    def rmap(off):
        return lambda s, g, c: (row_blk_off + s * n_chunks + c, off + g)

    kern = functools.partial(_gdn_kernel, lr=lr, hg=hg)
    return pl.pallas_call(
        kern,
        grid=(n_seq, n_groups, n_chunks),
        in_specs=[
            pl.BlockSpec((lr, gw), rmap(cb)),
            pl.BlockSpec((lr, gw), rmap(cb + wb)),
            pl.BlockSpec((lr, gw), rmap(cb + 2 * wb)),
            pl.BlockSpec((lr, gw), rmap(cb + 3 * wb)),
            pl.BlockSpec((lr, LANES), rmap(0)),
            pl.BlockSpec((CONV_W, gw), lambda s, g, c: (0, g)),
            pl.BlockSpec((CONV_W, gw), lambda s, g, c: (0, wb + g)),
            pl.BlockSpec((CONV_W, gw), lambda s, g, c: (0, 2 * wb + g)),
            pl.BlockSpec((1, 2, LANES), lambda s, g, c: (g, 0, 0)),
            pl.BlockSpec((1, HEAD_DIM_B), lambda s, g, c: (0, 0)),
            pl.BlockSpec((1, hg, HEAD_DIM_B, HEAD_DIM_B), lambda s, g, c: (s, g, 0, 0)),
            pl.BlockSpec((1, 8, gw), lambda s, g, c: (s, 0, g)),
            pl.BlockSpec((1, 8, gw), lambda s, g, c: (s, 0, wb + g)),
            pl.BlockSpec((1, 8, gw), lambda s, g, c: (s, 0, 2 * wb + g)),
        ],
        out_specs=[
            pl.BlockSpec((lr, gw), lambda s, g, c: (s * n_chunks + c, g)),
            pl.BlockSpec((1, hg, HEAD_DIM_B, HEAD_DIM_B), lambda s, g, c: (s, g, 0, 0)),
        ],
        out_shape=[
            jax.ShapeDtypeStruct((n_seq * n_chunks * lr, w_b), BF16),
            jax.ShapeDtypeStruct((n_seq, n_heads, HEAD_DIM_B, HEAD_DIM_B), F32),
        ],
        scratch_shapes=[
            pltpu.VMEM((hg, HEAD_DIM_B, HEAD_DIM_B), F32),
            pltpu.VMEM((8 + GDN_CHUNK, gw), F32),
            pltpu.VMEM((8 + GDN_CHUNK, gw), F32),
            pltpu.VMEM((8 + GDN_CHUNK, gw), F32),
        ],
        compiler_params=pltpu.CompilerParams(
            dimension_semantics=("arbitrary", "arbitrary", "arbitrary"), vmem_limit_bytes=VMEM_LIMIT),
        name=f"gdn_l{lr}",
    )(o32, o32, o32, o32, ba, conv_w, conv_w, conv_w, gate_par, norm_b, s0, halo0, halo0, halo0)


def _out_proj_kernel(ya_ref, yb_ref, w_ref, h_ref, pn_ref, o_ref, *, w_a):
    j = pl.program_id(1)
    tn = w_ref.shape[1]
    r = (jnp.dot(ya_ref[...], w_ref[0:w_a, :], preferred_element_type=F32)
         + jnp.dot(yb_ref[...], w_ref[w_a:, :], preferred_element_type=F32))
    o_ref[:, pl.ds(pl.multiple_of(j * tn, tn), tn)] = r

    @pl.when(j == pl.num_programs(1) - 1)
    def _():
        rows = 64

        def body(i, carry):
            sl = pl.ds(pl.multiple_of(i * rows, rows), rows)
            y = o_ref[sl, :]
            ms = jnp.mean(y * y, axis=-1, keepdims=True)
            o_ref[sl, :] = h_ref[sl, :] + y * lax.rsqrt(ms + EPS) * pn_ref[...]
            return carry
        lax.fori_loop(0, o_ref.shape[0] // rows, body, 0)


def _out_proj(ya, yb, w_out, h, post_norm):
    m, d = h.shape
    w_a = ya.shape[1]
    tm, tn = min(OUT_TM, m), OUT_TN
    kern = functools.partial(_out_proj_kernel, w_a=w_a)
    return pl.pallas_call(
        kern,
        grid=(m // tm, d // tn),
        in_specs=[
            pl.BlockSpec((tm, w_a), lambda i, j: (i, 0)),
            pl.BlockSpec((tm, yb.shape[1]), lambda i, j: (i, 0)),
            pl.BlockSpec((d, tn), lambda i, j: (0, j)),
            pl.BlockSpec((tm, d), lambda i, j: (i, 0)),
            pl.BlockSpec((1, d), lambda i, j: (0, 0)),
        ],
        out_specs=pl.BlockSpec((tm, d), lambda i, j: (i, 0)),
        out_shape=jax.ShapeDtypeStruct((m, d), F32),
        compiler_params=pltpu.CompilerParams(
            dimension_semantics=("arbitrary", "arbitrary"), vmem_limit_bytes=VMEM_LIMIT),
        name="out_proj",
    )(ya, yb, w_out, h, post_norm)


def _kv_layout_kernel(km_ref, vm_ref, k_ref, v_ref, ko_ref, vo_ref, kt_ref, vt_ref, *, n_heads):
    tm = k_ref.shape[0]
    n_lead = kt_ref.shape[0]
    rpt = 2 * n_heads

    @pl.when(pl.program_id(0) == 0)
    def _():
        kt_ref[...] = km_ref[...]
        vt_ref[...] = vm_ref[...]

    for p in range(rpt):
        piece = slice(p * LANES, (p + 1) * LANES)
        vrow = (p % 2) * n_heads + p // 2
        ko_ref[pl.ds(p, n_lead, stride=rpt), :] = kt_ref[:, piece]
        ko_ref[pl.ds(n_lead * rpt + p, tm - n_lead, stride=rpt), :] = k_ref[0:tm - n_lead, piece]
        vo_ref[pl.ds(vrow, n_lead, stride=rpt), :] = vt_ref[:, piece]
        vo_ref[pl.ds(n_lead * rpt + vrow, tm - n_lead, stride=rpt), :] = v_ref[0:tm - n_lead, piece]
    kt_ref[...] = k_ref[tm - n_lead:tm, :]
    vt_ref[...] = v_ref[tm - n_lead:tm, :]


def _kv_layout(k_lead, v_lead, p32, n_heads, w_a):
    seq = p32.shape[0]
    n_lead = k_lead.shape[0]
    tm = KV_TM
    rpt = 2 * n_heads
    n_blk = seq // tm
    spec_out = pl.BlockSpec((tm * rpt, LANES), lambda i: (i, 0))
    shape_out = jax.ShapeDtypeStruct(((n_lead + seq) * rpt, LANES), F32)
    return pl.pallas_call(
        functools.partial(_kv_layout_kernel, n_heads=n_heads),
        grid=(n_blk + 1,),
        in_specs=[
            pl.BlockSpec((n_lead, w_a), lambda i: (0, 0)),
            pl.BlockSpec((n_lead, w_a), lambda i: (0, 0)),
            pl.BlockSpec((tm, w_a), lambda i: (jnp.minimum(i, n_blk - 1), 0)),
            pl.BlockSpec((tm, w_a), lambda i: (jnp.minimum(i, n_blk - 1), 1)),
        ],
        out_specs=[spec_out, spec_out],
        out_shape=[shape_out, shape_out],
        scratch_shapes=[pltpu.VMEM((n_lead, w_a), F32), pltpu.VMEM((n_lead, w_a), F32)],
        compiler_params=pltpu.CompilerParams(dimension_semantics=("arbitrary",), vmem_limit_bytes=VMEM_LIMIT),
        name="kv_layout",
    )(k_lead, v_lead, p32, p32)


def kernel(x_prompt, x_sample, cache_k_a, cache_v_a, state_ssm_b, state_conv_b, meta_tokens, rel_bias, pre_norm,
           w_in, lambda_q1, lambda_k1, lambda_q2, lambda_k2, subln_a, conv_b, a_log_b, dt_bias_b, norm_b, w_out,
           post_norm):
    batch, seq, d_model = x_prompt.shape
    dec_b, dec_seq, _ = x_sample.shape
    depth = w_in.shape[0]
    assert batch == 1 and depth == 1
    n_heads_a = rel_bias.shape[1]
    w_a = n_heads_a * 2 * HEAD_DIM_A
    w_b = d_model - w_a
    n_heads_b = w_b // HEAD_DIM_B
    n_cache = cache_k_a.shape[2]
    past = n_cache - N_META
    n_ca = (n_cache // LANES) * LANES
    n_main = 4 * w_a + 4 * w_b
    lam_init = _lambda_init(0)
    hg = min(GDN_HEADS_PER_STEP, n_heads_b)
    n_groups = n_heads_b // hg
    n_dec = dec_b * dec_seq
    assert seq % PROJ_TM == 0 and seq % GDN_CHUNK == 0 and n_cache - n_ca + dec_seq <= LANES

    w_in_bf = w_in[0].astype(BF16)
    wb = w_in_bf[:, n_main:n_main + n_heads_b].reshape(d_model, n_groups, hg)
    wa = w_in_bf[:, n_main + n_heads_b:].reshape(d_model, n_groups, hg)
    w_tail = jnp.concatenate([wb, wa, jnp.zeros((d_model, n_groups, LANES - 2 * hg), BF16)], axis=-1)
    w_tail = w_tail.reshape(d_model, n_groups * LANES)
    w_out_bf = w_out[0].astype(BF16)
    gate_par = jnp.zeros((n_groups, 2, LANES), F32)
    gate_par = gate_par.at[:, 0, hg:2 * hg].set(a_log_b[0].reshape(n_groups, hg))
    gate_par = gate_par.at[:, 1, hg:2 * hg].set(dt_bias_b[0].reshape(n_groups, hg))
    lam_params = jnp.stack([lambda_q1[0], lambda_k1[0], lambda_q2[0], lambda_k2[0]])

    xp = x_prompt[0]
    n_small = -(-(n_dec + N_META) // 16) * 16
    xs = jnp.concatenate([x_sample.reshape(n_dec, d_model), meta_tokens.astype(F32),
                          jnp.zeros((n_small - n_dec - N_META, d_model), F32)], axis=0)
    p32, p16, pba = _in_proj(xp, pre_norm, w_in_bf, w_tail, w_a, n_main)
    s32, s16, sba = _in_proj(xs, pre_norm, w_in_bf, w_tail, w_a, n_main, single_tile=True)

    fb, mb, sa, se, lam_t = _bias_tables(rel_bias, lam_params, past, dec_seq, n_ca, lam_init)
    lam = lam_t[0, 0, 0:1]

    meta16 = s16[n_dec:n_dec + N_META]
    pad_m = jnp.zeros((LANES - N_META, w_a), BF16)
    km = jnp.concatenate([meta16[:, w_a:2 * w_a], pad_m], axis=0)
    vm = jnp.concatenate([meta16[:, 2 * w_a:], pad_m], axis=0)
    ya_p = _attn_prompt(lam, p16, p32, km, vm, fb, mb, subln_a, n_heads_a, lam_init)

    ck = cache_k_a[0].reshape(dec_b, n_cache * n_heads_a * 2, HEAD_DIM_A)
    cv = cache_v_a[0].reshape(dec_b, n_cache, n_heads_a, 2, HEAD_DIM_A).transpose(0, 1, 3, 2, 4)
    cv = cv.reshape(dec_b, n_cache * 2 * n_heads_a, HEAD_DIM_A)
    n_extra = n_cache - n_ca + dec_seq
    pad_e = jnp.zeros((dec_b, LANES - n_extra, w_a), BF16)
    ck_tail = cache_k_a[0, :, n_ca:].reshape(dec_b, n_cache - n_ca, w_a).astype(BF16)
    cv_tail = cache_v_a[0, :, n_ca:].reshape(dec_b, n_cache - n_ca, w_a).astype(BF16)
    ke = jnp.concatenate([ck_tail, s16[:n_dec, w_a:2 * w_a].reshape(dec_b, dec_seq, w_a), pad_e], axis=1)
    ve = jnp.concatenate([cv_tail, s16[:n_dec, 2 * w_a:].reshape(dec_b, dec_seq, w_a), pad_e], axis=1)
    ya_s = _attn_sample(lam, s16, s32, ck, cv, ke, ve, sa, se, subln_a, n_heads_a, dec_b, dec_seq, n_ca,
                        lam_init)

    col0 = 3 * w_a
    qkv_cols = slice(col0, col0 + 3 * w_b)
    zero_halo = jnp.zeros((1, 8, 3 * w_b), F32)
    gdn = functools.partial(_gdn, conv_w=conv_b[0], gate_par=gate_par, norm_b=norm_b, w_b=w_b, col0=col0)
    _, s_meta = gdn(s32, sba, s0=jnp.zeros((1, n_heads_b, HEAD_DIM_B, HEAD_DIM_B), F32), halo0=zero_halo,
                    n_seq=1, n_chunks=1, lr=N_META, row_blk_off=n_dec // N_META)
    meta_halo = jnp.concatenate([jnp.zeros((8 - (CONV_W - 1), 3 * w_b), F32),
                                 s32[n_dec + N_META - (CONV_W - 1):n_dec + N_META, qkv_cols]], axis=0)[None]
    yb_p, ssm_p = gdn(p32, pba, s0=s_meta, halo0=meta_halo, n_seq=1, n_chunks=seq // GDN_CHUNK, lr=GDN_CHUNK,
                      row_blk_off=0)
    samp_halo = jnp.concatenate([jnp.zeros((dec_b, 8 - (CONV_W - 1), 3 * w_b), F32), state_conv_b[0]], axis=1)
    yb_s, ssm_s = gdn(s32, sba, s0=state_ssm_b[0].astype(F32), halo0=samp_halo, n_seq=dec_b, n_chunks=1,
                      lr=dec_seq, row_blk_off=0)

    y_p = _out_proj(ya_p, yb_p, w_out_bf, xp, post_norm)
    y_s = _out_proj(ya_s, yb_s, w_out_bf, x_sample.reshape(n_dec, d_model), post_norm)

    hd = HEAD_DIM_A
    meta32 = s32[n_dec:n_dec + N_META]
    k_p, v_p = _kv_layout(meta32[:, :w_a], meta32[:, w_a:2 * w_a], p32, n_heads_a, w_a)
    v_p = v_p.reshape(N_META + seq, 2, n_heads_a, hd).transpose(0, 2, 1, 3)
    conv_p = p32[seq - (CONV_W - 1):, qkv_cols]
    k_s = s32[:n_dec, :w_a]
    v_s = s32[:n_dec, w_a:2 * w_a]
    conv_s = s32[:n_dec, qkv_cols].reshape(dec_b, dec_seq, 3 * w_b)[:, dec_seq - (CONV_W - 1):]
    return (
        y_p[None],
        y_s.reshape(dec_b, dec_seq, d_model),
        k_p.reshape(1, 1, N_META + seq, n_heads_a, 2, hd),
        v_p.reshape(1, 1, N_META + seq, n_heads_a, 2 * hd),
        ssm_p[None],
        conv_p[None, None],
        k_s.reshape(1, dec_b, dec_seq, n_heads_a, 2, hd),
        v_s.reshape(1, dec_b, dec_seq, n_heads_a, 2 * hd),
        ssm_s[None],
        conv_s[None],
    )
```

```python
import functools
import math

import jax
import jax.numpy as jnp
from jax import lax
from jax.experimental import pallas as pl
from jax.experimental.pallas import tpu as pltpu

F32 = jnp.float32
BF16 = jnp.bfloat16
HIGHEST = lax.Precision.HIGHEST

EPS = 1e-6
CHUNK = 64
N_META = 16
HEAD_DIM_A = 128
HEAD_DIM_B = 128
CONV_W = 4
N_BUCKETS = 32
MAX_DISTANCE = 1024
NEG = -1e30
LOG2E = math.log2(math.e)

LANES = 128
SUBLANES = 8
V7X_VMEM_BYTES = 64 * 1024 * 1024
VMEM_LIMIT = V7X_VMEM_BYTES - 8 * 1024 * 1024

ATT_BLOCK = 256
FAR_BLOCKS = 8
SAMPLE_HEADS_PER_STEP = 8
GDN_CHUNK = 128
GDN_HEADS_PER_STEP = 16
PROJ_TM = 512
PROJ_TN = 1024
PROJ_TN_SINGLE = 512
KV_TM = 512
OUT_TM = 512
OUT_TN = 512


def _lambda_init(layer):
    return 0.8 - 0.6 * math.exp(-0.3 * layer)


def _bias_saturation_distance():
    nb = N_BUCKETS // 2
    max_exact = nb // 2
    return int(math.ceil(max_exact * (MAX_DISTANCE / max_exact) ** ((nb - max_exact - 1) / (nb - max_exact)))) + 1


N_NEAR = -(-(_bias_saturation_distance() + ATT_BLOCK - 1) // ATT_BLOCK)
FAR_BUCKET = N_BUCKETS // 2 - 1


def _silu(x):
    return x * jax.nn.sigmoid(x)


def _in_proj_kernel(x_ref, pn_ref, w_ref, wt_ref, o32_ref, o16_ref, ba_ref, xn_ref, *, q_tiles, bf_tiles,
                    q_scale):
    j = pl.program_id(1)
    tm = x_ref.shape[0]
    rows = max(r for r in range(SUBLANES, 81, SUBLANES) if tm % r == 0)

    @pl.when(j == 0)
    def _():
        def body(r, carry):
            sl = pl.ds(pl.multiple_of(r * rows, 8), rows)
            x = x_ref[sl, :]
            ms = jnp.mean(x * x, axis=-1, keepdims=True)
            xn_ref[sl, :] = (x * lax.rsqrt(ms + EPS) * pn_ref[...]).astype(BF16)
            return carry
        lax.fori_loop(0, tm // rows, body, 0)
        ba_ref[...] = jnp.dot(xn_ref[...], wt_ref[...], preferred_element_type=F32)

    r = jnp.dot(xn_ref[...], w_ref[...], preferred_element_type=F32)

    @pl.when(j >= q_tiles)
    def _():
        o32_ref[...] = r

    @pl.when(j < q_tiles)
    def _():
        o16_ref[...] = (r * q_scale).astype(BF16)

    @pl.when(jnp.logical_and(j >= q_tiles, j < bf_tiles))
    def _():
        o16_ref[...] = r.astype(BF16)


def _in_proj(x, pre_norm, w_all, w_tail, w_a, n, single_tile=False):
    m, d = x.shape
    tm, tn = PROJ_TM, min(PROJ_TN, w_a)
    x_mode = {}
    if single_tile:
        tm, tn = m, min(PROJ_TN_SINGLE, w_a)
        x_mode = dict(pipeline_mode=pl.Buffered(1))
    q_tiles = w_a // tn
    bf_tiles = 3 * w_a // tn
    nt = w_tail.shape[1]
    kern = functools.partial(_in_proj_kernel, q_tiles=q_tiles, bf_tiles=bf_tiles, q_scale=HEAD_DIM_A ** -0.5 * LOG2E)
    return pl.pallas_call(
        kern,
        grid=(m // tm, n // tn),
        in_specs=[
            pl.BlockSpec((tm, d), lambda i, j: (i, 0), **x_mode),
            pl.BlockSpec((1, d), lambda i, j: (0, 0)),
            pl.BlockSpec((d, tn), lambda i, j: (0, j)),
            pl.BlockSpec((d, nt), lambda i, j: (0, 0)),
        ],
        out_specs=[
            pl.BlockSpec((tm, tn), lambda i, j: (i, jnp.maximum(j - q_tiles, 0))),
            pl.BlockSpec((tm, tn), lambda i, j: (i, jnp.minimum(j, bf_tiles - 1))),
            pl.BlockSpec((tm, nt), lambda i, j: (i, 0)),
        ],
        out_shape=[
            jax.ShapeDtypeStruct((m, n - w_a), F32),
            jax.ShapeDtypeStruct((m, 3 * w_a), BF16),
            jax.ShapeDtypeStruct((m, nt), F32),
        ],
        scratch_shapes=[pltpu.VMEM((tm, d), BF16)],
        compiler_params=pltpu.CompilerParams(
            dimension_semantics=("arbitrary", "arbitrary"), vmem_limit_bytes=VMEM_LIMIT),
        name="in_proj",
    )(x, pre_norm, w_all, w_tail)


def _bias_values(rb_ref, h, rel):
    nb = N_BUCKETS // 2
    max_exact = nb // 2
    n = jnp.abs(rel)
    nf = jnp.maximum(n, 1).astype(F32)
    large = max_exact + (jnp.log(nf / max_exact) / math.log(MAX_DISTANCE / max_exact)
                         * (nb - max_exact)).astype(jnp.int32)
    large = jnp.minimum(large, nb - 1)
    bucket = jnp.where(rel > 0, nb, 0) + jnp.where(n < max_exact, n, large)
    val = jnp.zeros(rel.shape, F32)
    for b in range(N_BUCKETS):
        val = jnp.where(bucket == b, rb_ref[b, h], val)
    return (val - rb_ref[FAR_BUCKET, h]) * LOG2E


def _bias_kernel(rb_ref, lamp_ref, fb_ref, mb_ref, sa_ref, se_ref, lam_ref, *, past, dec_seq, n_ca, lam_init):
    h = pl.program_id(0)
    t = ATT_BLOCK
    shift = CHUNK.bit_length() - 1
    i = lax.broadcasted_iota(jnp.int32, (t, t), 0)
    j = lax.broadcasted_iota(jnp.int32, (t, t), 1)
    for dd in range(N_NEAR):
        val = _bias_values(rb_ref, h, (j - i) - t * dd)
        if dd == 0:
            val = jnp.where((j >> shift) <= (i >> shift), val, NEG)
        fb_ref[0, dd] = val
    i = lax.broadcasted_iota(jnp.int32, (t, LANES), 0)
    j = lax.broadcasted_iota(jnp.int32, (t, LANES), 1)
    for qb in range(N_NEAR):
        val = _bias_values(rb_ref, h, (j - N_META) - (t * qb + i))
        mb_ref[0, qb] = jnp.where(j < N_META, val, NEG)
    i = lax.broadcasted_iota(jnp.int32, (dec_seq, n_ca), 0)
    j = lax.broadcasted_iota(jnp.int32, (dec_seq, n_ca), 1)
    sa_ref[0] = _bias_values(rb_ref, h, (j - N_META) - (past + i))
    i = lax.broadcasted_iota(jnp.int32, (dec_seq, LANES), 0)
    j = lax.broadcasted_iota(jnp.int32, (dec_seq, LANES), 1)
    n_extra = N_META + past - n_ca + dec_seq
    val = _bias_values(rb_ref, h, (n_ca - N_META + j) - (past + i))
    se_ref[0] = jnp.where(j < n_extra, val, NEG)
    lp = lamp_ref[...]
    s1 = jnp.sum(lp[0:1] * lp[1:2], axis=-1, keepdims=True)
    s2 = jnp.sum(lp[2:3] * lp[3:4], axis=-1, keepdims=True)
    lam_ref[0] = jnp.broadcast_to(jnp.exp(s1) - jnp.exp(s2) + lam_init, (SUBLANES, LANES))


def _bias_tables(rel_bias, lam_params, past, dec_seq, n_ca, lam_init):
    nh = rel_bias.shape[1]
    t = ATT_BLOCK
    kern = functools.partial(_bias_kernel, past=past, dec_seq=dec_seq, n_ca=n_ca, lam_init=lam_init)
    return pl.pallas_call(
        kern,
        grid=(nh,),
        in_specs=[
            pl.BlockSpec(memory_space=pltpu.SMEM),
            pl.BlockSpec((4, HEAD_DIM_A), lambda h: (0, 0)),
        ],
        out_specs=[
            pl.BlockSpec((1, N_NEAR, t, t), lambda h: (h, 0, 0, 0)),
            pl.BlockSpec((1, N_NEAR, t, LANES), lambda h: (h, 0, 0, 0)),
            pl.BlockSpec((1, dec_seq, n_ca), lambda h: (h, 0, 0)),
            pl.BlockSpec((1, dec_seq, LANES), lambda h: (h, 0, 0)),
            pl.BlockSpec((1, SUBLANES, LANES), lambda h: (h, 0, 0)),
        ],
        out_shape=[
            jax.ShapeDtypeStruct((nh, N_NEAR, t, t), F32),
            jax.ShapeDtypeStruct((nh, N_NEAR, t, LANES), F32),
            jax.ShapeDtypeStruct((nh, dec_seq, n_ca), F32),
            jax.ShapeDtypeStruct((nh, dec_seq, LANES), F32),
            jax.ShapeDtypeStruct((nh, SUBLANES, LANES), F32),
        ],
        compiler_params=pltpu.CompilerParams(dimension_semantics=("arbitrary",)),
        name="bias_tables",
    )(rel_bias, lam_params)


def _nt_dot(a, b):
    return lax.dot_general(a, b, (((1,), (1,)), ((), ())), preferred_element_type=F32)


def _attn_finish(o, z, subln, lam_init):
    ms = jnp.mean(o * o, axis=-1, keepdims=True)
    on = o * lax.rsqrt(ms + EPS) * subln * (1.0 - lam_init)
    return on * _silu(z)


def _attn_prompt_kernel(lam_ref, q_ref, k_ref, v_ref, km_ref, vm_ref, fb_ref, mb_ref, z_ref, sub_ref, o_ref,
                        m_ref, l_ref, acc_ref, sa_ref, sb_ref, *, lam_init):
    qi = pl.program_id(1)
    t = ATT_BLOCK
    d = HEAD_DIM_A
    hw = 2 * d

    def rep(x, n):
        return jnp.concatenate([x] * n, axis=1) if n > 1 else x

    def lane_fold(p):
        out = p[:, 0:LANES]
        for g in range(1, p.shape[1] // LANES):
            out = out + p[:, g * LANES:(g + 1) * LANES]
        return out

    def meta_init():
        vm = vm_ref[...]
        mb = mb_ref[0, jnp.minimum(qi, N_NEAR - 1)]
        for c in range(2):
            s = _nt_dot(q_ref[:, c * d:(c + 1) * d], km_ref[:, c * d:(c + 1) * d]) + mb
            m = jnp.broadcast_to(jnp.max(s, axis=-1, keepdims=True), (t, LANES))
            p = jnp.exp2(s - m)
            m_ref[c] = m
            l_ref[c] = p
            acc_ref[c] = jnp.dot(p.astype(BF16), vm, preferred_element_type=F32)

    def qk(start, width):
        rows = pl.ds(start, width)
        return [_nt_dot(q_ref[:, c * d:(c + 1) * d], k_ref[rows, c * d:(c + 1) * d]) for c in range(2)]

    def softmax_pv(s_both, start, width, bias):
        v = v_ref[pl.ds(start, width), :]
        for c in range(2):
            s = s_both[c]
            if bias is not None:
                s = s + bias
            m_old = m_ref[c]
            m_new = jnp.maximum(m_old, jnp.max(s, axis=-1, keepdims=True))
            p = jnp.exp2(s - rep(m_new, width // LANES))
            alpha = jnp.exp2(m_old - m_new)
            m_ref[c] = m_new
            l_ref[c] = alpha * l_ref[c] + lane_fold(p)
            acc_ref[c] = rep(alpha, hw // LANES) * acc_ref[c] + jnp.dot(
                p.astype(BF16), v, preferred_element_type=F32)

    wide = FAR_BLOCKS * t
    far_shift = FAR_BLOCKS.bit_length() - 1
    n_far = jnp.maximum(qi - (N_NEAR - 1), 0)
    n_wide = n_far >> far_shift

    def put(ref, s_both):
        ref[0] = s_both[0]
        ref[1] = s_both[1]

    @pl.when(n_wide > 0)
    def _():
        put(sa_ref, qk(0, wide))
        meta_init()

    @pl.when(n_wide == 0)
    def _():
        meta_init()

    def far_body(i, carry):
        b0 = pl.multiple_of(i * (2 * wide), wide)
        b1 = pl.multiple_of(b0 + wide, wide)
        b2 = pl.multiple_of(jnp.minimum(b0 + 2 * wide, (n_wide - 1) * wide), wide)
        put(sb_ref, qk(b1, wide))
        softmax_pv([sa_ref[0], sa_ref[1]], b0, wide, None)
        put(sa_ref, qk(b2, wide))
        softmax_pv([sb_ref[0], sb_ref[1]], b1, wide, None)
        return carry

    def far_body2(i, carry):
        far_body(2 * i, carry)
        return far_body(2 * i + 1, carry)
    lax.fori_loop(0, n_wide >> 2, far_body2, 0)
    lax.fori_loop((n_wide >> 2) * 2, n_wide >> 1, far_body, 0)

    @pl.when((n_wide & 1) == 1)
    def _():
        softmax_pv([sa_ref[0], sa_ref[1]], pl.multiple_of((n_wide - 1) * wide, wide), wide, None)

    n_rest = n_far - (n_wide << far_shift)
    for cnt in range(1, FAR_BLOCKS):
        @pl.when(n_rest == cnt)
        def _(cnt=cnt):
            first = n_wide << far_shift
            s_next = qk(pl.multiple_of(first * t, t), t)
            for b in range(cnt):
                s_cur = s_next
                if b + 1 < cnt:
                    s_next = qk(pl.multiple_of((first + b + 1) * t, t), t)
                softmax_pv(s_cur, pl.multiple_of((first + b) * t, t), t, None)

    @pl.when(qi >= N_NEAR - 1)
    def _():
        s_next = qk(pl.multiple_of((qi - (N_NEAR - 1)) * t, t), t)
        for dd in range(N_NEAR - 1, -1, -1):
            s_cur = s_next
            if dd > 0:
                s_next = qk(pl.multiple_of((qi - dd + 1) * t, t), t)
            softmax_pv(s_cur, pl.multiple_of((qi - dd) * t, t), t, fb_ref[0, dd])

    @pl.when(qi < N_NEAR - 1)
    def _():
        for dd in range(N_NEAR - 2, -1, -1):
            @pl.when(qi - dd >= 0)
            def _(dd=dd):
                start = pl.multiple_of((qi - dd) * t, t)
                softmax_pv(qk(start, t), start, t, fb_ref[0, dd])

    inv0 = 1.0 / jnp.sum(l_ref[0], axis=-1, keepdims=True)
    inv1 = lam_ref[0] / jnp.sum(l_ref[1], axis=-1, keepdims=True)
    o = acc_ref[0] * inv0 - acc_ref[1] * inv1
    o_ref[...] = _attn_finish(o, z_ref[...], sub_ref[...], lam_init).astype(BF16)


def _attn_prompt(lam, qkv16, o32, km, vm, fb, mb, subln, n_heads, lam_init):
    seq = qkv16.shape[0]
    t = ATT_BLOCK
    hw = 2 * HEAD_DIM_A
    kern = functools.partial(_attn_prompt_kernel, lam_init=lam_init)
    return pl.pallas_call(
        kern,
        grid=(n_heads, seq // t),
        in_specs=[
            pl.BlockSpec(memory_space=pltpu.SMEM),
            pl.BlockSpec((t, hw), lambda h, i: (i, h)),
            pl.BlockSpec((seq, hw), lambda h, i: (0, n_heads + h)),
            pl.BlockSpec((seq, hw), lambda h, i: (0, 2 * n_heads + h)),
            pl.BlockSpec((LANES, hw), lambda h, i: (0, h)),
            pl.BlockSpec((LANES, hw), lambda h, i: (0, h)),
            pl.BlockSpec((1, N_NEAR, t, t), lambda h, i: (h, 0, 0, 0)),
            pl.BlockSpec((1, N_NEAR, t, LANES), lambda h, i: (h, 0, 0, 0)),
            pl.BlockSpec((t, hw), lambda h, i: (i, 2 * n_heads + h)),
            pl.BlockSpec((1, hw), lambda h, i: (0, 0)),
        ],
        out_specs=pl.BlockSpec((t, hw), lambda h, i: (i, h)),
        out_shape=jax.ShapeDtypeStruct((seq, n_heads * hw), BF16),
        scratch_shapes=[
            pltpu.VMEM((2, t, LANES), F32),
            pltpu.VMEM((2, t, LANES), F32),
            pltpu.VMEM((2, t, hw), F32),
            pltpu.VMEM((2, t, FAR_BLOCKS * t), F32),
            pltpu.VMEM((2, t, FAR_BLOCKS * t), F32),
        ],
        compiler_params=pltpu.CompilerParams(
            dimension_semantics=("arbitrary", "arbitrary"), vmem_limit_bytes=VMEM_LIMIT),
        name="attn_prompt",
    )(lam, qkv16, qkv16, qkv16, km, vm, fb, mb, o32, subln)


def _attn_sample_kernel(lam_ref, q_ref, kc_ref, vc_ref, ke_ref, ve_ref, sa_ref, se_ref, z_ref, sub_ref, o_ref,
                        *, lam_init, n_heads, n_ca, hs):
    d = HEAD_DIM_A
    hw = 2 * d
    for i in range(hs):
        h = pl.program_id(1) * hs + i
        cols = slice(i * hw, (i + 1) * hw)
        vc = jnp.concatenate([vc_ref[0, pl.ds(c * n_heads + h, n_ca, stride=2 * n_heads), :] for c in range(2)],
                             axis=1).astype(BF16)
        ve = ve_ref[0, :, cols]
        outs = []
        for c in range(2):
            q = q_ref[:, i * hw + c * d:i * hw + (c + 1) * d]
            kc = kc_ref[0, pl.ds(2 * h + c, n_ca, stride=2 * n_heads), :].astype(BF16)
            s_a = _nt_dot(q, kc) + sa_ref[i]
            s_e = _nt_dot(q, ke_ref[0, :, i * hw + c * d:i * hw + (c + 1) * d]) + se_ref[i]
            m = jnp.maximum(jnp.max(s_a, axis=-1, keepdims=True), jnp.max(s_e, axis=-1, keepdims=True))
            p_a = jnp.exp2(s_a - m)
            p_e = jnp.exp2(s_e - m)
            l = jnp.sum(p_a, axis=-1, keepdims=True) + jnp.sum(p_e, axis=-1, keepdims=True)
            acc = (jnp.dot(p_a.astype(BF16), vc, preferred_element_type=F32)
                   + jnp.dot(p_e.astype(BF16), ve, preferred_element_type=F32))
            outs.append(acc / l)
        o = outs[0] - lam_ref[0] * outs[1]
        o_ref[:, cols] = _attn_finish(o, z_ref[:, cols], sub_ref[...], lam_init).astype(BF16)


def _attn_sample(lam, qkv16, o32, cache_k, cache_v, ke, ve, sa, se, subln, n_heads, dec_b, dec_seq, n_ca,
                 lam_init):
    hw = 2 * HEAD_DIM_A
    hs = min(SAMPLE_HEADS_PER_STEP, n_heads)
    n_grp = n_heads // hs
    kern = functools.partial(_attn_sample_kernel, lam_init=lam_init, n_heads=n_heads, n_ca=n_ca, hs=hs)
    return pl.pallas_call(
        kern,
        grid=(dec_b, n_grp),
        in_specs=[
            pl.BlockSpec(memory_space=pltpu.SMEM),
            pl.BlockSpec((dec_seq, hs * hw), lambda b, g: (b, g)),
            pl.BlockSpec((1, n_ca * n_heads * 2, HEAD_DIM_A), lambda b, g: (b, 0, 0)),
            pl.BlockSpec((1, n_ca * n_heads * 2, HEAD_DIM_A), lambda b, g: (b, 0, 0)),
            pl.BlockSpec((1, LANES, hs * hw), lambda b, g: (b, 0, g)),
            pl.BlockSpec((1, LANES, hs * hw), lambda b, g: (b, 0, g)),
            pl.BlockSpec((hs, dec_seq, n_ca), lambda b, g: (g, 0, 0)),
            pl.BlockSpec((hs, dec_seq, LANES), lambda b, g: (g, 0, 0)),
            pl.BlockSpec((dec_seq, hs * hw), lambda b, g: (b, 2 * n_grp + g)),
            pl.BlockSpec((1, hw), lambda b, g: (0, 0)),
        ],
        out_specs=pl.BlockSpec((dec_seq, hs * hw), lambda b, g: (b, g)),
        out_shape=jax.ShapeDtypeStruct((dec_b * dec_seq, n_heads * hw), BF16),
        compiler_params=pltpu.CompilerParams(
            dimension_semantics=("arbitrary", "arbitrary"), vmem_limit_bytes=VMEM_LIMIT),
        name="attn_sample",
    )(lam, qkv16, cache_k, cache_v, ke, ve, sa, se, o32, subln)


def _hdot(a, b):
    return jnp.dot(a, b, preferred_element_type=F32, precision=HIGHEST)


def _inv_unit_lower(a_list, row, col):
    sh = 4
    n = a_list[0].shape[0]
    eye = (row == col).astype(F32)
    same = (row >> sh) == (col >> sh)
    dblk = [jnp.where(same, a, 0.0) for a in a_list]
    t = [eye - x for x in dblk]
    pw = [x.astype(BF16) for x in dblk]
    for _ in range(sh - 1):
        pw = [jnp.dot(x, x, preferred_element_type=F32).astype(BF16) for x in pw]
        t = [ti + jnp.dot(ti.astype(BF16), x, preferred_element_type=F32) for ti, x in zip(t, pw)]
    while (1 << sh) < n:
        offm = jnp.logical_and((row >> (sh + 1)) == (col >> (sh + 1)), (row >> sh) != (col >> sh))
        off = [jnp.where(offm, a, 0.0).astype(BF16) for a in a_list]
        tb = [ti.astype(BF16) for ti in t]
        mid = [jnp.dot(o, x, preferred_element_type=F32).astype(BF16) for o, x in zip(off, tb)]
        t = [ti - jnp.dot(x, m, preferred_element_type=F32) for ti, x, m in zip(t, tb, mid)]
        sh += 1
    return t


def _gdn_kernel(q_ref, k_ref, v_ref, z_ref, ba_ref, cwq_ref, cwk_ref, cwv_ref, gp_ref, nb_ref, s0_ref,
                hq_ref, hk_ref, hv_ref, y_ref, sout_ref, s_ref, xq_ref, xk_ref, xv_ref, *, lr, hg):
    c = pl.program_id(2)
    n = GDN_CHUNK
    dh = HEAD_DIM_B
    halo = SUBLANES

    @pl.when(c == 0)
    def _():
        s_ref[...] = s0_ref[0]
        xq_ref[0:halo, :] = hq_ref[0]
        xk_ref[0:halo, :] = hk_ref[0]
        xv_ref[0:halo, :] = hv_ref[0]

    row_w = lax.broadcasted_iota(jnp.int32, (n, hg * dh), 0)

    def conv(x_ref, xbuf, cw_ref):
        xbuf[halo:halo + lr, :] = x_ref[...]
        if lr < n:
            xbuf[halo + lr:halo + n, :] = jnp.zeros((n - lr, hg * dh), F32)
        y = xbuf[halo - 3:halo - 3 + n, :] * cw_ref[0:1, :]
        for i in range(1, CONV_W):
            y = y + xbuf[halo - 3 + i:halo - 3 + i + n, :] * cw_ref[i:i + 1, :]
        y = _silu(y)
        if lr < n:
            y = jnp.where(row_w < lr, y, 0.0)
        xbuf[0:halo, :] = xbuf[lr:lr + halo, :]
        return y

    yq = conv(q_ref, xq_ref, cwq_ref)
    yk = conv(k_ref, xk_ref, cwk_ref)
    yv = conv(v_ref, xv_ref, cwv_ref)

    row = lax.broadcasted_iota(jnp.int32, (n, n), 0)
    col = lax.broadcasted_iota(jnp.int32, (n, n), 1)
    ba = ba_ref[...]
    if lr < n:
        ba = jnp.concatenate([ba, jnp.zeros((n - lr, LANES), F32)], axis=0)
    live = row < lr
    beta = jnp.where(jnp.logical_and(live, col < hg), jax.nn.sigmoid(ba), 0.0)
    gval = -jnp.exp(gp_ref[0, 0:1, :]) * jax.nn.softplus(ba + gp_ref[0, 1:2, :])
    g = jnp.where(jnp.logical_and(live, jnp.logical_and(col >= hg, col < 2 * hg)), gval, 0.0)
    gsum = _hdot((row >= col).astype(F32), g)
    gsum_t = gsum.T
    incl = row >= col
    strict = row > col

    def run(heads):
        cols = {hh: slice(hh * dh, (hh + 1) * dh) for hh in heads}
        gc = {hh: jnp.broadcast_to(gsum[:, hg + hh:hg + hh + 1], (n, n)) for hh in heads}
        bc = {hh: jnp.broadcast_to(beta[:, hh:hh + 1], (n, n)) for hh in heads}
        gam = {hh: jnp.where(incl, jnp.exp(jnp.minimum(gc[hh] - gsum_t[hg + hh:hg + hh + 1, :], 0.0)), 0.0)
               for hh in heads}
        qh = {hh: yq[:, cols[hh]] * (lax.rsqrt(jnp.sum(yq[:, cols[hh]] * yq[:, cols[hh]], axis=-1, keepdims=True)
                                               + EPS) * (dh ** -0.5)) for hh in heads}
        kh = {hh: yk[:, cols[hh]] * lax.rsqrt(jnp.sum(yk[:, cols[hh]] * yk[:, cols[hh]], axis=-1, keepdims=True)
                                              + EPS) for hh in heads}
        qb = {hh: qh[hh].astype(BF16) for hh in heads}
        kb = {hh: kh[hh].astype(BF16) for hh in heads}
        kk = {hh: _nt_dot(kb[hh], kb[hh]) for hh in heads}
        qk = {hh: _nt_dot(qb[hh], kb[hh]) for hh in heads}
        a = [jnp.where(strict, bc[hh] * kk[hh] * gam[hh], 0.0) for hh in heads]
        t = dict(zip(heads, _inv_unit_lower(a, row, col)))
        eg = {hh: jnp.exp(gc[hh]) for hh in heads}
        rhs = {hh: jnp.concatenate([yv[:, cols[hh]] * bc[hh], kh[hh] * (bc[hh] * eg[hh])], axis=1).astype(BF16)
               for hh in heads}
        uw = {hh: jnp.dot(t[hh].astype(BF16), rhs[hh], preferred_element_type=F32) for hh in heads}
        s = {hh: s_ref[hh] for hh in heads}
        sb = {hh: s[hh].astype(BF16) for hh in heads}
        v_new = {hh: uw[hh][:, :dh] - jnp.dot(uw[hh][:, dh:].astype(BF16), sb[hh], preferred_element_type=F32)
                 for hh in heads}
        vnb = {hh: v_new[hh].astype(BF16) for hh in heads}
        o = {hh: eg[hh] * jnp.dot(qb[hh], sb[hh], preferred_element_type=F32)
             + jnp.dot((qk[hh] * gam[hh]).astype(BF16), vnb[hh], preferred_element_type=F32) for hh in heads}
        g_last = {hh: gc[hh][n - 1:n, :] for hh in heads}
        kd = {hh: (kh[hh] * jnp.exp(g_last[hh] - gc[hh])).astype(BF16) for hh in heads}
        for hh in heads:
            s_ref[hh] = s[hh] * jnp.exp(g_last[hh]) + lax.dot_general(
                kd[hh], vnb[hh], (((0,), (0,)), ((), ())), preferred_element_type=F32)
        for hh in heads:
            on = o[hh] * lax.rsqrt(jnp.mean(o[hh] * o[hh], axis=-1, keepdims=True) + EPS) * nb_ref[...]
            y_ref[:, cols[hh]] = (on[0:lr] * _silu(z_ref[:, cols[hh]])).astype(BF16)

    run(list(range(hg)))

    @pl.when(c == pl.num_programs(2) - 1)
    def _():
        sout_ref[0] = s_ref[...]


def _gdn(o32, ba, conv_w, gate_par, norm_b, s0, halo0, *, n_seq, n_chunks, lr, row_blk_off, w_b, col0):
    hg = min(GDN_HEADS_PER_STEP, w_b // HEAD_DIM_B)
    gw = hg * HEAD_DIM_B
    n_groups = w_b // gw
    n_heads = w_b // HEAD_DIM_B
    cb = col0 // gw
    wb = w_b // gw

    def rmap(off):
        return lambda s, g, c: (row_blk_off + s * n_chunks + c, off + g)

    kern = functools.partial(_gdn_kernel, lr=lr, hg=hg)
    return pl.pallas_call(
        kern,
        grid=(n_seq, n_groups, n_chunks),
        in_specs=[
            pl.BlockSpec((lr, gw), rmap(cb)),
            pl.BlockSpec((lr, gw), rmap(cb + wb)),
            pl.BlockSpec((lr, gw), rmap(cb + 2 * wb)),
            pl.BlockSpec((lr, gw), rmap(cb + 3 * wb)),
            pl.BlockSpec((lr, LANES), rmap(0)),
            pl.BlockSpec((CONV_W, gw), lambda s, g, c: (0, g)),
            pl.BlockSpec((CONV_W, gw), lambda s, g, c: (0, wb + g)),
            pl.BlockSpec((CONV_W, gw), lambda s, g, c: (0, 2 * wb + g)),
            pl.BlockSpec((1, 2, LANES), lambda s, g, c: (g, 0, 0)),
            pl.BlockSpec((1, HEAD_DIM_B), lambda s, g, c: (0, 0)),
            pl.BlockSpec((1, hg, HEAD_DIM_B, HEAD_DIM_B), lambda s, g, c: (s, g, 0, 0)),
            pl.BlockSpec((1, SUBLANES, gw), lambda s, g, c: (s, 0, g)),
            pl.BlockSpec((1, SUBLANES, gw), lambda s, g, c: (s, 0, wb + g)),
            pl.BlockSpec((1, SUBLANES, gw), lambda s, g, c: (s, 0, 2 * wb + g)),
        ],
        out_specs=[
            pl.BlockSpec((lr, gw), lambda s, g, c: (s * n_chunks + c, g)),
            pl.BlockSpec((1, hg, HEAD_DIM_B, HEAD_DIM_B), lambda s, g, c: (s, g, 0, 0)),
        ],
        out_shape=[
            jax.ShapeDtypeStruct((n_seq * n_chunks * lr, w_b), BF16),
            jax.ShapeDtypeStruct((n_seq, n_heads, HEAD_DIM_B, HEAD_DIM_B), F32),
        ],
        scratch_shapes=[
            pltpu.VMEM((hg, HEAD_DIM_B, HEAD_DIM_B), F32),
            pltpu.VMEM((SUBLANES + GDN_CHUNK, gw), F32),
            pltpu.VMEM((SUBLANES + GDN_CHUNK, gw), F32),
            pltpu.VMEM((SUBLANES + GDN_CHUNK, gw), F32),
        ],
        compiler_params=pltpu.CompilerParams(
            dimension_semantics=("arbitrary", "arbitrary", "arbitrary"), vmem_limit_bytes=VMEM_LIMIT),
        name=f"gdn_l{lr}",
    )(o32, o32, o32, o32, ba, conv_w, conv_w, conv_w, gate_par, norm_b, s0, halo0, halo0, halo0)


def _out_proj_kernel(ya_ref, yb_ref, w_ref, h_ref, pn_ref, o_ref, *, w_a):
    j = pl.program_id(1)
    tn = w_ref.shape[1]
    r = (jnp.dot(ya_ref[...], w_ref[0:w_a, :], preferred_element_type=F32)
         + jnp.dot(yb_ref[...], w_ref[w_a:, :], preferred_element_type=F32))
    o_ref[:, pl.ds(pl.multiple_of(j * tn, tn), tn)] = r

    @pl.when(j == pl.num_programs(1) - 1)
    def _():
        rows = 64

        def body(i, carry):
            sl = pl.ds(pl.multiple_of(i * rows, rows), rows)
            y = o_ref[sl, :]
            ms = jnp.mean(y * y, axis=-1, keepdims=True)
            o_ref[sl, :] = h_ref[sl, :] + y * lax.rsqrt(ms + EPS) * pn_ref[...]
            return carry
        lax.fori_loop(0, o_ref.shape[0] // rows, body, 0)


def _out_proj(ya, yb, w_out, h, post_norm):
    m, d = h.shape
    w_a = ya.shape[1]
    tm, tn = min(OUT_TM, m), OUT_TN
    kern = functools.partial(_out_proj_kernel, w_a=w_a)
    return pl.pallas_call(
        kern,
        grid=(m // tm, d // tn),
        in_specs=[
            pl.BlockSpec((tm, w_a), lambda i, j: (i, 0)),
            pl.BlockSpec((tm, yb.shape[1]), lambda i, j: (i, 0)),
            pl.BlockSpec((d, tn), lambda i, j: (0, j)),
            pl.BlockSpec((tm, d), lambda i, j: (i, 0)),
            pl.BlockSpec((1, d), lambda i, j: (0, 0)),
        ],
        out_specs=pl.BlockSpec((tm, d), lambda i, j: (i, 0)),
        out_shape=jax.ShapeDtypeStruct((m, d), F32),
        compiler_params=pltpu.CompilerParams(
            dimension_semantics=("arbitrary", "arbitrary"), vmem_limit_bytes=VMEM_LIMIT),
        name="out_proj",
    )(ya, yb, w_out, h, post_norm)


def _kv_layout_kernel(km_ref, vm_ref, k_ref, v_ref, ko_ref, vo_ref, kt_ref, vt_ref, *, n_heads):
    tm = k_ref.shape[0]
    n_lead = kt_ref.shape[0]
    rpt = 2 * n_heads

    @pl.when(pl.program_id(0) == 0)
    def _():
        kt_ref[...] = km_ref[...]
        vt_ref[...] = vm_ref[...]

    for p in range(rpt):
        piece = slice(p * LANES, (p + 1) * LANES)
        vrow = (p % 2) * n_heads + p // 2
        ko_ref[pl.ds(p, n_lead, stride=rpt), :] = kt_ref[:, piece]
        ko_ref[pl.ds(n_lead * rpt + p, tm - n_lead, stride=rpt), :] = k_ref[0:tm - n_lead, piece]
        vo_ref[pl.ds(vrow, n_lead, stride=rpt), :] = vt_ref[:, piece]
        vo_ref[pl.ds(n_lead * rpt + vrow, tm - n_lead, stride=rpt), :] = v_ref[0:tm - n_lead, piece]
    kt_ref[...] = k_ref[tm - n_lead:tm, :]
    vt_ref[...] = v_ref[tm - n_lead:tm, :]


def _kv_layout(k_lead, v_lead, p32, n_heads, w_a):
    seq = p32.shape[0]
    n_lead = k_lead.shape[0]
    tm = KV_TM
    rpt = 2 * n_heads
    n_blk = seq // tm
    spec_out = pl.BlockSpec((tm * rpt, LANES), lambda i: (i, 0))
    shape_out = jax.ShapeDtypeStruct(((n_lead + seq) * rpt, LANES), F32)
    return pl.pallas_call(
        functools.partial(_kv_layout_kernel, n_heads=n_heads),
        grid=(n_blk + 1,),
        in_specs=[
            pl.BlockSpec((n_lead, w_a), lambda i: (0, 0)),
            pl.BlockSpec((n_lead, w_a), lambda i: (0, 0)),
            pl.BlockSpec((tm, w_a), lambda i: (jnp.minimum(i, n_blk - 1), 0)),
            pl.BlockSpec((tm, w_a), lambda i: (jnp.minimum(i, n_blk - 1), 1)),
        ],
        out_specs=[spec_out, spec_out],
        out_shape=[shape_out, shape_out],
        scratch_shapes=[pltpu.VMEM((n_lead, w_a), F32), pltpu.VMEM((n_lead, w_a), F32)],
        compiler_params=pltpu.CompilerParams(dimension_semantics=("arbitrary",), vmem_limit_bytes=VMEM_LIMIT),
        name="kv_layout",
    )(k_lead, v_lead, p32, p32)


def kernel(x_prompt, x_sample, cache_k_a, cache_v_a, state_ssm_b, state_conv_b, meta_tokens, rel_bias, pre_norm,
           w_in, lambda_q1, lambda_k1, lambda_q2, lambda_k2, subln_a, conv_b, a_log_b, dt_bias_b, norm_b, w_out,
           post_norm):
    batch, seq, d_model = x_prompt.shape
    dec_b, dec_seq, _ = x_sample.shape
    depth = w_in.shape[0]
    assert batch == 1 and depth == 1
    n_heads_a = rel_bias.shape[1]
    w_a = n_heads_a * 2 * HEAD_DIM_A
    w_b = d_model - w_a
    n_heads_b = w_b // HEAD_DIM_B
    n_cache = cache_k_a.shape[2]
    past = n_cache - N_META
    n_ca = (n_cache // LANES) * LANES
    n_main = 4 * w_a + 4 * w_b
    lam_init = _lambda_init(0)
    hg = min(GDN_HEADS_PER_STEP, n_heads_b)
    n_groups = n_heads_b // hg
    n_dec = dec_b * dec_seq
    assert seq % PROJ_TM == 0 and seq % GDN_CHUNK == 0 and n_cache - n_ca + dec_seq <= LANES

    w_in_bf = w_in[0].astype(BF16)
    wb = w_in_bf[:, n_main:n_main + n_heads_b].reshape(d_model, n_groups, hg)
    wa = w_in_bf[:, n_main + n_heads_b:].reshape(d_model, n_groups, hg)
    w_tail = jnp.concatenate([wb, wa, jnp.zeros((d_model, n_groups, LANES - 2 * hg), BF16)], axis=-1)
    w_tail = w_tail.reshape(d_model, n_groups * LANES)
    w_out_bf = w_out[0].astype(BF16)
    gate_par = jnp.zeros((n_groups, 2, LANES), F32)
    gate_par = gate_par.at[:, 0, hg:2 * hg].set(a_log_b[0].reshape(n_groups, hg))
    gate_par = gate_par.at[:, 1, hg:2 * hg].set(dt_bias_b[0].reshape(n_groups, hg))
    lam_params = jnp.stack([lambda_q1[0], lambda_k1[0], lambda_q2[0], lambda_k2[0]])

    xp = x_prompt[0]
    n_small = -(-(n_dec + N_META) // 16) * 16
    xs = jnp.concatenate([x_sample.reshape(n_dec, d_model), meta_tokens.astype(F32),
                          jnp.zeros((n_small - n_dec - N_META, d_model), F32)], axis=0)
    p32, p16, pba = _in_proj(xp, pre_norm, w_in_bf, w_tail, w_a, n_main)
    s32, s16, sba = _in_proj(xs, pre_norm, w_in_bf, w_tail, w_a, n_main, single_tile=True)

    fb, mb, sa, se, lam_t = _bias_tables(rel_bias, lam_params, past, dec_seq, n_ca, lam_init)
    lam = lam_t[0, 0, 0:1]

    meta16 = s16[n_dec:n_dec + N_META]
    pad_m = jnp.zeros((LANES - N_META, w_a), BF16)
    km = jnp.concatenate([meta16[:, w_a:2 * w_a], pad_m], axis=0)
    vm = jnp.concatenate([meta16[:, 2 * w_a:], pad_m], axis=0)
    ya_p = _attn_prompt(lam, p16, p32, km, vm, fb, mb, subln_a, n_heads_a, lam_init)

    ck = cache_k_a[0].reshape(dec_b, n_cache * n_heads_a * 2, HEAD_DIM_A)
    cv = cache_v_a[0].reshape(dec_b, n_cache, n_heads_a, 2, HEAD_DIM_A).transpose(0, 1, 3, 2, 4)
    cv = cv.reshape(dec_b, n_cache * 2 * n_heads_a, HEAD_DIM_A)
    n_extra = n_cache - n_ca + dec_seq
    pad_e = jnp.zeros((dec_b, LANES - n_extra, w_a), BF16)
    ck_tail = cache_k_a[0, :, n_ca:].reshape(dec_b, n_cache - n_ca, w_a).astype(BF16)
    cv_tail = cache_v_a[0, :, n_ca:].reshape(dec_b, n_cache - n_ca, w_a).astype(BF16)
    ke = jnp.concatenate([ck_tail, s16[:n_dec, w_a:2 * w_a].reshape(dec_b, dec_seq, w_a), pad_e], axis=1)
    ve = jnp.concatenate([cv_tail, s16[:n_dec, 2 * w_a:].reshape(dec_b, dec_seq, w_a), pad_e], axis=1)
    ya_s = _attn_sample(lam, s16, s32, ck, cv, ke, ve, sa, se, subln_a, n_heads_a, dec_b, dec_seq, n_ca,
                        lam_init)

    col0 = 3 * w_a
    qkv_cols = slice(col0, col0 + 3 * w_b)
    zero_halo = jnp.zeros((1, SUBLANES, 3 * w_b), F32)
    gdn = functools.partial(_gdn, conv_w=conv_b[0], gate_par=gate_par, norm_b=norm_b, w_b=w_b, col0=col0)
    _, s_meta = gdn(s32, sba, s0=jnp.zeros((1, n_heads_b, HEAD_DIM_B, HEAD_DIM_B), F32), halo0=zero_halo,
                    n_seq=1, n_chunks=1, lr=N_META, row_blk_off=n_dec // N_META)
    meta_halo = jnp.concatenate([jnp.zeros((SUBLANES - (CONV_W - 1), 3 * w_b), F32),
                                 s32[n_dec + N_META - (CONV_W - 1):n_dec + N_META, qkv_cols]], axis=0)[None]
    yb_p, ssm_p = gdn(p32, pba, s0=s_meta, halo0=meta_halo, n_seq=1, n_chunks=seq // GDN_CHUNK, lr=GDN_CHUNK,
                      row_blk_off=0)
    samp_halo = jnp.concatenate([jnp.zeros((dec_b, SUBLANES - (CONV_W - 1), 3 * w_b), F32), state_conv_b[0]], axis=1)
    yb_s, ssm_s = gdn(s32, sba, s0=state_ssm_b[0].astype(F32), halo0=samp_halo, n_seq=dec_b, n_chunks=1,
                      lr=dec_seq, row_blk_off=0)

    y_p = _out_proj(ya_p, yb_p, w_out_bf, xp, post_norm)
    y_s = _out_proj(ya_s, yb_s, w_out_bf, x_sample.reshape(n_dec, d_model), post_norm)

    hd = HEAD_DIM_A
    meta32 = s32[n_dec:n_dec + N_META]
    k_p, v_p = _kv_layout(meta32[:, :w_a], meta32[:, w_a:2 * w_a], p32, n_heads_a, w_a)
    v_p = v_p.reshape(N_META + seq, 2, n_heads_a, hd).transpose(0, 2, 1, 3)
    conv_p = p32[seq - (CONV_W - 1):, qkv_cols]
    k_s = s32[:n_dec, :w_a]
    v_s = s32[:n_dec, w_a:2 * w_a]
    conv_s = s32[:n_dec, qkv_cols].reshape(dec_b, dec_seq, 3 * w_b)[:, dec_seq - (CONV_W - 1):]
    return (
        y_p[None],
        y_s.reshape(dec_b, dec_seq, d_model),
        k_p.reshape(1, 1, N_META + seq, n_heads_a, 2, hd),
        v_p.reshape(1, 1, N_META + seq, n_heads_a, 2 * hd),
        ssm_p[None],
        conv_p[None, None],
        k_s.reshape(1, dec_b, dec_seq, n_heads_a, 2, hd),
        v_s.reshape(1, dec_b, dec_seq, n_heads_a, 2 * hd),
        ssm_s[None],
        conv_s[None],
    )
```

```python
import functools
import math

import jax
import jax.numpy as jnp
from jax import lax
from jax.experimental import pallas as pl
from jax.experimental.pallas import tpu as pltpu

F32 = jnp.float32
BF16 = jnp.bfloat16
HIGHEST = lax.Precision.HIGHEST

EPS = 1e-6
CHUNK = 64
N_META = 16
HEAD_DIM_A = 128
HEAD_DIM_B = 128
CONV_W = 4
N_BUCKETS = 32
MAX_DISTANCE = 1024
NEG = -1e30
LOG2E = math.log2(math.e)

LANES = 128
SUBLANES = 8
V7X_VMEM_BYTES = 64 * 1024 * 1024
VMEM_LIMIT = V7X_VMEM_BYTES - 8 * 1024 * 1024

ATT_BLOCK = 256
FAR_BLOCKS = 8
SAMPLE_HEADS_PER_STEP = 8
GDN_CHUNK = 128
GDN_HEADS_PER_STEP = 16
PROJ_TM = 512
PROJ_TN = 1024
PROJ_TN_SINGLE = 256
KV_TM = 512
OUT_TM = 512
OUT_TN = 512


def _lambda_init(layer):
    return 0.8 - 0.6 * math.exp(-0.3 * layer)


def _bias_saturation_distance():
    nb = N_BUCKETS // 2
    max_exact = nb // 2
    return int(math.ceil(max_exact * (MAX_DISTANCE / max_exact) ** ((nb - max_exact - 1) / (nb - max_exact)))) + 1


N_NEAR = -(-(_bias_saturation_distance() + ATT_BLOCK - 1) // ATT_BLOCK)
FAR_BUCKET = N_BUCKETS // 2 - 1


def _silu(x):
    return x * jax.nn.sigmoid(x)


def _in_proj_kernel(x_ref, pn_ref, w_ref, wt_ref, o32_ref, o16_ref, ba_ref, *rest, q_tiles, bf_tiles,
                    q_scale, emit_bf16_weights):
    wbf_ref, xn_ref = rest if emit_bf16_weights else (None, rest[0])
    j = pl.program_id(1)
    tm = x_ref.shape[0]
    rows = max(r for r in range(SUBLANES, 81, SUBLANES) if tm % r == 0)

    @pl.when(j == 0)
    def _():
        def body(r, carry):
            sl = pl.ds(pl.multiple_of(r * rows, 8), rows)
            x = x_ref[sl, :]
            ms = jnp.mean(x * x, axis=-1, keepdims=True)
            xn_ref[sl, :] = (x * lax.rsqrt(ms + EPS) * pn_ref[...]).astype(BF16)
            return carry
        lax.fori_loop(0, tm // rows, body, 0)
        ba_ref[...] = jnp.dot(xn_ref[...], wt_ref[...], preferred_element_type=F32)

    w = w_ref[...]
    if emit_bf16_weights:
        w = w.astype(BF16)
        wbf_ref[...] = w
    r = jnp.dot(xn_ref[...], w, preferred_element_type=F32)

    @pl.when(j >= q_tiles)
    def _():
        o32_ref[...] = r

    @pl.when(j < q_tiles)
    def _():
        o16_ref[...] = (r * q_scale).astype(BF16)

    @pl.when(jnp.logical_and(j >= q_tiles, j < bf_tiles))
    def _():
        o16_ref[...] = r.astype(BF16)


def _in_proj(x, pre_norm, w_all, w_tail, w_a, n, single_tile=False):
    m, d = x.shape
    tm, tn = PROJ_TM, min(PROJ_TN, w_a)
    x_mode = {}
    if single_tile:
        tm, tn = m, min(PROJ_TN_SINGLE, w_a)
        x_mode = dict(pipeline_mode=pl.Buffered(1))
    q_tiles = w_a // tn
    bf_tiles = 3 * w_a // tn
    nt = w_tail.shape[1]
    kern = functools.partial(_in_proj_kernel, q_tiles=q_tiles, bf_tiles=bf_tiles,
                             q_scale=HEAD_DIM_A ** -0.5 * LOG2E, emit_bf16_weights=single_tile)
    extra_specs = [pl.BlockSpec((d, tn), lambda i, j: (0, j))] if single_tile else []
    extra_shapes = [jax.ShapeDtypeStruct((d, n), BF16)] if single_tile else []
    return pl.pallas_call(
        kern,
        grid=(m // tm, n // tn),
        in_specs=[
            pl.BlockSpec((tm, d), lambda i, j: (i, 0), **x_mode),
            pl.BlockSpec((1, d), lambda i, j: (0, 0)),
            pl.BlockSpec((d, tn), lambda i, j: (0, j)),
            pl.BlockSpec((d, nt), lambda i, j: (0, 0)),
        ],
        out_specs=[
            pl.BlockSpec((tm, tn), lambda i, j: (i, jnp.maximum(j - q_tiles, 0))),
            pl.BlockSpec((tm, tn), lambda i, j: (i, jnp.minimum(j, bf_tiles - 1))),
            pl.BlockSpec((tm, nt), lambda i, j: (i, 0)),
        ] + extra_specs,
        out_shape=[
            jax.ShapeDtypeStruct((m, n - w_a), F32),
            jax.ShapeDtypeStruct((m, 3 * w_a), BF16),
            jax.ShapeDtypeStruct((m, nt), F32),
        ] + extra_shapes,
        scratch_shapes=[pltpu.VMEM((tm, d), BF16)],
        compiler_params=pltpu.CompilerParams(
            dimension_semantics=("arbitrary", "arbitrary"), vmem_limit_bytes=VMEM_LIMIT),
        name="in_proj",
    )(x, pre_norm, w_all, w_tail)


def _bias_values(rb_ref, h, rel):
    nb = N_BUCKETS // 2
    max_exact = nb // 2
    n = jnp.abs(rel)
    nf = jnp.maximum(n, 1).astype(F32)
    large = max_exact + (jnp.log(nf / max_exact) / math.log(MAX_DISTANCE / max_exact)
                         * (nb - max_exact)).astype(jnp.int32)
    large = jnp.minimum(large, nb - 1)
    bucket = jnp.where(rel > 0, nb, 0) + jnp.where(n < max_exact, n, large)
    val = jnp.zeros(rel.shape, F32)
    for b in range(N_BUCKETS):
        val = jnp.where(bucket == b, rb_ref[b, h], val)
    return (val - rb_ref[FAR_BUCKET, h]) * LOG2E


def _bias_kernel(rb_ref, lamp_ref, fb_ref, mb_ref, sa_ref, se_ref, lam_ref, *, past, dec_seq, n_ca, lam_init):
    h = pl.program_id(0)
    t = ATT_BLOCK
    shift = CHUNK.bit_length() - 1
    i = lax.broadcasted_iota(jnp.int32, (t, t), 0)
    j = lax.broadcasted_iota(jnp.int32, (t, t), 1)
    for dd in range(N_NEAR):
        val = _bias_values(rb_ref, h, (j - i) - t * dd)
        if dd == 0:
            val = jnp.where((j >> shift) <= (i >> shift), val, NEG)
        fb_ref[0, dd] = val
    i = lax.broadcasted_iota(jnp.int32, (t, LANES), 0)
    j = lax.broadcasted_iota(jnp.int32, (t, LANES), 1)
    for qb in range(N_NEAR):
        val = _bias_values(rb_ref, h, (j - N_META) - (t * qb + i))
        mb_ref[0, qb] = jnp.where(j < N_META, val, NEG)
    i = lax.broadcasted_iota(jnp.int32, (dec_seq, n_ca), 0)
    j = lax.broadcasted_iota(jnp.int32, (dec_seq, n_ca), 1)
    sa_ref[0] = _bias_values(rb_ref, h, (j - N_META) - (past + i))
    i = lax.broadcasted_iota(jnp.int32, (dec_seq, LANES), 0)
    j = lax.broadcasted_iota(jnp.int32, (dec_seq, LANES), 1)
    n_extra = N_META + past - n_ca + dec_seq
    val = _bias_values(rb_ref, h, (n_ca - N_META + j) - (past + i))
    se_ref[0] = jnp.where(j < n_extra, val, NEG)
    lp = lamp_ref[...]
    s1 = jnp.sum(lp[0:1] * lp[1:2], axis=-1, keepdims=True)
    s2 = jnp.sum(lp[2:3] * lp[3:4], axis=-1, keepdims=True)
    lam_ref[0] = jnp.broadcast_to(jnp.exp(s1) - jnp.exp(s2) + lam_init, (SUBLANES, LANES))


def _bias_tables(rel_bias, lam_params, past, dec_seq, n_ca, lam_init):
    nh = rel_bias.shape[1]
    t = ATT_BLOCK
    kern = functools.partial(_bias_kernel, past=past, dec_seq=dec_seq, n_ca=n_ca, lam_init=lam_init)
    return pl.pallas_call(
        kern,
        grid=(nh,),
        in_specs=[
            pl.BlockSpec(memory_space=pltpu.SMEM),
            pl.BlockSpec((4, HEAD_DIM_A), lambda h: (0, 0)),
        ],
        out_specs=[
            pl.BlockSpec((1, N_NEAR, t, t), lambda h: (h, 0, 0, 0)),
            pl.BlockSpec((1, N_NEAR, t, LANES), lambda h: (h, 0, 0, 0)),
            pl.BlockSpec((1, dec_seq, n_ca), lambda h: (h, 0, 0)),
            pl.BlockSpec((1, dec_seq, LANES), lambda h: (h, 0, 0)),
            pl.BlockSpec((1, SUBLANES, LANES), lambda h: (h, 0, 0)),
        ],
        out_shape=[
            jax.ShapeDtypeStruct((nh, N_NEAR, t, t), F32),
            jax.ShapeDtypeStruct((nh, N_NEAR, t, LANES), F32),
            jax.ShapeDtypeStruct((nh, dec_seq, n_ca), F32),
            jax.ShapeDtypeStruct((nh, dec_seq, LANES), F32),
            jax.ShapeDtypeStruct((nh, SUBLANES, LANES), F32),
        ],
        compiler_params=pltpu.CompilerParams(dimension_semantics=("arbitrary",)),
        name="bias_tables",
    )(rel_bias, lam_params)


def _nt_dot(a, b):
    return lax.dot_general(a, b, (((1,), (1,)), ((), ())), preferred_element_type=F32)


def _attn_finish(o, z, subln, lam_init):
    ms = jnp.mean(o * o, axis=-1, keepdims=True)
    on = o * lax.rsqrt(ms + EPS) * subln * (1.0 - lam_init)
    return on * _silu(z)


def _attn_prompt_kernel(lam_ref, q_ref, k_ref, v_ref, km_ref, vm_ref, fb_ref, mb_ref, z_ref, sub_ref, o_ref,
                        m_ref, l_ref, acc_ref, sa_ref, sb_ref, *, lam_init):
    qi = pl.program_id(1)
    t = ATT_BLOCK
    d = HEAD_DIM_A
    hw = 2 * d

    def rep(x, n):
        return jnp.concatenate([x] * n, axis=1) if n > 1 else x

    def lane_fold(p):
        out = p[:, 0:LANES]
        for g in range(1, p.shape[1] // LANES):
            out = out + p[:, g * LANES:(g + 1) * LANES]
        return out

    def meta_init():
        vm = vm_ref[...]
        mb = mb_ref[0, jnp.minimum(qi, N_NEAR - 1)]
        for c in range(2):
            s = _nt_dot(q_ref[:, c * d:(c + 1) * d], km_ref[:, c * d:(c + 1) * d]) + mb
            m = jnp.broadcast_to(jnp.max(s, axis=-1, keepdims=True), (t, LANES))
            p = jnp.exp2(s - m)
            m_ref[c] = m
            l_ref[c] = p
            acc_ref[c] = jnp.dot(p.astype(BF16), vm, preferred_element_type=F32)

    def qk(start, width):
        rows = pl.ds(start, width)
        return [_nt_dot(q_ref[:, c * d:(c + 1) * d], k_ref[rows, c * d:(c + 1) * d]) for c in range(2)]

    def softmax_pv(s_both, start, width, bias):
        v = v_ref[pl.ds(start, width), :]
        for c in range(2):
            s = s_both[c]
            if bias is not None:
                s = s + bias
            m_old = m_ref[c]
            m_new = jnp.maximum(m_old, jnp.max(s, axis=-1, keepdims=True))
            p = jnp.exp2(s - rep(m_new, width // LANES))
            alpha = jnp.exp2(m_old - m_new)
            m_ref[c] = m_new
            l_ref[c] = alpha * l_ref[c] + lane_fold(p)
            acc_ref[c] = rep(alpha, hw // LANES) * acc_ref[c] + jnp.dot(
                p.astype(BF16), v, preferred_element_type=F32)

    wide = FAR_BLOCKS * t
    far_shift = FAR_BLOCKS.bit_length() - 1
    n_far = jnp.maximum(qi - (N_NEAR - 1), 0)
    n_wide = n_far >> far_shift

    def put(ref, s_both):
        ref[0] = s_both[0]
        ref[1] = s_both[1]

    @pl.when(n_wide > 0)
    def _():
        put(sa_ref, qk(0, wide))
        meta_init()

    @pl.when(n_wide == 0)
    def _():
        meta_init()

    def far_body(i, carry):
        b0 = pl.multiple_of(i * (2 * wide), wide)
        b1 = pl.multiple_of(b0 + wide, wide)
        b2 = pl.multiple_of(jnp.minimum(b0 + 2 * wide, (n_wide - 1) * wide), wide)
        put(sb_ref, qk(b1, wide))
        softmax_pv([sa_ref[0], sa_ref[1]], b0, wide, None)
        put(sa_ref, qk(b2, wide))
        softmax_pv([sb_ref[0], sb_ref[1]], b1, wide, None)
        return carry

    def far_body2(i, carry):
        far_body(2 * i, carry)
        return far_body(2 * i + 1, carry)
    lax.fori_loop(0, n_wide >> 2, far_body2, 0)
    lax.fori_loop((n_wide >> 2) * 2, n_wide >> 1, far_body, 0)

    @pl.when((n_wide & 1) == 1)
    def _():
        softmax_pv([sa_ref[0], sa_ref[1]], pl.multiple_of((n_wide - 1) * wide, wide), wide, None)

    n_rest = n_far - (n_wide << far_shift)
    for cnt in range(1, FAR_BLOCKS):
        @pl.when(n_rest == cnt)
        def _(cnt=cnt):
            first = n_wide << far_shift
            s_next = qk(pl.multiple_of(first * t, t), t)
            for b in range(cnt):
                s_cur = s_next
                if b + 1 < cnt:
                    s_next = qk(pl.multiple_of((first + b + 1) * t, t), t)
                softmax_pv(s_cur, pl.multiple_of((first + b) * t, t), t, None)

    @pl.when(qi >= N_NEAR - 1)
    def _():
        s_next = qk(pl.multiple_of((qi - (N_NEAR - 1)) * t, t), t)
        for dd in range(N_NEAR - 1, -1, -1):
            s_cur = s_next
            if dd > 0:
                s_next = qk(pl.multiple_of((qi - dd + 1) * t, t), t)
            softmax_pv(s_cur, pl.multiple_of((qi - dd) * t, t), t, fb_ref[0, dd])

    @pl.when(qi < N_NEAR - 1)
    def _():
        for dd in range(N_NEAR - 2, -1, -1):
            @pl.when(qi - dd >= 0)
            def _(dd=dd):
                start = pl.multiple_of((qi - dd) * t, t)
                softmax_pv(qk(start, t), start, t, fb_ref[0, dd])

    inv0 = 1.0 / jnp.sum(l_ref[0], axis=-1, keepdims=True)
    inv1 = lam_ref[0] / jnp.sum(l_ref[1], axis=-1, keepdims=True)
    o = acc_ref[0] * inv0 - acc_ref[1] * inv1
    o_ref[...] = _attn_finish(o, z_ref[...], sub_ref[...], lam_init).astype(BF16)


def _attn_prompt(lam, qkv16, o32, km, vm, fb, mb, subln, n_heads, lam_init):
    seq = qkv16.shape[0]
    t = ATT_BLOCK
    hw = 2 * HEAD_DIM_A
    kern = functools.partial(_attn_prompt_kernel, lam_init=lam_init)
    return pl.pallas_call(
        kern,
        grid=(n_heads, seq // t),
        in_specs=[
            pl.BlockSpec(memory_space=pltpu.SMEM),
            pl.BlockSpec((t, hw), lambda h, i: (i, h)),
            pl.BlockSpec((seq, hw), lambda h, i: (0, n_heads + h)),
            pl.BlockSpec((seq, hw), lambda h, i: (0, 2 * n_heads + h)),
            pl.BlockSpec((LANES, hw), lambda h, i: (0, h)),
            pl.BlockSpec((LANES, hw), lambda h, i: (0, h)),
            pl.BlockSpec((1, N_NEAR, t, t), lambda h, i: (h, 0, 0, 0)),
            pl.BlockSpec((1, N_NEAR, t, LANES), lambda h, i: (h, 0, 0, 0)),
            pl.BlockSpec((t, hw), lambda h, i: (i, 2 * n_heads + h)),
            pl.BlockSpec((1, hw), lambda h, i: (0, 0)),
        ],
        out_specs=pl.BlockSpec((t, hw), lambda h, i: (i, h)),
        out_shape=jax.ShapeDtypeStruct((seq, n_heads * hw), BF16),
        scratch_shapes=[
            pltpu.VMEM((2, t, LANES), F32),
            pltpu.VMEM((2, t, LANES), F32),
            pltpu.VMEM((2, t, hw), F32),
            pltpu.VMEM((2, t, FAR_BLOCKS * t), F32),
            pltpu.VMEM((2, t, FAR_BLOCKS * t), F32),
        ],
        compiler_params=pltpu.CompilerParams(
            dimension_semantics=("arbitrary", "arbitrary"), vmem_limit_bytes=VMEM_LIMIT),
        name="attn_prompt",
    )(lam, qkv16, qkv16, qkv16, km, vm, fb, mb, o32, subln)


def _attn_sample_kernel(lam_ref, q_ref, kc_ref, vc_ref, ke_ref, ve_ref, sa_ref, se_ref, z_ref, sub_ref, o_ref,
                        *, lam_init, n_heads, n_ca, hs):
    d = HEAD_DIM_A
    hw = 2 * d
    for i in range(hs):
        h = pl.program_id(1) * hs + i
        cols = slice(i * hw, (i + 1) * hw)
        vc = jnp.concatenate([vc_ref[0, pl.ds(c * n_heads + h, n_ca, stride=2 * n_heads), :] for c in range(2)],
                             axis=1).astype(BF16)
        ve = ve_ref[0, :, cols]
        outs = []
        for c in range(2):
            q = q_ref[:, i * hw + c * d:i * hw + (c + 1) * d]
            kc = kc_ref[0, pl.ds(2 * h + c, n_ca, stride=2 * n_heads), :].astype(BF16)
            s_a = _nt_dot(q, kc) + sa_ref[i]
            s_e = _nt_dot(q, ke_ref[0, :, i * hw + c * d:i * hw + (c + 1) * d]) + se_ref[i]
            m = jnp.maximum(jnp.max(s_a, axis=-1, keepdims=True), jnp.max(s_e, axis=-1, keepdims=True))
            p_a = jnp.exp2(s_a - m)
            p_e = jnp.exp2(s_e - m)
            l = jnp.sum(p_a, axis=-1, keepdims=True) + jnp.sum(p_e, axis=-1, keepdims=True)
            acc = (jnp.dot(p_a.astype(BF16), vc, preferred_element_type=F32)
                   + jnp.dot(p_e.astype(BF16), ve, preferred_element_type=F32))
            outs.append(acc / l)
        o = outs[0] - lam_ref[0] * outs[1]
        o_ref[:, cols] = _attn_finish(o, z_ref[:, cols], sub_ref[...], lam_init).astype(BF16)


def _attn_sample(lam, qkv16, o32, cache_k, cache_v, ke, ve, sa, se, subln, n_heads, dec_b, dec_seq, n_ca,
                 lam_init):
    hw = 2 * HEAD_DIM_A
    hs = min(SAMPLE_HEADS_PER_STEP, n_heads)
    n_grp = n_heads // hs
    kern = functools.partial(_attn_sample_kernel, lam_init=lam_init, n_heads=n_heads, n_ca=n_ca, hs=hs)
    return pl.pallas_call(
        kern,
        grid=(dec_b, n_grp),
        in_specs=[
            pl.BlockSpec(memory_space=pltpu.SMEM),
            pl.BlockSpec((dec_seq, hs * hw), lambda b, g: (b, g)),
            pl.BlockSpec((1, n_ca * n_heads * 2, HEAD_DIM_A), lambda b, g: (b, 0, 0)),
            pl.BlockSpec((1, n_ca * n_heads * 2, HEAD_DIM_A), lambda b, g: (b, 0, 0)),
            pl.BlockSpec((1, LANES, hs * hw), lambda b, g: (b, 0, g)),
            pl.BlockSpec((1, LANES, hs * hw), lambda b, g: (b, 0, g)),
            pl.BlockSpec((hs, dec_seq, n_ca), lambda b, g: (g, 0, 0)),
            pl.BlockSpec((hs, dec_seq, LANES), lambda b, g: (g, 0, 0)),
            pl.BlockSpec((dec_seq, hs * hw), lambda b, g: (b, 2 * n_grp + g)),
            pl.BlockSpec((1, hw), lambda b, g: (0, 0)),
        ],
        out_specs=pl.BlockSpec((dec_seq, hs * hw), lambda b, g: (b, g)),
        out_shape=jax.ShapeDtypeStruct((dec_b * dec_seq, n_heads * hw), BF16),
        compiler_params=pltpu.CompilerParams(
            dimension_semantics=("arbitrary", "arbitrary"), vmem_limit_bytes=VMEM_LIMIT),
        name="attn_sample",
    )(lam, qkv16, cache_k, cache_v, ke, ve, sa, se, o32, subln)


def _hdot(a, b):
    return jnp.dot(a, b, preferred_element_type=F32, precision=HIGHEST)


def _inv_unit_lower(a_list, row, col):
    sh = 4
    n = a_list[0].shape[0]
    eye = (row == col).astype(F32)
    same = (row >> sh) == (col >> sh)
    dblk = [jnp.where(same, a, 0.0) for a in a_list]
    t = [eye - x for x in dblk]
    pw = [x.astype(BF16) for x in dblk]
    for _ in range(sh - 1):
        pw = [jnp.dot(x, x, preferred_element_type=F32).astype(BF16) for x in pw]
        t = [ti + jnp.dot(ti.astype(BF16), x, preferred_element_type=F32) for ti, x in zip(t, pw)]
    while (1 << sh) < n:
        offm = jnp.logical_and((row >> (sh + 1)) == (col >> (sh + 1)), (row >> sh) != (col >> sh))
        off = [jnp.where(offm, a, 0.0).astype(BF16) for a in a_list]
        tb = [ti.astype(BF16) for ti in t]
        mid = [jnp.dot(o, x, preferred_element_type=F32).astype(BF16) for o, x in zip(off, tb)]
        t = [ti - jnp.dot(x, m, preferred_element_type=F32) for ti, x, m in zip(t, tb, mid)]
        sh += 1
    return t


def _gdn_kernel(q_ref, k_ref, v_ref, z_ref, ba_ref, cwq_ref, cwk_ref, cwv_ref, gp_ref, nb_ref, s0_ref,
                hq_ref, hk_ref, hv_ref, y_ref, sout_ref, s_ref, xq_ref, xk_ref, xv_ref, *, lr, hg):
    c = pl.program_id(2)
    n = GDN_CHUNK
    dh = HEAD_DIM_B
    halo = SUBLANES

    @pl.when(c == 0)
    def _():
        s_ref[...] = s0_ref[0]
        xq_ref[0:halo, :] = hq_ref[0]
        xk_ref[0:halo, :] = hk_ref[0]
        xv_ref[0:halo, :] = hv_ref[0]

    row_w = lax.broadcasted_iota(jnp.int32, (n, hg * dh), 0)

    def conv(x_ref, xbuf, cw_ref):
        xbuf[halo:halo + lr, :] = x_ref[...]
        if lr < n:
            xbuf[halo + lr:halo + n, :] = jnp.zeros((n - lr, hg * dh), F32)
        y = xbuf[halo - 3:halo - 3 + n, :] * cw_ref[0:1, :]
        for i in range(1, CONV_W):
            y = y + xbuf[halo - 3 + i:halo - 3 + i + n, :] * cw_ref[i:i + 1, :]
        y = _silu(y)
        if lr < n:
            y = jnp.where(row_w < lr, y, 0.0)
        xbuf[0:halo, :] = xbuf[lr:lr + halo, :]
        return y

    yq = conv(q_ref, xq_ref, cwq_ref)
    yk = conv(k_ref, xk_ref, cwk_ref)
    yv = conv(v_ref, xv_ref, cwv_ref)

    row = lax.broadcasted_iota(jnp.int32, (n, n), 0)
    col = lax.broadcasted_iota(jnp.int32, (n, n), 1)
    ba = ba_ref[...]
    if lr < n:
        ba = jnp.concatenate([ba, jnp.zeros((n - lr, LANES), F32)], axis=0)
    live = row < lr
    beta = jnp.where(jnp.logical_and(live, col < hg), jax.nn.sigmoid(ba), 0.0)
    gval = -jnp.exp(gp_ref[0, 0:1, :]) * jax.nn.softplus(ba + gp_ref[0, 1:2, :])
    g = jnp.where(jnp.logical_and(live, jnp.logical_and(col >= hg, col < 2 * hg)), gval, 0.0)
    gsum = _hdot((row >= col).astype(F32), g)
    gsum_t = gsum.T
    incl = row >= col
    strict = row > col

    def run(heads):
        cols = {hh: slice(hh * dh, (hh + 1) * dh) for hh in heads}
        gc = {hh: jnp.broadcast_to(gsum[:, hg + hh:hg + hh + 1], (n, n)) for hh in heads}
        bc = {hh: jnp.broadcast_to(beta[:, hh:hh + 1], (n, n)) for hh in heads}
        gam = {hh: jnp.where(incl, jnp.exp(jnp.minimum(gc[hh] - gsum_t[hg + hh:hg + hh + 1, :], 0.0)), 0.0)
               for hh in heads}
        qh = {hh: yq[:, cols[hh]] * (lax.rsqrt(jnp.sum(yq[:, cols[hh]] * yq[:, cols[hh]], axis=-1, keepdims=True)
                                               + EPS) * (dh ** -0.5)) for hh in heads}
        kh = {hh: yk[:, cols[hh]] * lax.rsqrt(jnp.sum(yk[:, cols[hh]] * yk[:, cols[hh]], axis=-1, keepdims=True)
                                              + EPS) for hh in heads}
        qb = {hh: qh[hh].astype(BF16) for hh in heads}
        kb = {hh: kh[hh].astype(BF16) for hh in heads}
        kk = {hh: _nt_dot(kb[hh], kb[hh]) for hh in heads}
        qk = {hh: _nt_dot(qb[hh], kb[hh]) for hh in heads}
        a = [jnp.where(strict, bc[hh] * kk[hh] * gam[hh], 0.0) for hh in heads]
        t = dict(zip(heads, _inv_unit_lower(a, row, col)))
        eg = {hh: jnp.exp(gc[hh]) for hh in heads}
        rhs = {hh: jnp.concatenate([yv[:, cols[hh]] * bc[hh], kh[hh] * (bc[hh] * eg[hh])], axis=1).astype(BF16)
               for hh in heads}
        uw = {hh: jnp.dot(t[hh].astype(BF16), rhs[hh], preferred_element_type=F32) for hh in heads}
        s = {hh: s_ref[hh] for hh in heads}
        sb = {hh: s[hh].astype(BF16) for hh in heads}
        v_new = {hh: uw[hh][:, :dh] - jnp.dot(uw[hh][:, dh:].astype(BF16), sb[hh], preferred_element_type=F32)
                 for hh in heads}
        vnb = {hh: v_new[hh].astype(BF16) for hh in heads}
        o = {hh: eg[hh] * jnp.dot(qb[hh], sb[hh], preferred_element_type=F32)
             + jnp.dot((qk[hh] * gam[hh]).astype(BF16), vnb[hh], preferred_element_type=F32) for hh in heads}
        g_last = {hh: gc[hh][n - 1:n, :] for hh in heads}
        kd = {hh: (kh[hh] * jnp.exp(g_last[hh] - gc[hh])).astype(BF16) for hh in heads}
        for hh in heads:
            s_ref[hh] = s[hh] * jnp.exp(g_last[hh]) + lax.dot_general(
                kd[hh], vnb[hh], (((0,), (0,)), ((), ())), preferred_element_type=F32)
        for hh in heads:
            on = o[hh] * lax.rsqrt(jnp.mean(o[hh] * o[hh], axis=-1, keepdims=True) + EPS) * nb_ref[...]
            y_ref[:, cols[hh]] = (on[0:lr] * _silu(z_ref[:, cols[hh]])).astype(BF16)

    run(list(range(hg)))

    @pl.when(c == pl.num_programs(2) - 1)
    def _():
        sout_ref[0] = s_ref[...]


def _gdn(o32, ba, conv_w, gate_par, norm_b, s0, halo0, *, n_seq, n_chunks, lr, row_blk_off, w_b, col0):
    hg = min(GDN_HEADS_PER_STEP, w_b // HEAD_DIM_B)
    gw = hg * HEAD_DIM_B
    n_groups = w_b // gw
    n_heads = w_b // HEAD_DIM_B
    cb = col0 // gw
    wb = w_b // gw

    def rmap(off):
        return lambda s, g, c: (row_blk_off + s * n_chunks + c, off + g)

    kern = functools.partial(_gdn_kernel, lr=lr, hg=hg)
    return pl.pallas_call(
        kern,
        grid=(n_seq, n_groups, n_chunks),
        in_specs=[
            pl.BlockSpec((lr, gw), rmap(cb)),
            pl.BlockSpec((lr, gw), rmap(cb + wb)),
            pl.BlockSpec((lr, gw), rmap(cb + 2 * wb)),
            pl.BlockSpec((lr, gw), rmap(cb + 3 * wb)),
            pl.BlockSpec((lr, LANES), rmap(0)),
            pl.BlockSpec((CONV_W, gw), lambda s, g, c: (0, g)),
            pl.BlockSpec((CONV_W, gw), lambda s, g, c: (0, wb + g)),
            pl.BlockSpec((CONV_W, gw), lambda s, g, c: (0, 2 * wb + g)),
            pl.BlockSpec((1, 2, LANES), lambda s, g, c: (g, 0, 0)),
            pl.BlockSpec((1, HEAD_DIM_B), lambda s, g, c: (0, 0)),
            pl.BlockSpec((1, hg, HEAD_DIM_B, HEAD_DIM_B), lambda s, g, c: (s, g, 0, 0)),
            pl.BlockSpec((1, SUBLANES, gw), lambda s, g, c: (s, 0, g)),
            pl.BlockSpec((1, SUBLANES, gw), lambda s, g, c: (s, 0, wb + g)),
            pl.BlockSpec((1, SUBLANES, gw), lambda s, g, c: (s, 0, 2 * wb + g)),
        ],
        out_specs=[
            pl.BlockSpec((lr, gw), lambda s, g, c: (s * n_chunks + c, g)),
            pl.BlockSpec((1, hg, HEAD_DIM_B, HEAD_DIM_B), lambda s, g, c: (s, g, 0, 0)),
        ],
        out_shape=[
            jax.ShapeDtypeStruct((n_seq * n_chunks * lr, w_b), BF16),
            jax.ShapeDtypeStruct((n_seq, n_heads, HEAD_DIM_B, HEAD_DIM_B), F32),
        ],
        scratch_shapes=[
            pltpu.VMEM((hg, HEAD_DIM_B, HEAD_DIM_B), F32),
            pltpu.VMEM((SUBLANES + GDN_CHUNK, gw), F32),
            pltpu.VMEM((SUBLANES + GDN_CHUNK, gw), F32),
            pltpu.VMEM((SUBLANES + GDN_CHUNK, gw), F32),
        ],
        compiler_params=pltpu.CompilerParams(
            dimension_semantics=("arbitrary", "arbitrary", "arbitrary"), vmem_limit_bytes=VMEM_LIMIT),
        name=f"gdn_l{lr}",
    )(o32, o32, o32, o32, ba, conv_w, conv_w, conv_w, gate_par, norm_b, s0, halo0, halo0, halo0)


def _out_proj_kernel(ya_ref, yb_ref, w_ref, h_ref, pn_ref, o_ref, *, w_a):
    j = pl.program_id(1)
    tn = w_ref.shape[1]
    r = (jnp.dot(ya_ref[...], w_ref[0:w_a, :], preferred_element_type=F32)
         + jnp.dot(yb_ref[...], w_ref[w_a:, :], preferred_element_type=F32))
    o_ref[:, pl.ds(pl.multiple_of(j * tn, tn), tn)] = r

    @pl.when(j == pl.num_programs(1) - 1)
    def _():
        rows = 64

        def body(i, carry):
            sl = pl.ds(pl.multiple_of(i * rows, rows), rows)
            y = o_ref[sl, :]
            ms = jnp.mean(y * y, axis=-1, keepdims=True)
            o_ref[sl, :] = h_ref[sl, :] + y * lax.rsqrt(ms + EPS) * pn_ref[...]
            return carry
        lax.fori_loop(0, o_ref.shape[0] // rows, body, 0)


def _out_proj(ya, yb, w_out, h, post_norm):
    m, d = h.shape
    w_a = ya.shape[1]
    tm, tn = min(OUT_TM, m), OUT_TN
    kern = functools.partial(_out_proj_kernel, w_a=w_a)
    return pl.pallas_call(
        kern,
        grid=(m // tm, d // tn),
        in_specs=[
            pl.BlockSpec((tm, w_a), lambda i, j: (i, 0)),
            pl.BlockSpec((tm, yb.shape[1]), lambda i, j: (i, 0)),
            pl.BlockSpec((d, tn), lambda i, j: (0, j)),
            pl.BlockSpec((tm, d), lambda i, j: (i, 0)),
            pl.BlockSpec((1, d), lambda i, j: (0, 0)),
        ],
        out_specs=pl.BlockSpec((tm, d), lambda i, j: (i, 0)),
        out_shape=jax.ShapeDtypeStruct((m, d), F32),
        compiler_params=pltpu.CompilerParams(
            dimension_semantics=("arbitrary", "arbitrary"), vmem_limit_bytes=VMEM_LIMIT),
        name="out_proj",
    )(ya, yb, w_out, h, post_norm)


def _kv_layout_kernel(km_ref, vm_ref, k_ref, v_ref, ko_ref, vo_ref, kt_ref, vt_ref, *, n_heads):
    tm = k_ref.shape[0]
    n_lead = kt_ref.shape[0]
    rpt = 2 * n_heads

    @pl.when(pl.program_id(0) == 0)
    def _():
        kt_ref[...] = km_ref[...]
        vt_ref[...] = vm_ref[...]

    for p in range(rpt):
        piece = slice(p * LANES, (p + 1) * LANES)
        vrow = (p % 2) * n_heads + p // 2
        ko_ref[pl.ds(p, n_lead, stride=rpt), :] = kt_ref[:, piece]
        ko_ref[pl.ds(n_lead * rpt + p, tm - n_lead, stride=rpt), :] = k_ref[0:tm - n_lead, piece]
        vo_ref[pl.ds(vrow, n_lead, stride=rpt), :] = vt_ref[:, piece]
        vo_ref[pl.ds(n_lead * rpt + vrow, tm - n_lead, stride=rpt), :] = v_ref[0:tm - n_lead, piece]
    kt_ref[...] = k_ref[tm - n_lead:tm, :]
    vt_ref[...] = v_ref[tm - n_lead:tm, :]


def _kv_layout(k_lead, v_lead, p32, n_heads, w_a):
    seq = p32.shape[0]
    n_lead = k_lead.shape[0]
    tm = KV_TM
    rpt = 2 * n_heads
    n_blk = seq // tm
    spec_out = pl.BlockSpec((tm * rpt, LANES), lambda i: (i, 0))
    shape_out = jax.ShapeDtypeStruct(((n_lead + seq) * rpt, LANES), F32)
    return pl.pallas_call(
        functools.partial(_kv_layout_kernel, n_heads=n_heads),
        grid=(n_blk + 1,),
        in_specs=[
            pl.BlockSpec((n_lead, w_a), lambda i: (0, 0)),
            pl.BlockSpec((n_lead, w_a), lambda i: (0, 0)),
            pl.BlockSpec((tm, w_a), lambda i: (jnp.minimum(i, n_blk - 1), 0)),
            pl.BlockSpec((tm, w_a), lambda i: (jnp.minimum(i, n_blk - 1), 1)),
        ],
        out_specs=[spec_out, spec_out],
        out_shape=[shape_out, shape_out],
        scratch_shapes=[pltpu.VMEM((n_lead, w_a), F32), pltpu.VMEM((n_lead, w_a), F32)],
        compiler_params=pltpu.CompilerParams(dimension_semantics=("arbitrary",), vmem_limit_bytes=VMEM_LIMIT),
        name="kv_layout",
    )(k_lead, v_lead, p32, p32)


def kernel(x_prompt, x_sample, cache_k_a, cache_v_a, state_ssm_b, state_conv_b, meta_tokens, rel_bias, pre_norm,
           w_in, lambda_q1, lambda_k1, lambda_q2, lambda_k2, subln_a, conv_b, a_log_b, dt_bias_b, norm_b, w_out,
           post_norm):
    batch, seq, d_model = x_prompt.shape
    dec_b, dec_seq, _ = x_sample.shape
    depth = w_in.shape[0]
    assert batch == 1 and depth == 1
    n_heads_a = rel_bias.shape[1]
    w_a = n_heads_a * 2 * HEAD_DIM_A
    w_b = d_model - w_a
    n_heads_b = w_b // HEAD_DIM_B
    n_cache = cache_k_a.shape[2]
    past = n_cache - N_META
    n_ca = (n_cache // LANES) * LANES
    n_main = 4 * w_a + 4 * w_b
    lam_init = _lambda_init(0)
    hg = min(GDN_HEADS_PER_STEP, n_heads_b)
    n_groups = n_heads_b // hg
    n_dec = dec_b * dec_seq
    assert seq % PROJ_TM == 0 and seq % GDN_CHUNK == 0 and n_cache - n_ca + dec_seq <= LANES

    wb = w_in[0, :, n_main:n_main + n_heads_b].astype(BF16).reshape(d_model, n_groups, hg)
    wa = w_in[0, :, n_main + n_heads_b:].astype(BF16).reshape(d_model, n_groups, hg)
    w_tail = jnp.concatenate([wb, wa, jnp.zeros((d_model, n_groups, LANES - 2 * hg), BF16)], axis=-1)
    w_tail = w_tail.reshape(d_model, n_groups * LANES)
    w_out_bf = w_out[0].astype(BF16)
    gate_par = jnp.zeros((n_groups, 2, LANES), F32)
    gate_par = gate_par.at[:, 0, hg:2 * hg].set(a_log_b[0].reshape(n_groups, hg))
    gate_par = gate_par.at[:, 1, hg:2 * hg].set(dt_bias_b[0].reshape(n_groups, hg))
    lam_params = jnp.stack([lambda_q1[0], lambda_k1[0], lambda_q2[0], lambda_k2[0]])

    xp = x_prompt[0]
    n_small = -(-(n_dec + N_META) // 16) * 16
    xs = jnp.concatenate([x_sample.reshape(n_dec, d_model), meta_tokens.astype(F32),
                          jnp.zeros((n_small - n_dec - N_META, d_model), F32)], axis=0)
    s32, s16, sba, w_main_bf = _in_proj(xs, pre_norm, w_in[0], w_tail, w_a, n_main, single_tile=True)
    p32, p16, pba = _in_proj(xp, pre_norm, w_main_bf, w_tail, w_a, n_main)

    fb, mb, sa, se, lam_t = _bias_tables(rel_bias, lam_params, past, dec_seq, n_ca, lam_init)
    lam = lam_t[0, 0, 0:1]

    meta16 = s16[n_dec:n_dec + N_META]
    pad_m = jnp.zeros((LANES - N_META, w_a), BF16)
    km = jnp.concatenate([meta16[:, w_a:2 * w_a], pad_m], axis=0)
    vm = jnp.concatenate([meta16[:, 2 * w_a:], pad_m], axis=0)
    ya_p = _attn_prompt(lam, p16, p32, km, vm, fb, mb, subln_a, n_heads_a, lam_init)

    ck = cache_k_a[0].reshape(dec_b, n_cache * n_heads_a * 2, HEAD_DIM_A)
    cv = cache_v_a[0].reshape(dec_b, n_cache, n_heads_a, 2, HEAD_DIM_A).transpose(0, 1, 3, 2, 4)
    cv = cv.reshape(dec_b, n_cache * 2 * n_heads_a, HEAD_DIM_A)
    n_extra = n_cache - n_ca + dec_seq
    pad_e = jnp.zeros((dec_b, LANES - n_extra, w_a), BF16)
    ck_tail = cache_k_a[0, :, n_ca:].reshape(dec_b, n_cache - n_ca, w_a).astype(BF16)
    cv_tail = cache_v_a[0, :, n_ca:].reshape(dec_b, n_cache - n_ca, w_a).astype(BF16)
    ke = jnp.concatenate([ck_tail, s16[:n_dec, w_a:2 * w_a].reshape(dec_b, dec_seq, w_a), pad_e], axis=1)
    ve = jnp.concatenate([cv_tail, s16[:n_dec, 2 * w_a:].reshape(dec_b, dec_seq, w_a), pad_e], axis=1)
    ya_s = _attn_sample(lam, s16, s32, ck, cv, ke, ve, sa, se, subln_a, n_heads_a, dec_b, dec_seq, n_ca,
                        lam_init)

    col0 = 3 * w_a
    qkv_cols = slice(col0, col0 + 3 * w_b)
    zero_halo = jnp.zeros((1, SUBLANES, 3 * w_b), F32)
    gdn = functools.partial(_gdn, conv_w=conv_b[0], gate_par=gate_par, norm_b=norm_b, w_b=w_b, col0=col0)
    _, s_meta = gdn(s32, sba, s0=jnp.zeros((1, n_heads_b, HEAD_DIM_B, HEAD_DIM_B), F32), halo0=zero_halo,
                    n_seq=1, n_chunks=1, lr=N_META, row_blk_off=n_dec // N_META)
    meta_halo = jnp.concatenate([jnp.zeros((SUBLANES - (CONV_W - 1), 3 * w_b), F32),
                                 s32[n_dec + N_META - (CONV_W - 1):n_dec + N_META, qkv_cols]], axis=0)[None]
    yb_p, ssm_p = gdn(p32, pba, s0=s_meta, halo0=meta_halo, n_seq=1, n_chunks=seq // GDN_CHUNK, lr=GDN_CHUNK,
                      row_blk_off=0)
    samp_halo = jnp.concatenate([jnp.zeros((dec_b, SUBLANES - (CONV_W - 1), 3 * w_b), F32), state_conv_b[0]], axis=1)
    yb_s, ssm_s = gdn(s32, sba, s0=state_ssm_b[0].astype(F32), halo0=samp_halo, n_seq=dec_b, n_chunks=1,
                      lr=dec_seq, row_blk_off=0)

    y_p = _out_proj(ya_p, yb_p, w_out_bf, xp, post_norm)
    y_s = _out_proj(ya_s, yb_s, w_out_bf, x_sample.reshape(n_dec, d_model), post_norm)

    hd = HEAD_DIM_A
    meta32 = s32[n_dec:n_dec + N_META]
    k_p, v_p = _kv_layout(meta32[:, :w_a], meta32[:, w_a:2 * w_a], p32, n_heads_a, w_a)
    v_p = v_p.reshape(N_META + seq, 2, n_heads_a, hd).transpose(0, 2, 1, 3)
    conv_p = p32[seq - (CONV_W - 1):, qkv_cols]
    k_s = s32[:n_dec, :w_a]
    v_s = s32[:n_dec, w_a:2 * w_a]
    conv_s = s32[:n_dec, qkv_cols].reshape(dec_b, dec_seq, 3 * w_b)[:, dec_seq - (CONV_W - 1):]
    return (
        y_p[None],
        y_s.reshape(dec_b, dec_seq, d_model),
        k_p.reshape(1, 1, N_META + seq, n_heads_a, 2, hd),
        v_p.reshape(1, 1, N_META + seq, n_heads_a, 2 * hd),
        ssm_p[None],
        conv_p[None, None],
        k_s.reshape(1, dec_b, dec_seq, n_heads_a, 2, hd),
        v_s.reshape(1, dec_b, dec_seq, n_heads_a, 2 * hd),
        ssm_s[None],
        conv_s[None],
    )
```

```python
import functools
import math

import jax
import jax.numpy as jnp
from jax import lax
from jax.experimental import pallas as pl
from jax.experimental.pallas import tpu as pltpu

F32 = jnp.float32
BF16 = jnp.bfloat16
HIGHEST = lax.Precision.HIGHEST

EPS = 1e-6
CHUNK = 64
N_META = 16
HEAD_DIM_A = 128
HEAD_DIM_B = 128
CONV_W = 4
N_BUCKETS = 32
MAX_DISTANCE = 1024
NEG = -1e30
LOG2E = math.log2(math.e)

LANES = 128
SUBLANES = 8
V7X_VMEM_BYTES = 64 * 1024 * 1024
VMEM_LIMIT = V7X_VMEM_BYTES - 8 * 1024 * 1024

ATT_BLOCK = 256
FAR_BLOCKS = 8
SAMPLE_HEADS_PER_STEP = 8
GDN_CHUNK = 128
GDN_HEADS_PER_STEP = 16
PROJ_TM = 512
PROJ_TN = 1024
PROJ_TN_SINGLE = 512
KV_TM = 512
OUT_TM = 512
OUT_TN = 512


def _lambda_init(layer):
    return 0.8 - 0.6 * math.exp(-0.3 * layer)


def _bias_saturation_distance():
    nb = N_BUCKETS // 2
    max_exact = nb // 2
    return int(math.ceil(max_exact * (MAX_DISTANCE / max_exact) ** ((nb - max_exact - 1) / (nb - max_exact)))) + 1


N_NEAR = -(-(_bias_saturation_distance() + ATT_BLOCK - 1) // ATT_BLOCK)
FAR_BUCKET = N_BUCKETS // 2 - 1


def _silu(x):
    return x * jax.nn.sigmoid(x)


def _in_proj_kernel(x_ref, pn_ref, w_ref, wt_ref, o32_ref, o16_ref, ba_ref, xn_ref, *, q_tiles, bf_tiles,
                    q_scale):
    j = pl.program_id(1)
    tm = x_ref.shape[0]
    rows = max(r for r in range(SUBLANES, 81, SUBLANES) if tm % r == 0)

    @pl.when(j == 0)
    def _():
        def body(r, carry):
            sl = pl.ds(pl.multiple_of(r * rows, 8), rows)
            x = x_ref[sl, :]
            ms = jnp.mean(x * x, axis=-1, keepdims=True)
            xn_ref[sl, :] = (x * lax.rsqrt(ms + EPS) * pn_ref[...]).astype(BF16)
            return carry
        lax.fori_loop(0, tm // rows, body, 0)
        ba_ref[...] = jnp.dot(xn_ref[...], wt_ref[...], preferred_element_type=F32)

    r = jnp.dot(xn_ref[...], w_ref[...], preferred_element_type=F32)

    @pl.when(j >= q_tiles)
    def _():
        o32_ref[...] = r

    @pl.when(j < q_tiles)
    def _():
        o16_ref[...] = (r * q_scale).astype(BF16)

    @pl.when(jnp.logical_and(j >= q_tiles, j < bf_tiles))
    def _():
        o16_ref[...] = r.astype(BF16)


def _in_proj(x, pre_norm, w_all, w_tail, w_a, n, single_tile=False):
    m, d = x.shape
    tm, tn = PROJ_TM, min(PROJ_TN, w_a)
    x_mode = {}
    if single_tile:
        tm, tn = m, min(PROJ_TN_SINGLE, w_a)
        x_mode = dict(pipeline_mode=pl.Buffered(1))
    q_tiles = w_a // tn
    bf_tiles = 3 * w_a // tn
    nt = w_tail.shape[1]
    kern = functools.partial(_in_proj_kernel, q_tiles=q_tiles, bf_tiles=bf_tiles, q_scale=HEAD_DIM_A ** -0.5 * LOG2E)
    return pl.pallas_call(
        kern,
        grid=(m // tm, n // tn),
        in_specs=[
            pl.BlockSpec((tm, d), lambda i, j: (i, 0), **x_mode),
            pl.BlockSpec((1, d), lambda i, j: (0, 0)),
            pl.BlockSpec((d, tn), lambda i, j: (0, j)),
            pl.BlockSpec((d, nt), lambda i, j: (0, 0)),
        ],
        out_specs=[
            pl.BlockSpec((tm, tn), lambda i, j: (i, jnp.maximum(j - q_tiles, 0))),
            pl.BlockSpec((tm, tn), lambda i, j: (i, jnp.minimum(j, bf_tiles - 1))),
            pl.BlockSpec((tm, nt), lambda i, j: (i, 0)),
        ],
        out_shape=[
            jax.ShapeDtypeStruct((m, n - w_a), F32),
            jax.ShapeDtypeStruct((m, 3 * w_a), BF16),
            jax.ShapeDtypeStruct((m, nt), F32),
        ],
        scratch_shapes=[pltpu.VMEM((tm, d), BF16)],
        compiler_params=pltpu.CompilerParams(
            dimension_semantics=("arbitrary", "arbitrary"), vmem_limit_bytes=VMEM_LIMIT),
        name="in_proj",
    )(x, pre_norm, w_all, w_tail)


def _bias_values(rb_ref, h, rel):
    nb = N_BUCKETS // 2
    max_exact = nb // 2
    n = jnp.abs(rel)
    nf = jnp.maximum(n, 1).astype(F32)
    large = max_exact + (jnp.log(nf / max_exact) / math.log(MAX_DISTANCE / max_exact)
                         * (nb - max_exact)).astype(jnp.int32)
    large = jnp.minimum(large, nb - 1)
    bucket = jnp.where(rel > 0, nb, 0) + jnp.where(n < max_exact, n, large)
    val = jnp.zeros(rel.shape, F32)
    for b in range(N_BUCKETS):
        val = jnp.where(bucket == b, rb_ref[b, h], val)
    return (val - rb_ref[FAR_BUCKET, h]) * LOG2E


def _bias_kernel(rb_ref, lamp_ref, fb_ref, mb_ref, sa_ref, se_ref, lam_ref, *, past, dec_seq, n_ca, lam_init):
    h = pl.program_id(0)
    t = ATT_BLOCK
    shift = CHUNK.bit_length() - 1
    i = lax.broadcasted_iota(jnp.int32, (t, t), 0)
    j = lax.broadcasted_iota(jnp.int32, (t, t), 1)
    for dd in range(N_NEAR):
        val = _bias_values(rb_ref, h, (j - i) - t * dd)
        if dd == 0:
            val = jnp.where((j >> shift) <= (i >> shift), val, NEG)
        fb_ref[0, dd] = val
    i = lax.broadcasted_iota(jnp.int32, (t, LANES), 0)
    j = lax.broadcasted_iota(jnp.int32, (t, LANES), 1)
    for qb in range(N_NEAR):
        val = _bias_values(rb_ref, h, (j - N_META) - (t * qb + i))
        mb_ref[0, qb] = jnp.where(j < N_META, val, NEG)
    i = lax.broadcasted_iota(jnp.int32, (dec_seq, n_ca), 0)
    j = lax.broadcasted_iota(jnp.int32, (dec_seq, n_ca), 1)
    sa_ref[0] = _bias_values(rb_ref, h, (j - N_META) - (past + i))
    i = lax.broadcasted_iota(jnp.int32, (dec_seq, LANES), 0)
    j = lax.broadcasted_iota(jnp.int32, (dec_seq, LANES), 1)
    n_extra = N_META + past - n_ca + dec_seq
    val = _bias_values(rb_ref, h, (n_ca - N_META + j) - (past + i))
    se_ref[0] = jnp.where(j < n_extra, val, NEG)
    lp = lamp_ref[...]
    s1 = jnp.sum(lp[0:1] * lp[1:2], axis=-1, keepdims=True)
    s2 = jnp.sum(lp[2:3] * lp[3:4], axis=-1, keepdims=True)
    lam_ref[0] = jnp.broadcast_to(jnp.exp(s1) - jnp.exp(s2) + lam_init, (SUBLANES, LANES))


def _bias_tables(rel_bias, lam_params, past, dec_seq, n_ca, lam_init):
    nh = rel_bias.shape[1]
    t = ATT_BLOCK
    kern = functools.partial(_bias_kernel, past=past, dec_seq=dec_seq, n_ca=n_ca, lam_init=lam_init)
    return pl.pallas_call(
        kern,
        grid=(nh,),
        in_specs=[
            pl.BlockSpec(memory_space=pltpu.SMEM),
            pl.BlockSpec((4, HEAD_DIM_A), lambda h: (0, 0)),
        ],
        out_specs=[
            pl.BlockSpec((1, N_NEAR, t, t), lambda h: (h, 0, 0, 0)),
            pl.BlockSpec((1, N_NEAR, t, LANES), lambda h: (h, 0, 0, 0)),
            pl.BlockSpec((1, dec_seq, n_ca), lambda h: (h, 0, 0)),
            pl.BlockSpec((1, dec_seq, LANES), lambda h: (h, 0, 0)),
            pl.BlockSpec((1, SUBLANES, LANES), lambda h: (h, 0, 0)),
        ],
        out_shape=[
            jax.ShapeDtypeStruct((nh, N_NEAR, t, t), F32),
            jax.ShapeDtypeStruct((nh, N_NEAR, t, LANES), F32),
            jax.ShapeDtypeStruct((nh, dec_seq, n_ca), F32),
            jax.ShapeDtypeStruct((nh, dec_seq, LANES), F32),
            jax.ShapeDtypeStruct((nh, SUBLANES, LANES), F32),
        ],
        compiler_params=pltpu.CompilerParams(dimension_semantics=("arbitrary",)),
        name="bias_tables",
    )(rel_bias, lam_params)


def _nt_dot(a, b):
    return lax.dot_general(a, b, (((1,), (1,)), ((), ())), preferred_element_type=F32)


def _attn_finish(o, z, subln, lam_init):
    ms = jnp.mean(o * o, axis=-1, keepdims=True)
    on = o * lax.rsqrt(ms + EPS) * subln * (1.0 - lam_init)
    return on * _silu(z)


def _attn_prompt_kernel(lam_ref, q_ref, k_ref, v_ref, km_ref, vm_ref, fb_ref, mb_ref, z_ref, sub_ref, o_ref,
                        m_ref, l_ref, acc_ref, sa_ref, sb_ref, *, lam_init):
    qi = pl.program_id(1)
    t = ATT_BLOCK
    d = HEAD_DIM_A
    hw = 2 * d

    def rep(x, n):
        return jnp.concatenate([x] * n, axis=1) if n > 1 else x

    def lane_fold(p):
        out = p[:, 0:LANES]
        for g in range(1, p.shape[1] // LANES):
            out = out + p[:, g * LANES:(g + 1) * LANES]
        return out

    def meta_init():
        vm = vm_ref[...]
        mb = mb_ref[0, jnp.minimum(qi, N_NEAR - 1)]
        for c in range(2):
            s = _nt_dot(q_ref[:, c * d:(c + 1) * d], km_ref[:, c * d:(c + 1) * d]) + mb
            m = jnp.broadcast_to(jnp.max(s, axis=-1, keepdims=True), (t, LANES))
            p = jnp.exp2(s - m)
            m_ref[c] = m
            l_ref[c] = p
            acc_ref[c] = jnp.dot(p.astype(BF16), vm, preferred_element_type=F32)

    def qk(start, width):
        rows = pl.ds(start, width)
        return [_nt_dot(q_ref[:, c * d:(c + 1) * d], k_ref[rows, c * d:(c + 1) * d]) for c in range(2)]

    def softmax_pv(s_both, start, width, bias):
        v = v_ref[pl.ds(start, width), :]
        for c in range(2):
            s = s_both[c]
            if bias is not None:
                s = s + bias
            m_old = m_ref[c]
            m_new = jnp.maximum(m_old, jnp.max(s, axis=-1, keepdims=True))
            p = jnp.exp2(s - rep(m_new, width // LANES))
            alpha = jnp.exp2(m_old - m_new)
            m_ref[c] = m_new
            l_ref[c] = alpha * l_ref[c] + lane_fold(p)
            acc_ref[c] = rep(alpha, hw // LANES) * acc_ref[c] + jnp.dot(
                p.astype(BF16), v, preferred_element_type=F32)

    wide = FAR_BLOCKS * t
    far_shift = FAR_BLOCKS.bit_length() - 1
    n_far = jnp.maximum(qi - (N_NEAR - 1), 0)
    n_wide = n_far >> far_shift

    def put(ref, s_both):
        ref[0] = s_both[0]
        ref[1] = s_both[1]

    @pl.when(n_wide > 0)
    def _():
        put(sa_ref, qk(0, wide))
        meta_init()

    @pl.when(n_wide == 0)
    def _():
        meta_init()

    def far_body(i, carry):
        b0 = pl.multiple_of(i * (2 * wide), wide)
        b1 = pl.multiple_of(b0 + wide, wide)
        b2 = pl.multiple_of(jnp.minimum(b0 + 2 * wide, (n_wide - 1) * wide), wide)
        put(sb_ref, qk(b1, wide))
        softmax_pv([sa_ref[0], sa_ref[1]], b0, wide, None)
        put(sa_ref, qk(b2, wide))
        softmax_pv([sb_ref[0], sb_ref[1]], b1, wide, None)
        return carry

    def far_body2(i, carry):
        far_body(2 * i, carry)
        return far_body(2 * i + 1, carry)
    lax.fori_loop(0, n_wide >> 2, far_body2, 0)
    lax.fori_loop((n_wide >> 2) * 2, n_wide >> 1, far_body, 0)

    @pl.when((n_wide & 1) == 1)
    def _():
        softmax_pv([sa_ref[0], sa_ref[1]], pl.multiple_of((n_wide - 1) * wide, wide), wide, None)

    n_rest = n_far - (n_wide << far_shift)
    for cnt in range(FAR_BLOCKS):
        @pl.when(jnp.logical_and(qi >= N_NEAR - 1, n_rest == cnt))
        def _(cnt=cnt):
            first = qi - (N_NEAR - 1) - cnt
            n_tail = cnt + N_NEAR
            s_next = qk(pl.multiple_of(first * t, t), t)
            for b in range(n_tail):
                s_cur = s_next
                if b + 1 < n_tail:
                    s_next = qk(pl.multiple_of((first + b + 1) * t, t), t)
                bias = None if b < cnt else fb_ref[0, n_tail - 1 - b]
                softmax_pv(s_cur, pl.multiple_of((first + b) * t, t), t, bias)

    @pl.when(qi < N_NEAR - 1)
    def _():
        for dd in range(N_NEAR - 2, -1, -1):
            @pl.when(qi - dd >= 0)
            def _(dd=dd):
                start = pl.multiple_of((qi - dd) * t, t)
                softmax_pv(qk(start, t), start, t, fb_ref[0, dd])

    inv0 = 1.0 / jnp.sum(l_ref[0], axis=-1, keepdims=True)
    inv1 = lam_ref[0] / jnp.sum(l_ref[1], axis=-1, keepdims=True)
    o = acc_ref[0] * inv0 - acc_ref[1] * inv1
    o_ref[...] = _attn_finish(o, z_ref[...], sub_ref[...], lam_init).astype(BF16)


def _attn_prompt(lam, qkv16, o32, km, vm, fb, mb, subln, n_heads, lam_init):
    seq = qkv16.shape[0]
    t = ATT_BLOCK
    hw = 2 * HEAD_DIM_A
    kern = functools.partial(_attn_prompt_kernel, lam_init=lam_init)
    return pl.pallas_call(
        kern,
        grid=(n_heads, seq // t),
        in_specs=[
            pl.BlockSpec(memory_space=pltpu.SMEM),
            pl.BlockSpec((t, hw), lambda h, i: (i, h)),
            pl.BlockSpec((seq, hw), lambda h, i: (0, n_heads + h)),
            pl.BlockSpec((seq, hw), lambda h, i: (0, 2 * n_heads + h)),
            pl.BlockSpec((LANES, hw), lambda h, i: (0, h)),
            pl.BlockSpec((LANES, hw), lambda h, i: (0, h)),
            pl.BlockSpec((1, N_NEAR, t, t), lambda h, i: (h, 0, 0, 0)),
            pl.BlockSpec((1, N_NEAR, t, LANES), lambda h, i: (h, 0, 0, 0)),
            pl.BlockSpec((t, hw), lambda h, i: (i, 2 * n_heads + h)),
            pl.BlockSpec((1, hw), lambda h, i: (0, 0)),
        ],
        out_specs=pl.BlockSpec((t, hw), lambda h, i: (i, h)),
        out_shape=jax.ShapeDtypeStruct((seq, n_heads * hw), BF16),
        scratch_shapes=[
            pltpu.VMEM((2, t, LANES), F32),
            pltpu.VMEM((2, t, LANES), F32),
            pltpu.VMEM((2, t, hw), F32),
            pltpu.VMEM((2, t, FAR_BLOCKS * t), F32),
            pltpu.VMEM((2, t, FAR_BLOCKS * t), F32),
        ],
        compiler_params=pltpu.CompilerParams(
            dimension_semantics=("arbitrary", "arbitrary"), vmem_limit_bytes=VMEM_LIMIT),
        name="attn_prompt",
    )(lam, qkv16, qkv16, qkv16, km, vm, fb, mb, o32, subln)


def _attn_sample_kernel(lam_ref, q_ref, kc_ref, vc_ref, ke_ref, ve_ref, sa_ref, se_ref, z_ref, sub_ref, o_ref,
                        *, lam_init, n_heads, n_ca, hs):
    d = HEAD_DIM_A
    hw = 2 * d
    for i in range(hs):
        h = pl.program_id(1) * hs + i
        cols = slice(i * hw, (i + 1) * hw)
        vc = jnp.concatenate([vc_ref[0, pl.ds(c * n_heads + h, n_ca, stride=2 * n_heads), :] for c in range(2)],
                             axis=1).astype(BF16)
        ve = ve_ref[0, :, cols]
        outs = []
        for c in range(2):
            q = q_ref[:, i * hw + c * d:i * hw + (c + 1) * d]
            kc = kc_ref[0, pl.ds(2 * h + c, n_ca, stride=2 * n_heads), :].astype(BF16)
            s_a = _nt_dot(q, kc) + sa_ref[i]
            s_e = _nt_dot(q, ke_ref[0, :, i * hw + c * d:i * hw + (c + 1) * d]) + se_ref[i]
            m = jnp.maximum(jnp.max(s_a, axis=-1, keepdims=True), jnp.max(s_e, axis=-1, keepdims=True))
            p_a = jnp.exp2(s_a - m)
            p_e = jnp.exp2(s_e - m)
            l = jnp.sum(p_a, axis=-1, keepdims=True) + jnp.sum(p_e, axis=-1, keepdims=True)
            acc = (jnp.dot(p_a.astype(BF16), vc, preferred_element_type=F32)
                   + jnp.dot(p_e.astype(BF16), ve, preferred_element_type=F32))
            outs.append(acc / l)
        o = outs[0] - lam_ref[0] * outs[1]
        o_ref[:, cols] = _attn_finish(o, z_ref[:, cols], sub_ref[...], lam_init).astype(BF16)


def _attn_sample(lam, qkv16, o32, cache_k, cache_v, ke, ve, sa, se, subln, n_heads, dec_b, dec_seq, n_ca,
                 lam_init):
    hw = 2 * HEAD_DIM_A
    hs = min(SAMPLE_HEADS_PER_STEP, n_heads)
    n_grp = n_heads // hs
    kern = functools.partial(_attn_sample_kernel, lam_init=lam_init, n_heads=n_heads, n_ca=n_ca, hs=hs)
    return pl.pallas_call(
        kern,
        grid=(dec_b, n_grp),
        in_specs=[
            pl.BlockSpec(memory_space=pltpu.SMEM),
            pl.BlockSpec((dec_seq, hs * hw), lambda b, g: (b, g)),
            pl.BlockSpec((1, n_ca * n_heads * 2, HEAD_DIM_A), lambda b, g: (b, 0, 0)),
            pl.BlockSpec((1, n_ca * n_heads * 2, HEAD_DIM_A), lambda b, g: (b, 0, 0)),
            pl.BlockSpec((1, LANES, hs * hw), lambda b, g: (b, 0, g)),
            pl.BlockSpec((1, LANES, hs * hw), lambda b, g: (b, 0, g)),
            pl.BlockSpec((hs, dec_seq, n_ca), lambda b, g: (g, 0, 0)),
            pl.BlockSpec((hs, dec_seq, LANES), lambda b, g: (g, 0, 0)),
            pl.BlockSpec((dec_seq, hs * hw), lambda b, g: (b, 2 * n_grp + g)),
            pl.BlockSpec((1, hw), lambda b, g: (0, 0)),
        ],
        out_specs=pl.BlockSpec((dec_seq, hs * hw), lambda b, g: (b, g)),
        out_shape=jax.ShapeDtypeStruct((dec_b * dec_seq, n_heads * hw), BF16),
        compiler_params=pltpu.CompilerParams(
            dimension_semantics=("arbitrary", "arbitrary"), vmem_limit_bytes=VMEM_LIMIT),
        name="attn_sample",
    )(lam, qkv16, cache_k, cache_v, ke, ve, sa, se, o32, subln)


def _hdot(a, b):
    return jnp.dot(a, b, preferred_element_type=F32, precision=HIGHEST)


def _inv_unit_lower(a_list, row, col):
    sh = 4
    n = a_list[0].shape[0]
    eye = (row == col).astype(F32)
    same = (row >> sh) == (col >> sh)
    dblk = [jnp.where(same, a, 0.0) for a in a_list]
    t = [eye - x for x in dblk]
    pw = [x.astype(BF16) for x in dblk]
    for _ in range(sh - 1):
        pw = [jnp.dot(x, x, preferred_element_type=F32).astype(BF16) for x in pw]
        t = [ti + jnp.dot(ti.astype(BF16), x, preferred_element_type=F32) for ti, x in zip(t, pw)]
    while (1 << sh) < n:
        offm = jnp.logical_and((row >> (sh + 1)) == (col >> (sh + 1)), (row >> sh) != (col >> sh))
        off = [jnp.where(offm, a, 0.0).astype(BF16) for a in a_list]
        tb = [ti.astype(BF16) for ti in t]
        mid = [jnp.dot(o, x, preferred_element_type=F32).astype(BF16) for o, x in zip(off, tb)]
        t = [ti - jnp.dot(x, m, preferred_element_type=F32) for ti, x, m in zip(t, tb, mid)]
        sh += 1
    return t


def _gdn_kernel(q_ref, k_ref, v_ref, z_ref, ba_ref, cwq_ref, cwk_ref, cwv_ref, gp_ref, nb_ref, s0_ref,
                hq_ref, hk_ref, hv_ref, y_ref, sout_ref, s_ref, xq_ref, xk_ref, xv_ref, *, lr, hg):
    c = pl.program_id(2)
    n = GDN_CHUNK
    dh = HEAD_DIM_B
    halo = SUBLANES

    @pl.when(c == 0)
    def _():
        s_ref[...] = s0_ref[0]
        xq_ref[0:halo, :] = hq_ref[0]
        xk_ref[0:halo, :] = hk_ref[0]
        xv_ref[0:halo, :] = hv_ref[0]

    row_w = lax.broadcasted_iota(jnp.int32, (n, hg * dh), 0)

    def conv(x_ref, xbuf, cw_ref):
        xbuf[halo:halo + lr, :] = x_ref[...]
        if lr < n:
            xbuf[halo + lr:halo + n, :] = jnp.zeros((n - lr, hg * dh), F32)
        y = xbuf[halo - 3:halo - 3 + n, :] * cw_ref[0:1, :]
        for i in range(1, CONV_W):
            y = y + xbuf[halo - 3 + i:halo - 3 + i + n, :] * cw_ref[i:i + 1, :]
        y = _silu(y)
        if lr < n:
            y = jnp.where(row_w < lr, y, 0.0)
        xbuf[0:halo, :] = xbuf[lr:lr + halo, :]
        return y

    yq = conv(q_ref, xq_ref, cwq_ref)
    yk = conv(k_ref, xk_ref, cwk_ref)
    yv = conv(v_ref, xv_ref, cwv_ref)

    row = lax.broadcasted_iota(jnp.int32, (n, n), 0)
    col = lax.broadcasted_iota(jnp.int32, (n, n), 1)
    ba = ba_ref[...]
    if lr < n:
        ba = jnp.concatenate([ba, jnp.zeros((n - lr, LANES), F32)], axis=0)
    live = row < lr
    beta = jnp.where(jnp.logical_and(live, col < hg), jax.nn.sigmoid(ba), 0.0)
    gval = -jnp.exp(gp_ref[0, 0:1, :]) * jax.nn.softplus(ba + gp_ref[0, 1:2, :])
    g = jnp.where(jnp.logical_and(live, jnp.logical_and(col >= hg, col < 2 * hg)), gval, 0.0)
    gsum = _hdot((row >= col).astype(F32), g)
    gsum_t = gsum.T
    incl = row >= col
    strict = row > col

    def run(heads):
        cols = {hh: slice(hh * dh, (hh + 1) * dh) for hh in heads}
        gc = {hh: jnp.broadcast_to(gsum[:, hg + hh:hg + hh + 1], (n, n)) for hh in heads}
        bc = {hh: jnp.broadcast_to(beta[:, hh:hh + 1], (n, n)) for hh in heads}
        gam = {hh: jnp.where(incl, jnp.exp(jnp.minimum(gc[hh] - gsum_t[hg + hh:hg + hh + 1, :], 0.0)), 0.0)
               for hh in heads}
        qh = {hh: yq[:, cols[hh]] * (lax.rsqrt(jnp.sum(yq[:, cols[hh]] * yq[:, cols[hh]], axis=-1, keepdims=True)
                                               + EPS) * (dh ** -0.5)) for hh in heads}
        kh = {hh: yk[:, cols[hh]] * lax.rsqrt(jnp.sum(yk[:, cols[hh]] * yk[:, cols[hh]], axis=-1, keepdims=True)
                                              + EPS) for hh in heads}
        qb = {hh: qh[hh].astype(BF16) for hh in heads}
        kb = {hh: kh[hh].astype(BF16) for hh in heads}
        kk = {hh: _nt_dot(kb[hh], kb[hh]) for hh in heads}
        qk = {hh: _nt_dot(qb[hh], kb[hh]) for hh in heads}
        a = [jnp.where(strict, bc[hh] * kk[hh] * gam[hh], 0.0) for hh in heads]
        t = dict(zip(heads, _inv_unit_lower(a, row, col)))
        eg = {hh: jnp.exp(gc[hh]) for hh in heads}
        rhs = {hh: jnp.concatenate([yv[:, cols[hh]] * bc[hh], kh[hh] * (bc[hh] * eg[hh])], axis=1).astype(BF16)
               for hh in heads}
        uw = {hh: jnp.dot(t[hh].astype(BF16), rhs[hh], preferred_element_type=F32) for hh in heads}
        s = {hh: s_ref[hh] for hh in heads}
        sb = {hh: s[hh].astype(BF16) for hh in heads}
        v_new = {hh: uw[hh][:, :dh] - jnp.dot(uw[hh][:, dh:].astype(BF16), sb[hh], preferred_element_type=F32)
                 for hh in heads}
        vnb = {hh: v_new[hh].astype(BF16) for hh in heads}
        o = {hh: eg[hh] * jnp.dot(qb[hh], sb[hh], preferred_element_type=F32)
             + jnp.dot((qk[hh] * gam[hh]).astype(BF16), vnb[hh], preferred_element_type=F32) for hh in heads}
        g_last = {hh: gc[hh][n - 1:n, :] for hh in heads}
        kd = {hh: (kh[hh] * jnp.exp(g_last[hh] - gc[hh])).astype(BF16) for hh in heads}
        for hh in heads:
            s_ref[hh] = s[hh] * jnp.exp(g_last[hh]) + lax.dot_general(
                kd[hh], vnb[hh], (((0,), (0,)), ((), ())), preferred_element_type=F32)
        for hh in heads:
            on = o[hh] * lax.rsqrt(jnp.mean(o[hh] * o[hh], axis=-1, keepdims=True) + EPS) * nb_ref[...]
            y_ref[:, cols[hh]] = (on[0:lr] * _silu(z_ref[:, cols[hh]])).astype(BF16)

    run(list(range(hg)))

    @pl.when(c == pl.num_programs(2) - 1)
    def _():
        sout_ref[0] = s_ref[...]


def _gdn(o32, ba, conv_w, gate_par, norm_b, s0, halo0, *, n_seq, n_chunks, lr, row_blk_off, w_b, col0):
    hg = min(GDN_HEADS_PER_STEP, w_b // HEAD_DIM_B)
    gw = hg * HEAD_DIM_B
    n_groups = w_b // gw
    n_heads = w_b // HEAD_DIM_B
    cb = col0 // gw
    wb = w_b // gw

    def rmap(off):
        return lambda s, g, c: (row_blk_off + s * n_chunks + c, off + g)

    kern = functools.partial(_gdn_kernel, lr=lr, hg=hg)
    return pl.pallas_call(
        kern,
        grid=(n_seq, n_groups, n_chunks),
        in_specs=[
            pl.BlockSpec((lr, gw), rmap(cb)),
            pl.BlockSpec((lr, gw), rmap(cb + wb)),
            pl.BlockSpec((lr, gw), rmap(cb + 2 * wb)),
            pl.BlockSpec((lr, gw), rmap(cb + 3 * wb)),
            pl.BlockSpec((lr, LANES), rmap(0)),
            pl.BlockSpec((CONV_W, gw), lambda s, g, c: (0, g)),
            pl.BlockSpec((CONV_W, gw), lambda s, g, c: (0, wb + g)),
            pl.BlockSpec((CONV_W, gw), lambda s, g, c: (0, 2 * wb + g)),
            pl.BlockSpec((1, 2, LANES), lambda s, g, c: (g, 0, 0)),
            pl.BlockSpec((1, HEAD_DIM_B), lambda s, g, c: (0, 0)),
            pl.BlockSpec((1, hg, HEAD_DIM_B, HEAD_DIM_B), lambda s, g, c: (s, g, 0, 0)),
            pl.BlockSpec((1, SUBLANES, gw), lambda s, g, c: (s, 0, g)),
            pl.BlockSpec((1, SUBLANES, gw), lambda s, g, c: (s, 0, wb + g)),
            pl.BlockSpec((1, SUBLANES, gw), lambda s, g, c: (s, 0, 2 * wb + g)),
        ],
        out_specs=[
            pl.BlockSpec((lr, gw), lambda s, g, c: (s * n_chunks + c, g)),
            pl.BlockSpec((1, hg, HEAD_DIM_B, HEAD_DIM_B), lambda s, g, c: (s, g, 0, 0)),
        ],
        out_shape=[
            jax.ShapeDtypeStruct((n_seq * n_chunks * lr, w_b), BF16),
            jax.ShapeDtypeStruct((n_seq, n_heads, HEAD_DIM_B, HEAD_DIM_B), F32),
        ],
        scratch_shapes=[
            pltpu.VMEM((hg, HEAD_DIM_B, HEAD_DIM_B), F32),
            pltpu.VMEM((SUBLANES + GDN_CHUNK, gw), F32),
            pltpu.VMEM((SUBLANES + GDN_CHUNK, gw), F32),
            pltpu.VMEM((SUBLANES + GDN_CHUNK, gw), F32),
        ],
        compiler_params=pltpu.CompilerParams(
            dimension_semantics=("arbitrary", "arbitrary", "arbitrary"), vmem_limit_bytes=VMEM_LIMIT),
        name=f"gdn_l{lr}",
    )(o32, o32, o32, o32, ba, conv_w, conv_w, conv_w, gate_par, norm_b, s0, halo0, halo0, halo0)


def _out_proj_kernel(ya_ref, yb_ref, w_ref, h_ref, pn_ref, o_ref, *, w_a):
    j = pl.program_id(1)
    tn = w_ref.shape[1]
    r = (jnp.dot(ya_ref[...], w_ref[0:w_a, :], preferred_element_type=F32)
         + jnp.dot(yb_ref[...], w_ref[w_a:, :], preferred_element_type=F32))
    o_ref[:, pl.ds(pl.multiple_of(j * tn, tn), tn)] = r

    @pl.when(j == pl.num_programs(1) - 1)
    def _():
        rows = 64

        def body(i, carry):
            sl = pl.ds(pl.multiple_of(i * rows, rows), rows)
            y = o_ref[sl, :]
            ms = jnp.mean(y * y, axis=-1, keepdims=True)
            o_ref[sl, :] = h_ref[sl, :] + y * lax.rsqrt(ms + EPS) * pn_ref[...]
            return carry
        lax.fori_loop(0, o_ref.shape[0] // rows, body, 0)


def _out_proj(ya, yb, w_out, h, post_norm):
    m, d = h.shape
    w_a = ya.shape[1]
    tm, tn = min(OUT_TM, m), OUT_TN
    kern = functools.partial(_out_proj_kernel, w_a=w_a)
    return pl.pallas_call(
        kern,
        grid=(m // tm, d // tn),
        in_specs=[
            pl.BlockSpec((tm, w_a), lambda i, j: (i, 0)),
            pl.BlockSpec((tm, yb.shape[1]), lambda i, j: (i, 0)),
            pl.BlockSpec((d, tn), lambda i, j: (0, j)),
            pl.BlockSpec((tm, d), lambda i, j: (i, 0)),
            pl.BlockSpec((1, d), lambda i, j: (0, 0)),
        ],
        out_specs=pl.BlockSpec((tm, d), lambda i, j: (i, 0)),
        out_shape=jax.ShapeDtypeStruct((m, d), F32),
        compiler_params=pltpu.CompilerParams(
            dimension_semantics=("arbitrary", "arbitrary"), vmem_limit_bytes=VMEM_LIMIT),
        name="out_proj",
    )(ya, yb, w_out, h, post_norm)


def _kv_layout_kernel(km_ref, vm_ref, k_ref, v_ref, ko_ref, vo_ref, kt_ref, vt_ref, *, n_heads):
    tm = k_ref.shape[0]
    n_lead = kt_ref.shape[0]
    rpt = 2 * n_heads

    @pl.when(pl.program_id(0) == 0)
    def _():
        kt_ref[...] = km_ref[...]
        vt_ref[...] = vm_ref[...]

    for p in range(rpt):
        piece = slice(p * LANES, (p + 1) * LANES)
        vrow = (p % 2) * n_heads + p // 2
        ko_ref[pl.ds(p, n_lead, stride=rpt), :] = kt_ref[:, piece]
        ko_ref[pl.ds(n_lead * rpt + p, tm - n_lead, stride=rpt), :] = k_ref[0:tm - n_lead, piece]
        vo_ref[pl.ds(vrow, n_lead, stride=rpt), :] = vt_ref[:, piece]
        vo_ref[pl.ds(n_lead * rpt + vrow, tm - n_lead, stride=rpt), :] = v_ref[0:tm - n_lead, piece]
    kt_ref[...] = k_ref[tm - n_lead:tm, :]
    vt_ref[...] = v_ref[tm - n_lead:tm, :]


def _kv_layout(k_lead, v_lead, p32, n_heads, w_a):
    seq = p32.shape[0]
    n_lead = k_lead.shape[0]
    tm = KV_TM
    rpt = 2 * n_heads
    n_blk = seq // tm
    spec_out = pl.BlockSpec((tm * rpt, LANES), lambda i: (i, 0))
    shape_out = jax.ShapeDtypeStruct(((n_lead + seq) * rpt, LANES), F32)
    return pl.pallas_call(
        functools.partial(_kv_layout_kernel, n_heads=n_heads),
        grid=(n_blk + 1,),
        in_specs=[
            pl.BlockSpec((n_lead, w_a), lambda i: (0, 0)),
            pl.BlockSpec((n_lead, w_a), lambda i: (0, 0)),
            pl.BlockSpec((tm, w_a), lambda i: (jnp.minimum(i, n_blk - 1), 0)),
            pl.BlockSpec((tm, w_a), lambda i: (jnp.minimum(i, n_blk - 1), 1)),
        ],
        out_specs=[spec_out, spec_out],
        out_shape=[shape_out, shape_out],
        scratch_shapes=[pltpu.VMEM((n_lead, w_a), F32), pltpu.VMEM((n_lead, w_a), F32)],
        compiler_params=pltpu.CompilerParams(dimension_semantics=("arbitrary",), vmem_limit_bytes=VMEM_LIMIT),
        name="kv_layout",
    )(k_lead, v_lead, p32, p32)


def kernel(x_prompt, x_sample, cache_k_a, cache_v_a, state_ssm_b, state_conv_b, meta_tokens, rel_bias, pre_norm,
           w_in, lambda_q1, lambda_k1, lambda_q2, lambda_k2, subln_a, conv_b, a_log_b, dt_bias_b, norm_b, w_out,
           post_norm):
    batch, seq, d_model = x_prompt.shape
    dec_b, dec_seq, _ = x_sample.shape
    depth = w_in.shape[0]
    assert batch == 1 and depth == 1
    n_heads_a = rel_bias.shape[1]
    w_a = n_heads_a * 2 * HEAD_DIM_A
    w_b = d_model - w_a
    n_heads_b = w_b // HEAD_DIM_B
    n_cache = cache_k_a.shape[2]
    past = n_cache - N_META
    n_ca = (n_cache // LANES) * LANES
    n_main = 4 * w_a + 4 * w_b
    lam_init = _lambda_init(0)
    hg = min(GDN_HEADS_PER_STEP, n_heads_b)
    n_groups = n_heads_b // hg
    n_dec = dec_b * dec_seq
    assert seq % PROJ_TM == 0 and seq % GDN_CHUNK == 0 and n_cache - n_ca + dec_seq <= LANES

    w_in_bf = w_in[0].astype(BF16)
    wb = w_in_bf[:, n_main:n_main + n_heads_b].reshape(d_model, n_groups, hg)
    wa = w_in_bf[:, n_main + n_heads_b:].reshape(d_model, n_groups, hg)
    w_tail = jnp.concatenate([wb, wa, jnp.zeros((d_model, n_groups, LANES - 2 * hg), BF16)], axis=-1)
    w_tail = w_tail.reshape(d_model, n_groups * LANES)
    w_out_bf = w_out[0].astype(BF16)
    gate_par = jnp.zeros((n_groups, 2, LANES), F32)
    gate_par = gate_par.at[:, 0, hg:2 * hg].set(a_log_b[0].reshape(n_groups, hg))
    gate_par = gate_par.at[:, 1, hg:2 * hg].set(dt_bias_b[0].reshape(n_groups, hg))
    lam_params = jnp.stack([lambda_q1[0], lambda_k1[0], lambda_q2[0], lambda_k2[0]])

    xp = x_prompt[0]
    n_small = -(-(n_dec + N_META) // 16) * 16
    xs = jnp.concatenate([x_sample.reshape(n_dec, d_model), meta_tokens.astype(F32),
                          jnp.zeros((n_small - n_dec - N_META, d_model), F32)], axis=0)
    p32, p16, pba = _in_proj(xp, pre_norm, w_in_bf, w_tail, w_a, n_main)
    s32, s16, sba = _in_proj(xs, pre_norm, w_in_bf, w_tail, w_a, n_main, single_tile=True)

    fb, mb, sa, se, lam_t = _bias_tables(rel_bias, lam_params, past, dec_seq, n_ca, lam_init)
    lam = lam_t[0, 0, 0:1]

    meta16 = s16[n_dec:n_dec + N_META]
    pad_m = jnp.zeros((LANES - N_META, w_a), BF16)
    km = jnp.concatenate([meta16[:, w_a:2 * w_a], pad_m], axis=0)
    vm = jnp.concatenate([meta16[:, 2 * w_a:], pad_m], axis=0)
    ya_p = _attn_prompt(lam, p16, p32, km, vm, fb, mb, subln_a, n_heads_a, lam_init)

    ck = cache_k_a[0].reshape(dec_b, n_cache * n_heads_a * 2, HEAD_DIM_A)
    cv = cache_v_a[0].reshape(dec_b, n_cache, n_heads_a, 2, HEAD_DIM_A).transpose(0, 1, 3, 2, 4)
    cv = cv.reshape(dec_b, n_cache * 2 * n_heads_a, HEAD_DIM_A)
    n_extra = n_cache - n_ca + dec_seq
    pad_e = jnp.zeros((dec_b, LANES - n_extra, w_a), BF16)
    ck_tail = cache_k_a[0, :, n_ca:].reshape(dec_b, n_cache - n_ca, w_a).astype(BF16)
    cv_tail = cache_v_a[0, :, n_ca:].reshape(dec_b, n_cache - n_ca, w_a).astype(BF16)
    ke = jnp.concatenate([ck_tail, s16[:n_dec, w_a:2 * w_a].reshape(dec_b, dec_seq, w_a), pad_e], axis=1)
    ve = jnp.concatenate([cv_tail, s16[:n_dec, 2 * w_a:].reshape(dec_b, dec_seq, w_a), pad_e], axis=1)
    ya_s = _attn_sample(lam, s16, s32, ck, cv, ke, ve, sa, se, subln_a, n_heads_a, dec_b, dec_seq, n_ca,
                        lam_init)

    col0 = 3 * w_a
    qkv_cols = slice(col0, col0 + 3 * w_b)
    zero_halo = jnp.zeros((1, SUBLANES, 3 * w_b), F32)
    gdn = functools.partial(_gdn, conv_w=conv_b[0], gate_par=gate_par, norm_b=norm_b, w_b=w_b, col0=col0)
    _, s_meta = gdn(s32, sba, s0=jnp.zeros((1, n_heads_b, HEAD_DIM_B, HEAD_DIM_B), F32), halo0=zero_halo,
                    n_seq=1, n_chunks=1, lr=N_META, row_blk_off=n_dec // N_META)
    meta_halo = jnp.concatenate([jnp.zeros((SUBLANES - (CONV_W - 1), 3 * w_b), F32),
                                 s32[n_dec + N_META - (CONV_W - 1):n_dec + N_META, qkv_cols]], axis=0)[None]
    yb_p, ssm_p = gdn(p32, pba, s0=s_meta, halo0=meta_halo, n_seq=1, n_chunks=seq // GDN_CHUNK, lr=GDN_CHUNK,
                      row_blk_off=0)
    samp_halo = jnp.concatenate([jnp.zeros((dec_b, SUBLANES - (CONV_W - 1), 3 * w_b), F32), state_conv_b[0]], axis=1)
    yb_s, ssm_s = gdn(s32, sba, s0=state_ssm_b[0].astype(F32), halo0=samp_halo, n_seq=dec_b, n_chunks=1,
                      lr=dec_seq, row_blk_off=0)

    y_p = _out_proj(ya_p, yb_p, w_out_bf, xp, post_norm)
    y_s = _out_proj(ya_s, yb_s, w_out_bf, x_sample.reshape(n_dec, d_model), post_norm)

    hd = HEAD_DIM_A
    meta32 = s32[n_dec:n_dec + N_META]
    k_p, v_p = _kv_layout(meta32[:, :w_a], meta32[:, w_a:2 * w_a], p32, n_heads_a, w_a)
    v_p = v_p.reshape(N_META + seq, 2, n_heads_a, hd).transpose(0, 2, 1, 3)
    conv_p = p32[seq - (CONV_W - 1):, qkv_cols]
    k_s = s32[:n_dec, :w_a]
    v_s = s32[:n_dec, w_a:2 * w_a]
    conv_s = s32[:n_dec, qkv_cols].reshape(dec_b, dec_seq, 3 * w_b)[:, dec_seq - (CONV_W - 1):]
    return (
        y_p[None],
        y_s.reshape(dec_b, dec_seq, d_model),
        k_p.reshape(1, 1, N_META + seq, n_heads_a, 2, hd),
        v_p.reshape(1, 1, N_META + seq, n_heads_a, 2 * hd),
        ssm_p[None],
        conv_p[None, None],
        k_s.reshape(1, dec_b, dec_seq, n_heads_a, 2, hd),
        v_s.reshape(1, dec_b, dec_seq, n_heads_a, 2 * hd),
        ssm_s[None],
        conv_s[None],
    )
```

```python
import functools
import math

import jax
import jax.numpy as jnp
from jax import lax
from jax.experimental import pallas as pl
from jax.experimental.pallas import tpu as pltpu

F32 = jnp.float32
BF16 = jnp.bfloat16
HIGHEST = lax.Precision.HIGHEST

EPS = 1e-6
CHUNK = 64
N_META = 16
HEAD_DIM_A = 128
HEAD_DIM_B = 128
CONV_W = 4
N_BUCKETS = 32
MAX_DISTANCE = 1024
NEG = -1e30
LOG2E = math.log2(math.e)

LANES = 128
SUBLANES = 8
V7X_VMEM_BYTES = 64 * 1024 * 1024
VMEM_LIMIT = V7X_VMEM_BYTES - 8 * 1024 * 1024

ATT_BLOCK = 256
FAR_BLOCKS = 8
SAMPLE_HEADS_PER_STEP = 8
GDN_CHUNK = 128
GDN_HEADS_PER_STEP = 16
PROJ_TM = 512
PROJ_TN = 1024
PROJ_TN_SINGLE = 512
KV_TM = 512
OUT_TM = 512
OUT_TN = 512


def _lambda_init(layer):
    return 0.8 - 0.6 * math.exp(-0.3 * layer)


def _bias_saturation_distance():
    nb = N_BUCKETS // 2
    max_exact = nb // 2
    return int(math.ceil(max_exact * (MAX_DISTANCE / max_exact) ** ((nb - max_exact - 1) / (nb - max_exact)))) + 1


N_NEAR = -(-(_bias_saturation_distance() + ATT_BLOCK - 1) // ATT_BLOCK)
FAR_BUCKET = N_BUCKETS // 2 - 1


def _silu(x):
    return x * jax.nn.sigmoid(x)


def _in_proj_kernel(x_ref, pn_ref, w_ref, wt_ref, o32_ref, o16_ref, ba_ref, xn_ref, *, q_tiles, bf_tiles,
                    q_scale):
    j = pl.program_id(1)
    tm = x_ref.shape[0]
    rows = max(r for r in range(SUBLANES, 81, SUBLANES) if tm % r == 0)

    @pl.when(j == 0)
    def _():
        def body(r, carry):
            sl = pl.ds(pl.multiple_of(r * rows, 8), rows)
            x = x_ref[sl, :]
            ms = jnp.mean(x * x, axis=-1, keepdims=True)
            xn_ref[sl, :] = (x * lax.rsqrt(ms + EPS) * pn_ref[...]).astype(BF16)
            return carry
        lax.fori_loop(0, tm // rows, body, 0)
        ba_ref[...] = jnp.dot(xn_ref[...], wt_ref[...], preferred_element_type=F32)

    r = jnp.dot(xn_ref[...], w_ref[...], preferred_element_type=F32)

    @pl.when(j >= q_tiles)
    def _():
        o32_ref[...] = r

    @pl.when(j < q_tiles)
    def _():
        o16_ref[...] = (r * q_scale).astype(BF16)

    @pl.when(jnp.logical_and(j >= q_tiles, j < bf_tiles))
    def _():
        o16_ref[...] = r.astype(BF16)


def _in_proj(x, pre_norm, w_all, w_tail, w_a, n, single_tile=False):
    m, d = x.shape
    tm, tn = PROJ_TM, min(PROJ_TN, w_a)
    x_mode = {}
    if single_tile:
        tm, tn = m, min(PROJ_TN_SINGLE, w_a)
        x_mode = dict(pipeline_mode=pl.Buffered(1))
    q_tiles = w_a // tn
    bf_tiles = 3 * w_a // tn
    nt = w_tail.shape[1]
    kern = functools.partial(_in_proj_kernel, q_tiles=q_tiles, bf_tiles=bf_tiles, q_scale=HEAD_DIM_A ** -0.5 * LOG2E)
    return pl.pallas_call(
        kern,
        grid=(m // tm, n // tn),
        in_specs=[
            pl.BlockSpec((tm, d), lambda i, j: (i, 0), **x_mode),
            pl.BlockSpec((1, d), lambda i, j: (0, 0)),
            pl.BlockSpec((d, tn), lambda i, j: (0, j)),
            pl.BlockSpec((d, nt), lambda i, j: (0, 0)),
        ],
        out_specs=[
            pl.BlockSpec((tm, tn), lambda i, j: (i, jnp.maximum(j - q_tiles, 0))),
            pl.BlockSpec((tm, tn), lambda i, j: (i, jnp.minimum(j, bf_tiles - 1))),
            pl.BlockSpec((tm, nt), lambda i, j: (i, 0)),
        ],
        out_shape=[
            jax.ShapeDtypeStruct((m, n - w_a), F32),
            jax.ShapeDtypeStruct((m, 3 * w_a), BF16),
            jax.ShapeDtypeStruct((m, nt), F32),
        ],
        scratch_shapes=[pltpu.VMEM((tm, d), BF16)],
        compiler_params=pltpu.CompilerParams(
            dimension_semantics=("arbitrary", "arbitrary"), vmem_limit_bytes=VMEM_LIMIT),
        name="in_proj",
    )(x, pre_norm, w_all, w_tail)


def _bias_values(rb_ref, h, rel):
    nb = N_BUCKETS // 2
    max_exact = nb // 2
    n = jnp.abs(rel)
    nf = jnp.maximum(n, 1).astype(F32)
    large = max_exact + (jnp.log(nf / max_exact) / math.log(MAX_DISTANCE / max_exact)
                         * (nb - max_exact)).astype(jnp.int32)
    large = jnp.minimum(large, nb - 1)
    bucket = jnp.where(rel > 0, nb, 0) + jnp.where(n < max_exact, n, large)
    val = jnp.zeros(rel.shape, F32)
    for b in range(N_BUCKETS):
        val = jnp.where(bucket == b, rb_ref[b, h], val)
    return (val - rb_ref[FAR_BUCKET, h]) * LOG2E


def _bias_kernel(rb_ref, lamp_ref, fb_ref, mb_ref, sa_ref, se_ref, lam_ref, *, past, dec_seq, n_ca, lam_init):
    h = pl.program_id(0)
    t = ATT_BLOCK
    shift = CHUNK.bit_length() - 1
    i = lax.broadcasted_iota(jnp.int32, (t, t), 0)
    j = lax.broadcasted_iota(jnp.int32, (t, t), 1)
    for dd in range(N_NEAR):
        val = _bias_values(rb_ref, h, (j - i) - t * dd)
        if dd == 0:
            val = jnp.where((j >> shift) <= (i >> shift), val, NEG)
        fb_ref[0, dd] = val
    i = lax.broadcasted_iota(jnp.int32, (t, LANES), 0)
    j = lax.broadcasted_iota(jnp.int32, (t, LANES), 1)
    for qb in range(N_NEAR):
        val = _bias_values(rb_ref, h, (j - N_META) - (t * qb + i))
        mb_ref[0, qb] = jnp.where(j < N_META, val, NEG)
    i = lax.broadcasted_iota(jnp.int32, (dec_seq, n_ca), 0)
    j = lax.broadcasted_iota(jnp.int32, (dec_seq, n_ca), 1)
    sa_ref[0] = _bias_values(rb_ref, h, (j - N_META) - (past + i))
    i = lax.broadcasted_iota(jnp.int32, (dec_seq, LANES), 0)
    j = lax.broadcasted_iota(jnp.int32, (dec_seq, LANES), 1)
    n_extra = N_META + past - n_ca + dec_seq
    val = _bias_values(rb_ref, h, (n_ca - N_META + j) - (past + i))
    se_ref[0] = jnp.where(j < n_extra, val, NEG)
    lp = lamp_ref[...]
    s1 = jnp.sum(lp[0:1] * lp[1:2], axis=-1, keepdims=True)
    s2 = jnp.sum(lp[2:3] * lp[3:4], axis=-1, keepdims=True)
    lam_ref[0] = jnp.broadcast_to(jnp.exp(s1) - jnp.exp(s2) + lam_init, (SUBLANES, LANES))


def _bias_tables(rel_bias, lam_params, past, dec_seq, n_ca, lam_init):
    nh = rel_bias.shape[1]
    t = ATT_BLOCK
    kern = functools.partial(_bias_kernel, past=past, dec_seq=dec_seq, n_ca=n_ca, lam_init=lam_init)
    return pl.pallas_call(
        kern,
        grid=(nh,),
        in_specs=[
            pl.BlockSpec(memory_space=pltpu.SMEM),
            pl.BlockSpec((4, HEAD_DIM_A), lambda h: (0, 0)),
        ],
        out_specs=[
            pl.BlockSpec((1, N_NEAR, t, t), lambda h: (h, 0, 0, 0)),
            pl.BlockSpec((1, N_NEAR, t, LANES), lambda h: (h, 0, 0, 0)),
            pl.BlockSpec((1, dec_seq, n_ca), lambda h: (h, 0, 0)),
            pl.BlockSpec((1, dec_seq, LANES), lambda h: (h, 0, 0)),
            pl.BlockSpec((1, SUBLANES, LANES), lambda h: (h, 0, 0)),
        ],
        out_shape=[
            jax.ShapeDtypeStruct((nh, N_NEAR, t, t), F32),
            jax.ShapeDtypeStruct((nh, N_NEAR, t, LANES), F32),
            jax.ShapeDtypeStruct((nh, dec_seq, n_ca), F32),
            jax.ShapeDtypeStruct((nh, dec_seq, LANES), F32),
            jax.ShapeDtypeStruct((nh, SUBLANES, LANES), F32),
        ],
        compiler_params=pltpu.CompilerParams(dimension_semantics=("arbitrary",)),
        name="bias_tables",
    )(rel_bias, lam_params)


def _nt_dot(a, b):
    return lax.dot_general(a, b, (((1,), (1,)), ((), ())), preferred_element_type=F32)


def _attn_finish(o, z, subln, lam_init):
    ms = jnp.mean(o * o, axis=-1, keepdims=True)
    on = o * lax.rsqrt(ms + EPS) * subln * (1.0 - lam_init)
    return on * _silu(z)


def _attn_prompt_kernel(lam_ref, q_ref, k_ref, v_ref, km_ref, vm_ref, fb_ref, mb_ref, z_ref, sub_ref, o_ref,
                        m_ref, l_ref, acc_ref, sa_ref, sb_ref, *, lam_init):
    qi = pl.program_id(1)
    t = ATT_BLOCK
    d = HEAD_DIM_A
    hw = 2 * d

    def rep(x, n):
        return jnp.concatenate([x] * n, axis=1) if n > 1 else x

    def lane_fold(p):
        out = p[:, 0:LANES]
        for g in range(1, p.shape[1] // LANES):
            out = out + p[:, g * LANES:(g + 1) * LANES]
        return out

    def meta_init():
        vm = vm_ref[...]
        mb = mb_ref[0, jnp.minimum(qi, N_NEAR - 1)]
        for c in range(2):
            s = _nt_dot(q_ref[:, c * d:(c + 1) * d], km_ref[:, c * d:(c + 1) * d]) + mb
            m = jnp.broadcast_to(jnp.max(s, axis=-1, keepdims=True), (t, LANES))
            p = jnp.exp2(s - m)
            m_ref[c] = m
            l_ref[c] = p
            acc_ref[c] = jnp.dot(p.astype(BF16), vm, preferred_element_type=F32)

    def qk(start, width):
        rows = pl.ds(start, width)
        return [_nt_dot(q_ref[:, c * d:(c + 1) * d], k_ref[rows, c * d:(c + 1) * d]) for c in range(2)]

    def softmax_pv(s_both, start, width, bias):
        v = v_ref[pl.ds(start, width), :]
        for c in range(2):
            s = s_both[c]
            if bias is not None:
                s = s + bias
            m_old = m_ref[c]
            m_new = jnp.maximum(m_old, jnp.max(s, axis=-1, keepdims=True))
            p = jnp.exp2(s - rep(m_new, width // LANES))
            alpha = jnp.exp2(m_old - m_new)
            m_ref[c] = m_new
            l_ref[c] = alpha * l_ref[c] + lane_fold(p)
            acc_ref[c] = rep(alpha, hw // LANES) * acc_ref[c] + jnp.dot(
                p.astype(BF16), v, preferred_element_type=F32)

    wide = FAR_BLOCKS * t
    far_shift = FAR_BLOCKS.bit_length() - 1
    n_far = jnp.maximum(qi - (N_NEAR - 1), 0)
    n_wide = n_far >> far_shift

    def put(ref, s_both):
        ref[0] = s_both[0]
        ref[1] = s_both[1]

    @pl.when(n_wide > 0)
    def _():
        put(sa_ref, qk(0, wide))
        meta_init()

    @pl.when(n_wide == 0)
    def _():
        meta_init()

    def far_body(i, carry):
        b0 = pl.multiple_of(i * (2 * wide), wide)
        b1 = pl.multiple_of(b0 + wide, wide)
        b2 = pl.multiple_of(jnp.minimum(b0 + 2 * wide, (n_wide - 1) * wide), wide)
        put(sb_ref, qk(b1, wide))
        softmax_pv([sa_ref[0], sa_ref[1]], b0, wide, None)
        put(sa_ref, qk(b2, wide))
        softmax_pv([sb_ref[0], sb_ref[1]], b1, wide, None)
        return carry

    def far_body2(i, carry):
        far_body(2 * i, carry)
        return far_body(2 * i + 1, carry)
    lax.fori_loop(0, n_wide >> 2, far_body2, 0)
    lax.fori_loop((n_wide >> 2) * 2, n_wide >> 1, far_body, 0)

    @pl.when((n_wide & 1) == 1)
    def _():
        softmax_pv([sa_ref[0], sa_ref[1]], pl.multiple_of((n_wide - 1) * wide, wide), wide, None)

    n_rest = n_far - (n_wide << far_shift)
    for cnt in range(FAR_BLOCKS):
        @pl.when(jnp.logical_and(qi >= N_NEAR - 1, n_rest == cnt))
        def _(cnt=cnt):
            first = qi - (N_NEAR - 1) - cnt
            n_tail = cnt + N_NEAR
            s_next = qk(pl.multiple_of(first * t, t), t)
            for b in range(n_tail):
                s_cur = s_next
                if b + 1 < n_tail:
                    s_next = qk(pl.multiple_of((first + b + 1) * t, t), t)
                bias = None if b < cnt else fb_ref[0, n_tail - 1 - b]
                softmax_pv(s_cur, pl.multiple_of((first + b) * t, t), t, bias)

    @pl.when(qi < N_NEAR - 1)
    def _():
        for dd in range(N_NEAR - 2, -1, -1):
            @pl.when(qi - dd >= 0)
            def _(dd=dd):
                start = pl.multiple_of((qi - dd) * t, t)
                softmax_pv(qk(start, t), start, t, fb_ref[0, dd])

    inv0 = 1.0 / jnp.sum(l_ref[0], axis=-1, keepdims=True)
    inv1 = lam_ref[0] / jnp.sum(l_ref[1], axis=-1, keepdims=True)
    o = acc_ref[0] * inv0 - acc_ref[1] * inv1
    o_ref[...] = _attn_finish(o, z_ref[...], sub_ref[...], lam_init).astype(BF16)


def _attn_prompt(lam, qkv16, o32, km, vm, fb, mb, subln, n_heads, lam_init):
    seq = qkv16.shape[0]
    t = ATT_BLOCK
    hw = 2 * HEAD_DIM_A
    kern = functools.partial(_attn_prompt_kernel, lam_init=lam_init)
    return pl.pallas_call(
        kern,
        grid=(n_heads, seq // t),
        in_specs=[
            pl.BlockSpec(memory_space=pltpu.SMEM),
            pl.BlockSpec((t, hw), lambda h, i: (i, h)),
            pl.BlockSpec((seq, hw), lambda h, i: (0, n_heads + h)),
            pl.BlockSpec((seq, hw), lambda h, i: (0, 2 * n_heads + h)),
            pl.BlockSpec((LANES, hw), lambda h, i: (0, h)),
            pl.BlockSpec((LANES, hw), lambda h, i: (0, h)),
            pl.BlockSpec((1, N_NEAR, t, t), lambda h, i: (h, 0, 0, 0)),
            pl.BlockSpec((1, N_NEAR, t, LANES), lambda h, i: (h, 0, 0, 0)),
            pl.BlockSpec((t, hw), lambda h, i: (i, 2 * n_heads + h)),
            pl.BlockSpec((1, hw), lambda h, i: (0, 0)),
        ],
        out_specs=pl.BlockSpec((t, hw), lambda h, i: (i, h)),
        out_shape=jax.ShapeDtypeStruct((seq, n_heads * hw), BF16),
        scratch_shapes=[
            pltpu.VMEM((2, t, LANES), F32),
            pltpu.VMEM((2, t, LANES), F32),
            pltpu.VMEM((2, t, hw), F32),
            pltpu.VMEM((2, t, FAR_BLOCKS * t), F32),
            pltpu.VMEM((2, t, FAR_BLOCKS * t), F32),
        ],
        compiler_params=pltpu.CompilerParams(
            dimension_semantics=("arbitrary", "arbitrary"), vmem_limit_bytes=VMEM_LIMIT),
        name="attn_prompt",
    )(lam, qkv16, qkv16, qkv16, km, vm, fb, mb, o32, subln)


def _attn_sample_kernel(lam_ref, q_ref, kc_ref, vc_ref, ke_ref, ve_ref, sa_ref, se_ref, z_ref, sub_ref, o_ref,
                        *, lam_init, n_heads, n_ca, hs):
    d = HEAD_DIM_A
    hw = 2 * d
    for i in range(hs):
        h = pl.program_id(1) * hs + i
        cols = slice(i * hw, (i + 1) * hw)
        vc = jnp.concatenate([vc_ref[0, pl.ds(c * n_heads + h, n_ca, stride=2 * n_heads), :] for c in range(2)],
                             axis=1)
        ve = ve_ref[0, :, cols]
        outs = []
        for c in range(2):
            q = q_ref[:, i * hw + c * d:i * hw + (c + 1) * d]
            kc = kc_ref[0, pl.ds(2 * h + c, n_ca, stride=2 * n_heads), :]
            s_a = _nt_dot(q.astype(F32), kc) + sa_ref[i]
            s_e = _nt_dot(q, ke_ref[0, :, i * hw + c * d:i * hw + (c + 1) * d]) + se_ref[i]
            m = jnp.maximum(jnp.max(s_a, axis=-1, keepdims=True), jnp.max(s_e, axis=-1, keepdims=True))
            p_a = jnp.exp2(s_a - m)
            p_e = jnp.exp2(s_e - m)
            l = jnp.sum(p_a, axis=-1, keepdims=True) + jnp.sum(p_e, axis=-1, keepdims=True)
            acc = (jnp.dot(p_a, vc, preferred_element_type=F32)
                   + jnp.dot(p_e.astype(BF16), ve, preferred_element_type=F32))
            outs.append(acc / l)
        o = outs[0] - lam_ref[0] * outs[1]
        o_ref[:, cols] = _attn_finish(o, z_ref[:, cols], sub_ref[...], lam_init).astype(BF16)


def _attn_sample(lam, qkv16, o32, cache_k, cache_v, ke, ve, sa, se, subln, n_heads, dec_b, dec_seq, n_ca,
                 lam_init):
    hw = 2 * HEAD_DIM_A
    hs = min(SAMPLE_HEADS_PER_STEP, n_heads)
    n_grp = n_heads // hs
    kern = functools.partial(_attn_sample_kernel, lam_init=lam_init, n_heads=n_heads, n_ca=n_ca, hs=hs)
    return pl.pallas_call(
        kern,
        grid=(dec_b, n_grp),
        in_specs=[
            pl.BlockSpec(memory_space=pltpu.SMEM),
            pl.BlockSpec((dec_seq, hs * hw), lambda b, g: (b, g)),
            pl.BlockSpec((1, n_ca * n_heads * 2, HEAD_DIM_A), lambda b, g: (b, 0, 0)),
            pl.BlockSpec((1, n_ca * n_heads * 2, HEAD_DIM_A), lambda b, g: (b, 0, 0)),
            pl.BlockSpec((1, LANES, hs * hw), lambda b, g: (b, 0, g)),
            pl.BlockSpec((1, LANES, hs * hw), lambda b, g: (b, 0, g)),
            pl.BlockSpec((hs, dec_seq, n_ca), lambda b, g: (g, 0, 0)),
            pl.BlockSpec((hs, dec_seq, LANES), lambda b, g: (g, 0, 0)),
            pl.BlockSpec((dec_seq, hs * hw), lambda b, g: (b, 2 * n_grp + g)),
            pl.BlockSpec((1, hw), lambda b, g: (0, 0)),
        ],
        out_specs=pl.BlockSpec((dec_seq, hs * hw), lambda b, g: (b, g)),
        out_shape=jax.ShapeDtypeStruct((dec_b * dec_seq, n_heads * hw), BF16),
        compiler_params=pltpu.CompilerParams(
            dimension_semantics=("arbitrary", "arbitrary"), vmem_limit_bytes=VMEM_LIMIT),
        name="attn_sample",
    )(lam, qkv16, cache_k, cache_v, ke, ve, sa, se, o32, subln)


def _hdot(a, b):
    return jnp.dot(a, b, preferred_element_type=F32, precision=HIGHEST)


def _inv_unit_lower(a_list, row, col):
    sh = 4
    n = a_list[0].shape[0]
    eye = (row == col).astype(F32)
    same = (row >> sh) == (col >> sh)
    dblk = [jnp.where(same, a, 0.0) for a in a_list]
    t = [eye - x for x in dblk]
    pw = [x.astype(BF16) for x in dblk]
    for _ in range(sh - 1):
        pw = [jnp.dot(x, x, preferred_element_type=F32).astype(BF16) for x in pw]
        t = [ti + jnp.dot(ti.astype(BF16), x, preferred_element_type=F32) for ti, x in zip(t, pw)]
    while (1 << sh) < n:
        offm = jnp.logical_and((row >> (sh + 1)) == (col >> (sh + 1)), (row >> sh) != (col >> sh))
        off = [jnp.where(offm, a, 0.0).astype(BF16) for a in a_list]
        tb = [ti.astype(BF16) for ti in t]
        mid = [jnp.dot(o, x, preferred_element_type=F32).astype(BF16) for o, x in zip(off, tb)]
        t = [ti - jnp.dot(x, m, preferred_element_type=F32) for ti, x, m in zip(t, tb, mid)]
        sh += 1
    return t


def _gdn_kernel(q_ref, k_ref, v_ref, z_ref, ba_ref, cwq_ref, cwk_ref, cwv_ref, gp_ref, nb_ref, s0_ref,
                hq_ref, hk_ref, hv_ref, y_ref, sout_ref, s_ref, xq_ref, xk_ref, xv_ref, *, lr, hg):
    c = pl.program_id(2)
    n = GDN_CHUNK
    dh = HEAD_DIM_B
    halo = SUBLANES

    @pl.when(c == 0)
    def _():
        s_ref[...] = s0_ref[0]
        xq_ref[0:halo, :] = hq_ref[0]
        xk_ref[0:halo, :] = hk_ref[0]
        xv_ref[0:halo, :] = hv_ref[0]

    row_w = lax.broadcasted_iota(jnp.int32, (n, hg * dh), 0)

    def conv(x_ref, xbuf, cw_ref):
        xbuf[halo:halo + lr, :] = x_ref[...]
        if lr < n:
            xbuf[halo + lr:halo + n, :] = jnp.zeros((n - lr, hg * dh), F32)
        y = xbuf[halo - 3:halo - 3 + n, :] * cw_ref[0:1, :]
        for i in range(1, CONV_W):
            y = y + xbuf[halo - 3 + i:halo - 3 + i + n, :] * cw_ref[i:i + 1, :]
        y = _silu(y)
        if lr < n:
            y = jnp.where(row_w < lr, y, 0.0)
        xbuf[0:halo, :] = xbuf[lr:lr + halo, :]
        return y

    yq = conv(q_ref, xq_ref, cwq_ref)
    yk = conv(k_ref, xk_ref, cwk_ref)
    yv = conv(v_ref, xv_ref, cwv_ref)

    row = lax.broadcasted_iota(jnp.int32, (n, n), 0)
    col = lax.broadcasted_iota(jnp.int32, (n, n), 1)
    ba = ba_ref[...]
    if lr < n:
        ba = jnp.concatenate([ba, jnp.zeros((n - lr, LANES), F32)], axis=0)
    live = row < lr
    beta = jnp.where(jnp.logical_and(live, col < hg), jax.nn.sigmoid(ba), 0.0)
    gval = -jnp.exp(gp_ref[0, 0:1, :]) * jax.nn.softplus(ba + gp_ref[0, 1:2, :])
    g = jnp.where(jnp.logical_and(live, jnp.logical_and(col >= hg, col < 2 * hg)), gval, 0.0)
    gsum = _hdot((row >= col).astype(F32), g)
    gsum_t = gsum.T
    incl = row >= col
    strict = row > col

    def run(heads):
        cols = {hh: slice(hh * dh, (hh + 1) * dh) for hh in heads}
        gc = {hh: jnp.broadcast_to(gsum[:, hg + hh:hg + hh + 1], (n, n)) for hh in heads}
        bc = {hh: jnp.broadcast_to(beta[:, hh:hh + 1], (n, n)) for hh in heads}
        gam = {hh: jnp.where(incl, jnp.exp(jnp.minimum(gc[hh] - gsum_t[hg + hh:hg + hh + 1, :], 0.0)), 0.0)
               for hh in heads}
        qh = {hh: yq[:, cols[hh]] * (lax.rsqrt(jnp.sum(yq[:, cols[hh]] * yq[:, cols[hh]], axis=-1, keepdims=True)
                                               + EPS) * (dh ** -0.5)) for hh in heads}
        kh = {hh: yk[:, cols[hh]] * lax.rsqrt(jnp.sum(yk[:, cols[hh]] * yk[:, cols[hh]], axis=-1, keepdims=True)
                                              + EPS) for hh in heads}
        qb = {hh: qh[hh].astype(BF16) for hh in heads}
        kb = {hh: kh[hh].astype(BF16) for hh in heads}
        kk = {hh: _nt_dot(kb[hh], kb[hh]) for hh in heads}
        qk = {hh: _nt_dot(qb[hh], kb[hh]) for hh in heads}
        a = [jnp.where(strict, bc[hh] * kk[hh] * gam[hh], 0.0) for hh in heads]
        t = dict(zip(heads, _inv_unit_lower(a, row, col)))
        eg = {hh: jnp.exp(gc[hh]) for hh in heads}
        rhs = {hh: jnp.concatenate([yv[:, cols[hh]] * bc[hh], kh[hh] * (bc[hh] * eg[hh])], axis=1).astype(BF16)
               for hh in heads}
        uw = {hh: jnp.dot(t[hh].astype(BF16), rhs[hh], preferred_element_type=F32) for hh in heads}
        s = {hh: s_ref[hh] for hh in heads}
        sb = {hh: s[hh].astype(BF16) for hh in heads}
        v_new = {hh: uw[hh][:, :dh] - jnp.dot(uw[hh][:, dh:].astype(BF16), sb[hh], preferred_element_type=F32)
                 for hh in heads}
        vnb = {hh: v_new[hh].astype(BF16) for hh in heads}
        o = {hh: eg[hh] * jnp.dot(qb[hh], sb[hh], preferred_element_type=F32)
             + jnp.dot((qk[hh] * gam[hh]).astype(BF16), vnb[hh], preferred_element_type=F32) for hh in heads}
        g_last = {hh: gc[hh][n - 1:n, :] for hh in heads}
        kd = {hh: (kh[hh] * jnp.exp(g_last[hh] - gc[hh])).astype(BF16) for hh in heads}
        for hh in heads:
            s_ref[hh] = s[hh] * jnp.exp(g_last[hh]) + lax.dot_general(
                kd[hh], vnb[hh], (((0,), (0,)), ((), ())), preferred_element_type=F32)
        for hh in heads:
            on = o[hh] * lax.rsqrt(jnp.mean(o[hh] * o[hh], axis=-1, keepdims=True) + EPS) * nb_ref[...]
            y_ref[:, cols[hh]] = (on[0:lr] * _silu(z_ref[:, cols[hh]])).astype(BF16)

    run(list(range(hg)))

    @pl.when(c == pl.num_programs(2) - 1)
    def _():
        sout_ref[0] = s_ref[...]


def _gdn(o32, ba, conv_w, gate_par, norm_b, s0, halo0, *, n_seq, n_chunks, lr, row_blk_off, w_b, col0):
    hg = min(GDN_HEADS_PER_STEP, w_b // HEAD_DIM_B)
    gw = hg * HEAD_DIM_B
    n_groups = w_b // gw
    n_heads = w_b // HEAD_DIM_B
    cb = col0 // gw
    wb = w_b // gw

    def rmap(off):
        return lambda s, g, c: (row_blk_off + s * n_chunks + c, off + g)

    kern = functools.partial(_gdn_kernel, lr=lr, hg=hg)
    return pl.pallas_call(
        kern,
        grid=(n_seq, n_groups, n_chunks),
        in_specs=[
            pl.BlockSpec((lr, gw), rmap(cb)),
            pl.BlockSpec((lr, gw), rmap(cb + wb)),
            pl.BlockSpec((lr, gw), rmap(cb + 2 * wb)),
            pl.BlockSpec((lr, gw), rmap(cb + 3 * wb)),
            pl.BlockSpec((lr, LANES), rmap(0)),
            pl.BlockSpec((CONV_W, gw), lambda s, g, c: (0, g)),
            pl.BlockSpec((CONV_W, gw), lambda s, g, c: (0, wb + g)),
            pl.BlockSpec((CONV_W, gw), lambda s, g, c: (0, 2 * wb + g)),
            pl.BlockSpec((1, 2, LANES), lambda s, g, c: (g, 0, 0)),
            pl.BlockSpec((1, HEAD_DIM_B), lambda s, g, c: (0, 0)),
            pl.BlockSpec((1, hg, HEAD_DIM_B, HEAD_DIM_B), lambda s, g, c: (s, g, 0, 0)),
            pl.BlockSpec((1, SUBLANES, gw), lambda s, g, c: (s, 0, g)),
            pl.BlockSpec((1, SUBLANES, gw), lambda s, g, c: (s, 0, wb + g)),
            pl.BlockSpec((1, SUBLANES, gw), lambda s, g, c: (s, 0, 2 * wb + g)),
        ],
        out_specs=[
            pl.BlockSpec((lr, gw), lambda s, g, c: (s * n_chunks + c, g)),
            pl.BlockSpec((1, hg, HEAD_DIM_B, HEAD_DIM_B), lambda s, g, c: (s, g, 0, 0)),
        ],
        out_shape=[
            jax.ShapeDtypeStruct((n_seq * n_chunks * lr, w_b), BF16),
            jax.ShapeDtypeStruct((n_seq, n_heads, HEAD_DIM_B, HEAD_DIM_B), F32),
        ],
        scratch_shapes=[
            pltpu.VMEM((hg, HEAD_DIM_B, HEAD_DIM_B), F32),
            pltpu.VMEM((SUBLANES + GDN_CHUNK, gw), F32),
            pltpu.VMEM((SUBLANES + GDN_CHUNK, gw), F32),
            pltpu.VMEM((SUBLANES + GDN_CHUNK, gw), F32),
        ],
        compiler_params=pltpu.CompilerParams(
            dimension_semantics=("arbitrary", "arbitrary", "arbitrary"), vmem_limit_bytes=VMEM_LIMIT),
        name=f"gdn_l{lr}",
    )(o32, o32, o32, o32, ba, conv_w, conv_w, conv_w, gate_par, norm_b, s0, halo0, halo0, halo0)


def _out_proj_kernel(ya_ref, yb_ref, w_ref, h_ref, pn_ref, o_ref, *, w_a):
    j = pl.program_id(1)
    tn = w_ref.shape[1]
    r = (jnp.dot(ya_ref[...], w_ref[0:w_a, :], preferred_element_type=F32)
         + jnp.dot(yb_ref[...], w_ref[w_a:, :], preferred_element_type=F32))
    o_ref[:, pl.ds(pl.multiple_of(j * tn, tn), tn)] = r

    @pl.when(j == pl.num_programs(1) - 1)
    def _():
        rows = 64

        def body(i, carry):
            sl = pl.ds(pl.multiple_of(i * rows, rows), rows)
            y = o_ref[sl, :]
            ms = jnp.mean(y * y, axis=-1, keepdims=True)
            o_ref[sl, :] = h_ref[sl, :] + y * lax.rsqrt(ms + EPS) * pn_ref[...]
            return carry
        lax.fori_loop(0, o_ref.shape[0] // rows, body, 0)


def _out_proj(ya, yb, w_out, h, post_norm):
    m, d = h.shape
    w_a = ya.shape[1]
    tm, tn = min(OUT_TM, m), OUT_TN
    kern = functools.partial(_out_proj_kernel, w_a=w_a)
    return pl.pallas_call(
        kern,
        grid=(m // tm, d // tn),
        in_specs=[
            pl.BlockSpec((tm, w_a), lambda i, j: (i, 0)),
            pl.BlockSpec((tm, yb.shape[1]), lambda i, j: (i, 0)),
            pl.BlockSpec((d, tn), lambda i, j: (0, j)),
            pl.BlockSpec((tm, d), lambda i, j: (i, 0)),
            pl.BlockSpec((1, d), lambda i, j: (0, 0)),
        ],
        out_specs=pl.BlockSpec((tm, d), lambda i, j: (i, 0)),
        out_shape=jax.ShapeDtypeStruct((m, d), F32),
        compiler_params=pltpu.CompilerParams(
            dimension_semantics=("arbitrary", "arbitrary"), vmem_limit_bytes=VMEM_LIMIT),
        name="out_proj",
    )(ya, yb, w_out, h, post_norm)


def _kv_layout_kernel(km_ref, vm_ref, k_ref, v_ref, ko_ref, vo_ref, kt_ref, vt_ref, *, n_heads):
    tm = k_ref.shape[0]
    n_lead = kt_ref.shape[0]
    rpt = 2 * n_heads

    @pl.when(pl.program_id(0) == 0)
    def _():
        kt_ref[...] = km_ref[...]
        vt_ref[...] = vm_ref[...]

    for p in range(rpt):
        piece = slice(p * LANES, (p + 1) * LANES)
        vrow = (p % 2) * n_heads + p // 2
        ko_ref[pl.ds(p, n_lead, stride=rpt), :] = kt_ref[:, piece]
        ko_ref[pl.ds(n_lead * rpt + p, tm - n_lead, stride=rpt), :] = k_ref[0:tm - n_lead, piece]
        vo_ref[pl.ds(vrow, n_lead, stride=rpt), :] = vt_ref[:, piece]
        vo_ref[pl.ds(n_lead * rpt + vrow, tm - n_lead, stride=rpt), :] = v_ref[0:tm - n_lead, piece]
    kt_ref[...] = k_ref[tm - n_lead:tm, :]
    vt_ref[...] = v_ref[tm - n_lead:tm, :]


def _kv_layout(k_lead, v_lead, p32, n_heads, w_a):
    seq = p32.shape[0]
    n_lead = k_lead.shape[0]
    tm = KV_TM
    rpt = 2 * n_heads
    n_blk = seq // tm
    spec_out = pl.BlockSpec((tm * rpt, LANES), lambda i: (i, 0))
    shape_out = jax.ShapeDtypeStruct(((n_lead + seq) * rpt, LANES), F32)
    return pl.pallas_call(
        functools.partial(_kv_layout_kernel, n_heads=n_heads),
        grid=(n_blk + 1,),
        in_specs=[
            pl.BlockSpec((n_lead, w_a), lambda i: (0, 0)),
            pl.BlockSpec((n_lead, w_a), lambda i: (0, 0)),
            pl.BlockSpec((tm, w_a), lambda i: (jnp.minimum(i, n_blk - 1), 0)),
            pl.BlockSpec((tm, w_a), lambda i: (jnp.minimum(i, n_blk - 1), 1)),
        ],
        out_specs=[spec_out, spec_out],
        out_shape=[shape_out, shape_out],
        scratch_shapes=[pltpu.VMEM((n_lead, w_a), F32), pltpu.VMEM((n_lead, w_a), F32)],
        compiler_params=pltpu.CompilerParams(dimension_semantics=("arbitrary",), vmem_limit_bytes=VMEM_LIMIT),
        name="kv_layout",
    )(k_lead, v_lead, p32, p32)


def kernel(x_prompt, x_sample, cache_k_a, cache_v_a, state_ssm_b, state_conv_b, meta_tokens, rel_bias, pre_norm,
           w_in, lambda_q1, lambda_k1, lambda_q2, lambda_k2, subln_a, conv_b, a_log_b, dt_bias_b, norm_b, w_out,
           post_norm):
    batch, seq, d_model = x_prompt.shape
    dec_b, dec_seq, _ = x_sample.shape
    depth = w_in.shape[0]
    assert batch == 1 and depth == 1
    n_heads_a = rel_bias.shape[1]
    w_a = n_heads_a * 2 * HEAD_DIM_A
    w_b = d_model - w_a
    n_heads_b = w_b // HEAD_DIM_B
    n_cache = cache_k_a.shape[2]
    past = n_cache - N_META
    n_ca = (n_cache // LANES) * LANES
    n_main = 4 * w_a + 4 * w_b
    lam_init = _lambda_init(0)
    hg = min(GDN_HEADS_PER_STEP, n_heads_b)
    n_groups = n_heads_b // hg
    n_dec = dec_b * dec_seq
    assert seq % PROJ_TM == 0 and seq % GDN_CHUNK == 0 and n_cache - n_ca + dec_seq <= LANES

    w_in_bf = w_in[0].astype(BF16)
    wb = w_in_bf[:, n_main:n_main + n_heads_b].reshape(d_model, n_groups, hg)
    wa = w_in_bf[:, n_main + n_heads_b:].reshape(d_model, n_groups, hg)
    w_tail = jnp.concatenate([wb, wa, jnp.zeros((d_model, n_groups, LANES - 2 * hg), BF16)], axis=-1)
    w_tail = w_tail.reshape(d_model, n_groups * LANES)
    w_out_bf = w_out[0].astype(BF16)
    gate_par = jnp.zeros((n_groups, 2, LANES), F32)
    gate_par = gate_par.at[:, 0, hg:2 * hg].set(a_log_b[0].reshape(n_groups, hg))
    gate_par = gate_par.at[:, 1, hg:2 * hg].set(dt_bias_b[0].reshape(n_groups, hg))
    lam_params = jnp.stack([lambda_q1[0], lambda_k1[0], lambda_q2[0], lambda_k2[0]])

    xp = x_prompt[0]
    n_small = -(-(n_dec + N_META) // 16) * 16
    xs = jnp.concatenate([x_sample.reshape(n_dec, d_model), meta_tokens.astype(F32),
                          jnp.zeros((n_small - n_dec - N_META, d_model), F32)], axis=0)
    p32, p16, pba = _in_proj(xp, pre_norm, w_in_bf, w_tail, w_a, n_main)
    s32, s16, sba = _in_proj(xs, pre_norm, w_in_bf, w_tail, w_a, n_main, single_tile=True)

    fb, mb, sa, se, lam_t = _bias_tables(rel_bias, lam_params, past, dec_seq, n_ca, lam_init)
    lam = lam_t[0, 0, 0:1]

    meta16 = s16[n_dec:n_dec + N_META]
    pad_m = jnp.zeros((LANES - N_META, w_a), BF16)
    km = jnp.concatenate([meta16[:, w_a:2 * w_a], pad_m], axis=0)
    vm = jnp.concatenate([meta16[:, 2 * w_a:], pad_m], axis=0)
    ya_p = _attn_prompt(lam, p16, p32, km, vm, fb, mb, subln_a, n_heads_a, lam_init)

    ck = cache_k_a[0].reshape(dec_b, n_cache * n_heads_a * 2, HEAD_DIM_A)
    cv = cache_v_a[0].reshape(dec_b, n_cache, n_heads_a, 2, HEAD_DIM_A).transpose(0, 1, 3, 2, 4)
    cv = cv.reshape(dec_b, n_cache * 2 * n_heads_a, HEAD_DIM_A)
    n_extra = n_cache - n_ca + dec_seq
    pad_e = jnp.zeros((dec_b, LANES - n_extra, w_a), BF16)
    ck_tail = cache_k_a[0, :, n_ca:].reshape(dec_b, n_cache - n_ca, w_a).astype(BF16)
    cv_tail = cache_v_a[0, :, n_ca:].reshape(dec_b, n_cache - n_ca, w_a).astype(BF16)
    ke = jnp.concatenate([ck_tail, s16[:n_dec, w_a:2 * w_a].reshape(dec_b, dec_seq, w_a), pad_e], axis=1)
    ve = jnp.concatenate([cv_tail, s16[:n_dec, 2 * w_a:].reshape(dec_b, dec_seq, w_a), pad_e], axis=1)
    ya_s = _attn_sample(lam, s16, s32, ck, cv, ke, ve, sa, se, subln_a, n_heads_a, dec_b, dec_seq, n_ca,
                        lam_init)

    col0 = 3 * w_a
    qkv_cols = slice(col0, col0 + 3 * w_b)
    zero_halo = jnp.zeros((1, SUBLANES, 3 * w_b), F32)
    gdn = functools.partial(_gdn, conv_w=conv_b[0], gate_par=gate_par, norm_b=norm_b, w_b=w_b, col0=col0)
    _, s_meta = gdn(s32, sba, s0=jnp.zeros((1, n_heads_b, HEAD_DIM_B, HEAD_DIM_B), F32), halo0=zero_halo,
                    n_seq=1, n_chunks=1, lr=N_META, row_blk_off=n_dec // N_META)
    meta_halo = jnp.concatenate([jnp.zeros((SUBLANES - (CONV_W - 1), 3 * w_b), F32),
                                 s32[n_dec + N_META - (CONV_W - 1):n_dec + N_META, qkv_cols]], axis=0)[None]
    yb_p, ssm_p = gdn(p32, pba, s0=s_meta, halo0=meta_halo, n_seq=1, n_chunks=seq // GDN_CHUNK, lr=GDN_CHUNK,
                      row_blk_off=0)
    samp_halo = jnp.concatenate([jnp.zeros((dec_b, SUBLANES - (CONV_W - 1), 3 * w_b), F32), state_conv_b[0]], axis=1)
    yb_s, ssm_s = gdn(s32, sba, s0=state_ssm_b[0].astype(F32), halo0=samp_halo, n_seq=dec_b, n_chunks=1,
                      lr=dec_seq, row_blk_off=0)

    y_p = _out_proj(ya_p, yb_p, w_out_bf, xp, post_norm)
    y_s = _out_proj(ya_s, yb_s, w_out_bf, x_sample.reshape(n_dec, d_model), post_norm)

    hd = HEAD_DIM_A
    meta32 = s32[n_dec:n_dec + N_META]
    k_p, v_p = _kv_layout(meta32[:, :w_a], meta32[:, w_a:2 * w_a], p32, n_heads_a, w_a)
    v_p = v_p.reshape(N_META + seq, 2, n_heads_a, hd).transpose(0, 2, 1, 3)
    conv_p = p32[seq - (CONV_W - 1):, qkv_cols]
    k_s = s32[:n_dec, :w_a]
    v_s = s32[:n_dec, w_a:2 * w_a]
    conv_s = s32[:n_dec, qkv_cols].reshape(dec_b, dec_seq, 3 * w_b)[:, dec_seq - (CONV_W - 1):]
    return (
        y_p[None],
        y_s.reshape(dec_b, dec_seq, d_model),
        k_p.reshape(1, 1, N_META + seq, n_heads_a, 2, hd),
        v_p.reshape(1, 1, N_META + seq, n_heads_a, 2 * hd),
        ssm_p[None],
        conv_p[None, None],
        k_s.reshape(1, dec_b, dec_seq, n_heads_a, 2, hd),
        v_s.reshape(1, dec_b, dec_seq, n_heads_a, 2 * hd),
        ssm_s[None],
        conv_s[None],
    )
```

```python
import functools
import math

import jax
import jax.numpy as jnp
from jax import lax
from jax.experimental import pallas as pl
from jax.experimental.pallas import tpu as pltpu

F32 = jnp.float32
BF16 = jnp.bfloat16
HIGHEST = lax.Precision.HIGHEST

EPS = 1e-6
CHUNK = 64
N_META = 16
HEAD_DIM_A = 128
HEAD_DIM_B = 128
CONV_W = 4
N_BUCKETS = 32
MAX_DISTANCE = 1024
NEG = -1e30
LOG2E = math.log2(math.e)

LANES = 128
SUBLANES = 8
V7X_VMEM_BYTES = 64 * 1024 * 1024
VMEM_LIMIT = V7X_VMEM_BYTES - 8 * 1024 * 1024

ATT_BLOCK = 256
FAR_BLOCKS = 8
SAMPLE_HEADS_PER_STEP = 8
GDN_CHUNK = 128
GDN_HEADS_PER_STEP = 16
PROJ_TM = 512
PROJ_TN = 1024
PROJ_TN_SINGLE = 512
KV_TM = 512
OUT_TM = 512
OUT_TN = 512


def _lambda_init(layer):
    return 0.8 - 0.6 * math.exp(-0.3 * layer)


def _bias_saturation_distance():
    nb = N_BUCKETS // 2
    max_exact = nb // 2
    return int(math.ceil(max_exact * (MAX_DISTANCE / max_exact) ** ((nb - max_exact - 1) / (nb - max_exact)))) + 1


N_NEAR = -(-(_bias_saturation_distance() + ATT_BLOCK - 1) // ATT_BLOCK)
FAR_BUCKET = N_BUCKETS // 2 - 1


def _silu(x):
    h = 0.5 * x
    return h + h * jnp.tanh(h)


def _in_proj_kernel(x_ref, pn_ref, w_ref, wt_ref, o32_ref, o16_ref, ba_ref, xn_ref, *, q_tiles, bf_tiles,
                    q_scale):
    j = pl.program_id(1)
    tm = x_ref.shape[0]
    rows = max(r for r in range(SUBLANES, 81, SUBLANES) if tm % r == 0)

    @pl.when(j == 0)
    def _():
        def body(r, carry):
            sl = pl.ds(pl.multiple_of(r * rows, 8), rows)
            x = x_ref[sl, :]
            ms = jnp.mean(x * x, axis=-1, keepdims=True)
            xn_ref[sl, :] = (x * lax.rsqrt(ms + EPS) * pn_ref[...]).astype(BF16)
            return carry
        lax.fori_loop(0, tm // rows, body, 0)
        ba_ref[...] = jnp.dot(xn_ref[...], wt_ref[...], preferred_element_type=F32)

    r = jnp.dot(xn_ref[...], w_ref[...], preferred_element_type=F32)

    @pl.when(j >= q_tiles)
    def _():
        o32_ref[...] = r

    @pl.when(j < q_tiles)
    def _():
        o16_ref[...] = (r * q_scale).astype(BF16)

    @pl.when(jnp.logical_and(j >= q_tiles, j < bf_tiles))
    def _():
        o16_ref[...] = r.astype(BF16)


def _in_proj(x, pre_norm, w_all, w_tail, w_a, n, single_tile=False):
    m, d = x.shape
    tm, tn = PROJ_TM, min(PROJ_TN, w_a)
    x_mode = {}
    if single_tile:
        tm, tn = m, min(PROJ_TN_SINGLE, w_a)
        x_mode = dict(pipeline_mode=pl.Buffered(1))
    q_tiles = w_a // tn
    bf_tiles = 3 * w_a // tn
    nt = w_tail.shape[1]
    kern = functools.partial(_in_proj_kernel, q_tiles=q_tiles, bf_tiles=bf_tiles, q_scale=HEAD_DIM_A ** -0.5 * LOG2E)
    return pl.pallas_call(
        kern,
        grid=(m // tm, n // tn),
        in_specs=[
            pl.BlockSpec((tm, d), lambda i, j: (i, 0), **x_mode),
            pl.BlockSpec((1, d), lambda i, j: (0, 0)),
            pl.BlockSpec((d, tn), lambda i, j: (0, j)),
            pl.BlockSpec((d, nt), lambda i, j: (0, 0)),
        ],
        out_specs=[
            pl.BlockSpec((tm, tn), lambda i, j: (i, jnp.maximum(j - q_tiles, 0))),
            pl.BlockSpec((tm, tn), lambda i, j: (i, jnp.minimum(j, bf_tiles - 1))),
            pl.BlockSpec((tm, nt), lambda i, j: (i, 0)),
        ],
        out_shape=[
            jax.ShapeDtypeStruct((m, n - w_a), F32),
            jax.ShapeDtypeStruct((m, 3 * w_a), BF16),
            jax.ShapeDtypeStruct((m, nt), F32),
        ],
        scratch_shapes=[pltpu.VMEM((tm, d), BF16)],
        compiler_params=pltpu.CompilerParams(
            dimension_semantics=("arbitrary", "arbitrary"), vmem_limit_bytes=VMEM_LIMIT),
        name="in_proj",
    )(x, pre_norm, w_all, w_tail)


def _bias_values(rb_ref, h, rel):
    nb = N_BUCKETS // 2
    max_exact = nb // 2
    n = jnp.abs(rel)
    nf = jnp.maximum(n, 1).astype(F32)
    large = max_exact + (jnp.log(nf / max_exact) / math.log(MAX_DISTANCE / max_exact)
                         * (nb - max_exact)).astype(jnp.int32)
    large = jnp.minimum(large, nb - 1)
    bucket = jnp.where(rel > 0, nb, 0) + jnp.where(n < max_exact, n, large)
    val = jnp.zeros(rel.shape, F32)
    for b in range(N_BUCKETS):
        val = jnp.where(bucket == b, rb_ref[b, h], val)
    return (val - rb_ref[FAR_BUCKET, h]) * LOG2E


def _bias_kernel(rb_ref, lamp_ref, fb_ref, mb_ref, sa_ref, se_ref, lam_ref, *, past, dec_seq, n_ca, lam_init):
    h = pl.program_id(0)
    t = ATT_BLOCK
    shift = CHUNK.bit_length() - 1
    i = lax.broadcasted_iota(jnp.int32, (t, t), 0)
    j = lax.broadcasted_iota(jnp.int32, (t, t), 1)
    for dd in range(N_NEAR):
        val = _bias_values(rb_ref, h, (j - i) - t * dd)
        if dd == 0:
            val = jnp.where((j >> shift) <= (i >> shift), val, NEG)
        fb_ref[0, dd] = val
    i = lax.broadcasted_iota(jnp.int32, (t, LANES), 0)
    j = lax.broadcasted_iota(jnp.int32, (t, LANES), 1)
    for qb in range(N_NEAR):
        val = _bias_values(rb_ref, h, (j - N_META) - (t * qb + i))
        mb_ref[0, qb] = jnp.where(j < N_META, val, NEG)
    i = lax.broadcasted_iota(jnp.int32, (dec_seq, n_ca), 0)
    j = lax.broadcasted_iota(jnp.int32, (dec_seq, n_ca), 1)
    sa_ref[0] = _bias_values(rb_ref, h, (j - N_META) - (past + i))
    i = lax.broadcasted_iota(jnp.int32, (dec_seq, LANES), 0)
    j = lax.broadcasted_iota(jnp.int32, (dec_seq, LANES), 1)
    n_extra = N_META + past - n_ca + dec_seq
    val = _bias_values(rb_ref, h, (n_ca - N_META + j) - (past + i))
    se_ref[0] = jnp.where(j < n_extra, val, NEG)
    lp = lamp_ref[...]
    s1 = jnp.sum(lp[0:1] * lp[1:2], axis=-1, keepdims=True)
    s2 = jnp.sum(lp[2:3] * lp[3:4], axis=-1, keepdims=True)
    lam_ref[0] = jnp.broadcast_to(jnp.exp(s1) - jnp.exp(s2) + lam_init, (SUBLANES, LANES))


def _bias_tables(rel_bias, lam_params, past, dec_seq, n_ca, lam_init):
    nh = rel_bias.shape[1]
    t = ATT_BLOCK
    kern = functools.partial(_bias_kernel, past=past, dec_seq=dec_seq, n_ca=n_ca, lam_init=lam_init)
    return pl.pallas_call(
        kern,
        grid=(nh,),
        in_specs=[
            pl.BlockSpec(memory_space=pltpu.SMEM),
            pl.BlockSpec((4, HEAD_DIM_A), lambda h: (0, 0)),
        ],
        out_specs=[
            pl.BlockSpec((1, N_NEAR, t, t), lambda h: (h, 0, 0, 0)),
            pl.BlockSpec((1, N_NEAR, t, LANES), lambda h: (h, 0, 0, 0)),
            pl.BlockSpec((1, dec_seq, n_ca), lambda h: (h, 0, 0)),
            pl.BlockSpec((1, dec_seq, LANES), lambda h: (h, 0, 0)),
            pl.BlockSpec((1, SUBLANES, LANES), lambda h: (h, 0, 0)),
        ],
        out_shape=[
            jax.ShapeDtypeStruct((nh, N_NEAR, t, t), F32),
            jax.ShapeDtypeStruct((nh, N_NEAR, t, LANES), F32),
            jax.ShapeDtypeStruct((nh, dec_seq, n_ca), F32),
            jax.ShapeDtypeStruct((nh, dec_seq, LANES), F32),
            jax.ShapeDtypeStruct((nh, SUBLANES, LANES), F32),
        ],
        compiler_params=pltpu.CompilerParams(dimension_semantics=("arbitrary",)),
        name="bias_tables",
    )(rel_bias, lam_params)


def _nt_dot(a, b):
    return lax.dot_general(a, b, (((1,), (1,)), ((), ())), preferred_element_type=F32)


def _attn_finish(o, z, subln, lam_init):
    ms = jnp.mean(o * o, axis=-1, keepdims=True)
    on = o * lax.rsqrt(ms + EPS) * subln * (1.0 - lam_init)
    return on * _silu(z)


def _attn_prompt_kernel(lam_ref, q_ref, k_ref, v_ref, km_ref, vm_ref, fb_ref, mb_ref, z_ref, sub_ref, o_ref,
                        m_ref, l_ref, acc_ref, sa_ref, sb_ref, *, lam_init):
    qi = pl.program_id(1)
    t = ATT_BLOCK
    d = HEAD_DIM_A
    hw = 2 * d

    def rep(x, n):
        return jnp.concatenate([x] * n, axis=1) if n > 1 else x

    def lane_fold(p):
        out = p[:, 0:LANES]
        for g in range(1, p.shape[1] // LANES):
            out = out + p[:, g * LANES:(g + 1) * LANES]
        return out

    def meta_init():
        vm = vm_ref[...]
        mb = mb_ref[0, jnp.minimum(qi, N_NEAR - 1)]
        for c in range(2):
            s = _nt_dot(q_ref[:, c * d:(c + 1) * d], km_ref[:, c * d:(c + 1) * d]) + mb
            m = jnp.broadcast_to(jnp.max(s, axis=-1, keepdims=True), (t, LANES))
            p = jnp.exp2(s - m)
            m_ref[c] = m
            l_ref[c] = p
            acc_ref[c] = jnp.dot(p.astype(BF16), vm, preferred_element_type=F32)

    def qk(start, width):
        rows = pl.ds(start, width)
        return [_nt_dot(q_ref[:, c * d:(c + 1) * d], k_ref[rows, c * d:(c + 1) * d]) for c in range(2)]

    def softmax_pv(s_both, start, width, bias):
        v = v_ref[pl.ds(start, width), :]
        for c in range(2):
            s = s_both[c]
            if bias is not None:
                s = s + bias
            m_old = m_ref[c]
            m_new = jnp.maximum(m_old, jnp.max(s, axis=-1, keepdims=True))
            p = jnp.exp2(s - rep(m_new, width // LANES))
            alpha = jnp.exp2(m_old - m_new)
            m_ref[c] = m_new
            l_ref[c] = alpha * l_ref[c] + lane_fold(p)
            acc_ref[c] = rep(alpha, hw // LANES) * acc_ref[c] + jnp.dot(
                p.astype(BF16), v, preferred_element_type=F32)

    wide = FAR_BLOCKS * t
    far_shift = FAR_BLOCKS.bit_length() - 1
    n_far = jnp.maximum(qi - (N_NEAR - 1), 0)
    n_wide = n_far >> far_shift

    def put(ref, s_both):
        ref[0] = s_both[0]
        ref[1] = s_both[1]

    @pl.when(n_wide > 0)
    def _():
        put(sa_ref, qk(0, wide))
        meta_init()

    @pl.when(n_wide == 0)
    def _():
        meta_init()

    def far_body(i, carry):
        b0 = pl.multiple_of(i * (2 * wide), wide)
        b1 = pl.multiple_of(b0 + wide, wide)
        b2 = pl.multiple_of(jnp.minimum(b0 + 2 * wide, (n_wide - 1) * wide), wide)
        put(sb_ref, qk(b1, wide))
        softmax_pv([sa_ref[0], sa_ref[1]], b0, wide, None)
        put(sa_ref, qk(b2, wide))
        softmax_pv([sb_ref[0], sb_ref[1]], b1, wide, None)
        return carry

    def far_body2(i, carry):
        far_body(2 * i, carry)
        return far_body(2 * i + 1, carry)
    lax.fori_loop(0, n_wide >> 2, far_body2, 0)
    lax.fori_loop((n_wide >> 2) * 2, n_wide >> 1, far_body, 0)

    @pl.when((n_wide & 1) == 1)
    def _():
        softmax_pv([sa_ref[0], sa_ref[1]], pl.multiple_of((n_wide - 1) * wide, wide), wide, None)

    n_rest = n_far - (n_wide << far_shift)
    for cnt in range(FAR_BLOCKS):
        @pl.when(jnp.logical_and(qi >= N_NEAR - 1, n_rest == cnt))
        def _(cnt=cnt):
            first = qi - (N_NEAR - 1) - cnt
            n_tail = cnt + N_NEAR
            s_next = qk(pl.multiple_of(first * t, t), t)
            for b in range(n_tail):
                s_cur = s_next
                if b + 1 < n_tail:
                    s_next = qk(pl.multiple_of((first + b + 1) * t, t), t)
                bias = None if b < cnt else fb_ref[0, n_tail - 1 - b]
                softmax_pv(s_cur, pl.multiple_of((first + b) * t, t), t, bias)

    @pl.when(qi < N_NEAR - 1)
    def _():
        for dd in range(N_NEAR - 2, -1, -1):
            @pl.when(qi - dd >= 0)
            def _(dd=dd):
                start = pl.multiple_of((qi - dd) * t, t)
                softmax_pv(qk(start, t), start, t, fb_ref[0, dd])

    inv0 = 1.0 / jnp.sum(l_ref[0], axis=-1, keepdims=True)
    inv1 = lam_ref[0] / jnp.sum(l_ref[1], axis=-1, keepdims=True)
    o = acc_ref[0] * inv0 - acc_ref[1] * inv1
    o_ref[...] = _attn_finish(o, z_ref[...], sub_ref[...], lam_init).astype(BF16)


def _attn_prompt(lam, qkv16, o32, km, vm, fb, mb, subln, n_heads, lam_init):
    seq = qkv16.shape[0]
    t = ATT_BLOCK
    hw = 2 * HEAD_DIM_A
    kern = functools.partial(_attn_prompt_kernel, lam_init=lam_init)
    return pl.pallas_call(
        kern,
        grid=(n_heads, seq // t),
        in_specs=[
            pl.BlockSpec(memory_space=pltpu.SMEM),
            pl.BlockSpec((t, hw), lambda h, i: (i, h)),
            pl.BlockSpec((seq, hw), lambda h, i: (0, n_heads + h)),
            pl.BlockSpec((seq, hw), lambda h, i: (0, 2 * n_heads + h)),
            pl.BlockSpec((LANES, hw), lambda h, i: (0, h)),
            pl.BlockSpec((LANES, hw), lambda h, i: (0, h)),
            pl.BlockSpec((1, N_NEAR, t, t), lambda h, i: (h, 0, 0, 0)),
            pl.BlockSpec((1, N_NEAR, t, LANES), lambda h, i: (h, 0, 0, 0)),
            pl.BlockSpec((t, hw), lambda h, i: (i, 2 * n_heads + h)),
            pl.BlockSpec((1, hw), lambda h, i: (0, 0)),
        ],
        out_specs=pl.BlockSpec((t, hw), lambda h, i: (i, h)),
        out_shape=jax.ShapeDtypeStruct((seq, n_heads * hw), BF16),
        scratch_shapes=[
            pltpu.VMEM((2, t, LANES), F32),
            pltpu.VMEM((2, t, LANES), F32),
            pltpu.VMEM((2, t, hw), F32),
            pltpu.VMEM((2, t, FAR_BLOCKS * t), F32),
            pltpu.VMEM((2, t, FAR_BLOCKS * t), F32),
        ],
        compiler_params=pltpu.CompilerParams(
            dimension_semantics=("arbitrary", "arbitrary"), vmem_limit_bytes=VMEM_LIMIT),
        name="attn_prompt",
    )(lam, qkv16, qkv16, qkv16, km, vm, fb, mb, o32, subln)


def _attn_sample_kernel(lam_ref, q_ref, kc_ref, vc_ref, ke_ref, ve_ref, sa_ref, se_ref, z_ref, sub_ref, o_ref,
                        *, lam_init, n_heads, n_ca, hs):
    d = HEAD_DIM_A
    hw = 2 * d
    for i in range(hs):
        h = pl.program_id(1) * hs + i
        cols = slice(i * hw, (i + 1) * hw)
        vc = jnp.concatenate([vc_ref[0, pl.ds(c * n_heads + h, n_ca, stride=2 * n_heads), :] for c in range(2)],
                             axis=1).astype(BF16)
        ve = ve_ref[0, :, cols]
        outs = []
        for c in range(2):
            q = q_ref[:, i * hw + c * d:i * hw + (c + 1) * d]
            kc = kc_ref[0, pl.ds(2 * h + c, n_ca, stride=2 * n_heads), :].astype(BF16)
            s_a = _nt_dot(q, kc) + sa_ref[i]
            s_e = _nt_dot(q, ke_ref[0, :, i * hw + c * d:i * hw + (c + 1) * d]) + se_ref[i]
            m = jnp.maximum(jnp.max(s_a, axis=-1, keepdims=True), jnp.max(s_e, axis=-1, keepdims=True))
            p_a = jnp.exp2(s_a - m)
            p_e = jnp.exp2(s_e - m)
            l = jnp.sum(p_a, axis=-1, keepdims=True) + jnp.sum(p_e, axis=-1, keepdims=True)
            acc = (jnp.dot(p_a.astype(BF16), vc, preferred_element_type=F32)
                   + jnp.dot(p_e.astype(BF16), ve, preferred_element_type=F32))
            outs.append(acc / l)
        o = outs[0] - lam_ref[0] * outs[1]
        o_ref[:, cols] = _attn_finish(o, z_ref[:, cols], sub_ref[...], lam_init).astype(BF16)


def _attn_sample(lam, qkv16, o32, cache_k, cache_v, ke, ve, sa, se, subln, n_heads, dec_b, dec_seq, n_ca,
                 lam_init):
    hw = 2 * HEAD_DIM_A
    hs = min(SAMPLE_HEADS_PER_STEP, n_heads)
    n_grp = n_heads // hs
    kern = functools.partial(_attn_sample_kernel, lam_init=lam_init, n_heads=n_heads, n_ca=n_ca, hs=hs)
    return pl.pallas_call(
        kern,
        grid=(dec_b, n_grp),
        in_specs=[
            pl.BlockSpec(memory_space=pltpu.SMEM),
            pl.BlockSpec((dec_seq, hs * hw), lambda b, g: (b, g)),
            pl.BlockSpec((1, n_ca * n_heads * 2, HEAD_DIM_A), lambda b, g: (b, 0, 0)),
            pl.BlockSpec((1, n_ca * n_heads * 2, HEAD_DIM_A), lambda b, g: (b, 0, 0)),
            pl.BlockSpec((1, LANES, hs * hw), lambda b, g: (b, 0, g)),
            pl.BlockSpec((1, LANES, hs * hw), lambda b, g: (b, 0, g)),
            pl.BlockSpec((hs, dec_seq, n_ca), lambda b, g: (g, 0, 0)),
            pl.BlockSpec((hs, dec_seq, LANES), lambda b, g: (g, 0, 0)),
            pl.BlockSpec((dec_seq, hs * hw), lambda b, g: (b, 2 * n_grp + g)),
            pl.BlockSpec((1, hw), lambda b, g: (0, 0)),
        ],
        out_specs=pl.BlockSpec((dec_seq, hs * hw), lambda b, g: (b, g)),
        out_shape=jax.ShapeDtypeStruct((dec_b * dec_seq, n_heads * hw), BF16),
        compiler_params=pltpu.CompilerParams(
            dimension_semantics=("arbitrary", "arbitrary"), vmem_limit_bytes=VMEM_LIMIT),
        name="attn_sample",
    )(lam, qkv16, cache_k, cache_v, ke, ve, sa, se, o32, subln)


def _hdot(a, b):
    return jnp.dot(a, b, preferred_element_type=F32, precision=HIGHEST)


def _inv_unit_lower(a_list, row, col):
    sh = 4
    n = a_list[0].shape[0]
    eye = (row == col).astype(F32)
    same = (row >> sh) == (col >> sh)
    dblk = [jnp.where(same, a, 0.0) for a in a_list]
    t = [eye - x for x in dblk]
    pw = [x.astype(BF16) for x in dblk]
    for _ in range(sh - 1):
        pw = [jnp.dot(x, x, preferred_element_type=F32).astype(BF16) for x in pw]
        t = [ti + jnp.dot(ti.astype(BF16), x, preferred_element_type=F32) for ti, x in zip(t, pw)]
    while (1 << sh) < n:
        offm = jnp.logical_and((row >> (sh + 1)) == (col >> (sh + 1)), (row >> sh) != (col >> sh))
        off = [jnp.where(offm, a, 0.0).astype(BF16) for a in a_list]
        tb = [ti.astype(BF16) for ti in t]
        mid = [jnp.dot(o, x, preferred_element_type=F32).astype(BF16) for o, x in zip(off, tb)]
        t = [ti - jnp.dot(x, m, preferred_element_type=F32) for ti, x, m in zip(t, tb, mid)]
        sh += 1
    return t


def _gdn_kernel(q_ref, k_ref, v_ref, z_ref, ba_ref, cwq_ref, cwk_ref, cwv_ref, gp_ref, nb_ref, s0_ref,
                hq_ref, hk_ref, hv_ref, y_ref, sout_ref, s_ref, xq_ref, xk_ref, xv_ref, *, lr, hg):
    c = pl.program_id(2)
    n = GDN_CHUNK
    dh = HEAD_DIM_B
    halo = SUBLANES

    @pl.when(c == 0)
    def _():
        s_ref[...] = s0_ref[0]
        xq_ref[0:halo, :] = hq_ref[0]
        xk_ref[0:halo, :] = hk_ref[0]
        xv_ref[0:halo, :] = hv_ref[0]

    row_w = lax.broadcasted_iota(jnp.int32, (n, hg * dh), 0)

    def conv(x_ref, xbuf, cw_ref):
        xbuf[halo:halo + lr, :] = x_ref[...]
        if lr < n:
            xbuf[halo + lr:halo + n, :] = jnp.zeros((n - lr, hg * dh), F32)
        y = xbuf[halo - 3:halo - 3 + n, :] * cw_ref[0:1, :]
        for i in range(1, CONV_W):
            y = y + xbuf[halo - 3 + i:halo - 3 + i + n, :] * cw_ref[i:i + 1, :]
        y = _silu(y)
        if lr < n:
            y = jnp.where(row_w < lr, y, 0.0)
        xbuf[0:halo, :] = xbuf[lr:lr + halo, :]
        return y

    yq = conv(q_ref, xq_ref, cwq_ref)
    yk = conv(k_ref, xk_ref, cwk_ref)
    yv = conv(v_ref, xv_ref, cwv_ref)

    row = lax.broadcasted_iota(jnp.int32, (n, n), 0)
    col = lax.broadcasted_iota(jnp.int32, (n, n), 1)
    ba = ba_ref[...]
    if lr < n:
        ba = jnp.concatenate([ba, jnp.zeros((n - lr, LANES), F32)], axis=0)
    live = row < lr
    beta = jnp.where(jnp.logical_and(live, col < hg), jax.nn.sigmoid(ba), 0.0)
    gval = -jnp.exp(gp_ref[0, 0:1, :]) * jax.nn.softplus(ba + gp_ref[0, 1:2, :])
    g = jnp.where(jnp.logical_and(live, jnp.logical_and(col >= hg, col < 2 * hg)), gval, 0.0)
    gsum = _hdot((row >= col).astype(F32), g)
    gsum_t = gsum.T
    incl = row >= col
    strict = row > col

    def run(heads):
        cols = {hh: slice(hh * dh, (hh + 1) * dh) for hh in heads}
        gc = {hh: jnp.broadcast_to(gsum[:, hg + hh:hg + hh + 1], (n, n)) for hh in heads}
        bc = {hh: jnp.broadcast_to(beta[:, hh:hh + 1], (n, n)) for hh in heads}
        gam = {hh: jnp.where(incl, jnp.exp(jnp.minimum(gc[hh] - gsum_t[hg + hh:hg + hh + 1, :], 0.0)), 0.0)
               for hh in heads}
        qh = {hh: yq[:, cols[hh]] * (lax.rsqrt(jnp.sum(yq[:, cols[hh]] * yq[:, cols[hh]], axis=-1, keepdims=True)
                                               + EPS) * (dh ** -0.5)) for hh in heads}
        kh = {hh: yk[:, cols[hh]] * lax.rsqrt(jnp.sum(yk[:, cols[hh]] * yk[:, cols[hh]], axis=-1, keepdims=True)
                                              + EPS) for hh in heads}
        qb = {hh: qh[hh].astype(BF16) for hh in heads}
        kb = {hh: kh[hh].astype(BF16) for hh in heads}
        kk = {hh: _nt_dot(kb[hh], kb[hh]) for hh in heads}
        qk = {hh: _nt_dot(qb[hh], kb[hh]) for hh in heads}
        a = [jnp.where(strict, bc[hh] * kk[hh] * gam[hh], 0.0) for hh in heads]
        t = dict(zip(heads, _inv_unit_lower(a, row, col)))
        eg = {hh: jnp.exp(gc[hh]) for hh in heads}
        rhs = {hh: jnp.concatenate([yv[:, cols[hh]] * bc[hh], kh[hh] * (bc[hh] * eg[hh])], axis=1).astype(BF16)
               for hh in heads}
        uw = {hh: jnp.dot(t[hh].astype(BF16), rhs[hh], preferred_element_type=F32) for hh in heads}
        s = {hh: s_ref[hh] for hh in heads}
        sb = {hh: s[hh].astype(BF16) for hh in heads}
        v_new = {hh: uw[hh][:, :dh] - jnp.dot(uw[hh][:, dh:].astype(BF16), sb[hh], preferred_element_type=F32)
                 for hh in heads}
        vnb = {hh: v_new[hh].astype(BF16) for hh in heads}
        o = {hh: eg[hh] * jnp.dot(qb[hh], sb[hh], preferred_element_type=F32)
             + jnp.dot((qk[hh] * gam[hh]).astype(BF16), vnb[hh], preferred_element_type=F32) for hh in heads}
        g_last = {hh: gc[hh][n - 1:n, :] for hh in heads}
        kd = {hh: (kh[hh] * jnp.exp(g_last[hh] - gc[hh])).astype(BF16) for hh in heads}
        for hh in heads:
            s_ref[hh] = s[hh] * jnp.exp(g_last[hh]) + lax.dot_general(
                kd[hh], vnb[hh], (((0,), (0,)), ((), ())), preferred_element_type=F32)
        for hh in heads:
            on = o[hh] * lax.rsqrt(jnp.mean(o[hh] * o[hh], axis=-1, keepdims=True) + EPS) * nb_ref[...]
            y_ref[:, cols[hh]] = (on[0:lr] * _silu(z_ref[:, cols[hh]])).astype(BF16)

    run(list(range(hg)))

    @pl.when(c == pl.num_programs(2) - 1)
    def _():
        sout_ref[0] = s_ref[...]


def _gdn(o32, ba, conv_w, gate_par, norm_b, s0, halo0, *, n_seq, n_chunks, lr, row_blk_off, w_b, col0):
    hg = min(GDN_HEADS_PER_STEP, w_b // HEAD_DIM_B)
    gw = hg * HEAD_DIM_B
    n_groups = w_b // gw
    n_heads = w_b // HEAD_DIM_B
    cb = col0 // gw
    wb = w_b // gw

    def rmap(off):
        return lambda s, g, c: (row_blk_off + s * n_chunks + c, off + g)

    kern = functools.partial(_gdn_kernel, lr=lr, hg=hg)
    return pl.pallas_call(
        kern,
        grid=(n_seq, n_groups, n_chunks),
        in_specs=[
            pl.BlockSpec((lr, gw), rmap(cb)),
            pl.BlockSpec((lr, gw), rmap(cb + wb)),
            pl.BlockSpec((lr, gw), rmap(cb + 2 * wb)),
            pl.BlockSpec((lr, gw), rmap(cb + 3 * wb)),
            pl.BlockSpec((lr, LANES), rmap(0)),
            pl.BlockSpec((CONV_W, gw), lambda s, g, c: (0, g)),
            pl.BlockSpec((CONV_W, gw), lambda s, g, c: (0, wb + g)),
            pl.BlockSpec((CONV_W, gw), lambda s, g, c: (0, 2 * wb + g)),
            pl.BlockSpec((1, 2, LANES), lambda s, g, c: (g, 0, 0)),
            pl.BlockSpec((1, HEAD_DIM_B), lambda s, g, c: (0, 0)),
            pl.BlockSpec((1, hg, HEAD_DIM_B, HEAD_DIM_B), lambda s, g, c: (s, g, 0, 0)),
            pl.BlockSpec((1, SUBLANES, gw), lambda s, g, c: (s, 0, g)),
            pl.BlockSpec((1, SUBLANES, gw), lambda s, g, c: (s, 0, wb + g)),
            pl.BlockSpec((1, SUBLANES, gw), lambda s, g, c: (s, 0, 2 * wb + g)),
        ],
        out_specs=[
            pl.BlockSpec((lr, gw), lambda s, g, c: (s * n_chunks + c, g)),
            pl.BlockSpec((1, hg, HEAD_DIM_B, HEAD_DIM_B), lambda s, g, c: (s, g, 0, 0)),
        ],
        out_shape=[
            jax.ShapeDtypeStruct((n_seq * n_chunks * lr, w_b), BF16),
            jax.ShapeDtypeStruct((n_seq, n_heads, HEAD_DIM_B, HEAD_DIM_B), F32),
        ],
        scratch_shapes=[
            pltpu.VMEM((hg, HEAD_DIM_B, HEAD_DIM_B), F32),
            pltpu.VMEM((SUBLANES + GDN_CHUNK, gw), F32),
            pltpu.VMEM((SUBLANES + GDN_CHUNK, gw), F32),
            pltpu.VMEM((SUBLANES + GDN_CHUNK, gw), F32),
        ],
        compiler_params=pltpu.CompilerParams(
            dimension_semantics=("arbitrary", "arbitrary", "arbitrary"), vmem_limit_bytes=VMEM_LIMIT),
        name=f"gdn_l{lr}",
    )(o32, o32, o32, o32, ba, conv_w, conv_w, conv_w, gate_par, norm_b, s0, halo0, halo0, halo0)


def _out_proj_kernel(ya_ref, yb_ref, w_ref, h_ref, pn_ref, o_ref, *, w_a):
    j = pl.program_id(1)
    tn = w_ref.shape[1]
    r = (jnp.dot(ya_ref[...], w_ref[0:w_a, :], preferred_element_type=F32)
         + jnp.dot(yb_ref[...], w_ref[w_a:, :], preferred_element_type=F32))
    o_ref[:, pl.ds(pl.multiple_of(j * tn, tn), tn)] = r

    @pl.when(j == pl.num_programs(1) - 1)
    def _():
        rows = 64

        def body(i, carry):
            sl = pl.ds(pl.multiple_of(i * rows, rows), rows)
            y = o_ref[sl, :]
            ms = jnp.mean(y * y, axis=-1, keepdims=True)
            o_ref[sl, :] = h_ref[sl, :] + y * lax.rsqrt(ms + EPS) * pn_ref[...]
            return carry
        lax.fori_loop(0, o_ref.shape[0] // rows, body, 0)


def _out_proj(ya, yb, w_out, h, post_norm):
    m, d = h.shape
    w_a = ya.shape[1]
    tm, tn = min(OUT_TM, m), OUT_TN
    kern = functools.partial(_out_proj_kernel, w_a=w_a)
    return pl.pallas_call(
        kern,
        grid=(m // tm, d // tn),
        in_specs=[
            pl.BlockSpec((tm, w_a), lambda i, j: (i, 0)),
            pl.BlockSpec((tm, yb.shape[1]), lambda i, j: (i, 0)),
            pl.BlockSpec((d, tn), lambda i, j: (0, j)),
            pl.BlockSpec((tm, d), lambda i, j: (i, 0)),
            pl.BlockSpec((1, d), lambda i, j: (0, 0)),
        ],
        out_specs=pl.BlockSpec((tm, d), lambda i, j: (i, 0)),
        out_shape=jax.ShapeDtypeStruct((m, d), F32),
        compiler_params=pltpu.CompilerParams(
            dimension_semantics=("arbitrary", "arbitrary"), vmem_limit_bytes=VMEM_LIMIT),
        name="out_proj",
    )(ya, yb, w_out, h, post_norm)


def _kv_layout_kernel(km_ref, vm_ref, k_ref, v_ref, ko_ref, vo_ref, kt_ref, vt_ref, *, n_heads):
    tm = k_ref.shape[0]
    n_lead = kt_ref.shape[0]
    rpt = 2 * n_heads

    @pl.when(pl.program_id(0) == 0)
    def _():
        kt_ref[...] = km_ref[...]
        vt_ref[...] = vm_ref[...]

    for p in range(rpt):
        piece = slice(p * LANES, (p + 1) * LANES)
        vrow = (p % 2) * n_heads + p // 2
        ko_ref[pl.ds(p, n_lead, stride=rpt), :] = kt_ref[:, piece]
        ko_ref[pl.ds(n_lead * rpt + p, tm - n_lead, stride=rpt), :] = k_ref[0:tm - n_lead, piece]
        vo_ref[pl.ds(vrow, n_lead, stride=rpt), :] = vt_ref[:, piece]
        vo_ref[pl.ds(n_lead * rpt + vrow, tm - n_lead, stride=rpt), :] = v_ref[0:tm - n_lead, piece]
    kt_ref[...] = k_ref[tm - n_lead:tm, :]
    vt_ref[...] = v_ref[tm - n_lead:tm, :]


def _kv_layout(k_lead, v_lead, p32, n_heads, w_a):
    seq = p32.shape[0]
    n_lead = k_lead.shape[0]
    tm = KV_TM
    rpt = 2 * n_heads
    n_blk = seq // tm
    spec_out = pl.BlockSpec((tm * rpt, LANES), lambda i: (i, 0))
    shape_out = jax.ShapeDtypeStruct(((n_lead + seq) * rpt, LANES), F32)
    return pl.pallas_call(
        functools.partial(_kv_layout_kernel, n_heads=n_heads),
        grid=(n_blk + 1,),
        in_specs=[
            pl.BlockSpec((n_lead, w_a), lambda i: (0, 0)),
            pl.BlockSpec((n_lead, w_a), lambda i: (0, 0)),
            pl.BlockSpec((tm, w_a), lambda i: (jnp.minimum(i, n_blk - 1), 0)),
            pl.BlockSpec((tm, w_a), lambda i: (jnp.minimum(i, n_blk - 1), 1)),
        ],
        out_specs=[spec_out, spec_out],
        out_shape=[shape_out, shape_out],
        scratch_shapes=[pltpu.VMEM((n_lead, w_a), F32), pltpu.VMEM((n_lead, w_a), F32)],
        compiler_params=pltpu.CompilerParams(dimension_semantics=("arbitrary",), vmem_limit_bytes=VMEM_LIMIT),
        name="kv_layout",
    )(k_lead, v_lead, p32, p32)


def kernel(x_prompt, x_sample, cache_k_a, cache_v_a, state_ssm_b, state_conv_b, meta_tokens, rel_bias, pre_norm,
           w_in, lambda_q1, lambda_k1, lambda_q2, lambda_k2, subln_a, conv_b, a_log_b, dt_bias_b, norm_b, w_out,
           post_norm):
    batch, seq, d_model = x_prompt.shape
    dec_b, dec_seq, _ = x_sample.shape
    depth = w_in.shape[0]
    assert batch == 1 and depth == 1
    n_heads_a = rel_bias.shape[1]
    w_a = n_heads_a * 2 * HEAD_DIM_A
    w_b = d_model - w_a
    n_heads_b = w_b // HEAD_DIM_B
    n_cache = cache_k_a.shape[2]
    past = n_cache - N_META
    n_ca = (n_cache // LANES) * LANES
    n_main = 4 * w_a + 4 * w_b
    lam_init = _lambda_init(0)
    hg = min(GDN_HEADS_PER_STEP, n_heads_b)
    n_groups = n_heads_b // hg
    n_dec = dec_b * dec_seq
    assert seq % PROJ_TM == 0 and seq % GDN_CHUNK == 0 and n_cache - n_ca + dec_seq <= LANES

    w_in_bf = w_in[0].astype(BF16)
    wb = w_in_bf[:, n_main:n_main + n_heads_b].reshape(d_model, n_groups, hg)
    wa = w_in_bf[:, n_main + n_heads_b:].reshape(d_model, n_groups, hg)
    w_tail = jnp.concatenate([wb, wa, jnp.zeros((d_model, n_groups, LANES - 2 * hg), BF16)], axis=-1)
    w_tail = w_tail.reshape(d_model, n_groups * LANES)
    w_out_bf = w_out[0].astype(BF16)
    gate_par = jnp.zeros((n_groups, 2, LANES), F32)
    gate_par = gate_par.at[:, 0, hg:2 * hg].set(a_log_b[0].reshape(n_groups, hg))
    gate_par = gate_par.at[:, 1, hg:2 * hg].set(dt_bias_b[0].reshape(n_groups, hg))
    lam_params = jnp.stack([lambda_q1[0], lambda_k1[0], lambda_q2[0], lambda_k2[0]])

    xp = x_prompt[0]
    n_small = -(-(n_dec + N_META) // 16) * 16
    xs = jnp.concatenate([x_sample.reshape(n_dec, d_model), meta_tokens.astype(F32),
                          jnp.zeros((n_small - n_dec - N_META, d_model), F32)], axis=0)
    p32, p16, pba = _in_proj(xp, pre_norm, w_in_bf, w_tail, w_a, n_main)
    s32, s16, sba = _in_proj(xs, pre_norm, w_in_bf, w_tail, w_a, n_main, single_tile=True)

    fb, mb, sa, se, lam_t = _bias_tables(rel_bias, lam_params, past, dec_seq, n_ca, lam_init)
    lam = lam_t[0, 0, 0:1]

    meta16 = s16[n_dec:n_dec + N_META]
    pad_m = jnp.zeros((LANES - N_META, w_a), BF16)
    km = jnp.concatenate([meta16[:, w_a:2 * w_a], pad_m], axis=0)
    vm = jnp.concatenate([meta16[:, 2 * w_a:], pad_m], axis=0)
    ya_p = _attn_prompt(lam, p16, p32, km, vm, fb, mb, subln_a, n_heads_a, lam_init)

    ck = cache_k_a[0].reshape(dec_b, n_cache * n_heads_a * 2, HEAD_DIM_A)
    cv = cache_v_a[0].reshape(dec_b, n_cache, n_heads_a, 2, HEAD_DIM_A).transpose(0, 1, 3, 2, 4)
    cv = cv.reshape(dec_b, n_cache * 2 * n_heads_a, HEAD_DIM_A)
    n_extra = n_cache - n_ca + dec_seq
    pad_e = jnp.zeros((dec_b, LANES - n_extra, w_a), BF16)
    ck_tail = cache_k_a[0, :, n_ca:].reshape(dec_b, n_cache - n_ca, w_a).astype(BF16)
    cv_tail = cache_v_a[0, :, n_ca:].reshape(dec_b, n_cache - n_ca, w_a).astype(BF16)
    ke = jnp.concatenate([ck_tail, s16[:n_dec, w_a:2 * w_a].reshape(dec_b, dec_seq, w_a), pad_e], axis=1)
    ve = jnp.concatenate([cv_tail, s16[:n_dec, 2 * w_a:].reshape(dec_b, dec_seq, w_a), pad_e], axis=1)
    ya_s = _attn_sample(lam, s16, s32, ck, cv, ke, ve, sa, se, subln_a, n_heads_a, dec_b, dec_seq, n_ca,
                        lam_init)

    col0 = 3 * w_a
    qkv_cols = slice(col0, col0 + 3 * w_b)
    zero_halo = jnp.zeros((1, SUBLANES, 3 * w_b), F32)
    gdn = functools.partial(_gdn, conv_w=conv_b[0], gate_par=gate_par, norm_b=norm_b, w_b=w_b, col0=col0)
    _, s_meta = gdn(s32, sba, s0=jnp.zeros((1, n_heads_b, HEAD_DIM_B, HEAD_DIM_B), F32), halo0=zero_halo,
                    n_seq=1, n_chunks=1, lr=N_META, row_blk_off=n_dec // N_META)
    meta_halo = jnp.concatenate([jnp.zeros((SUBLANES - (CONV_W - 1), 3 * w_b), F32),
                                 s32[n_dec + N_META - (CONV_W - 1):n_dec + N_META, qkv_cols]], axis=0)[None]
    yb_p, ssm_p = gdn(p32, pba, s0=s_meta, halo0=meta_halo, n_seq=1, n_chunks=seq // GDN_CHUNK, lr=GDN_CHUNK,
                      row_blk_off=0)
    samp_halo = jnp.concatenate([jnp.zeros((dec_b, SUBLANES - (CONV_W - 1), 3 * w_b), F32), state_conv_b[0]], axis=1)
    yb_s, ssm_s = gdn(s32, sba, s0=state_ssm_b[0].astype(F32), halo0=samp_halo, n_seq=dec_b, n_chunks=1,
                      lr=dec_seq, row_blk_off=0)

    y_p = _out_proj(ya_p, yb_p, w_out_bf, xp, post_norm)
    y_s = _out_proj(ya_s, yb_s, w_out_bf, x_sample.reshape(n_dec, d_model), post_norm)

    hd = HEAD_DIM_A
    meta32 = s32[n_dec:n_dec + N_META]
    k_p, v_p = _kv_layout(meta32[:, :w_a], meta32[:, w_a:2 * w_a], p32, n_heads_a, w_a)
    v_p = v_p.reshape(N_META + seq, 2, n_heads_a, hd).transpose(0, 2, 1, 3)
    conv_p = p32[seq - (CONV_W - 1):, qkv_cols]
    k_s = s32[:n_dec, :w_a]
    v_s = s32[:n_dec, w_a:2 * w_a]
    conv_s = s32[:n_dec, qkv_cols].reshape(dec_b, dec_seq, 3 * w_b)[:, dec_seq - (CONV_W - 1):]
    return (
        y_p[None],
        y_s.reshape(dec_b, dec_seq, d_model),
        k_p.reshape(1, 1, N_META + seq, n_heads_a, 2, hd),
        v_p.reshape(1, 1, N_META + seq, n_heads_a, 2 * hd),
        ssm_p[None],
        conv_p[None, None],
        k_s.reshape(1, dec_b, dec_seq, n_heads_a, 2, hd),
        v_s.reshape(1, dec_b, dec_seq, n_heads_a, 2 * hd),
        ssm_s[None],
        conv_s[None],
    )
```
